```python
import math
import jax
import jax.numpy as jnp
from jax import lax
import numpy as np

D_MODEL = 2048
BATCH = 4
SEQ = 2048
DEPTH = 4

CHUNK = 64
Q_BLOCK = 128
N_MIXERS = 3
DEEPNORM_ALPHA = (2.0 * DEPTH) ** 0.25
DEEPNORM_BETA = (8.0 * DEPTH) ** -0.25
N_MLA_LAYERS = (DEPTH + 2) // 3
N_FOX_LAYERS = (DEPTH + 1) // 3
N_DSA_LAYERS = DEPTH // 3

MLA_HEADS = 16
MLA_NOPE_DIM = 128
MLA_ROPE_DIM = 64
MLA_V_DIM = 128
MLA_Q_RANK = 512
MLA_KV_RANK = 512
ROPE_THETA = 10000.0

FOX_HEADS = 16
FOX_HEAD_DIM = D_MODEL // FOX_HEADS

DSA_HEADS = 16
DSA_HEAD_DIM = D_MODEL // DSA_HEADS
IDX_HEADS = 16
IDX_DIM = 64
IDX_TOPK_MAX = 256

T5_BUCKETS = 32
T5_MAX_DISTANCE = 128

MOE_GROUPS = 4
MOE_EXPERTS_PER_GROUP = 8
MOE_EXPERTS = MOE_GROUPS * MOE_EXPERTS_PER_GROUP
MOE_TOPK = 2
MOE_D_FF = 512

LN_EPS = 1e-5
RMS_EPS = 1e-6
NEG_INF = -1e30

MLA_IN_DIM = MLA_Q_RANK + MLA_KV_RANK + MLA_ROPE_DIM
FOX_IN_DIM = 3 * FOX_HEADS * FOX_HEAD_DIM + FOX_HEADS
DSA_SPLITS = [DSA_HEADS * DSA_HEAD_DIM, DSA_HEAD_DIM, DSA_HEAD_DIM, IDX_HEADS * IDX_DIM, IDX_DIM, IDX_HEADS]
DSA_IN_DIM = sum(DSA_SPLITS)

kernel_name = 'hybrid_chunk_causal_encoder'

F32 = jnp.float32


def _split(h, sizes):
    return jnp.split(h, np.cumsum(sizes)[:-1].tolist(), axis=-1)


def _layernorm(x, g, b):
    xf = x.astype(F32)
    mu = jnp.mean(xf, axis=-1, keepdims=True)
    var = jnp.mean(jnp.square(xf - mu), axis=-1, keepdims=True)
    return ((xf - mu) * lax.rsqrt(var + LN_EPS) * g.astype(F32) + b.astype(F32)).astype(x.dtype)


def _rmsnorm(x, g):
    xf = x.astype(F32)
    return (xf * lax.rsqrt(jnp.mean(xf * xf, axis=-1, keepdims=True) + RMS_EPS) * g.astype(F32)).astype(x.dtype)


def _rope(x, pos):
    half = x.shape[-1] // 2
    inv_freq = ROPE_THETA ** (-jnp.arange(half, dtype=F32) / half)
    ang = pos.astype(F32)[:, None] * inv_freq[None, :]
    cos = jnp.cos(ang)[None, :, None, :]
    sin = jnp.sin(ang)[None, :, None, :]
    xf = x.astype(F32)
    x1, x2 = xf[..., :half], xf[..., half:]
    return jnp.concatenate([x1 * cos - x2 * sin, x2 * cos + x1 * sin], axis=-1).astype(x.dtype)


def _t5_bucket(rel):
    nb = T5_BUCKETS // 2
    max_exact = nb // 2
    ret = jnp.where(rel > 0, nb, 0)
    n = jnp.abs(rel)
    nf = jnp.maximum(n, 1).astype(F32)
    large = max_exact + (jnp.log(nf / max_exact) / math.log(T5_MAX_DISTANCE / max_exact)
                         * (nb - max_exact)).astype(jnp.int32)
    large = jnp.minimum(large, nb - 1)
    return ret + jnp.where(n < max_exact, n, large)


def _block(a):
    b, s = a.shape[:2]
    return jnp.moveaxis(a.reshape(b, s // Q_BLOCK, Q_BLOCK, *a.shape[2:]), 1, 0)


def _unblock(a):
    a = jnp.moveaxis(a, 0, 1)
    return a.reshape(a.shape[0], a.shape[1] * a.shape[2], *a.shape[3:])


def _block_starts(s):
    return jnp.arange(s // Q_BLOCK, dtype=jnp.int32) * Q_BLOCK


def _attend(qb, k, v, bias, mask, scale):
    s = jnp.einsum('bqhd,bkhd->bhqk', qb, k, preferred_element_type=F32) * scale + bias
    s = jnp.where(mask, s, NEG_INF)
    p = jax.nn.softmax(s, axis=-1).astype(v.dtype)
    return jnp.einsum('bhqk,bkhd->bqhd', p, v)


def _mla(x, pos, w_in, q_norm, kv_norm, w_q_up, w_kv_up, w_o):
    B, S, _ = x.shape
    c_q, c_kv, k_rope = _split(x @ w_in, [MLA_Q_RANK, MLA_KV_RANK, MLA_ROPE_DIM])
    c_q = _rmsnorm(c_q, q_norm)
    c_kv = _rmsnorm(c_kv, kv_norm)
    q = (c_q @ w_q_up).reshape(B, S, MLA_HEADS, MLA_NOPE_DIM + MLA_ROPE_DIM)
    kv = (c_kv @ w_kv_up).reshape(B, S, MLA_HEADS, MLA_NOPE_DIM + MLA_V_DIM)
    q = jnp.concatenate([q[..., :MLA_NOPE_DIM], _rope(q[..., MLA_NOPE_DIM:], pos)], axis=-1)
    k_rope = jnp.broadcast_to(_rope(k_rope[:, :, None, :], pos), (B, S, MLA_HEADS, MLA_ROPE_DIM))
    k = jnp.concatenate([kv[..., :MLA_NOPE_DIM], k_rope], axis=-1)
    v = kv[..., MLA_NOPE_DIM:]
    scale = (MLA_NOPE_DIM + MLA_ROPE_DIM) ** -0.5
    kchunk = pos // CHUNK

    def block(args):
        qb, start = args
        qchunk = (start + jnp.arange(Q_BLOCK, dtype=jnp.int32)) // CHUNK
        mask = kchunk[None, :] <= qchunk[:, None]
        return _attend(qb, k, v, 0.0, mask, scale)

    o = _unblock(lax.map(block, (_block(q), _block_starts(S))))
    return o.reshape(B, S, MLA_HEADS * MLA_V_DIM) @ w_o


def _fox(x, pos, w_in, b_f, w_o):
    B, S, _ = x.shape
    hd = FOX_HEADS * FOX_HEAD_DIM
    q, k, v, f_logit = _split(x @ w_in, [hd, hd, hd, FOX_HEADS])
    q = q.reshape(B, S, FOX_HEADS, FOX_HEAD_DIM)
    k = k.reshape(B, S, FOX_HEADS, FOX_HEAD_DIM)
    v = v.reshape(B, S, FOX_HEADS, FOX_HEAD_DIM)
    log_f = jax.nn.log_sigmoid((f_logit + b_f).astype(F32))
    c = jnp.cumsum(log_f, axis=1)
    c_k = jnp.transpose(c, (0, 2, 1))[:, :, None, :]
    scale = FOX_HEAD_DIM ** -0.5

    def block(args):
        qb, cqb, start = args
        qpos = start + jnp.arange(Q_BLOCK, dtype=jnp.int32)
        mask = pos[None, :] <= qpos[:, None]
        bias = jnp.transpose(cqb, (0, 2, 1))[..., None] - c_k
        return _attend(qb, k, v, bias, mask, scale)

    o = _unblock(lax.map(block, (_block(q), _block(c), _block_starts(S))))
    return o.reshape(B, S, hd) @ w_o


def _dsa(x, pos, t5_bias, w_in, w_o):
    B, S, _ = x.shape
    topk = min(IDX_TOPK_MAX, S // 4)
    q, k, v, q_idx, k_idx, w_idx = _split(x @ w_in, DSA_SPLITS)
    q = q.reshape(B, S, DSA_HEADS, DSA_HEAD_DIM)
    q_idx = q_idx.reshape(B, S, IDX_HEADS, IDX_DIM)
    w_idx = w_idx.astype(F32) * (IDX_HEADS * IDX_DIM) ** -0.5
    scale = DSA_HEAD_DIM ** -0.5
    kchunk = pos // CHUNK
    gather = jax.vmap(lambda a, i: a[i])

    def block(args):
        qb, qib, wib, start = args
        qpos = start + jnp.arange(Q_BLOCK, dtype=jnp.int32)
        qchunk = qpos // CHUNK
        idx_logits = jnp.einsum('bqhd,bkd->bqhk', qib, k_idx, preferred_element_type=F32)
        score = jnp.einsum('bqhk,bqh->bqk', jax.nn.relu(idx_logits), wib)
        admissible = kchunk[None, :] <= qchunk[:, None]
        score = jnp.where(admissible[None], score, NEG_INF)
        _, sel = lax.top_k(score, topk)
        valid = (sel // CHUNK) <= qchunk[None, :, None]
        k_sel = gather(k, sel)
        v_sel = gather(v, sel)
        rel_bias = t5_bias[_t5_bucket(sel - qpos[None, :, None])].astype(F32)
        s = (jnp.einsum('bqhd,bqkd->bhqk', qb, k_sel, preferred_element_type=F32) * scale
             + jnp.transpose(rel_bias, (0, 3, 1, 2)))
        s = jnp.where(valid[:, None], s, NEG_INF)
        p = jax.nn.softmax(s, axis=-1).astype(v.dtype)
        return jnp.einsum('bhqk,bqkd->bqhd', p, v_sel)

    o = _unblock(lax.map(block, (_block(q), _block(q_idx), _block(w_idx), _block_starts(S))))
    return o.reshape(B, S, DSA_HEADS * DSA_HEAD_DIM) @ w_o


def _hier_moe(x, w_group, b_group, w_router, b_router, w_gate, w_up, w_down):
    B, S, D = x.shape
    n = B * S
    xf = x.reshape(n, D)
    p_group = jax.nn.softmax((xf @ w_group + b_group).astype(F32), axis=-1)
    g_sel = jnp.argmax(p_group, axis=-1)
    p_g = jnp.take_along_axis(p_group, g_sel[:, None], axis=-1)
    e_logits = (xf @ w_router + b_router).astype(F32).reshape(n, MOE_GROUPS, MOE_EXPERTS_PER_GROUP)
    e_logits = jnp.take_along_axis(e_logits, g_sel[:, None, None], axis=1)[:, 0]
    top_v, top_i = lax.top_k(e_logits, MOE_TOPK)
    top_w = jax.nn.softmax(top_v, axis=-1) * p_g
    gate_in_group = jnp.einsum('nk,nke->ne', top_w, jax.nn.one_hot(top_i, MOE_EXPERTS_PER_GROUP, dtype=F32))
    gates = (gate_in_group[:, None, :] * jax.nn.one_hot(g_sel, MOE_GROUPS, dtype=F32)[:, :, None]).astype(x.dtype)
    y = jnp.zeros_like(xf)
    for g in range(MOE_GROUPS):
        sl = slice(g * MOE_EXPERTS_PER_GROUP, (g + 1) * MOE_EXPERTS_PER_GROUP)
        h_gate = jnp.einsum('nd,edf->nef', xf, w_gate[sl])
        h_up = jnp.einsum('nd,edf->nef', xf, w_up[sl])
        h = jax.nn.silu(h_gate) * h_up * gates[:, g, :, None]
        y = y + jnp.einsum('nef,efd->nd', h, w_down[sl])
    return y.reshape(B, S, D)


def setup_inputs(seed: int = 0) -> dict:
    key = jax.random.key(seed)
    ks = jax.random.split(key, 24)

    def w(k, shape, fan_in, gain=1.0):
        return jax.random.normal(k, shape, F32) * (gain * fan_in ** -0.5)

    def gain_vec(k, shape):
        return 1.0 + 0.02 * jax.random.normal(k, shape, F32)

    D = D_MODEL
    return {
        'x': jax.random.normal(ks[0], (BATCH, SEQ, D), F32),
        't5_rel_bias': 0.1 * jax.random.normal(ks[1], (T5_BUCKETS, DSA_HEADS), F32),
        'mla_w_in': w(ks[2], (N_MLA_LAYERS, D, MLA_IN_DIM), D),
        'mla_q_norm': gain_vec(ks[3], (N_MLA_LAYERS, MLA_Q_RANK)),
        'mla_kv_norm': gain_vec(ks[4], (N_MLA_LAYERS, MLA_KV_RANK)),
        'mla_w_q_up': w(ks[5], (N_MLA_LAYERS, MLA_Q_RANK, MLA_HEADS * (MLA_NOPE_DIM + MLA_ROPE_DIM)), MLA_Q_RANK),
        'mla_w_kv_up': w(ks[6], (N_MLA_LAYERS, MLA_KV_RANK, MLA_HEADS * (MLA_NOPE_DIM + MLA_V_DIM)), MLA_KV_RANK),
        'mla_w_o': w(ks[7], (N_MLA_LAYERS, MLA_HEADS * MLA_V_DIM, D), MLA_HEADS * MLA_V_DIM, DEEPNORM_BETA),
        'fox_w_in': w(ks[8], (N_FOX_LAYERS, D, FOX_IN_DIM), D),
        'fox_b_f': 1.0 + 3.0 * jax.random.uniform(ks[9], (N_FOX_LAYERS, FOX_HEADS), F32),
        'fox_w_o': w(ks[10], (N_FOX_LAYERS, FOX_HEADS * FOX_HEAD_DIM, D), FOX_HEADS * FOX_HEAD_DIM, DEEPNORM_BETA),
        'dsa_w_in': w(ks[11], (N_DSA_LAYERS, D, DSA_IN_DIM), D),
        'dsa_w_o': w(ks[12], (N_DSA_LAYERS, DSA_HEADS * DSA_HEAD_DIM, D), DSA_HEADS * DSA_HEAD_DIM, DEEPNORM_BETA),
        'ln_g': gain_vec(ks[13], (DEPTH, 2, D)),
        'ln_b': 0.02 * jax.random.normal(ks[14], (DEPTH, 2, D), F32),
        'moe_w_group': w(ks[15], (DEPTH, D, MOE_GROUPS), D),
        'moe_b_group': 0.01 * jax.random.normal(ks[16], (DEPTH, MOE_GROUPS), F32),
        'moe_w_router': w(ks[17], (DEPTH, D, MOE_EXPERTS), D),
        'moe_b_router': 0.01 * jax.random.normal(ks[18], (DEPTH, MOE_EXPERTS), F32),
        'moe_w_gate': w(ks[19], (DEPTH, MOE_EXPERTS, D, MOE_D_FF), D),
        'moe_w_up': w(ks[20], (DEPTH, MOE_EXPERTS, D, MOE_D_FF), D),
        'moe_w_down': w(ks[21], (DEPTH, MOE_EXPERTS, MOE_D_FF, D), MOE_D_FF, DEEPNORM_BETA),
    }


def reference(x, t5_rel_bias, mla_w_in, mla_q_norm, mla_kv_norm, mla_w_q_up, mla_w_kv_up, mla_w_o,
              fox_w_in, fox_b_f, fox_w_o, dsa_w_in, dsa_w_o, ln_g, ln_b,
              moe_w_group, moe_b_group, moe_w_router, moe_b_router, moe_w_gate, moe_w_up, moe_w_down):
    pos = jnp.arange(x.shape[1], dtype=jnp.int32)
    for layer in range(DEPTH):
        kind = layer % N_MIXERS
        j = layer // N_MIXERS
        if kind == 0:
            y = _mla(x, pos, mla_w_in[j], mla_q_norm[j], mla_kv_norm[j], mla_w_q_up[j], mla_w_kv_up[j], mla_w_o[j])
        elif kind == 1:
            y = _fox(x, pos, fox_w_in[j], fox_b_f[j], fox_w_o[j])
        else:
            y = _dsa(x, pos, t5_rel_bias, dsa_w_in[j], dsa_w_o[j])
        x = _layernorm(DEEPNORM_ALPHA * x + y, ln_g[layer, 0], ln_b[layer, 0])
        y = _hier_moe(x, moe_w_group[layer], moe_b_group[layer], moe_w_router[layer], moe_b_router[layer],
                      moe_w_gate[layer], moe_w_up[layer], moe_w_down[layer])
        x = _layernorm(DEEPNORM_ALPHA * x + y, ln_g[layer, 1], ln_b[layer, 1])
    return x
```

```python
import functools
import math

import numpy as np
import jax
import jax.numpy as jnp
from jax import lax
from jax.experimental import pallas as pl
from jax.experimental.pallas import tpu as pltpu

F32 = jnp.float32
BF16 = jnp.bfloat16
I32 = jnp.int32

D_MODEL = 2048
DEPTH = 4
CHUNK = 64
N_MIXERS = 3
DEEPNORM_ALPHA = (2.0 * DEPTH) ** 0.25

HEADS = 16
HEAD_DIM = 128
MLA_NOPE_DIM = 128
MLA_ROPE_DIM = 64
MLA_Q_RANK = 512
MLA_KV_RANK = 512
MLA_QK_PAD = 256
ROPE_THETA = 10000.0

IDX_HEADS = 16
IDX_DIM = 64
IDX_TOPK_MAX = 256

T5_BUCKETS = 32
T5_MAX_DISTANCE = 128
T5_FAR_BUCKET = T5_BUCKETS // 2 - 1
T5_BAND_BACK = 128

MOE_GROUPS = 4
MOE_EXPERTS_PER_GROUP = 8
MOE_EXPERTS = MOE_GROUPS * MOE_EXPERTS_PER_GROUP
MOE_D_FF = 512

LN_EPS = 1e-5
RMS_EPS = 1e-6
NEG_INF = -1e30

LANES = 128
VMEM_LIMIT = 56 * 1024 * 1024

ATTN_TQ = 256
MOE_TM = 256
COMBINE_TM = 256


def _params(semantics, vmem=VMEM_LIMIT):
    return pltpu.CompilerParams(dimension_semantics=semantics, vmem_limit_bytes=vmem)


def _mm_kernel(a_ref, w_ref, o_ref, *, scale, scaled_blocks):
    acc = jnp.dot(a_ref[...], w_ref[...], preferred_element_type=F32)
    if scaled_blocks:
        acc = acc * jnp.where(pl.program_id(0) < scaled_blocks, scale, 1.0).astype(F32)
    o_ref[...] = acc.astype(o_ref.dtype)


def _matmul(a, w, *, n_out, tn, tm=1024, col_block_offset=0, scale=1.0, scaled_blocks=0, out_dtype=BF16):
    m, k = a.shape
    tm = min(tm, m)
    kern = functools.partial(_mm_kernel, scale=scale, scaled_blocks=scaled_blocks)
    return pl.pallas_call(
        kern,
        grid=(n_out // tn, m // tm),
        in_specs=[pl.BlockSpec((tm, k), lambda j, i: (i, 0)),
                  pl.BlockSpec((k, tn), lambda j, i: (0, j + col_block_offset))],
        out_specs=pl.BlockSpec((tm, tn), lambda j, i: (i, j)),
        out_shape=jax.ShapeDtypeStruct((m, n_out), out_dtype),
        compiler_params=_params(("arbitrary", "arbitrary")),
    )(a, w)


def _split3(v):
    hi = v.astype(BF16)
    r1 = v - hi.astype(F32)
    mid = r1.astype(BF16)
    lo = (r1 - mid.astype(F32)).astype(BF16)
    return hi, mid, lo


def _dot_precise(x, w):
    xh, xm, xl = _split3(x)
    wh, wm, wl = _split3(w)
    d = lambda p, q: jnp.dot(p, q, preferred_element_type=F32)
    return (d(xl, wh) + d(xm, wm) + d(xh, wl)) + (d(xm, wh) + d(xh, wm)) + d(xh, wh)


def _small_mm_kernel(x_ref, w_ref, b_ref, o_ref):
    o_ref[...] = _dot_precise(x_ref[...], w_ref[...]) + b_ref[...]


def _small_matmul(x, w, b, tm=512):
    m, k = x.shape
    tm = min(tm, m)
    return pl.pallas_call(
        _small_mm_kernel,
        grid=(m // tm,),
        in_specs=[pl.BlockSpec((tm, k), lambda i: (i, 0)),
                  pl.BlockSpec((k, LANES), lambda i: (0, 0)),
                  pl.BlockSpec((1, LANES), lambda i: (0, 0))],
        out_specs=pl.BlockSpec((tm, LANES), lambda i: (i, 0)),
        out_shape=jax.ShapeDtypeStruct((m, LANES), F32),
        compiler_params=_params(("arbitrary",)),
    )(x, w, b)


def _pad_cols(w, width=LANES):
    return jnp.pad(w, ((0, 0), (0, width - w.shape[1])))


def _layernorm_rows(z, g, b):
    mu = jnp.mean(z, axis=-1, keepdims=True)
    zc = z - mu
    var = jnp.mean(zc * zc, axis=-1, keepdims=True)
    return zc * lax.rsqrt(var + LN_EPS) * g + b


def _mm_res_ln_kernel(a_ref, w_ref, x_ref, g_ref, b_ref, o_ref, obf_ref):
    y = jnp.dot(a_ref[...], w_ref[...], preferred_element_type=F32)
    out = _layernorm_rows(DEEPNORM_ALPHA * x_ref[...] + y, g_ref[...], b_ref[...])
    o_ref[...] = out
    obf_ref[...] = out.astype(BF16)


def _mm_res_ln(a, w, x, g, b, tm=256):
    m, k = a.shape
    d = w.shape[1]
    tm = min(tm, m)
    return pl.pallas_call(
        _mm_res_ln_kernel,
        grid=(m // tm,),
        in_specs=[pl.BlockSpec((tm, k), lambda i: (i, 0)),
                  pl.BlockSpec((k, d), lambda i: (0, 0)),
                  pl.BlockSpec((tm, d), lambda i: (i, 0)),
                  pl.BlockSpec((1, d), lambda i: (0, 0)),
                  pl.BlockSpec((1, d), lambda i: (0, 0))],
        out_specs=[pl.BlockSpec((tm, d), lambda i: (i, 0)),
                   pl.BlockSpec((tm, d), lambda i: (i, 0))],
        out_shape=[jax.ShapeDtypeStruct((m, d), F32), jax.ShapeDtypeStruct((m, d), BF16)],
        compiler_params=_params(("arbitrary",)),
    )(a, w, x, g, b)


def _rope_tables(seq):
    half = MLA_ROPE_DIM // 2
    inv_freq = ROPE_THETA ** (-jnp.arange(half, dtype=F32) / half)
    ang = jnp.arange(seq, dtype=I32).astype(F32)[:, None] * inv_freq[None, :]
    cos, sin = jnp.cos(ang), jnp.sin(ang)
    z = jnp.zeros_like(cos)
    cos_t = jnp.concatenate([cos, cos, z, z], axis=1)
    sin_a = jnp.concatenate([-sin, z, z, z], axis=1)
    sin_b = jnp.concatenate([z, sin, z, z], axis=1)
    return cos_t, sin_a, sin_b


def _rope_lanes(r, cos_t, sin_a, sin_b):
    from_right = pltpu.roll(r, LANES - MLA_ROPE_DIM // 2, 1)
    from_left = pltpu.roll(r, MLA_ROPE_DIM // 2, 1)
    return r * cos_t + from_right * sin_a + from_left * sin_b


def _rms_rows(v, g):
    return v * lax.rsqrt(jnp.mean(v * v, axis=-1, keepdims=True) + RMS_EPS) * g


def _mla_in_kernel(x_ref, w_ref, qn_ref, kvn_ref, cos_ref, sina_ref, sinb_ref, cq_ref, ckv_ref, kr_ref):
    h = jnp.dot(x_ref[...], w_ref[...], preferred_element_type=F32)
    cq_ref[...] = _rms_rows(h[:, :MLA_Q_RANK], qn_ref[...]).astype(BF16)
    ckv_ref[...] = _rms_rows(h[:, MLA_Q_RANK:MLA_Q_RANK + MLA_KV_RANK], kvn_ref[...]).astype(BF16)
    r = h[:, MLA_Q_RANK + MLA_KV_RANK:]
    kr_ref[...] = _rope_lanes(r, cos_ref[...], sina_ref[...], sinb_ref[...]).astype(BF16)


def _mla_in(x_bf, w_in_pad, q_norm, kv_norm, tables, seq, tm=512):
    m, k = x_bf.shape
    tm = min(tm, seq)
    nw = w_in_pad.shape[1]
    per_seq = seq // tm
    tab_spec = pl.BlockSpec((tm, LANES), lambda i: (i % per_seq, 0))
    return pl.pallas_call(
        _mla_in_kernel,
        grid=(m // tm,),
        in_specs=[pl.BlockSpec((tm, k), lambda i: (i, 0)),
                  pl.BlockSpec((k, nw), lambda i: (0, 0)),
                  pl.BlockSpec((1, MLA_Q_RANK), lambda i: (0, 0)),
                  pl.BlockSpec((1, MLA_KV_RANK), lambda i: (0, 0)),
                  tab_spec, tab_spec, tab_spec],
        out_specs=[pl.BlockSpec((tm, MLA_Q_RANK), lambda i: (i, 0)),
                   pl.BlockSpec((tm, MLA_KV_RANK), lambda i: (i, 0)),
                   pl.BlockSpec((tm, LANES), lambda i: (i, 0))],
        out_shape=[jax.ShapeDtypeStruct((m, MLA_Q_RANK), BF16),
                   jax.ShapeDtypeStruct((m, MLA_KV_RANK), BF16),
                   jax.ShapeDtypeStruct((m, LANES), BF16)],
        compiler_params=_params(("arbitrary",)),
    )(x_bf, w_in_pad, q_norm, kv_norm, *tables)


def _softmax_pv(s, v):
    m = jnp.max(s, axis=1, keepdims=True)
    p = jnp.exp(s - m)
    l = jnp.sum(p, axis=1, keepdims=True)
    o = jnp.dot(p.astype(BF16), v, preferred_element_type=F32)
    return o / l


def _nt_dot(q, k):
    return lax.dot_general(q, k, (((1,), (1,)), ((), ())), preferred_element_type=F32)


def _mla_attn_kernel(q_ref, kn_ref, v_ref, kr_ref, cos_ref, sina_ref, sinb_ref, o_ref, qc_ref, kc_ref, *, tq):
    seq = q_ref.shape[0]
    r = q_ref[:, MLA_NOPE_DIM:].astype(F32)
    qc_ref[:, :MLA_NOPE_DIM] = q_ref[:, :MLA_NOPE_DIM]
    qc_ref[:, MLA_NOPE_DIM:] = _rope_lanes(r, cos_ref[...], sina_ref[...], sinb_ref[...]).astype(BF16)
    kc_ref[:, :MLA_NOPE_DIM] = kn_ref[...]
    kc_ref[:, MLA_NOPE_DIM:] = kr_ref[...]
    for i in range(seq // tq):
        kv = (i + 1) * tq
        s = _nt_dot(qc_ref[i * tq:kv, :], kc_ref[:kv, :])
        row = lax.broadcasted_iota(I32, (tq, kv), 0) + i * tq
        col = lax.broadcasted_iota(I32, (tq, kv), 1)
        s = jnp.where((col // CHUNK) <= (row // CHUNK), s, NEG_INF)
        o_ref[i * tq:kv, :] = _softmax_pv(s, v_ref[:kv, :]).astype(o_ref.dtype)


def _mla_attn(q, kv, kr, tables, batch, seq):
    tq = min(ATTN_TQ, seq)
    kern = functools.partial(_mla_attn_kernel, tq=tq)
    tab_spec = pl.BlockSpec((seq, LANES), lambda b, h: (0, 0))
    return pl.pallas_call(
        kern,
        grid=(batch, HEADS),
        in_specs=[pl.BlockSpec((seq, MLA_QK_PAD), lambda b, h: (b, h)),
                  pl.BlockSpec((seq, HEAD_DIM), lambda b, h: (b, 2 * h)),
                  pl.BlockSpec((seq, HEAD_DIM), lambda b, h: (b, 2 * h + 1)),
                  pl.BlockSpec((seq, LANES), lambda b, h: (b, 0)),
                  tab_spec, tab_spec, tab_spec],
        out_specs=pl.BlockSpec((seq, HEAD_DIM), lambda b, h: (b, h)),
        out_shape=jax.ShapeDtypeStruct((batch * seq, HEADS * HEAD_DIM), BF16),
        scratch_shapes=[pltpu.VMEM((seq, MLA_QK_PAD), BF16), pltpu.VMEM((seq, MLA_QK_PAD), BF16)],
        compiler_params=_params(("arbitrary", "arbitrary")),
    )(q, kv, kv, kr, *tables)


def _mla_q_up_layout(w_q_up):
    rank = w_q_up.shape[0]
    w = w_q_up.reshape(rank, HEADS, MLA_NOPE_DIM + MLA_ROPE_DIM)
    w = jnp.pad(w, ((0, 0), (0, 0), (0, MLA_QK_PAD - MLA_NOPE_DIM - MLA_ROPE_DIM)))
    return w.reshape(rank, HEADS * MLA_QK_PAD)


def _mla_layer(x, x_bf, w_in, q_norm, kv_norm, w_q_up, w_kv_up, w_o, ln_g, ln_b, batch, seq):
    tables = _rope_tables(seq)
    w_in_pad = jnp.pad(w_in, ((0, 0), (0, MLA_Q_RANK + MLA_KV_RANK + LANES - w_in.shape[1]))).astype(BF16)
    cq, ckv, kr = _mla_in(x_bf, w_in_pad, q_norm[None, :], kv_norm[None, :], tables, seq)
    scale = (MLA_NOPE_DIM + MLA_ROPE_DIM) ** -0.5
    nq = HEADS * MLA_QK_PAD
    q = _matmul(cq, _mla_q_up_layout(w_q_up).astype(BF16), n_out=nq, tn=1024, scale=scale, scaled_blocks=nq // 1024)
    kv = _matmul(ckv, w_kv_up.astype(BF16), n_out=w_kv_up.shape[1], tn=1024)
    o = _mla_attn(q, kv, kr, tables, batch, seq)
    return _mm_res_ln(o, w_o.astype(BF16), x, ln_g[None, :], ln_b[None, :])


def _log_sigmoid(z):
    return jnp.minimum(z, 0.0) - jnp.log1p(jnp.exp(-jnp.abs(z)))


def _fox_gate_kernel(f_ref, c_ref, ct_ref, *, blk):
    seq = f_ref.shape[0]
    tri = (lax.broadcasted_iota(I32, (blk, blk), 1) <= lax.broadcasted_iota(I32, (blk, blk), 0)).astype(BF16)
    carry = jnp.zeros((1, LANES), F32)
    for j in range(seq // blk):
        lf = _log_sigmoid(f_ref[j * blk:(j + 1) * blk, :])
        hi, mid, lo = _split3(lf)
        d = lambda p: jnp.dot(tri, p, preferred_element_type=F32)
        c = (d(lo) + d(mid)) + d(hi) + carry
        c_ref[j * blk:(j + 1) * blk, :] = c
        ct_ref[:, j * blk:(j + 1) * blk] = c.T[:HEADS, :]
        carry = c[blk - 1:blk, :]


def _fox_gate(f_logit, batch, seq):
    blk = min(256, seq)
    return pl.pallas_call(
        functools.partial(_fox_gate_kernel, blk=blk),
        grid=(batch,),
        in_specs=[pl.BlockSpec((seq, LANES), lambda b: (b, 0))],
        out_specs=[pl.BlockSpec((seq, LANES), lambda b: (b, 0)),
                   pl.BlockSpec((None, HEADS, seq), lambda b: (b, 0, 0))],
        out_shape=[jax.ShapeDtypeStruct((batch * seq, LANES), F32),
                   jax.ShapeDtypeStruct((batch, HEADS, seq), F32)],
        compiler_params=_params(("arbitrary",)),
    )(f_logit)


def _fox_attn_kernel(q_ref, k_ref, v_ref, ccol_ref, crow_ref, o_ref, *, tq):
    seq = q_ref.shape[0]
    h = pl.program_id(1)
    lane = lax.broadcasted_iota(I32, (seq, LANES), 1)
    cq = jnp.sum(jnp.where(lane == h, ccol_ref[...], 0.0), axis=1, keepdims=True)
    for i in range(seq // tq):
        kv = (i + 1) * tq
        s = _nt_dot(q_ref[i * tq:kv, :], k_ref[:kv, :])
        s = (s + cq[i * tq:kv, :]) - crow_ref[:, :kv]
        row = lax.broadcasted_iota(I32, (tq, kv), 0) + i * tq
        col = lax.broadcasted_iota(I32, (tq, kv), 1)
        s = jnp.where(col <= row, s, NEG_INF)
        o_ref[i * tq:kv, :] = _softmax_pv(s, v_ref[:kv, :]).astype(o_ref.dtype)


def _fox_attn(qkv, c_col, c_row, batch, seq):
    tq = min(ATTN_TQ, seq)
    return pl.pallas_call(
        functools.partial(_fox_attn_kernel, tq=tq),
        grid=(batch, HEADS),
        in_specs=[pl.BlockSpec((seq, HEAD_DIM), lambda b, h: (b, h)),
                  pl.BlockSpec((seq, HEAD_DIM), lambda b, h: (b, HEADS + h)),
                  pl.BlockSpec((seq, HEAD_DIM), lambda b, h: (b, 2 * HEADS + h)),
                  pl.BlockSpec((seq, LANES), lambda b, h: (b, 0)),
                  pl.BlockSpec((None, None, 1, seq), lambda b, h: (b, h, 0, 0))],
        out_specs=pl.BlockSpec((seq, HEAD_DIM), lambda b, h: (b, h)),
        out_shape=jax.ShapeDtypeStruct((batch * seq, HEADS * HEAD_DIM), BF16),
        compiler_params=_params(("arbitrary", "arbitrary")),
    )(qkv, qkv, qkv, c_col, c_row)


def _fox_layer(x, x_bf, w_in, b_f, w_o, ln_g, ln_b, batch, seq):
    hd = HEADS * HEAD_DIM
    scale = HEAD_DIM ** -0.5
    qkv = _matmul(x_bf, w_in.astype(BF16), n_out=3 * hd, tn=1024, scale=scale, scaled_blocks=hd // 1024)
    f_logit = _small_matmul(x, _pad_cols(w_in[:, 3 * hd:]), _pad_cols(b_f[None, :]))
    c_col, c_t = _fox_gate(f_logit, batch, seq)
    o = _fox_attn(qkv, c_col, c_t[:, :, None, :], batch, seq)
    return _mm_res_ln(o, w_o.astype(BF16), x, ln_g[None, :], ln_b[None, :])


def _t5_bucket_table(tq):
    a = np.arange(tq, dtype=np.int64)[:, None]
    b = np.arange(tq + T5_BAND_BACK, dtype=np.int64)[None, :]
    rel = (b - T5_BAND_BACK) - a
    nb = T5_BUCKETS // 2
    max_exact = nb // 2
    ret = np.where(rel > 0, nb, 0)
    n = np.abs(rel)
    nf = np.maximum(n, 1).astype(np.float32)
    large = max_exact + (np.log(nf / np.float32(max_exact)) / np.float32(math.log(T5_MAX_DISTANCE / max_exact))
                         * np.float32(nb - max_exact)).astype(np.int32)
    large = np.minimum(large, nb - 1)
    return (ret + np.where(n < max_exact, n, large)).astype(np.int32)


def _t5_band_kernel(t5_ref, bucket_ref, o_ref):
    bucket = bucket_ref[...]

    def per_head(h, carry):
        acc = jnp.zeros(bucket.shape, F32)
        for b in range(T5_BUCKETS):
            acc = jnp.where(bucket == b, t5_ref[b, h], acc)
        o_ref[h] = acc - t5_ref[T5_FAR_BUCKET, h]
        return carry

    lax.fori_loop(0, HEADS, per_head, 0)


def _t5_band(t5_bias, tq):
    bucket = jnp.asarray(_t5_bucket_table(tq))
    width = tq + T5_BAND_BACK
    return pl.pallas_call(
        _t5_band_kernel,
        grid=(1,),
        in_specs=[pl.BlockSpec(memory_space=pltpu.SMEM),
                  pl.BlockSpec((tq, width), lambda i: (0, 0))],
        out_specs=pl.BlockSpec((HEADS, tq, width), lambda i: (0, 0, 0)),
        out_shape=jax.ShapeDtypeStruct((HEADS, tq, width), F32),
        compiler_params=_params(("arbitrary",)),
    )(t5_bias, bucket)


def _sortable_key(v):
    bits = pltpu.bitcast(v, I32)
    return bits ^ ((bits >> 31) & 0x7FFFFFFF)


def _kth_largest_key(key, k):
    rows = key.shape[0]

    def count_ge(t):
        return jnp.sum((key >= t).astype(I32), axis=1, keepdims=True)

    t0 = jnp.where(count_ge(jnp.zeros((rows, 1), I32)) >= k, 0, jnp.iinfo(jnp.int32).min).astype(I32)

    def body(it, t):
        cand = t | (jnp.int32(1) << (30 - it))
        return jnp.where(count_ge(cand) >= k, cand, t)

    return lax.fori_loop(0, 31, body, t0)


def _dsa_index_kernel(qi_ref, tail_ref, wtail_ref, o_ref, *, tq, topk):
    seq = tail_ref.shape[0]
    i = pl.program_id(1)
    k_idx = tail_ref[:, :IDX_DIM].astype(BF16)
    w = wtail_ref[:, IDX_DIM:IDX_DIM + IDX_HEADS] * ((IDX_HEADS * IDX_DIM) ** -0.5)
    score = jnp.zeros((tq, seq), F32)
    for h in range(IDX_HEADS):
        logits = _nt_dot(qi_ref[:, h * IDX_DIM:(h + 1) * IDX_DIM], k_idx)
        score = score + jnp.maximum(logits, 0.0) * w[:, h:h + 1]
    row = lax.broadcasted_iota(I32, (tq, seq), 0) + i * tq
    col = lax.broadcasted_iota(I32, (tq, seq), 1)
    admissible = (col // CHUNK) <= (row // CHUNK)
    key = _sortable_key(jnp.where(admissible, score, NEG_INF))
    thr = _kth_largest_key(key, topk)
    o_ref[...] = jnp.where(admissible & (key >= thr), 0.0, NEG_INF).astype(o_ref.dtype)


def _dsa_index(q_idx, tail, batch, seq, topk):
    tq = min(ATTN_TQ, seq)
    per_seq = seq // tq
    return pl.pallas_call(
        functools.partial(_dsa_index_kernel, tq=tq, topk=topk),
        grid=(batch, per_seq),
        in_specs=[pl.BlockSpec((tq, IDX_HEADS * IDX_DIM), lambda b, i: (b * per_seq + i, 0)),
                  pl.BlockSpec((seq, LANES), lambda b, i: (b, 0)),
                  pl.BlockSpec((tq, LANES), lambda b, i: (b * per_seq + i, 0))],
        out_specs=pl.BlockSpec((tq, seq), lambda b, i: (b * per_seq + i, 0)),
        out_shape=jax.ShapeDtypeStruct((batch * seq, seq), BF16),
        compiler_params=_params(("arbitrary", "arbitrary")),
    )(q_idx, tail, tail)


def _dsa_attn_kernel(q_ref, k_ref, v_ref, sel_ref, band_ref, o_ref, *, tq):
    seq = k_ref.shape[0]
    width = band_ref.shape[1]
    for i in range(seq // tq):
        kv = (i + 1) * tq
        near = min(width, kv)
        far = kv - near
        dead = sel_ref[i * tq:kv, :kv] < 0.0
        s = _nt_dot(q_ref[i * tq:kv, :], k_ref[:kv, :])
        s_near = jnp.where(dead[:, far:], NEG_INF, s[:, far:] + band_ref[:, width - near:])
        m = jnp.max(s_near, axis=1, keepdims=True)
        if far:
            s_far = jnp.where(dead[:, :far], NEG_INF, s[:, :far])
            m = jnp.maximum(m, jnp.max(s_far, axis=1, keepdims=True))
        p = jnp.exp(s_near - m)
        l = jnp.sum(p, axis=1, keepdims=True)
        o = jnp.dot(p.astype(BF16), v_ref[far:kv, :], preferred_element_type=F32)
        if far:
            p = jnp.exp(s_far - m)
            l = l + jnp.sum(p, axis=1, keepdims=True)
            o = o + jnp.dot(p.astype(BF16), v_ref[:far, :], preferred_element_type=F32)
        o_ref[i * tq:kv, :] = (o / l).astype(o_ref.dtype)


def _dsa_attn(qkv, sel, band, batch, seq):
    tq = min(ATTN_TQ, seq)
    width = band.shape[2]
    return pl.pallas_call(
        functools.partial(_dsa_attn_kernel, tq=tq),
        grid=(batch, HEADS),
        in_specs=[pl.BlockSpec((seq, HEAD_DIM), lambda b, h: (b, h)),
                  pl.BlockSpec((seq, HEAD_DIM), lambda b, h: (b, HEADS)),
                  pl.BlockSpec((seq, HEAD_DIM), lambda b, h: (b, HEADS + 1)),
                  pl.BlockSpec((seq, seq), lambda b, h: (b, 0)),
                  pl.BlockSpec((None, tq, width), lambda b, h: (h, 0, 0))],
        out_specs=pl.BlockSpec((seq, HEAD_DIM), lambda b, h: (b, h)),
        out_shape=jax.ShapeDtypeStruct((batch * seq, HEADS * HEAD_DIM), BF16),
        compiler_params=_params(("arbitrary", "arbitrary")),
    )(qkv, qkv, qkv, sel, band)


def _dsa_layer(x, x_bf, t5_bias, w_in, w_o, ln_g, ln_b, batch, seq):
    hd = HEADS * HEAD_DIM
    n_qkv = hd + 2 * HEAD_DIM
    n_qi = IDX_HEADS * IDX_DIM
    scale = HEAD_DIM ** -0.5
    w_bf = w_in.astype(BF16)
    qkv = _matmul(x_bf, w_bf, n_out=n_qkv, tn=256, scale=scale, scaled_blocks=hd // 256)
    q_idx = _matmul(x_bf, w_bf, n_out=n_qi, tn=256, col_block_offset=n_qkv // 256)
    tail = _small_matmul(x, _pad_cols(w_in[:, n_qkv + n_qi:]), jnp.zeros((1, LANES), F32))
    topk = min(IDX_TOPK_MAX, seq // 4)
    sel = _dsa_index(q_idx, tail, batch, seq, topk)
    band = _t5_band(t5_bias, min(ATTN_TQ, seq))
    o = _dsa_attn(qkv, sel, band, batch, seq)
    return _mm_res_ln(o, w_o.astype(BF16), x, ln_g[None, :], ln_b[None, :])


def _route_kernel(l_ref, o_ref):
    logit = l_ref[...]
    lane = lax.broadcasted_iota(I32, logit.shape, 1)
    big = jnp.int32(LANES)

    def first_argmax(mask):
        top = jnp.max(jnp.where(mask, logit, -jnp.inf), axis=1, keepdims=True)
        idx = jnp.min(jnp.where(mask & (logit == top), lane, big), axis=1, keepdims=True)
        return top, idx

    is_group = lane < MOE_GROUPS
    g_top, g_sel = first_argmax(is_group)
    p_g = 1.0 / jnp.sum(jnp.where(is_group, jnp.exp(logit - g_top), 0.0), axis=1, keepdims=True)
    lo = MOE_GROUPS + g_sel * MOE_EXPERTS_PER_GROUP
    in_group = (lane >= lo) & (lane < lo + MOE_EXPERTS_PER_GROUP)
    v1, i1 = first_argmax(in_group)
    v2, i2 = first_argmax(in_group & (lane != i1))
    e2 = jnp.exp(v2 - v1)
    w1 = 1.0 / (1.0 + e2)
    w2 = e2 / (1.0 + e2)
    out = jnp.where(lane == 0, (i1 - MOE_GROUPS).astype(F32), 0.0)
    out = jnp.where(lane == 1, (i2 - MOE_GROUPS).astype(F32), out)
    out = jnp.where(lane == 2, w1 * p_g, out)
    out = jnp.where(lane == 3, w2 * p_g, out)
    o_ref[...] = out


def _route(logits, tm=512):
    m = logits.shape[0]
    tm = min(tm, m)
    return pl.pallas_call(
        _route_kernel,
        grid=(m // tm,),
        in_specs=[pl.BlockSpec((tm, LANES), lambda i: (i, 0))],
        out_specs=pl.BlockSpec((tm, LANES), lambda i: (i, 0)),
        out_shape=jax.ShapeDtypeStruct((m, LANES), F32),
        compiler_params=_params(("arbitrary",)),
    )(logits)


def _gather_rows(idx_ref, base, src_hbm, dst, sem, n_rows):
    def body(r, carry):
        pltpu.make_async_copy(src_hbm.at[pl.ds(idx_ref[base + r], 1)], dst.at[pl.ds(r, 1)], sem).start()
        return carry
    lax.fori_loop(0, n_rows, body, 0, unroll=8)


def _wait_rows(src_hbm, dst, sem, n_rows):
    pltpu.make_async_copy(src_hbm.at[pl.ds(0, n_rows)], dst, sem).wait()


def _moe_ffn_kernel(tile_expert_ref, tile_valid_ref, row_token_ref, x_hbm, wg_ref, wu_ref, wd_ref, o_ref,
                    xbuf, wg_bf, wu_bf, wd_bf, sem, *, tm):
    j = pl.program_id(0)
    n_tiles = pl.num_programs(0)
    slot = j % 2

    @pl.when((j == 0) & (tile_valid_ref[0] == 1))
    def _():
        _gather_rows(row_token_ref, 0, x_hbm, xbuf.at[0], sem.at[0], tm)

    @pl.when((j + 1 < n_tiles) & (tile_valid_ref[jnp.minimum(j + 1, n_tiles - 1)] == 1))
    def _():
        _gather_rows(row_token_ref, (j + 1) * tm, x_hbm, xbuf.at[1 - slot], sem.at[1 - slot], tm)

    new_expert = (j == 0) | (tile_expert_ref[j] != tile_expert_ref[jnp.maximum(j - 1, 0)])

    @pl.when(new_expert & (tile_valid_ref[j] == 1))
    def _():
        wg_bf[...] = wg_ref[...].astype(BF16)
        wu_bf[...] = wu_ref[...].astype(BF16)
        wd_bf[...] = wd_ref[...].astype(BF16)

    @pl.when(tile_valid_ref[j] == 1)
    def _():
        _wait_rows(x_hbm, xbuf.at[slot], sem.at[slot], tm)
        xb = xbuf[slot].astype(BF16)
        hg = jnp.dot(xb, wg_bf[...], preferred_element_type=F32)
        hu = jnp.dot(xb, wu_bf[...], preferred_element_type=F32)
        act = (hg * (1.0 / (1.0 + jnp.exp(-hg))) * hu).astype(BF16)
        o_ref[...] = jnp.dot(act, wd_bf[...], preferred_element_type=F32)

    @pl.when(tile_valid_ref[j] == 0)
    def _():
        o_ref[...] = jnp.zeros(o_ref.shape, o_ref.dtype)


def _moe_ffn(x, w_gate, w_up, w_down, tile_expert, tile_valid, row_token, tm):
    n_tiles = tile_expert.shape[0]
    d = x.shape[1]
    f = w_gate.shape[2]
    grid_spec = pltpu.PrefetchScalarGridSpec(
        num_scalar_prefetch=3,
        grid=(n_tiles,),
        in_specs=[pl.BlockSpec(memory_space=pl.ANY),
                  pl.BlockSpec((None, d, f), lambda j, te, tv, rt: (te[j], 0, 0)),
                  pl.BlockSpec((None, d, f), lambda j, te, tv, rt: (te[j], 0, 0)),
                  pl.BlockSpec((None, f, d), lambda j, te, tv, rt: (te[j], 0, 0))],
        out_specs=pl.BlockSpec((tm, d), lambda j, te, tv, rt: (j, 0)),
        scratch_shapes=[pltpu.VMEM((2, tm, d), F32),
                        pltpu.VMEM((d, f), BF16), pltpu.VMEM((d, f), BF16), pltpu.VMEM((f, d), BF16),
                        pltpu.SemaphoreType.DMA((2,))],
    )
    return pl.pallas_call(
        functools.partial(_moe_ffn_kernel, tm=tm),
        grid_spec=grid_spec,
        out_shape=jax.ShapeDtypeStruct((n_tiles * tm, d), F32),
        compiler_params=_params(("arbitrary",)),
    )(tile_expert, tile_valid, row_token, x, w_gate, w_up, w_down)


def _moe_combine_kernel(pos_ref, y_hbm, x_ref, r_ref, g_ref, b_ref, o_ref, obf_ref, ybuf, sem, *, tm):
    i = pl.program_id(0)
    n = pl.num_programs(0)
    slot = i % 2
    rows = 2 * tm

    @pl.when(i == 0)
    def _():
        _gather_rows(pos_ref, 0, y_hbm, ybuf.at[0], sem.at[0], rows)

    @pl.when(i + 1 < n)
    def _():
        _gather_rows(pos_ref, (i + 1) * rows, y_hbm, ybuf.at[1 - slot], sem.at[1 - slot], rows)

    _wait_rows(y_hbm, ybuf.at[slot], sem.at[slot], rows)
    g1 = r_ref[:, 2:3]
    g2 = r_ref[:, 3:4]
    y = g1 * ybuf[slot, :tm, :] + g2 * ybuf[slot, tm:, :]
    out = _layernorm_rows(DEEPNORM_ALPHA * x_ref[...] + y, g_ref[...], b_ref[...])
    o_ref[...] = out
    obf_ref[...] = out.astype(BF16)


def _moe_combine(pos, y_sorted, x, route, g, b, tm):
    m, d = x.shape
    grid_spec = pltpu.PrefetchScalarGridSpec(
        num_scalar_prefetch=1,
        grid=(m // tm,),
        in_specs=[pl.BlockSpec(memory_space=pl.ANY),
                  pl.BlockSpec((tm, d), lambda i, p: (i, 0)),
                  pl.BlockSpec((tm, LANES), lambda i, p: (i, 0)),
                  pl.BlockSpec((1, d), lambda i, p: (0, 0)),
                  pl.BlockSpec((1, d), lambda i, p: (0, 0))],
        out_specs=[pl.BlockSpec((tm, d), lambda i, p: (i, 0)),
                   pl.BlockSpec((tm, d), lambda i, p: (i, 0))],
        scratch_shapes=[pltpu.VMEM((2, 2 * tm, d), F32), pltpu.SemaphoreType.DMA((2,))],
    )
    return pl.pallas_call(
        functools.partial(_moe_combine_kernel, tm=tm),
        grid_spec=grid_spec,
        out_shape=[jax.ShapeDtypeStruct((m, d), F32), jax.ShapeDtypeStruct((m, d), BF16)],
        compiler_params=_params(("arbitrary",)),
    )(pos, y_sorted, x, route, g, b)


def _dispatch_tables(route, n_tokens, tm):
    expert = route[:, :2].astype(I32)
    onehot = (expert[:, :, None] == jnp.arange(MOE_EXPERTS, dtype=I32)).astype(I32)
    flat = onehot.reshape(n_tokens * 2, MOE_EXPERTS)
    running = jnp.cumsum(flat, axis=0)
    counts = running[-1]
    rank = jnp.sum((running - flat) * flat, axis=1)
    padded = ((counts + tm - 1) // tm) * tm
    ends = jnp.cumsum(padded)
    starts = ends - padded
    pos = starts[expert.reshape(-1)] + rank
    n_rows = n_tokens * 2 + MOE_EXPERTS * tm
    n_tiles = n_rows // tm
    token = jnp.repeat(jnp.arange(n_tokens, dtype=I32), 2)
    row_token = jnp.zeros((n_rows,), I32).at[pos].set(token)
    tile_start = jnp.arange(n_tiles, dtype=I32) * tm
    tile_expert = jnp.minimum(jnp.searchsorted(ends, tile_start, side="right"), MOE_EXPERTS - 1).astype(I32)
    tile_valid = (tile_start < ends[-1]).astype(I32)
    pos_tiles = pos.reshape(n_tokens // tm, tm, 2).transpose(0, 2, 1).reshape(-1)
    return tile_expert, tile_valid, row_token, pos_tiles.astype(I32)


def _moe_layer(x, w_group, b_group, w_router, b_router, w_gate, w_up, w_down, ln_g, ln_b):
    n_tokens = x.shape[0]
    w_route = _pad_cols(jnp.concatenate([w_group, w_router], axis=1))
    b_route = _pad_cols(jnp.concatenate([b_group, b_router])[None, :])
    route = _route(_small_matmul(x, w_route, b_route))
    tm = min(MOE_TM, n_tokens)
    tile_expert, tile_valid, row_token, pos = _dispatch_tables(route, n_tokens, tm)
    y_sorted = _moe_ffn(x, w_gate, w_up, w_down, tile_expert, tile_valid, row_token, tm)
    return _moe_combine(pos, y_sorted, x, route, ln_g[None, :], ln_b[None, :], tm)


def kernel(x, t5_rel_bias, mla_w_in, mla_q_norm, mla_kv_norm, mla_w_q_up, mla_w_kv_up, mla_w_o, fox_w_in, fox_b_f, fox_w_o, dsa_w_in, dsa_w_o, ln_g, ln_b, moe_w_group, moe_b_group, moe_w_router, moe_b_router, moe_w_gate, moe_w_up, moe_w_down):
    batch, seq, d = x.shape
    x = x.reshape(batch * seq, d)
    x_bf = x.astype(BF16)
    for layer in range(DEPTH):
        kind = layer % N_MIXERS
        j = layer // N_MIXERS
        g0, b0 = ln_g[layer, 0], ln_b[layer, 0]
        if kind == 0:
            x, x_bf = _mla_layer(x, x_bf, mla_w_in[j], mla_q_norm[j], mla_kv_norm[j], mla_w_q_up[j],
                                 mla_w_kv_up[j], mla_w_o[j], g0, b0, batch, seq)
        elif kind == 1:
            x, x_bf = _fox_layer(x, x_bf, fox_w_in[j], fox_b_f[j], fox_w_o[j], g0, b0, batch, seq)
        else:
            x, x_bf = _dsa_layer(x, x_bf, t5_rel_bias, dsa_w_in[j], dsa_w_o[j], g0, b0, batch, seq)
        x, x_bf = _moe_layer(x, moe_w_group[layer], moe_b_group[layer], moe_w_router[layer], moe_b_router[layer],
                             moe_w_gate[layer], moe_w_up[layer], moe_w_down[layer], ln_g[layer, 1], ln_b[layer, 1])
    return x.reshape(batch, seq, d)
```

```python
import functools
import math

import numpy as np
import jax
import jax.numpy as jnp
from jax import lax
from jax.experimental import pallas as pl
from jax.experimental.pallas import tpu as pltpu

F32 = jnp.float32
BF16 = jnp.bfloat16
I32 = jnp.int32

D_MODEL = 2048
DEPTH = 4
CHUNK = 64
N_MIXERS = 3
DEEPNORM_ALPHA = (2.0 * DEPTH) ** 0.25

HEADS = 16
HEAD_DIM = 128
MLA_NOPE_DIM = 128
MLA_ROPE_DIM = 64
MLA_Q_RANK = 512
MLA_KV_RANK = 512
MLA_QK_PAD = 256
ROPE_THETA = 10000.0

IDX_HEADS = 16
IDX_DIM = 64
IDX_TOPK_MAX = 256

T5_BUCKETS = 32
T5_MAX_DISTANCE = 128
T5_FAR_BUCKET = T5_BUCKETS // 2 - 1
T5_BAND_BACK = 128

MOE_GROUPS = 4
MOE_EXPERTS_PER_GROUP = 8
MOE_EXPERTS = MOE_GROUPS * MOE_EXPERTS_PER_GROUP
MOE_D_FF = 512

LN_EPS = 1e-5
RMS_EPS = 1e-6
NEG_INF = -1e30

LANES = 128
VMEM_LIMIT = 56 * 1024 * 1024

ATTN_TQ = 256
MOE_TM = 256
COMBINE_TM = 256


def _params(semantics, vmem=VMEM_LIMIT):
    return pltpu.CompilerParams(dimension_semantics=semantics, vmem_limit_bytes=vmem)


def _mm_kernel(a_ref, w_ref, o_ref, *, scale, scaled_blocks):
    acc = jnp.dot(a_ref[...], w_ref[...], preferred_element_type=F32)
    if scaled_blocks:
        acc = acc * jnp.where(pl.program_id(0) < scaled_blocks, scale, 1.0).astype(F32)
    o_ref[...] = acc.astype(o_ref.dtype)


def _matmul(a, w, *, n_out, tn, tm=1024, col_block_offset=0, scale=1.0, scaled_blocks=0, out_dtype=BF16):
    m, k = a.shape
    tm = min(tm, m)
    kern = functools.partial(_mm_kernel, scale=scale, scaled_blocks=scaled_blocks)
    return pl.pallas_call(
        kern,
        name="proj_mm",
        grid=(n_out // tn, m // tm),
        in_specs=[pl.BlockSpec((tm, k), lambda j, i: (i, 0)),
                  pl.BlockSpec((k, tn), lambda j, i: (0, j + col_block_offset))],
        out_specs=pl.BlockSpec((tm, tn), lambda j, i: (i, j)),
        out_shape=jax.ShapeDtypeStruct((m, n_out), out_dtype),
        compiler_params=_params(("arbitrary", "arbitrary")),
    )(a, w)


def _split3(v):
    hi = v.astype(BF16)
    r1 = v - hi.astype(F32)
    mid = r1.astype(BF16)
    lo = (r1 - mid.astype(F32)).astype(BF16)
    return hi, mid, lo


def _dot_precise(x, w):
    xh = x.astype(BF16)
    xm = (x - xh.astype(F32)).astype(BF16)
    wh = w.astype(BF16)
    wm = (w - wh.astype(F32)).astype(BF16)
    d = lambda p, q: jnp.dot(p, q, preferred_element_type=F32)
    return (d(xm, wh) + d(xh, wm)) + d(xh, wh)


def _small_mm_kernel(x_ref, w_ref, b_ref, o_ref):
    o_ref[...] = _dot_precise(x_ref[...], w_ref[...]) + b_ref[...]


def _small_matmul(x, w, b, tm=512):
    m, k = x.shape
    tm = min(tm, m)
    return pl.pallas_call(
        _small_mm_kernel,
        name="small_mm",
        grid=(m // tm,),
        in_specs=[pl.BlockSpec((tm, k), lambda i: (i, 0)),
                  pl.BlockSpec((k, LANES), lambda i: (0, 0)),
                  pl.BlockSpec((1, LANES), lambda i: (0, 0))],
        out_specs=pl.BlockSpec((tm, LANES), lambda i: (i, 0)),
        out_shape=jax.ShapeDtypeStruct((m, LANES), F32),
        compiler_params=_params(("arbitrary",)),
    )(x, w, b)


def _pad_cols(w, width=LANES):
    return jnp.pad(w, ((0, 0), (0, width - w.shape[1])))


def _layernorm_rows(z, g, b):
    mu = jnp.mean(z, axis=-1, keepdims=True)
    zc = z - mu
    var = jnp.mean(zc * zc, axis=-1, keepdims=True)
    return zc * lax.rsqrt(var + LN_EPS) * g + b


def _mm_res_ln_kernel(a_ref, w_ref, x_ref, g_ref, b_ref, o_ref, obf_ref):
    y = jnp.dot(a_ref[...], w_ref[...], preferred_element_type=F32)
    out = _layernorm_rows(DEEPNORM_ALPHA * x_ref[...] + y, g_ref[...], b_ref[...])
    o_ref[...] = out
    obf_ref[...] = out.astype(BF16)


def _mm_res_ln(a, w, x, g, b, tm=512):
    m, k = a.shape
    d = w.shape[1]
    tm = min(tm, m)
    return pl.pallas_call(
        _mm_res_ln_kernel,
        name="out_proj_res_ln",
        grid=(m // tm,),
        in_specs=[pl.BlockSpec((tm, k), lambda i: (i, 0)),
                  pl.BlockSpec((k, d), lambda i: (0, 0)),
                  pl.BlockSpec((tm, d), lambda i: (i, 0)),
                  pl.BlockSpec((1, d), lambda i: (0, 0)),
                  pl.BlockSpec((1, d), lambda i: (0, 0))],
        out_specs=[pl.BlockSpec((tm, d), lambda i: (i, 0)),
                   pl.BlockSpec((tm, d), lambda i: (i, 0))],
        out_shape=[jax.ShapeDtypeStruct((m, d), F32), jax.ShapeDtypeStruct((m, d), BF16)],
        compiler_params=_params(("arbitrary",)),
    )(a, w, x, g, b)


def _rope_tables(seq):
    half = MLA_ROPE_DIM // 2
    inv_freq = ROPE_THETA ** (-jnp.arange(half, dtype=F32) / half)
    ang = jnp.arange(seq, dtype=I32).astype(F32)[:, None] * inv_freq[None, :]
    cos, sin = jnp.cos(ang), jnp.sin(ang)
    z = jnp.zeros_like(cos)
    cos_t = jnp.concatenate([cos, cos, z, z], axis=1)
    sin_a = jnp.concatenate([-sin, z, z, z], axis=1)
    sin_b = jnp.concatenate([z, sin, z, z], axis=1)
    return cos_t, sin_a, sin_b


def _rope_lanes(r, cos_t, sin_a, sin_b):
    from_right = pltpu.roll(r, LANES - MLA_ROPE_DIM // 2, 1)
    from_left = pltpu.roll(r, MLA_ROPE_DIM // 2, 1)
    return r * cos_t + from_right * sin_a + from_left * sin_b


def _rms_rows(v, g):
    return v * lax.rsqrt(jnp.mean(v * v, axis=-1, keepdims=True) + RMS_EPS) * g


def _mla_in_kernel(x_ref, w_ref, qn_ref, kvn_ref, cos_ref, sina_ref, sinb_ref, cq_ref, ckv_ref, kr_ref):
    h = jnp.dot(x_ref[...], w_ref[...], preferred_element_type=F32)
    cq_ref[...] = _rms_rows(h[:, :MLA_Q_RANK], qn_ref[...]).astype(BF16)
    ckv_ref[...] = _rms_rows(h[:, MLA_Q_RANK:MLA_Q_RANK + MLA_KV_RANK], kvn_ref[...]).astype(BF16)
    r = h[:, MLA_Q_RANK + MLA_KV_RANK:]
    kr_ref[...] = _rope_lanes(r, cos_ref[...], sina_ref[...], sinb_ref[...]).astype(BF16)


def _mla_in(x_bf, w_in_pad, q_norm, kv_norm, tables, seq, tm=512):
    m, k = x_bf.shape
    tm = min(tm, seq)
    nw = w_in_pad.shape[1]
    per_seq = seq // tm
    tab_spec = pl.BlockSpec((tm, LANES), lambda i: (i % per_seq, 0))
    return pl.pallas_call(
        _mla_in_kernel,
        name="mla_in",
        grid=(m // tm,),
        in_specs=[pl.BlockSpec((tm, k), lambda i: (i, 0)),
                  pl.BlockSpec((k, nw), lambda i: (0, 0)),
                  pl.BlockSpec((1, MLA_Q_RANK), lambda i: (0, 0)),
                  pl.BlockSpec((1, MLA_KV_RANK), lambda i: (0, 0)),
                  tab_spec, tab_spec, tab_spec],
        out_specs=[pl.BlockSpec((tm, MLA_Q_RANK), lambda i: (i, 0)),
                   pl.BlockSpec((tm, MLA_KV_RANK), lambda i: (i, 0)),
                   pl.BlockSpec((tm, LANES), lambda i: (i, 0))],
        out_shape=[jax.ShapeDtypeStruct((m, MLA_Q_RANK), BF16),
                   jax.ShapeDtypeStruct((m, MLA_KV_RANK), BF16),
                   jax.ShapeDtypeStruct((m, LANES), BF16)],
        compiler_params=_params(("arbitrary",)),
    )(x_bf, w_in_pad, q_norm, kv_norm, *tables)


def _softmax_pv(s, v):
    m = jnp.max(s, axis=1, keepdims=True)
    p = jnp.exp(s - m)
    l = jnp.sum(p, axis=1, keepdims=True)
    o = jnp.dot(p.astype(BF16), v, preferred_element_type=F32)
    return o / l


def _nt_dot(q, k):
    return lax.dot_general(q, k, (((1,), (1,)), ((), ())), preferred_element_type=F32)


def _mla_attn_kernel(q_ref, kn_ref, v_ref, kr_ref, cos_ref, sina_ref, sinb_ref, o_ref, qc_ref, kc_ref, *, tq):
    seq = q_ref.shape[0]
    r = q_ref[:, MLA_NOPE_DIM:].astype(F32)
    qc_ref[:, :MLA_NOPE_DIM] = q_ref[:, :MLA_NOPE_DIM]
    qc_ref[:, MLA_NOPE_DIM:] = _rope_lanes(r, cos_ref[...], sina_ref[...], sinb_ref[...]).astype(BF16)
    kc_ref[:, :MLA_NOPE_DIM] = kn_ref[...]
    kc_ref[:, MLA_NOPE_DIM:] = kr_ref[...]
    for i in range(seq // tq):
        kv = (i + 1) * tq
        s = _nt_dot(qc_ref[i * tq:kv, :], kc_ref[:kv, :])
        row = lax.broadcasted_iota(I32, (tq, kv), 0) + i * tq
        col = lax.broadcasted_iota(I32, (tq, kv), 1)
        s = jnp.where((col // CHUNK) <= (row // CHUNK), s, NEG_INF)
        o_ref[i * tq:kv, :] = _softmax_pv(s, v_ref[:kv, :]).astype(o_ref.dtype)


def _mla_attn(q, kv, kr, tables, batch, seq):
    tq = min(ATTN_TQ, seq)
    kern = functools.partial(_mla_attn_kernel, tq=tq)
    tab_spec = pl.BlockSpec((seq, LANES), lambda b, h: (0, 0))
    return pl.pallas_call(
        kern,
        name="mla_attn",
        grid=(batch, HEADS),
        in_specs=[pl.BlockSpec((seq, MLA_QK_PAD), lambda b, h: (b, h)),
                  pl.BlockSpec((seq, HEAD_DIM), lambda b, h: (b, 2 * h)),
                  pl.BlockSpec((seq, HEAD_DIM), lambda b, h: (b, 2 * h + 1)),
                  pl.BlockSpec((seq, LANES), lambda b, h: (b, 0)),
                  tab_spec, tab_spec, tab_spec],
        out_specs=pl.BlockSpec((seq, HEAD_DIM), lambda b, h: (b, h)),
        out_shape=jax.ShapeDtypeStruct((batch * seq, HEADS * HEAD_DIM), BF16),
        scratch_shapes=[pltpu.VMEM((seq, MLA_QK_PAD), BF16), pltpu.VMEM((seq, MLA_QK_PAD), BF16)],
        compiler_params=_params(("arbitrary", "arbitrary")),
    )(q, kv, kv, kr, *tables)


def _mla_q_up_layout(w_q_up):
    rank = w_q_up.shape[0]
    w = w_q_up.reshape(rank, HEADS, MLA_NOPE_DIM + MLA_ROPE_DIM)
    w = jnp.pad(w, ((0, 0), (0, 0), (0, MLA_QK_PAD - MLA_NOPE_DIM - MLA_ROPE_DIM)))
    return w.reshape(rank, HEADS * MLA_QK_PAD)


def _mla_layer(x, x_bf, w_in, q_norm, kv_norm, w_q_up, w_kv_up, w_o, ln_g, ln_b, batch, seq):
    tables = _rope_tables(seq)
    w_in_pad = jnp.pad(w_in, ((0, 0), (0, MLA_Q_RANK + MLA_KV_RANK + LANES - w_in.shape[1]))).astype(BF16)
    cq, ckv, kr = _mla_in(x_bf, w_in_pad, q_norm[None, :], kv_norm[None, :], tables, seq)
    scale = (MLA_NOPE_DIM + MLA_ROPE_DIM) ** -0.5
    nq = HEADS * MLA_QK_PAD
    q = _matmul(cq, _mla_q_up_layout(w_q_up).astype(BF16), n_out=nq, tn=1024, scale=scale, scaled_blocks=nq // 1024)
    kv = _matmul(ckv, w_kv_up.astype(BF16), n_out=w_kv_up.shape[1], tn=1024)
    o = _mla_attn(q, kv, kr, tables, batch, seq)
    return _mm_res_ln(o, w_o.astype(BF16), x, ln_g[None, :], ln_b[None, :])


def _log_sigmoid(z):
    return jnp.minimum(z, 0.0) - jnp.log1p(jnp.exp(-jnp.abs(z)))


def _fox_gate_kernel(f_ref, c_ref, ct_ref, *, blk):
    seq = f_ref.shape[0]
    tri = (lax.broadcasted_iota(I32, (blk, blk), 1) <= lax.broadcasted_iota(I32, (blk, blk), 0)).astype(BF16)
    carry = jnp.zeros((1, LANES), F32)
    for j in range(seq // blk):
        lf = _log_sigmoid(f_ref[j * blk:(j + 1) * blk, :])
        hi, mid, lo = _split3(lf)
        d = lambda p: jnp.dot(tri, p, preferred_element_type=F32)
        c = (d(lo) + d(mid)) + d(hi) + carry
        c_ref[j * blk:(j + 1) * blk, :] = c
        ct_ref[:, j * blk:(j + 1) * blk] = c.T[:HEADS, :]
        carry = c[blk - 1:blk, :]


def _fox_gate(f_logit, batch, seq):
    blk = min(256, seq)
    return pl.pallas_call(
        functools.partial(_fox_gate_kernel, blk=blk),
        name="fox_gate",
        grid=(batch,),
        in_specs=[pl.BlockSpec((seq, LANES), lambda b: (b, 0))],
        out_specs=[pl.BlockSpec((seq, LANES), lambda b: (b, 0)),
                   pl.BlockSpec((None, HEADS, seq), lambda b: (b, 0, 0))],
        out_shape=[jax.ShapeDtypeStruct((batch * seq, LANES), F32),
                   jax.ShapeDtypeStruct((batch, HEADS, seq), F32)],
        compiler_params=_params(("arbitrary",)),
    )(f_logit)


def _fox_attn_kernel(q_ref, k_ref, v_ref, ccol_ref, crow_ref, o_ref, *, tq):
    seq = q_ref.shape[0]
    h = pl.program_id(1)
    lane = lax.broadcasted_iota(I32, (seq, LANES), 1)
    cq = jnp.sum(jnp.where(lane == h, ccol_ref[...], 0.0), axis=1, keepdims=True)
    for i in range(seq // tq):
        kv = (i + 1) * tq
        s = _nt_dot(q_ref[i * tq:kv, :], k_ref[:kv, :])
        s = (s + cq[i * tq:kv, :]) - crow_ref[:, :kv]
        row = lax.broadcasted_iota(I32, (tq, kv), 0) + i * tq
        col = lax.broadcasted_iota(I32, (tq, kv), 1)
        s = jnp.where(col <= row, s, NEG_INF)
        o_ref[i * tq:kv, :] = _softmax_pv(s, v_ref[:kv, :]).astype(o_ref.dtype)


def _fox_attn(qkv, c_col, c_row, batch, seq):
    tq = min(ATTN_TQ, seq)
    return pl.pallas_call(
        functools.partial(_fox_attn_kernel, tq=tq),
        name="fox_attn",
        grid=(batch, HEADS),
        in_specs=[pl.BlockSpec((seq, HEAD_DIM), lambda b, h: (b, h)),
                  pl.BlockSpec((seq, HEAD_DIM), lambda b, h: (b, HEADS + h)),
                  pl.BlockSpec((seq, HEAD_DIM), lambda b, h: (b, 2 * HEADS + h)),
                  pl.BlockSpec((seq, LANES), lambda b, h: (b, 0)),
                  pl.BlockSpec((None, None, 1, seq), lambda b, h: (b, h, 0, 0))],
        out_specs=pl.BlockSpec((seq, HEAD_DIM), lambda b, h: (b, h)),
        out_shape=jax.ShapeDtypeStruct((batch * seq, HEADS * HEAD_DIM), BF16),
        compiler_params=_params(("arbitrary", "arbitrary")),
    )(qkv, qkv, qkv, c_col, c_row)


def _fox_layer(x, x_bf, w_in, b_f, w_o, ln_g, ln_b, batch, seq):
    hd = HEADS * HEAD_DIM
    scale = HEAD_DIM ** -0.5
    qkv = _matmul(x_bf, w_in.astype(BF16), n_out=3 * hd, tn=1024, scale=scale, scaled_blocks=hd // 1024)
    f_logit = _small_matmul(x, _pad_cols(w_in[:, 3 * hd:]), _pad_cols(b_f[None, :]))
    c_col, c_t = _fox_gate(f_logit, batch, seq)
    o = _fox_attn(qkv, c_col, c_t[:, :, None, :], batch, seq)
    return _mm_res_ln(o, w_o.astype(BF16), x, ln_g[None, :], ln_b[None, :])


def _t5_bucket_table(tq):
    a = np.arange(tq, dtype=np.int64)[:, None]
    b = np.arange(tq + T5_BAND_BACK, dtype=np.int64)[None, :]
    rel = (b - T5_BAND_BACK) - a
    nb = T5_BUCKETS // 2
    max_exact = nb // 2
    ret = np.where(rel > 0, nb, 0)
    n = np.abs(rel)
    nf = np.maximum(n, 1).astype(np.float32)
    large = max_exact + (np.log(nf / np.float32(max_exact)) / np.float32(math.log(T5_MAX_DISTANCE / max_exact))
                         * np.float32(nb - max_exact)).astype(np.int32)
    large = np.minimum(large, nb - 1)
    return (ret + np.where(n < max_exact, n, large)).astype(np.int32)


def _t5_band_kernel(t5_ref, bucket_ref, o_ref):
    bucket = bucket_ref[...]

    def per_head(h, carry):
        acc = jnp.zeros(bucket.shape, F32)
        for b in range(T5_BUCKETS):
            acc = jnp.where(bucket == b, t5_ref[b, h], acc)
        o_ref[h] = acc - t5_ref[T5_FAR_BUCKET, h]
        return carry

    lax.fori_loop(0, HEADS, per_head, 0)


def _t5_band(t5_bias, tq):
    bucket = jnp.asarray(_t5_bucket_table(tq))
    width = tq + T5_BAND_BACK
    return pl.pallas_call(
        _t5_band_kernel,
        name="t5_band",
        grid=(1,),
        in_specs=[pl.BlockSpec(memory_space=pltpu.SMEM),
                  pl.BlockSpec((tq, width), lambda i: (0, 0))],
        out_specs=pl.BlockSpec((HEADS, tq, width), lambda i: (0, 0, 0)),
        out_shape=jax.ShapeDtypeStruct((HEADS, tq, width), F32),
        compiler_params=_params(("arbitrary",)),
    )(t5_bias, bucket)


def _sortable_key(v):
    bits = pltpu.bitcast(v, I32)
    return bits ^ ((bits >> 31) & 0x7FFFFFFF)


def _kth_largest_key(key, k):
    rows = key.shape[0]

    def count_ge(t):
        return jnp.sum((key >= t).astype(I32), axis=1, keepdims=True)

    t0 = jnp.where(count_ge(jnp.zeros((rows, 1), I32)) >= k, 0, jnp.iinfo(jnp.int32).min).astype(I32)

    def body(it, t):
        cand = t | (jnp.int32(1) << (30 - it))
        return jnp.where(count_ge(cand) >= k, cand, t)

    return lax.fori_loop(0, 31, body, t0)


def _dsa_index_kernel(qi_ref, tail_ref, wtail_ref, o_ref, *, tq, topk):
    seq = tail_ref.shape[0]
    i = pl.program_id(1)
    k_idx = tail_ref[:, :IDX_DIM].astype(BF16)
    w = wtail_ref[:, IDX_DIM:IDX_DIM + IDX_HEADS] * ((IDX_HEADS * IDX_DIM) ** -0.5)
    score = jnp.zeros((tq, seq), F32)
    for h in range(IDX_HEADS):
        logits = _nt_dot(qi_ref[:, h * IDX_DIM:(h + 1) * IDX_DIM], k_idx)
        score = score + jnp.maximum(logits, 0.0) * w[:, h:h + 1]
    row = lax.broadcasted_iota(I32, (tq, seq), 0) + i * tq
    col = lax.broadcasted_iota(I32, (tq, seq), 1)
    admissible = (col // CHUNK) <= (row // CHUNK)
    key = _sortable_key(jnp.where(admissible, score, NEG_INF))
    thr = _kth_largest_key(key, topk)
    o_ref[...] = jnp.where(admissible & (key >= thr), 0.0, NEG_INF).astype(o_ref.dtype)


def _dsa_index(q_idx, tail, batch, seq, topk):
    tq = min(ATTN_TQ, seq)
    per_seq = seq // tq
    return pl.pallas_call(
        functools.partial(_dsa_index_kernel, tq=tq, topk=topk),
        name="dsa_index",
        grid=(batch, per_seq),
        in_specs=[pl.BlockSpec((tq, IDX_HEADS * IDX_DIM), lambda b, i: (b * per_seq + i, 0)),
                  pl.BlockSpec((seq, LANES), lambda b, i: (b, 0)),
                  pl.BlockSpec((tq, LANES), lambda b, i: (b * per_seq + i, 0))],
        out_specs=pl.BlockSpec((tq, seq), lambda b, i: (b * per_seq + i, 0)),
        out_shape=jax.ShapeDtypeStruct((batch * seq, seq), BF16),
        compiler_params=_params(("arbitrary", "arbitrary")),
    )(q_idx, tail, tail)


def _dsa_attn_kernel(q_ref, k_ref, v_ref, sel_ref, band_ref, o_ref, *, tq):
    seq = k_ref.shape[0]
    width = band_ref.shape[1]
    for i in range(seq // tq):
        kv = (i + 1) * tq
        near = min(width, kv)
        far = kv - near
        dead = sel_ref[i * tq:kv, :kv] < 0.0
        s = _nt_dot(q_ref[i * tq:kv, :], k_ref[:kv, :])
        s_near = jnp.where(dead[:, far:], NEG_INF, s[:, far:] + band_ref[:, width - near:])
        m = jnp.max(s_near, axis=1, keepdims=True)
        if far:
            s_far = jnp.where(dead[:, :far], NEG_INF, s[:, :far])
            m = jnp.maximum(m, jnp.max(s_far, axis=1, keepdims=True))
        p = jnp.exp(s_near - m)
        l = jnp.sum(p, axis=1, keepdims=True)
        o = jnp.dot(p.astype(BF16), v_ref[far:kv, :], preferred_element_type=F32)
        if far:
            p = jnp.exp(s_far - m)
            l = l + jnp.sum(p, axis=1, keepdims=True)
            o = o + jnp.dot(p.astype(BF16), v_ref[:far, :], preferred_element_type=F32)
        o_ref[i * tq:kv, :] = (o / l).astype(o_ref.dtype)


def _dsa_attn(qkv, sel, band, batch, seq):
    tq = min(ATTN_TQ, seq)
    width = band.shape[2]
    return pl.pallas_call(
        functools.partial(_dsa_attn_kernel, tq=tq),
        name="dsa_attn",
        grid=(batch, HEADS),
        in_specs=[pl.BlockSpec((seq, HEAD_DIM), lambda b, h: (b, h)),
                  pl.BlockSpec((seq, HEAD_DIM), lambda b, h: (b, HEADS)),
                  pl.BlockSpec((seq, HEAD_DIM), lambda b, h: (b, HEADS + 1)),
                  pl.BlockSpec((seq, seq), lambda b, h: (b, 0)),
                  pl.BlockSpec((None, tq, width), lambda b, h: (h, 0, 0))],
        out_specs=pl.BlockSpec((seq, HEAD_DIM), lambda b, h: (b, h)),
        out_shape=jax.ShapeDtypeStruct((batch * seq, HEADS * HEAD_DIM), BF16),
        compiler_params=_params(("arbitrary", "arbitrary")),
    )(qkv, qkv, qkv, sel, band)


def _dsa_layer(x, x_bf, t5_bias, w_in, w_o, ln_g, ln_b, batch, seq):
    hd = HEADS * HEAD_DIM
    n_qkv = hd + 2 * HEAD_DIM
    n_qi = IDX_HEADS * IDX_DIM
    scale = HEAD_DIM ** -0.5
    w_bf = w_in.astype(BF16)
    qkv = _matmul(x_bf, w_bf, n_out=n_qkv, tn=256, scale=scale, scaled_blocks=hd // 256)
    q_idx = _matmul(x_bf, w_bf, n_out=n_qi, tn=256, col_block_offset=n_qkv // 256)
    tail = _small_matmul(x, _pad_cols(w_in[:, n_qkv + n_qi:]), jnp.zeros((1, LANES), F32))
    topk = min(IDX_TOPK_MAX, seq // 4)
    sel = _dsa_index(q_idx, tail, batch, seq, topk)
    band = _t5_band(t5_bias, min(ATTN_TQ, seq))
    o = _dsa_attn(qkv, sel, band, batch, seq)
    return _mm_res_ln(o, w_o.astype(BF16), x, ln_g[None, :], ln_b[None, :])


def _route_kernel(l_ref, o_ref):
    logit = l_ref[...]
    lane = lax.broadcasted_iota(I32, logit.shape, 1)
    big = jnp.int32(LANES)

    def first_argmax(mask):
        top = jnp.max(jnp.where(mask, logit, -jnp.inf), axis=1, keepdims=True)
        idx = jnp.min(jnp.where(mask & (logit == top), lane, big), axis=1, keepdims=True)
        return top, idx

    is_group = lane < MOE_GROUPS
    g_top, g_sel = first_argmax(is_group)
    p_g = 1.0 / jnp.sum(jnp.where(is_group, jnp.exp(logit - g_top), 0.0), axis=1, keepdims=True)
    lo = MOE_GROUPS + g_sel * MOE_EXPERTS_PER_GROUP
    in_group = (lane >= lo) & (lane < lo + MOE_EXPERTS_PER_GROUP)
    v1, i1 = first_argmax(in_group)
    v2, i2 = first_argmax(in_group & (lane != i1))
    e2 = jnp.exp(v2 - v1)
    w1 = 1.0 / (1.0 + e2)
    w2 = e2 / (1.0 + e2)
    out = jnp.where(lane == 0, (i1 - MOE_GROUPS).astype(F32), 0.0)
    out = jnp.where(lane == 1, (i2 - MOE_GROUPS).astype(F32), out)
    out = jnp.where(lane == 2, w1 * p_g, out)
    out = jnp.where(lane == 3, w2 * p_g, out)
    o_ref[...] = out


def _route(logits, tm=512):
    m = logits.shape[0]
    tm = min(tm, m)
    return pl.pallas_call(
        _route_kernel,
        name="moe_route",
        grid=(m // tm,),
        in_specs=[pl.BlockSpec((tm, LANES), lambda i: (i, 0))],
        out_specs=pl.BlockSpec((tm, LANES), lambda i: (i, 0)),
        out_shape=jax.ShapeDtypeStruct((m, LANES), F32),
        compiler_params=_params(("arbitrary",)),
    )(logits)


def _gather_rows(idx_ref, base, src_hbm, dst, sem, n_rows):
    def body(r, carry):
        pltpu.make_async_copy(src_hbm.at[pl.ds(idx_ref[base + r], 1)], dst.at[pl.ds(r, 1)], sem).start()
        return carry
    lax.fori_loop(0, n_rows, body, 0, unroll=8)


def _wait_rows(src_hbm, dst, sem, n_rows):
    pltpu.make_async_copy(src_hbm.at[pl.ds(0, n_rows)], dst, sem).wait()


def _moe_ffn_kernel(tile_expert_ref, tile_valid_ref, row_token_ref, x_hbm, wg_ref, wu_ref, wd_ref, o_ref,
                    xbuf, wg_bf, wu_bf, wd_bf, sem, *, tm):
    j = pl.program_id(0)
    n_tiles = pl.num_programs(0)
    slot = j % 2

    @pl.when((j == 0) & (tile_valid_ref[0] == 1))
    def _():
        _gather_rows(row_token_ref, 0, x_hbm, xbuf.at[0], sem.at[0], tm)

    @pl.when((j + 1 < n_tiles) & (tile_valid_ref[jnp.minimum(j + 1, n_tiles - 1)] == 1))
    def _():
        _gather_rows(row_token_ref, (j + 1) * tm, x_hbm, xbuf.at[1 - slot], sem.at[1 - slot], tm)

    new_expert = (j == 0) | (tile_expert_ref[j] != tile_expert_ref[jnp.maximum(j - 1, 0)])

    @pl.when(new_expert & (tile_valid_ref[j] == 1))
    def _():
        wg_bf[...] = wg_ref[...].astype(BF16)
        wu_bf[...] = wu_ref[...].astype(BF16)
        wd_bf[...] = wd_ref[...].astype(BF16)

    @pl.when(tile_valid_ref[j] == 1)
    def _():
        _wait_rows(x_hbm, xbuf.at[slot], sem.at[slot], tm)
        xb = xbuf[slot].astype(BF16)
        hg = jnp.dot(xb, wg_bf[...], preferred_element_type=F32)
        hu = jnp.dot(xb, wu_bf[...], preferred_element_type=F32)
        act = (hg * (1.0 / (1.0 + jnp.exp(-hg))) * hu).astype(BF16)
        o_ref[...] = jnp.dot(act, wd_bf[...], preferred_element_type=F32)

    @pl.when(tile_valid_ref[j] == 0)
    def _():
        o_ref[...] = jnp.zeros(o_ref.shape, o_ref.dtype)


def _moe_ffn(x, w_gate, w_up, w_down, layer, tile_expert, tile_valid, row_token, tm):
    n_tiles = tile_expert.shape[0]
    d = x.shape[1]
    f = w_gate.shape[3]
    grid_spec = pltpu.PrefetchScalarGridSpec(
        num_scalar_prefetch=3,
        grid=(n_tiles,),
        in_specs=[pl.BlockSpec(memory_space=pl.ANY),
                  pl.BlockSpec((None, None, d, f), lambda j, te, tv, rt: (layer, te[j], 0, 0)),
                  pl.BlockSpec((None, None, d, f), lambda j, te, tv, rt: (layer, te[j], 0, 0)),
                  pl.BlockSpec((None, None, f, d), lambda j, te, tv, rt: (layer, te[j], 0, 0))],
        out_specs=pl.BlockSpec((tm, d), lambda j, te, tv, rt: (j, 0)),
        scratch_shapes=[pltpu.VMEM((2, tm, d), F32),
                        pltpu.VMEM((d, f), BF16), pltpu.VMEM((d, f), BF16), pltpu.VMEM((f, d), BF16),
                        pltpu.SemaphoreType.DMA((2,))],
    )
    return pl.pallas_call(
        functools.partial(_moe_ffn_kernel, tm=tm),
        name="moe_ffn",
        grid_spec=grid_spec,
        out_shape=jax.ShapeDtypeStruct((n_tiles * tm, d), F32),
        compiler_params=_params(("arbitrary",)),
    )(tile_expert, tile_valid, row_token, x, w_gate, w_up, w_down)


def _moe_combine_kernel(pos_ref, y_hbm, x_ref, r_ref, g_ref, b_ref, o_ref, obf_ref, ybuf, sem, *, tm):
    i = pl.program_id(0)
    n = pl.num_programs(0)
    slot = i % 2
    rows = 2 * tm

    @pl.when(i == 0)
    def _():
        _gather_rows(pos_ref, 0, y_hbm, ybuf.at[0], sem.at[0], rows)

    @pl.when(i + 1 < n)
    def _():
        _gather_rows(pos_ref, (i + 1) * rows, y_hbm, ybuf.at[1 - slot], sem.at[1 - slot], rows)

    _wait_rows(y_hbm, ybuf.at[slot], sem.at[slot], rows)
    g1 = r_ref[:, 2:3]
    g2 = r_ref[:, 3:4]
    y = g1 * ybuf[slot, :tm, :] + g2 * ybuf[slot, tm:, :]
    out = _layernorm_rows(DEEPNORM_ALPHA * x_ref[...] + y, g_ref[...], b_ref[...])
    o_ref[...] = out
    obf_ref[...] = out.astype(BF16)


def _moe_combine(pos, y_sorted, x, route, g, b, tm):
    m, d = x.shape
    grid_spec = pltpu.PrefetchScalarGridSpec(
        num_scalar_prefetch=1,
        grid=(m // tm,),
        in_specs=[pl.BlockSpec(memory_space=pl.ANY),
                  pl.BlockSpec((tm, d), lambda i, p: (i, 0)),
                  pl.BlockSpec((tm, LANES), lambda i, p: (i, 0)),
                  pl.BlockSpec((1, d), lambda i, p: (0, 0)),
                  pl.BlockSpec((1, d), lambda i, p: (0, 0))],
        out_specs=[pl.BlockSpec((tm, d), lambda i, p: (i, 0)),
                   pl.BlockSpec((tm, d), lambda i, p: (i, 0))],
        scratch_shapes=[pltpu.VMEM((2, 2 * tm, d), F32), pltpu.SemaphoreType.DMA((2,))],
    )
    return pl.pallas_call(
        functools.partial(_moe_combine_kernel, tm=tm),
        name="moe_combine",
        grid_spec=grid_spec,
        out_shape=[jax.ShapeDtypeStruct((m, d), F32), jax.ShapeDtypeStruct((m, d), BF16)],
        compiler_params=_params(("arbitrary",)),
    )(pos, y_sorted, x, route, g, b)


def _dispatch_tables(route, n_tokens, tm):
    expert = route[:, :2].astype(I32)
    onehot = (expert[:, :, None] == jnp.arange(MOE_EXPERTS, dtype=I32)).astype(I32)
    flat = onehot.reshape(n_tokens * 2, MOE_EXPERTS)
    running = jnp.cumsum(flat, axis=0)
    counts = running[-1]
    rank = jnp.sum((running - flat) * flat, axis=1)
    padded = ((counts + tm - 1) // tm) * tm
    ends = jnp.cumsum(padded)
    starts = ends - padded
    pos = starts[expert.reshape(-1)] + rank
    n_rows = n_tokens * 2 + MOE_EXPERTS * tm
    n_tiles = n_rows // tm
    token = jnp.repeat(jnp.arange(n_tokens, dtype=I32), 2)
    row_token = jnp.zeros((n_rows,), I32).at[pos].set(token)
    tile_start = jnp.arange(n_tiles, dtype=I32) * tm
    tile_expert = jnp.minimum(jnp.searchsorted(ends, tile_start, side="right"), MOE_EXPERTS - 1).astype(I32)
    tile_valid = (tile_start < ends[-1]).astype(I32)
    pos_tiles = pos.reshape(n_tokens // tm, tm, 2).transpose(0, 2, 1).reshape(-1)
    return tile_expert, tile_valid, row_token, pos_tiles.astype(I32)


def _moe_layer(x, w_group, b_group, w_router, b_router, w_gate, w_up, w_down, layer, ln_g, ln_b):
    n_tokens = x.shape[0]
    w_route = _pad_cols(jnp.concatenate([w_group, w_router], axis=1))
    b_route = _pad_cols(jnp.concatenate([b_group, b_router])[None, :])
    route = _route(_small_matmul(x, w_route, b_route))
    tm = min(MOE_TM, n_tokens)
    tile_expert, tile_valid, row_token, pos = _dispatch_tables(route, n_tokens, tm)
    y_sorted = _moe_ffn(x, w_gate, w_up, w_down, layer, tile_expert, tile_valid, row_token, tm)
    return _moe_combine(pos, y_sorted, x, route, ln_g[None, :], ln_b[None, :], tm)


def kernel(x, t5_rel_bias, mla_w_in, mla_q_norm, mla_kv_norm, mla_w_q_up, mla_w_kv_up, mla_w_o, fox_w_in, fox_b_f, fox_w_o, dsa_w_in, dsa_w_o, ln_g, ln_b, moe_w_group, moe_b_group, moe_w_router, moe_b_router, moe_w_gate, moe_w_up, moe_w_down):
    batch, seq, d = x.shape
    x = x.reshape(batch * seq, d)
    x_bf = x.astype(BF16)
    for layer in range(DEPTH):
        kind = layer % N_MIXERS
        j = layer // N_MIXERS
        g0, b0 = ln_g[layer, 0], ln_b[layer, 0]
        if kind == 0:
            x, x_bf = _mla_layer(x, x_bf, mla_w_in[j], mla_q_norm[j], mla_kv_norm[j], mla_w_q_up[j],
                                 mla_w_kv_up[j], mla_w_o[j], g0, b0, batch, seq)
        elif kind == 1:
            x, x_bf = _fox_layer(x, x_bf, fox_w_in[j], fox_b_f[j], fox_w_o[j], g0, b0, batch, seq)
        else:
            x, x_bf = _dsa_layer(x, x_bf, t5_rel_bias, dsa_w_in[j], dsa_w_o[j], g0, b0, batch, seq)
        x, x_bf = _moe_layer(x, moe_w_group[layer], moe_b_group[layer], moe_w_router[layer], moe_b_router[layer],
                             moe_w_gate, moe_w_up, moe_w_down, layer, ln_g[layer, 1], ln_b[layer, 1])
    return x.reshape(batch, seq, d)
```

```python
import functools
import math

import numpy as np
import jax
import jax.numpy as jnp
from jax import lax
from jax.experimental import pallas as pl
from jax.experimental.pallas import tpu as pltpu

F32 = jnp.float32
BF16 = jnp.bfloat16
I32 = jnp.int32

D_MODEL = 2048
DEPTH = 4
CHUNK = 64
N_MIXERS = 3
DEEPNORM_ALPHA = (2.0 * DEPTH) ** 0.25

HEADS = 16
HEAD_DIM = 128
MLA_NOPE_DIM = 128
MLA_ROPE_DIM = 64
MLA_Q_RANK = 512
MLA_KV_RANK = 512
MLA_QK_PAD = 256
ROPE_THETA = 10000.0

IDX_HEADS = 16
IDX_DIM = 64
IDX_TOPK_MAX = 256

T5_BUCKETS = 32
T5_MAX_DISTANCE = 128
T5_FAR_BUCKET = T5_BUCKETS // 2 - 1
T5_BAND_BACK = 128

MOE_GROUPS = 4
MOE_EXPERTS_PER_GROUP = 8
MOE_EXPERTS = MOE_GROUPS * MOE_EXPERTS_PER_GROUP
MOE_D_FF = 512

LN_EPS = 1e-5
RMS_EPS = 1e-6
NEG_INF = -1e30

LANES = 128
TOKEN_ROWS = D_MODEL // LANES
VMEM_LIMIT = 56 * 1024 * 1024

ATTN_TQ = 256
MOE_TM = 256


def _params(semantics, vmem=VMEM_LIMIT):
    return pltpu.CompilerParams(dimension_semantics=semantics, vmem_limit_bytes=vmem)


def _mm_kernel(a_ref, w_ref, o_ref, *, scale, scaled_blocks):
    acc = jnp.dot(a_ref[...], w_ref[...], preferred_element_type=F32)
    if scaled_blocks:
        acc = acc * jnp.where(pl.program_id(0) < scaled_blocks, scale, 1.0).astype(F32)
    o_ref[...] = acc.astype(o_ref.dtype)


def _matmul(a, w, *, n_out, tn, tm=1024, col_block_offset=0, scale=1.0, scaled_blocks=0, out_dtype=BF16):
    m, k = a.shape
    tm = min(tm, m)
    kern = functools.partial(_mm_kernel, scale=scale, scaled_blocks=scaled_blocks)
    return pl.pallas_call(
        kern,
        name="proj_mm",
        grid=(n_out // tn, m // tm),
        in_specs=[pl.BlockSpec((tm, k), lambda j, i: (i, 0)),
                  pl.BlockSpec((k, tn), lambda j, i: (0, j + col_block_offset))],
        out_specs=pl.BlockSpec((tm, tn), lambda j, i: (i, j)),
        out_shape=jax.ShapeDtypeStruct((m, n_out), out_dtype),
        compiler_params=_params(("arbitrary", "arbitrary")),
    )(a, w)


def _split3(v):
    hi = v.astype(BF16)
    r1 = v - hi.astype(F32)
    mid = r1.astype(BF16)
    lo = (r1 - mid.astype(F32)).astype(BF16)
    return hi, mid, lo


def _dot_precise(x, w):
    xh = x.astype(BF16)
    xm = (x - xh.astype(F32)).astype(BF16)
    wh = w.astype(BF16)
    wm = (w - wh.astype(F32)).astype(BF16)
    d = lambda p, q: jnp.dot(p, q, preferred_element_type=F32)
    return (d(xm, wh) + d(xh, wm)) + d(xh, wh)


def _small_mm_kernel(x_ref, w_ref, b_ref, o_ref):
    o_ref[...] = _dot_precise(x_ref[...], w_ref[...]) + b_ref[...]


def _small_matmul(x, w, b, tm=512):
    m, k = x.shape
    tm = min(tm, m)
    return pl.pallas_call(
        _small_mm_kernel,
        name="small_mm",
        grid=(m // tm,),
        in_specs=[pl.BlockSpec((tm, k), lambda i: (i, 0)),
                  pl.BlockSpec((k, LANES), lambda i: (0, 0)),
                  pl.BlockSpec((1, LANES), lambda i: (0, 0))],
        out_specs=pl.BlockSpec((tm, LANES), lambda i: (i, 0)),
        out_shape=jax.ShapeDtypeStruct((m, LANES), F32),
        compiler_params=_params(("arbitrary",)),
    )(x, w, b)


def _pad_cols(w, width=LANES):
    return jnp.pad(w, ((0, 0), (0, width - w.shape[1])))


def _layernorm_rows(z, g, b):
    mu = jnp.mean(z, axis=-1, keepdims=True)
    zc = z - mu
    var = jnp.mean(zc * zc, axis=-1, keepdims=True)
    return zc * lax.rsqrt(var + LN_EPS) * g + b


def _mm_res_ln_kernel(a_ref, w_ref, x_ref, g_ref, b_ref, o3_ref):
    y = jnp.dot(a_ref[...], w_ref[...], preferred_element_type=F32)
    out = _layernorm_rows(DEEPNORM_ALPHA * x_ref[...] + y, g_ref[...], b_ref[...])
    _store_token_slabs(o3_ref, out, out.shape[0])


def _mm_res_ln(a, w, x, g, b, tm=512):
    m, k = a.shape
    d = w.shape[1]
    tm = min(tm, m)
    return pl.pallas_call(
        _mm_res_ln_kernel,
        name="out_proj_res_ln",
        grid=(m // tm,),
        in_specs=[pl.BlockSpec((tm, k), lambda i: (i, 0)),
                  pl.BlockSpec((k, d), lambda i: (0, 0)),
                  pl.BlockSpec((tm, d), lambda i: (i, 0)),
                  pl.BlockSpec((1, d), lambda i: (0, 0)),
                  pl.BlockSpec((1, d), lambda i: (0, 0))],
        out_specs=pl.BlockSpec((tm * TOKEN_ROWS, LANES), lambda i: (i, 0)),
        out_shape=jax.ShapeDtypeStruct((m * TOKEN_ROWS, LANES), F32),
        compiler_params=_params(("arbitrary",)),
    )(a, w, x, g, b)


def _rope_tables(seq):
    half = MLA_ROPE_DIM // 2
    inv_freq = ROPE_THETA ** (-jnp.arange(half, dtype=F32) / half)
    ang = jnp.arange(seq, dtype=I32).astype(F32)[:, None] * inv_freq[None, :]
    cos, sin = jnp.cos(ang), jnp.sin(ang)
    z = jnp.zeros_like(cos)
    cos_t = jnp.concatenate([cos, cos, z, z], axis=1)
    sin_a = jnp.concatenate([-sin, z, z, z], axis=1)
    sin_b = jnp.concatenate([z, sin, z, z], axis=1)
    return cos_t, sin_a, sin_b


def _rope_lanes(r, cos_t, sin_a, sin_b):
    from_right = pltpu.roll(r, LANES - MLA_ROPE_DIM // 2, 1)
    from_left = pltpu.roll(r, MLA_ROPE_DIM // 2, 1)
    return r * cos_t + from_right * sin_a + from_left * sin_b


def _rms_rows(v, g):
    return v * lax.rsqrt(jnp.mean(v * v, axis=-1, keepdims=True) + RMS_EPS) * g


def _mla_in_kernel(x_ref, w_ref, qn_ref, kvn_ref, cos_ref, sina_ref, sinb_ref, cq_ref, ckv_ref, kr_ref):
    h = jnp.dot(x_ref[...], w_ref[...], preferred_element_type=F32)
    cq_ref[...] = _rms_rows(h[:, :MLA_Q_RANK], qn_ref[...]).astype(BF16)
    ckv_ref[...] = _rms_rows(h[:, MLA_Q_RANK:MLA_Q_RANK + MLA_KV_RANK], kvn_ref[...]).astype(BF16)
    r = h[:, MLA_Q_RANK + MLA_KV_RANK:]
    kr_ref[...] = _rope_lanes(r, cos_ref[...], sina_ref[...], sinb_ref[...]).astype(BF16)


def _mla_in(x_bf, w_in_pad, q_norm, kv_norm, tables, seq, tm=512):
    m, k = x_bf.shape
    tm = min(tm, seq)
    nw = w_in_pad.shape[1]
    per_seq = seq // tm
    tab_spec = pl.BlockSpec((tm, LANES), lambda i: (i % per_seq, 0))
    return pl.pallas_call(
        _mla_in_kernel,
        name="mla_in",
        grid=(m // tm,),
        in_specs=[pl.BlockSpec((tm, k), lambda i: (i, 0)),
                  pl.BlockSpec((k, nw), lambda i: (0, 0)),
                  pl.BlockSpec((1, MLA_Q_RANK), lambda i: (0, 0)),
                  pl.BlockSpec((1, MLA_KV_RANK), lambda i: (0, 0)),
                  tab_spec, tab_spec, tab_spec],
        out_specs=[pl.BlockSpec((tm, MLA_Q_RANK), lambda i: (i, 0)),
                   pl.BlockSpec((tm, MLA_KV_RANK), lambda i: (i, 0)),
                   pl.BlockSpec((tm, LANES), lambda i: (i, 0))],
        out_shape=[jax.ShapeDtypeStruct((m, MLA_Q_RANK), BF16),
                   jax.ShapeDtypeStruct((m, MLA_KV_RANK), BF16),
                   jax.ShapeDtypeStruct((m, LANES), BF16)],
        compiler_params=_params(("arbitrary",)),
    )(x_bf, w_in_pad, q_norm, kv_norm, *tables)


def _softmax_pv(s, v):
    m = jnp.max(s, axis=1, keepdims=True)
    p = jnp.exp(s - m)
    l = jnp.sum(p, axis=1, keepdims=True)
    o = jnp.dot(p.astype(BF16), v, preferred_element_type=F32)
    return o / l


def _nt_dot(q, k):
    return lax.dot_general(q, k, (((1,), (1,)), ((), ())), preferred_element_type=F32)


def _mla_attn_kernel(q_ref, kn_ref, v_ref, kr_ref, cos_ref, sina_ref, sinb_ref, o_ref, qc_ref, kc_ref, *, tq):
    seq = q_ref.shape[0]
    r = q_ref[:, MLA_NOPE_DIM:].astype(F32)
    qc_ref[:, :MLA_NOPE_DIM] = q_ref[:, :MLA_NOPE_DIM]
    qc_ref[:, MLA_NOPE_DIM:] = _rope_lanes(r, cos_ref[...], sina_ref[...], sinb_ref[...]).astype(BF16)
    kc_ref[:, :MLA_NOPE_DIM] = kn_ref[...]
    kc_ref[:, MLA_NOPE_DIM:] = kr_ref[...]
    for i in range(seq // tq):
        kv = (i + 1) * tq
        s = _nt_dot(qc_ref[i * tq:kv, :], kc_ref[:kv, :])
        row = lax.broadcasted_iota(I32, (tq, kv), 0) + i * tq
        col = lax.broadcasted_iota(I32, (tq, kv), 1)
        s = jnp.where((col // CHUNK) <= (row // CHUNK), s, NEG_INF)
        o_ref[i * tq:kv, :] = _softmax_pv(s, v_ref[:kv, :]).astype(o_ref.dtype)


def _mla_attn(q, kv, kr, tables, batch, seq):
    tq = min(ATTN_TQ, seq)
    kern = functools.partial(_mla_attn_kernel, tq=tq)
    tab_spec = pl.BlockSpec((seq, LANES), lambda b, h: (0, 0))
    return pl.pallas_call(
        kern,
        name="mla_attn",
        grid=(batch, HEADS),
        in_specs=[pl.BlockSpec((seq, MLA_QK_PAD), lambda b, h: (b, h)),
                  pl.BlockSpec((seq, HEAD_DIM), lambda b, h: (b, 2 * h)),
                  pl.BlockSpec((seq, HEAD_DIM), lambda b, h: (b, 2 * h + 1)),
                  pl.BlockSpec((seq, LANES), lambda b, h: (b, 0)),
                  tab_spec, tab_spec, tab_spec],
        out_specs=pl.BlockSpec((seq, HEAD_DIM), lambda b, h: (b, h)),
        out_shape=jax.ShapeDtypeStruct((batch * seq, HEADS * HEAD_DIM), BF16),
        scratch_shapes=[pltpu.VMEM((seq, MLA_QK_PAD), BF16), pltpu.VMEM((seq, MLA_QK_PAD), BF16)],
        compiler_params=_params(("arbitrary", "arbitrary")),
    )(q, kv, kv, kr, *tables)


def _mla_q_up_layout(w_q_up):
    rank = w_q_up.shape[0]
    w = w_q_up.reshape(rank, HEADS, MLA_NOPE_DIM + MLA_ROPE_DIM)
    w = jnp.pad(w, ((0, 0), (0, 0), (0, MLA_QK_PAD - MLA_NOPE_DIM - MLA_ROPE_DIM)))
    return w.reshape(rank, HEADS * MLA_QK_PAD)


def _mla_layer(x, x_bf, w_in, q_norm, kv_norm, w_q_up, w_kv_up, w_o, ln_g, ln_b, batch, seq):
    tables = _rope_tables(seq)
    w_in_pad = jnp.pad(w_in, ((0, 0), (0, MLA_Q_RANK + MLA_KV_RANK + LANES - w_in.shape[1]))).astype(BF16)
    cq, ckv, kr = _mla_in(x_bf, w_in_pad, q_norm[None, :], kv_norm[None, :], tables, seq)
    scale = (MLA_NOPE_DIM + MLA_ROPE_DIM) ** -0.5
    nq = HEADS * MLA_QK_PAD
    q = _matmul(cq, _mla_q_up_layout(w_q_up).astype(BF16), n_out=nq, tn=1024, scale=scale, scaled_blocks=nq // 1024)
    kv = _matmul(ckv, w_kv_up.astype(BF16), n_out=w_kv_up.shape[1], tn=1024)
    o = _mla_attn(q, kv, kr, tables, batch, seq)
    return _mm_res_ln(o, w_o.astype(BF16), x, ln_g[None, :], ln_b[None, :])


def _log_sigmoid(z):
    return jnp.minimum(z, 0.0) - jnp.log1p(jnp.exp(-jnp.abs(z)))


def _fox_gate_kernel(f_ref, c_ref, ct_ref, *, blk):
    seq = f_ref.shape[0]
    tri = (lax.broadcasted_iota(I32, (blk, blk), 1) <= lax.broadcasted_iota(I32, (blk, blk), 0)).astype(BF16)
    carry = jnp.zeros((1, LANES), F32)
    for j in range(seq // blk):
        lf = _log_sigmoid(f_ref[j * blk:(j + 1) * blk, :])
        hi, mid, lo = _split3(lf)
        d = lambda p: jnp.dot(tri, p, preferred_element_type=F32)
        c = (d(lo) + d(mid)) + d(hi) + carry
        c_ref[j * blk:(j + 1) * blk, :] = c
        ct_ref[:, j * blk:(j + 1) * blk] = c.T[:HEADS, :]
        carry = c[blk - 1:blk, :]


def _fox_gate(f_logit, batch, seq):
    blk = min(256, seq)
    return pl.pallas_call(
        functools.partial(_fox_gate_kernel, blk=blk),
        name="fox_gate",
        grid=(batch,),
        in_specs=[pl.BlockSpec((seq, LANES), lambda b: (b, 0))],
        out_specs=[pl.BlockSpec((seq, LANES), lambda b: (b, 0)),
                   pl.BlockSpec((None, HEADS, seq), lambda b: (b, 0, 0))],
        out_shape=[jax.ShapeDtypeStruct((batch * seq, LANES), F32),
                   jax.ShapeDtypeStruct((batch, HEADS, seq), F32)],
        compiler_params=_params(("arbitrary",)),
    )(f_logit)


def _fox_attn_kernel(q_ref, k_ref, v_ref, ccol_ref, crow_ref, o_ref, *, tq):
    seq = q_ref.shape[0]
    h = pl.program_id(1)
    lane = lax.broadcasted_iota(I32, (seq, LANES), 1)
    cq = jnp.sum(jnp.where(lane == h, ccol_ref[...], 0.0), axis=1, keepdims=True)
    for i in range(seq // tq):
        kv = (i + 1) * tq
        s = _nt_dot(q_ref[i * tq:kv, :], k_ref[:kv, :])
        s = (s + cq[i * tq:kv, :]) - crow_ref[:, :kv]
        row = lax.broadcasted_iota(I32, (tq, kv), 0) + i * tq
        col = lax.broadcasted_iota(I32, (tq, kv), 1)
        s = jnp.where(col <= row, s, NEG_INF)
        o_ref[i * tq:kv, :] = _softmax_pv(s, v_ref[:kv, :]).astype(o_ref.dtype)


def _fox_attn(qkv, c_col, c_row, batch, seq):
    tq = min(ATTN_TQ, seq)
    return pl.pallas_call(
        functools.partial(_fox_attn_kernel, tq=tq),
        name="fox_attn",
        grid=(batch, HEADS),
        in_specs=[pl.BlockSpec((seq, HEAD_DIM), lambda b, h: (b, h)),
                  pl.BlockSpec((seq, HEAD_DIM), lambda b, h: (b, HEADS + h)),
                  pl.BlockSpec((seq, HEAD_DIM), lambda b, h: (b, 2 * HEADS + h)),
                  pl.BlockSpec((seq, LANES), lambda b, h: (b, 0)),
                  pl.BlockSpec((None, None, 1, seq), lambda b, h: (b, h, 0, 0))],
        out_specs=pl.BlockSpec((seq, HEAD_DIM), lambda b, h: (b, h)),
        out_shape=jax.ShapeDtypeStruct((batch * seq, HEADS * HEAD_DIM), BF16),
        compiler_params=_params(("arbitrary", "arbitrary")),
    )(qkv, qkv, qkv, c_col, c_row)


def _fox_layer(x, x_bf, w_in, b_f, w_o, ln_g, ln_b, batch, seq):
    hd = HEADS * HEAD_DIM
    scale = HEAD_DIM ** -0.5
    qkv = _matmul(x_bf, w_in.astype(BF16), n_out=3 * hd, tn=1024, scale=scale, scaled_blocks=hd // 1024)
    f_logit = _small_matmul(x, _pad_cols(w_in[:, 3 * hd:]), _pad_cols(b_f[None, :]))
    c_col, c_t = _fox_gate(f_logit, batch, seq)
    o = _fox_attn(qkv, c_col, c_t[:, :, None, :], batch, seq)
    return _mm_res_ln(o, w_o.astype(BF16), x, ln_g[None, :], ln_b[None, :])


def _t5_bucket_table(tq):
    a = np.arange(tq, dtype=np.int64)[:, None]
    b = np.arange(tq + T5_BAND_BACK, dtype=np.int64)[None, :]
    rel = (b - T5_BAND_BACK) - a
    nb = T5_BUCKETS // 2
    max_exact = nb // 2
    ret = np.where(rel > 0, nb, 0)
    n = np.abs(rel)
    nf = np.maximum(n, 1).astype(np.float32)
    large = max_exact + (np.log(nf / np.float32(max_exact)) / np.float32(math.log(T5_MAX_DISTANCE / max_exact))
                         * np.float32(nb - max_exact)).astype(np.int32)
    large = np.minimum(large, nb - 1)
    return (ret + np.where(n < max_exact, n, large)).astype(np.int32)


def _t5_band_kernel(t5_ref, bucket_ref, o_ref):
    bucket = bucket_ref[...]

    def per_head(h, carry):
        acc = jnp.zeros(bucket.shape, F32)
        for b in range(T5_BUCKETS):
            acc = jnp.where(bucket == b, t5_ref[b, h], acc)
        o_ref[h] = acc - t5_ref[T5_FAR_BUCKET, h]
        return carry

    lax.fori_loop(0, HEADS, per_head, 0)


def _t5_band(t5_bias, tq):
    bucket = jnp.asarray(_t5_bucket_table(tq))
    width = tq + T5_BAND_BACK
    return pl.pallas_call(
        _t5_band_kernel,
        name="t5_band",
        grid=(1,),
        in_specs=[pl.BlockSpec(memory_space=pltpu.SMEM),
                  pl.BlockSpec((tq, width), lambda i: (0, 0))],
        out_specs=pl.BlockSpec((HEADS, tq, width), lambda i: (0, 0, 0)),
        out_shape=jax.ShapeDtypeStruct((HEADS, tq, width), F32),
        compiler_params=_params(("arbitrary",)),
    )(t5_bias, bucket)


def _sortable_key(v):
    bits = pltpu.bitcast(v, I32)
    return bits ^ ((bits >> 31) & 0x7FFFFFFF)


def _kth_largest_key(key, k):
    rows = key.shape[0]

    def count_ge(t):
        return jnp.sum((key >= t).astype(I32), axis=1, keepdims=True)

    t0 = jnp.where(count_ge(jnp.zeros((rows, 1), I32)) >= k, 0, jnp.iinfo(jnp.int32).min).astype(I32)

    def body(it, t):
        cand = t | (jnp.int32(1) << (30 - it))
        return jnp.where(count_ge(cand) >= k, cand, t)

    return lax.fori_loop(0, 31, body, t0)


def _dsa_index_kernel(qi_ref, tail_ref, wtail_ref, o_ref, *, tq, topk):
    seq = tail_ref.shape[0]
    i = pl.program_id(1)
    k_idx = tail_ref[:, :IDX_DIM].astype(BF16)
    w = wtail_ref[:, IDX_DIM:IDX_DIM + IDX_HEADS] * ((IDX_HEADS * IDX_DIM) ** -0.5)
    score = jnp.zeros((tq, seq), F32)
    for h in range(IDX_HEADS):
        logits = _nt_dot(qi_ref[:, h * IDX_DIM:(h + 1) * IDX_DIM], k_idx)
        score = score + jnp.maximum(logits, 0.0) * w[:, h:h + 1]
    row = lax.broadcasted_iota(I32, (tq, seq), 0) + i * tq
    col = lax.broadcasted_iota(I32, (tq, seq), 1)
    admissible = (col // CHUNK) <= (row // CHUNK)
    key = _sortable_key(jnp.where(admissible, score, NEG_INF))
    thr = _kth_largest_key(key, topk)
    o_ref[...] = jnp.where(admissible & (key >= thr), 0.0, NEG_INF).astype(o_ref.dtype)


def _dsa_index(q_idx, tail, batch, seq, topk):
    tq = min(ATTN_TQ, seq)
    per_seq = seq // tq
    return pl.pallas_call(
        functools.partial(_dsa_index_kernel, tq=tq, topk=topk),
        name="dsa_index",
        grid=(batch, per_seq),
        in_specs=[pl.BlockSpec((tq, IDX_HEADS * IDX_DIM), lambda b, i: (b * per_seq + i, 0)),
                  pl.BlockSpec((seq, LANES), lambda b, i: (b, 0)),
                  pl.BlockSpec((tq, LANES), lambda b, i: (b * per_seq + i, 0))],
        out_specs=pl.BlockSpec((tq, seq), lambda b, i: (b * per_seq + i, 0)),
        out_shape=jax.ShapeDtypeStruct((batch * seq, seq), BF16),
        compiler_params=_params(("arbitrary", "arbitrary")),
    )(q_idx, tail, tail)


def _dsa_attn_kernel(q_ref, k_ref, v_ref, sel_ref, band_ref, o_ref, *, tq):
    seq = k_ref.shape[0]
    width = band_ref.shape[1]
    for i in range(seq // tq):
        kv = (i + 1) * tq
        near = min(width, kv)
        far = kv - near
        dead = sel_ref[i * tq:kv, :kv] < 0.0
        s = _nt_dot(q_ref[i * tq:kv, :], k_ref[:kv, :])
        s_near = jnp.where(dead[:, far:], NEG_INF, s[:, far:] + band_ref[:, width - near:])
        m = jnp.max(s_near, axis=1, keepdims=True)
        if far:
            s_far = jnp.where(dead[:, :far], NEG_INF, s[:, :far])
            m = jnp.maximum(m, jnp.max(s_far, axis=1, keepdims=True))
        p = jnp.exp(s_near - m)
        l = jnp.sum(p, axis=1, keepdims=True)
        o = jnp.dot(p.astype(BF16), v_ref[far:kv, :], preferred_element_type=F32)
        if far:
            p = jnp.exp(s_far - m)
            l = l + jnp.sum(p, axis=1, keepdims=True)
            o = o + jnp.dot(p.astype(BF16), v_ref[:far, :], preferred_element_type=F32)
        o_ref[i * tq:kv, :] = (o / l).astype(o_ref.dtype)


def _dsa_attn(qkv, sel, band, batch, seq):
    tq = min(ATTN_TQ, seq)
    width = band.shape[2]
    return pl.pallas_call(
        functools.partial(_dsa_attn_kernel, tq=tq),
        name="dsa_attn",
        grid=(batch, HEADS),
        in_specs=[pl.BlockSpec((seq, HEAD_DIM), lambda b, h: (b, h)),
                  pl.BlockSpec((seq, HEAD_DIM), lambda b, h: (b, HEADS)),
                  pl.BlockSpec((seq, HEAD_DIM), lambda b, h: (b, HEADS + 1)),
                  pl.BlockSpec((seq, seq), lambda b, h: (b, 0)),
                  pl.BlockSpec((None, tq, width), lambda b, h: (h, 0, 0))],
        out_specs=pl.BlockSpec((seq, HEAD_DIM), lambda b, h: (b, h)),
        out_shape=jax.ShapeDtypeStruct((batch * seq, HEADS * HEAD_DIM), BF16),
        compiler_params=_params(("arbitrary", "arbitrary")),
    )(qkv, qkv, qkv, sel, band)


def _dsa_layer(x, x_bf, t5_bias, w_in, w_o, ln_g, ln_b, batch, seq):
    hd = HEADS * HEAD_DIM
    n_qkv = hd + 2 * HEAD_DIM
    n_qi = IDX_HEADS * IDX_DIM
    scale = HEAD_DIM ** -0.5
    w_bf = w_in.astype(BF16)
    qkv = _matmul(x_bf, w_bf, n_out=n_qkv, tn=256, scale=scale, scaled_blocks=hd // 256)
    q_idx = _matmul(x_bf, w_bf, n_out=n_qi, tn=256, col_block_offset=n_qkv // 256)
    tail = _small_matmul(x, _pad_cols(w_in[:, n_qkv + n_qi:]), jnp.zeros((1, LANES), F32))
    topk = min(IDX_TOPK_MAX, seq // 4)
    sel = _dsa_index(q_idx, tail, batch, seq, topk)
    band = _t5_band(t5_bias, min(ATTN_TQ, seq))
    o = _dsa_attn(qkv, sel, band, batch, seq)
    return _mm_res_ln(o, w_o.astype(BF16), x, ln_g[None, :], ln_b[None, :])


def _store_token_slabs(ref, v, tm):
    for c in range(TOKEN_ROWS):
        ref[pl.ds(c, tm, stride=TOKEN_ROWS), :] = v[:, c * LANES:(c + 1) * LANES]


def _load_token_slabs(ref, tm):
    return jnp.concatenate([ref[pl.ds(c, tm, stride=TOKEN_ROWS), :] for c in range(TOKEN_ROWS)], axis=1)


def _route_kernel(x_ref, w_ref, b_ref, o_ref, cnt_ref, run_ref):
    i = pl.program_id(0)

    @pl.when(i == 0)
    def _():
        run_ref[...] = jnp.zeros(run_ref.shape, F32)

    tm = o_ref.shape[0]
    logit = _dot_precise(_load_token_slabs(x_ref, tm), w_ref[...]) + b_ref[...]
    lane = lax.broadcasted_iota(I32, logit.shape, 1)
    big = jnp.int32(LANES)

    def first_argmax(mask):
        top = jnp.max(jnp.where(mask, logit, -jnp.inf), axis=1, keepdims=True)
        idx = jnp.min(jnp.where(mask & (logit == top), lane, big), axis=1, keepdims=True)
        return top, idx

    is_group = lane < MOE_GROUPS
    g_top, g_sel = first_argmax(is_group)
    p_g = 1.0 / jnp.sum(jnp.where(is_group, jnp.exp(logit - g_top), 0.0), axis=1, keepdims=True)
    lo = MOE_GROUPS + g_sel * MOE_EXPERTS_PER_GROUP
    in_group = (lane >= lo) & (lane < lo + MOE_EXPERTS_PER_GROUP)
    v1, i1 = first_argmax(in_group)
    v2, i2 = first_argmax(in_group & (lane != i1))
    e2 = jnp.exp(v2 - v1)
    w1 = 1.0 / (1.0 + e2)
    w2 = e2 / (1.0 + e2)
    e1 = i1 - MOE_GROUPS
    e2 = i2 - MOE_GROUPS
    pick1 = lane == e1
    pick2 = lane == e2
    both = jnp.where(pick1 | pick2, 1.0, 0.0)
    earlier = lax.broadcasted_iota(I32, (tm, tm), 1) < lax.broadcasted_iota(I32, (tm, tm), 0)
    before = jnp.dot(jnp.where(earlier, 1.0, 0.0).astype(BF16), both.astype(BF16),
                     preferred_element_type=F32) + run_ref[...]
    r1 = jnp.sum(jnp.where(pick1, before, 0.0), axis=1, keepdims=True)
    r2 = jnp.sum(jnp.where(pick2, before, 0.0), axis=1, keepdims=True)
    run_ref[...] = run_ref[...] + jnp.sum(both, axis=0, keepdims=True)
    cnt_ref[...] = run_ref[...]
    out = jnp.where(lane == 0, e1.astype(F32), 0.0)
    out = jnp.where(lane == 1, e2.astype(F32), out)
    out = jnp.where(lane == 2, w1 * p_g, out)
    out = jnp.where(lane == 3, w2 * p_g, out)
    out = jnp.where(lane == 4, r1, out)
    out = jnp.where(lane == 5, r2, out)
    o_ref[...] = out


def _route(x3, w, b, tm=256):
    m = x3.shape[0] // TOKEN_ROWS
    tm = min(tm, m)
    k = w.shape[0]
    return pl.pallas_call(
        _route_kernel,
        name="moe_route",
        grid=(m // tm,),
        in_specs=[pl.BlockSpec((tm * TOKEN_ROWS, LANES), lambda i: (i, 0)),
                  pl.BlockSpec((k, LANES), lambda i: (0, 0)),
                  pl.BlockSpec((1, LANES), lambda i: (0, 0))],
        out_specs=[pl.BlockSpec((tm, LANES), lambda i: (i, 0)),
                   pl.BlockSpec((1, LANES), lambda i: (0, 0))],
        out_shape=[jax.ShapeDtypeStruct((m, LANES), F32), jax.ShapeDtypeStruct((1, LANES), F32)],
        scratch_shapes=[pltpu.VMEM((1, LANES), F32)],
        compiler_params=_params(("arbitrary",)),
    )(x3, w, b)


def _cast_rows(src_ref, dst_ref):
    rows = 32 * 8 * LANES // src_ref.shape[1]

    def body(i, carry):
        at = pl.ds(pl.multiple_of(i * rows, rows), rows)
        dst_ref[at, :] = src_ref[at, :].astype(dst_ref.dtype)
        return carry
    lax.fori_loop(0, src_ref.shape[0] // rows, body, 0, unroll=2)


def _moe_ffn_kernel(te_ref, tv_ref, tf_ref, tn_ref, src_ref, dst_ref,
                    x3_hbm, wg_hbm, wu_hbm, wd_hbm, y_hbm,
                    xbuf, ybuf, wg_f, wu_f, wd_f, wg_bf, wu_bf, wd_bf, sem_g, sem_s, sem_w, sem_z,
                    *, tm, layer, pad_base):
    j = pl.program_id(0)
    slot = j % 2
    slab = tm * TOKEN_ROWS
    valid = tv_ref[j] == 1
    flush = (j > 0) & jnp.logical_not(valid) & (tv_ref[jnp.maximum(j - 1, 0)] == 1)

    def weight_copies(e):
        return (pltpu.make_async_copy(wg_hbm.at[layer, e], wg_f, sem_w.at[0]),
                pltpu.make_async_copy(wu_hbm.at[layer, e], wu_f, sem_w.at[1]),
                pltpu.make_async_copy(wd_hbm.at[layer, e], wd_f, sem_w.at[2]))

    def gather_copy(tile, r, to_slot):
        row = pl.multiple_of(src_ref[tile * tm + r] * TOKEN_ROWS, TOKEN_ROWS)
        return pltpu.make_async_copy(x3_hbm.at[pl.ds(row, TOKEN_ROWS)],
                                     xbuf.at[to_slot, pl.ds(r * TOKEN_ROWS, TOKEN_ROWS)], sem_g.at[to_slot])

    def scatter_copy(tile, r, from_slot):
        row = pl.multiple_of(dst_ref[(tile + 1) * tm + r] * TOKEN_ROWS, TOKEN_ROWS)
        return pltpu.make_async_copy(ybuf.at[from_slot, pl.ds(r * TOKEN_ROWS, TOKEN_ROWS)],
                                     y_hbm.at[pl.ds(row, TOKEN_ROWS)], sem_s.at[from_slot])

    def zero_fill(k):
        first_row = (pad_base + k * tm) * TOKEN_ROWS
        return pltpu.make_async_copy(ybuf.at[1], y_hbm.at[pl.ds(first_row, slab)], sem_z.at[0])

    def wait_gather(s):
        pltpu.make_async_copy(x3_hbm.at[pl.ds(0, slab)], xbuf.at[s], sem_g.at[s]).wait()

    def wait_scatter(s):
        pltpu.make_async_copy(ybuf.at[s], y_hbm.at[pl.ds(0, slab)], sem_s.at[s]).wait()

    @pl.when(j == 0)
    def _():
        for c in weight_copies(te_ref[0]):
            c.start()
        ybuf[1] = jnp.zeros((slab, LANES), F32)
        for k in range(MOE_EXPERTS):
            zero_fill(k).start()

        def first(r, carry):
            gather_copy(0, r, 0).start()
            return carry
        lax.fori_loop(0, tm, first, 0)

    @pl.when(j == 1)
    def _():
        for k in range(MOE_EXPERTS):
            zero_fill(k).wait()

    @pl.when(valid & (tf_ref[j] == 1))
    def _():
        for c in weight_copies(te_ref[j]):
            c.wait()
        _cast_rows(wg_f, wg_bf)
        _cast_rows(wu_f, wu_bf)
        _cast_rows(wd_f, wd_bf)

        @pl.when(tn_ref[j] >= 0)
        def _():
            for c in weight_copies(tn_ref[j]):
                c.start()

    @pl.when((j > 0) & (valid | flush))
    def _():
        wait_scatter(slot)

    def compute_tile(s):
        wait_gather(s)
        xb = _load_token_slabs(xbuf.at[s], tm).astype(BF16)
        for r in range(tm):
            gather_copy(j + 1, r, 1 - s).start()
            scatter_copy(j - 1, r, 1 - s).start()
        hg = jnp.dot(xb, wg_bf[...], preferred_element_type=F32)
        hu = jnp.dot(xb, wu_bf[...], preferred_element_type=F32)
        act = (hg * (1.0 / (1.0 + jnp.exp(-hg))) * hu).astype(BF16)
        out = jnp.dot(act, wd_bf[...], preferred_element_type=F32)
        _store_token_slabs(ybuf.at[s], out, tm)

    for s in range(2):
        pl.when(valid & (slot == s))(functools.partial(compute_tile, s))

    @pl.when(flush)
    def _():
        def last(r, carry):
            scatter_copy(j - 1, r, 1 - slot).start()
            return carry
        lax.fori_loop(0, tm, last, 0)
        wait_scatter(1 - slot)
        wait_gather(slot)


def _moe_ffn(x3, w_gate, w_up, w_down, layer, tables, pad_base, tm):
    n_tiles = tables[0].shape[0]
    n_out_rows = pad_base + MOE_EXPERTS * tm
    d, f = w_gate.shape[2], w_gate.shape[3]
    slab = tm * TOKEN_ROWS
    any_spec = pl.BlockSpec(memory_space=pl.ANY)
    grid_spec = pltpu.PrefetchScalarGridSpec(
        num_scalar_prefetch=6,
        grid=(n_tiles,),
        in_specs=[any_spec, any_spec, any_spec, any_spec],
        out_specs=any_spec,
        scratch_shapes=[pltpu.VMEM((2, slab, LANES), F32), pltpu.VMEM((2, slab, LANES), F32),
                        pltpu.VMEM((d, f), F32), pltpu.VMEM((d, f), F32), pltpu.VMEM((f, d), F32),
                        pltpu.VMEM((d, f), BF16), pltpu.VMEM((d, f), BF16), pltpu.VMEM((f, d), BF16),
                        pltpu.SemaphoreType.DMA((2,)), pltpu.SemaphoreType.DMA((2,)),
                        pltpu.SemaphoreType.DMA((3,)), pltpu.SemaphoreType.DMA((1,))],
    )
    return pl.pallas_call(
        functools.partial(_moe_ffn_kernel, tm=tm, layer=layer, pad_base=pad_base),
        name="moe_ffn",
        grid_spec=grid_spec,
        out_shape=jax.ShapeDtypeStruct((n_out_rows * TOKEN_ROWS, LANES), F32),
        compiler_params=_params(("arbitrary",)),
    )(*tables, x3, w_gate, w_up, w_down)


def _moe_combine_kernel(y0_ref, y1_ref, x_ref, r_ref, g_ref, b_ref, o_ref, obf_ref, *, tm):
    g1 = r_ref[:, 2:3]
    g2 = r_ref[:, 3:4]
    pieces = []
    for c in range(TOKEN_ROWS):
        rows = pl.ds(c, tm, stride=TOKEN_ROWS)
        pieces.append(DEEPNORM_ALPHA * x_ref[rows, :] + (g1 * y0_ref[rows, :] + g2 * y1_ref[rows, :]))
    out = _layernorm_rows(jnp.concatenate(pieces, axis=1), g_ref[...], b_ref[...])
    o_ref[...] = out
    obf_ref[...] = out.astype(BF16)


def _moe_combine(y_tok, x3, route, g, b, tm):
    m = route.shape[0]
    d = g.shape[1]
    second = m // tm
    slab_spec = lambda off: pl.BlockSpec((tm * TOKEN_ROWS, LANES), lambda i: (i + off, 0))
    return pl.pallas_call(
        functools.partial(_moe_combine_kernel, tm=tm),
        name="moe_combine",
        grid=(m // tm,),
        in_specs=[slab_spec(0), slab_spec(second), slab_spec(0),
                  pl.BlockSpec((tm, LANES), lambda i: (i, 0)),
                  pl.BlockSpec((1, d), lambda i: (0, 0)),
                  pl.BlockSpec((1, d), lambda i: (0, 0))],
        out_specs=[pl.BlockSpec((tm, d), lambda i: (i, 0)),
                   pl.BlockSpec((tm, d), lambda i: (i, 0))],
        out_shape=[jax.ShapeDtypeStruct((m, d), F32), jax.ShapeDtypeStruct((m, d), BF16)],
        compiler_params=_params(("arbitrary",)),
    )(y_tok, y_tok, x3, route, g, b)


def _dispatch_tables(route, counts, n_tokens, tm):
    expert = route[:, 0:2].astype(I32)
    rank = route[:, 4:6].astype(I32)
    counts = counts[0, :MOE_EXPERTS].astype(I32)
    padded = ((counts + tm - 1) // tm) * tm
    ends = jnp.cumsum(padded)
    starts = ends - padded
    pos = (starts[expert] + rank).reshape(-1)
    n_rows = n_tokens * 2 + MOE_EXPERTS * tm
    n_tiles = n_rows // tm
    code = (jnp.arange(n_tokens, dtype=I32)[:, None] * 2 + jnp.arange(2, dtype=I32)[None, :]).reshape(-1)
    row_code = jnp.full((n_rows,), -1, I32).at[pos].set(code)
    is_pad = row_code < 0
    row = jnp.arange(n_rows, dtype=I32)
    row_src = jnp.where(is_pad, 0, row_code >> 1)
    tile_start = jnp.arange(n_tiles, dtype=I32) * tm
    tile_expert = jnp.minimum(jnp.searchsorted(ends, tile_start, side="right"), MOE_EXPERTS - 1).astype(I32)
    tile_valid = (tile_start < ends[-1]).astype(I32)
    row_expert = jnp.repeat(tile_expert, tm)
    real_before = ((jnp.cumsum(counts) - counts)[row_expert]
                   + jnp.minimum(row - starts[row_expert], counts[row_expert]))
    real_before = jnp.where(row < ends[-1], real_before, 2 * n_tokens)
    pad_base = 2 * n_tokens + tm
    row_dst = jnp.where(is_pad, pad_base + row - real_before, (row_code & 1) * n_tokens + (row_code >> 1))
    row_src = jnp.concatenate([row_src, jnp.zeros((tm,), I32)])
    row_dst = jnp.concatenate([2 * n_tokens + jnp.arange(tm, dtype=I32), row_dst])
    tile_first = jnp.concatenate([jnp.ones((1,), I32), (tile_expert[1:] != tile_expert[:-1]).astype(I32)])
    ids = jnp.arange(MOE_EXPERTS, dtype=I32)
    later = (ids[None, :] > ids[:, None]) & (counts[None, :] > 0)
    next_used = jnp.min(jnp.where(later, ids[None, :], MOE_EXPERTS), axis=1)
    next_used = jnp.where(next_used == MOE_EXPERTS, -1, next_used).astype(I32)
    tile_next = next_used[tile_expert]
    return (tile_expert, tile_valid, tile_first, tile_next, row_src, row_dst), pad_base


def _moe_layer(x3, w_group, b_group, w_router, b_router, w_gate, w_up, w_down, layer, ln_g, ln_b):
    n_tokens = x3.shape[0] // TOKEN_ROWS
    w_route = _pad_cols(jnp.concatenate([w_group, w_router], axis=1))
    b_route = _pad_cols(jnp.concatenate([b_group, b_router])[None, :])
    route, counts = _route(x3, w_route, b_route)
    tm = min(MOE_TM, n_tokens)
    tables, pad_base = _dispatch_tables(route, counts, n_tokens, tm)
    y_tok = _moe_ffn(x3, w_gate, w_up, w_down, layer, tables, pad_base, tm)
    return _moe_combine(y_tok, x3, route, ln_g[None, :], ln_b[None, :], tm)


def kernel(x, t5_rel_bias, mla_w_in, mla_q_norm, mla_kv_norm, mla_w_q_up, mla_w_kv_up, mla_w_o, fox_w_in, fox_b_f, fox_w_o, dsa_w_in, dsa_w_o, ln_g, ln_b, moe_w_group, moe_b_group, moe_w_router, moe_b_router, moe_w_gate, moe_w_up, moe_w_down):
    batch, seq, d = x.shape
    x = x.reshape(batch * seq, d)
    x_bf = x.astype(BF16)
    for layer in range(DEPTH):
        kind = layer % N_MIXERS
        j = layer // N_MIXERS
        g0, b0 = ln_g[layer, 0], ln_b[layer, 0]
        if kind == 0:
            x3 = _mla_layer(x, x_bf, mla_w_in[j], mla_q_norm[j], mla_kv_norm[j], mla_w_q_up[j],
                            mla_w_kv_up[j], mla_w_o[j], g0, b0, batch, seq)
        elif kind == 1:
            x3 = _fox_layer(x, x_bf, fox_w_in[j], fox_b_f[j], fox_w_o[j], g0, b0, batch, seq)
        else:
            x3 = _dsa_layer(x, x_bf, t5_rel_bias, dsa_w_in[j], dsa_w_o[j], g0, b0, batch, seq)
        x, x_bf = _moe_layer(x3, moe_w_group[layer], moe_b_group[layer], moe_w_router[layer], moe_b_router[layer],
                             moe_w_gate, moe_w_up, moe_w_down, layer, ln_g[layer, 1], ln_b[layer, 1])
    return x.reshape(batch, seq, d)
```

```python
import functools
import math

import numpy as np
import jax
import jax.numpy as jnp
from jax import lax
from jax.experimental import pallas as pl
from jax.experimental.pallas import tpu as pltpu

F32 = jnp.float32
BF16 = jnp.bfloat16
I32 = jnp.int32

D_MODEL = 2048
DEPTH = 4
CHUNK = 64
N_MIXERS = 3
DEEPNORM_ALPHA = (2.0 * DEPTH) ** 0.25

HEADS = 16
HEAD_DIM = 128
MLA_NOPE_DIM = 128
MLA_ROPE_DIM = 64
MLA_Q_RANK = 512
MLA_KV_RANK = 512
MLA_QK_PAD = 256
ROPE_THETA = 10000.0

IDX_HEADS = 16
IDX_DIM = 64
IDX_TOPK_MAX = 256

T5_BUCKETS = 32
T5_MAX_DISTANCE = 128
T5_FAR_BUCKET = T5_BUCKETS // 2 - 1
T5_BAND_BACK = 128

MOE_GROUPS = 4
MOE_EXPERTS_PER_GROUP = 8
MOE_EXPERTS = MOE_GROUPS * MOE_EXPERTS_PER_GROUP
MOE_D_FF = 512

LN_EPS = 1e-5
RMS_EPS = 1e-6
NEG_INF = -1e30

LANES = 128
VMEM_LIMIT = 56 * 1024 * 1024

ATTN_TQ = 256
MOE_TM = 256
ROW_ALIGN = 16
TILE_ROWS = -(-(2 * MOE_TM + MOE_EXPERTS * (ROW_ALIGN - 1)) // 256) * 256
TILE_CHUNKS = TILE_ROWS // ROW_ALIGN


def _params(semantics, vmem=VMEM_LIMIT):
    return pltpu.CompilerParams(dimension_semantics=semantics, vmem_limit_bytes=vmem)


def _mm_kernel(a_ref, w_ref, o_ref, *, scale, scaled_blocks):
    acc = jnp.dot(a_ref[...], w_ref[...], preferred_element_type=F32)
    if scaled_blocks:
        acc = acc * jnp.where(pl.program_id(0) < scaled_blocks, scale, 1.0).astype(F32)
    o_ref[...] = acc.astype(o_ref.dtype)


def _matmul(a, w, *, n_out, tn, tm=1024, col_block_offset=0, scale=1.0, scaled_blocks=0, out_dtype=BF16):
    m, k = a.shape
    tm = min(tm, m)
    kern = functools.partial(_mm_kernel, scale=scale, scaled_blocks=scaled_blocks)
    return pl.pallas_call(
        kern,
        name="proj_mm",
        grid=(n_out // tn, m // tm),
        in_specs=[pl.BlockSpec((tm, k), lambda j, i: (i, 0)),
                  pl.BlockSpec((k, tn), lambda j, i: (0, j + col_block_offset))],
        out_specs=pl.BlockSpec((tm, tn), lambda j, i: (i, j)),
        out_shape=jax.ShapeDtypeStruct((m, n_out), out_dtype),
        compiler_params=_params(("arbitrary", "arbitrary")),
    )(a, w)


def _split3(v):
    hi = v.astype(BF16)
    r1 = v - hi.astype(F32)
    mid = r1.astype(BF16)
    lo = (r1 - mid.astype(F32)).astype(BF16)
    return hi, mid, lo


def _dot_precise(x, w):
    xh = x.astype(BF16)
    xm = (x - xh.astype(F32)).astype(BF16)
    wh = w.astype(BF16)
    wm = (w - wh.astype(F32)).astype(BF16)
    d = lambda p, q: jnp.dot(p, q, preferred_element_type=F32)
    return (d(xm, wh) + d(xh, wm)) + d(xh, wh)


def _small_mm_kernel(x_ref, w_ref, b_ref, o_ref):
    o_ref[...] = _dot_precise(x_ref[...], w_ref[...]) + b_ref[...]


def _small_matmul(x, w, b, tm=512):
    m, k = x.shape
    tm = min(tm, m)
    return pl.pallas_call(
        _small_mm_kernel,
        name="small_mm",
        grid=(m // tm,),
        in_specs=[pl.BlockSpec((tm, k), lambda i: (i, 0)),
                  pl.BlockSpec((k, LANES), lambda i: (0, 0)),
                  pl.BlockSpec((1, LANES), lambda i: (0, 0))],
        out_specs=pl.BlockSpec((tm, LANES), lambda i: (i, 0)),
        out_shape=jax.ShapeDtypeStruct((m, LANES), F32),
        compiler_params=_params(("arbitrary",)),
    )(x, w, b)


def _pad_cols(w, width=LANES):
    return jnp.pad(w, ((0, 0), (0, width - w.shape[1])))


def _layernorm_rows(z, g, b):
    mu = jnp.mean(z, axis=-1, keepdims=True)
    zc = z - mu
    var = jnp.mean(zc * zc, axis=-1, keepdims=True)
    return zc * lax.rsqrt(var + LN_EPS) * g + b


def _mm_res_ln_kernel(a_ref, w_ref, x_ref, g_ref, b_ref, o_ref, obf_ref):
    y = jnp.dot(a_ref[...], w_ref[...], preferred_element_type=F32)
    out = _layernorm_rows(DEEPNORM_ALPHA * x_ref[...] + y, g_ref[...], b_ref[...])
    o_ref[...] = out
    obf_ref[...] = out.astype(BF16)


def _mm_res_ln(a, w, x, g, b, tm=512):
    m, k = a.shape
    d = w.shape[1]
    tm = min(tm, m)
    return pl.pallas_call(
        _mm_res_ln_kernel,
        name="out_proj_res_ln",
        grid=(m // tm,),
        in_specs=[pl.BlockSpec((tm, k), lambda i: (i, 0)),
                  pl.BlockSpec((k, d), lambda i: (0, 0)),
                  pl.BlockSpec((tm, d), lambda i: (i, 0)),
                  pl.BlockSpec((1, d), lambda i: (0, 0)),
                  pl.BlockSpec((1, d), lambda i: (0, 0))],
        out_specs=[pl.BlockSpec((tm, d), lambda i: (i, 0)),
                   pl.BlockSpec((tm, d), lambda i: (i, 0))],
        out_shape=[jax.ShapeDtypeStruct((m, d), F32), jax.ShapeDtypeStruct((m, d), BF16)],
        compiler_params=_params(("arbitrary",)),
    )(a, w, x, g, b)


def _rope_tables(seq):
    half = MLA_ROPE_DIM // 2
    inv_freq = ROPE_THETA ** (-jnp.arange(half, dtype=F32) / half)
    ang = jnp.arange(seq, dtype=I32).astype(F32)[:, None] * inv_freq[None, :]
    cos, sin = jnp.cos(ang), jnp.sin(ang)
    z = jnp.zeros_like(cos)
    cos_t = jnp.concatenate([cos, cos, z, z], axis=1)
    sin_a = jnp.concatenate([-sin, z, z, z], axis=1)
    sin_b = jnp.concatenate([z, sin, z, z], axis=1)
    return cos_t, sin_a, sin_b


def _rope_lanes(r, cos_t, sin_a, sin_b):
    from_right = pltpu.roll(r, LANES - MLA_ROPE_DIM // 2, 1)
    from_left = pltpu.roll(r, MLA_ROPE_DIM // 2, 1)
    return r * cos_t + from_right * sin_a + from_left * sin_b


def _rms_rows(v, g):
    return v * lax.rsqrt(jnp.mean(v * v, axis=-1, keepdims=True) + RMS_EPS) * g


def _mla_in_kernel(x_ref, w_ref, qn_ref, kvn_ref, cos_ref, sina_ref, sinb_ref, cq_ref, ckv_ref, kr_ref):
    h = jnp.dot(x_ref[...], w_ref[...], preferred_element_type=F32)
    cq_ref[...] = _rms_rows(h[:, :MLA_Q_RANK], qn_ref[...]).astype(BF16)
    ckv_ref[...] = _rms_rows(h[:, MLA_Q_RANK:MLA_Q_RANK + MLA_KV_RANK], kvn_ref[...]).astype(BF16)
    r = h[:, MLA_Q_RANK + MLA_KV_RANK:]
    kr_ref[...] = _rope_lanes(r, cos_ref[...], sina_ref[...], sinb_ref[...]).astype(BF16)


def _mla_in(x_bf, w_in_pad, q_norm, kv_norm, tables, seq, tm=512):
    m, k = x_bf.shape
    tm = min(tm, seq)
    nw = w_in_pad.shape[1]
    per_seq = seq // tm
    tab_spec = pl.BlockSpec((tm, LANES), lambda i: (i % per_seq, 0))
    return pl.pallas_call(
        _mla_in_kernel,
        name="mla_in",
        grid=(m // tm,),
        in_specs=[pl.BlockSpec((tm, k), lambda i: (i, 0)),
                  pl.BlockSpec((k, nw), lambda i: (0, 0)),
                  pl.BlockSpec((1, MLA_Q_RANK), lambda i: (0, 0)),
                  pl.BlockSpec((1, MLA_KV_RANK), lambda i: (0, 0)),
                  tab_spec, tab_spec, tab_spec],
        out_specs=[pl.BlockSpec((tm, MLA_Q_RANK), lambda i: (i, 0)),
                   pl.BlockSpec((tm, MLA_KV_RANK), lambda i: (i, 0)),
                   pl.BlockSpec((tm, LANES), lambda i: (i, 0))],
        out_shape=[jax.ShapeDtypeStruct((m, MLA_Q_RANK), BF16),
                   jax.ShapeDtypeStruct((m, MLA_KV_RANK), BF16),
                   jax.ShapeDtypeStruct((m, LANES), BF16)],
        compiler_params=_params(("arbitrary",)),
    )(x_bf, w_in_pad, q_norm, kv_norm, *tables)


def _softmax_pv(s, v):
    m = jnp.max(s, axis=1, keepdims=True)
    p = jnp.exp(s - m)
    l = jnp.sum(p, axis=1, keepdims=True)
    o = jnp.dot(p.astype(BF16), v, preferred_element_type=F32)
    return o / l


def _nt_dot(q, k):
    return lax.dot_general(q, k, (((1,), (1,)), ((), ())), preferred_element_type=F32)


def _mla_attn_kernel(q_ref, kn_ref, v_ref, kr_ref, cos_ref, sina_ref, sinb_ref, o_ref, qc_ref, kc_ref, *, tq):
    seq = q_ref.shape[0]
    r = q_ref[:, MLA_NOPE_DIM:].astype(F32)
    qc_ref[:, :MLA_NOPE_DIM] = q_ref[:, :MLA_NOPE_DIM]
    qc_ref[:, MLA_NOPE_DIM:] = _rope_lanes(r, cos_ref[...], sina_ref[...], sinb_ref[...]).astype(BF16)
    kc_ref[:, :MLA_NOPE_DIM] = kn_ref[...]
    kc_ref[:, MLA_NOPE_DIM:] = kr_ref[...]
    for i in range(seq // tq):
        kv = (i + 1) * tq
        s = _nt_dot(qc_ref[i * tq:kv, :], kc_ref[:kv, :])
        row = lax.broadcasted_iota(I32, (tq, kv), 0) + i * tq
        col = lax.broadcasted_iota(I32, (tq, kv), 1)
        s = jnp.where((col // CHUNK) <= (row // CHUNK), s, NEG_INF)
        o_ref[i * tq:kv, :] = _softmax_pv(s, v_ref[:kv, :]).astype(o_ref.dtype)


def _mla_attn(q, kv, kr, tables, batch, seq):
    tq = min(ATTN_TQ, seq)
    kern = functools.partial(_mla_attn_kernel, tq=tq)
    tab_spec = pl.BlockSpec((seq, LANES), lambda b, h: (0, 0))
    return pl.pallas_call(
        kern,
        name="mla_attn",
        grid=(batch, HEADS),
        in_specs=[pl.BlockSpec((seq, MLA_QK_PAD), lambda b, h: (b, h)),
                  pl.BlockSpec((seq, HEAD_DIM), lambda b, h: (b, 2 * h)),
                  pl.BlockSpec((seq, HEAD_DIM), lambda b, h: (b, 2 * h + 1)),
                  pl.BlockSpec((seq, LANES), lambda b, h: (b, 0)),
                  tab_spec, tab_spec, tab_spec],
        out_specs=pl.BlockSpec((seq, HEAD_DIM), lambda b, h: (b, h)),
        out_shape=jax.ShapeDtypeStruct((batch * seq, HEADS * HEAD_DIM), BF16),
        scratch_shapes=[pltpu.VMEM((seq, MLA_QK_PAD), BF16), pltpu.VMEM((seq, MLA_QK_PAD), BF16)],
        compiler_params=_params(("arbitrary", "arbitrary")),
    )(q, kv, kv, kr, *tables)


def _mla_q_up_layout(w_q_up):
    rank = w_q_up.shape[0]
    w = w_q_up.reshape(rank, HEADS, MLA_NOPE_DIM + MLA_ROPE_DIM)
    w = jnp.pad(w, ((0, 0), (0, 0), (0, MLA_QK_PAD - MLA_NOPE_DIM - MLA_ROPE_DIM)))
    return w.reshape(rank, HEADS * MLA_QK_PAD)


def _mla_layer(x, x_bf, w_in, q_norm, kv_norm, w_q_up, w_kv_up, w_o, ln_g, ln_b, batch, seq):
    tables = _rope_tables(seq)
    w_in_pad = jnp.pad(w_in, ((0, 0), (0, MLA_Q_RANK + MLA_KV_RANK + LANES - w_in.shape[1]))).astype(BF16)
    cq, ckv, kr = _mla_in(x_bf, w_in_pad, q_norm[None, :], kv_norm[None, :], tables, seq)
    scale = (MLA_NOPE_DIM + MLA_ROPE_DIM) ** -0.5
    nq = HEADS * MLA_QK_PAD
    q = _matmul(cq, _mla_q_up_layout(w_q_up).astype(BF16), n_out=nq, tn=1024, scale=scale, scaled_blocks=nq // 1024)
    kv = _matmul(ckv, w_kv_up.astype(BF16), n_out=w_kv_up.shape[1], tn=1024)
    o = _mla_attn(q, kv, kr, tables, batch, seq)
    return _mm_res_ln(o, w_o.astype(BF16), x, ln_g[None, :], ln_b[None, :])


def _log_sigmoid(z):
    return jnp.minimum(z, 0.0) - jnp.log1p(jnp.exp(-jnp.abs(z)))


def _fox_gate_kernel(f_ref, c_ref, ct_ref, *, blk):
    seq = f_ref.shape[0]
    tri = (lax.broadcasted_iota(I32, (blk, blk), 1) <= lax.broadcasted_iota(I32, (blk, blk), 0)).astype(BF16)
    carry = jnp.zeros((1, LANES), F32)
    for j in range(seq // blk):
        lf = _log_sigmoid(f_ref[j * blk:(j + 1) * blk, :])
        hi, mid, lo = _split3(lf)
        d = lambda p: jnp.dot(tri, p, preferred_element_type=F32)
        c = (d(lo) + d(mid)) + d(hi) + carry
        c_ref[j * blk:(j + 1) * blk, :] = c
        ct_ref[:, j * blk:(j + 1) * blk] = c.T[:HEADS, :]
        carry = c[blk - 1:blk, :]


def _fox_gate(f_logit, batch, seq):
    blk = min(256, seq)
    return pl.pallas_call(
        functools.partial(_fox_gate_kernel, blk=blk),
        name="fox_gate",
        grid=(batch,),
        in_specs=[pl.BlockSpec((seq, LANES), lambda b: (b, 0))],
        out_specs=[pl.BlockSpec((seq, LANES), lambda b: (b, 0)),
                   pl.BlockSpec((None, HEADS, seq), lambda b: (b, 0, 0))],
        out_shape=[jax.ShapeDtypeStruct((batch * seq, LANES), F32),
                   jax.ShapeDtypeStruct((batch, HEADS, seq), F32)],
        compiler_params=_params(("arbitrary",)),
    )(f_logit)


def _fox_attn_kernel(q_ref, k_ref, v_ref, ccol_ref, crow_ref, o_ref, *, tq):
    seq = q_ref.shape[0]
    h = pl.program_id(1)
    lane = lax.broadcasted_iota(I32, (seq, LANES), 1)
    cq = jnp.sum(jnp.where(lane == h, ccol_ref[...], 0.0), axis=1, keepdims=True)
    for i in range(seq // tq):
        kv = (i + 1) * tq
        s = _nt_dot(q_ref[i * tq:kv, :], k_ref[:kv, :])
        s = (s + cq[i * tq:kv, :]) - crow_ref[:, :kv]
        row = lax.broadcasted_iota(I32, (tq, kv), 0) + i * tq
        col = lax.broadcasted_iota(I32, (tq, kv), 1)
        s = jnp.where(col <= row, s, NEG_INF)
        o_ref[i * tq:kv, :] = _softmax_pv(s, v_ref[:kv, :]).astype(o_ref.dtype)


def _fox_attn(qkv, c_col, c_row, batch, seq):
    tq = min(ATTN_TQ, seq)
    return pl.pallas_call(
        functools.partial(_fox_attn_kernel, tq=tq),
        name="fox_attn",
        grid=(batch, HEADS),
        in_specs=[pl.BlockSpec((seq, HEAD_DIM), lambda b, h: (b, h)),
                  pl.BlockSpec((seq, HEAD_DIM), lambda b, h: (b, HEADS + h)),
                  pl.BlockSpec((seq, HEAD_DIM), lambda b, h: (b, 2 * HEADS + h)),
                  pl.BlockSpec((seq, LANES), lambda b, h: (b, 0)),
                  pl.BlockSpec((None, None, 1, seq), lambda b, h: (b, h, 0, 0))],
        out_specs=pl.BlockSpec((seq, HEAD_DIM), lambda b, h: (b, h)),
        out_shape=jax.ShapeDtypeStruct((batch * seq, HEADS * HEAD_DIM), BF16),
        compiler_params=_params(("arbitrary", "arbitrary")),
    )(qkv, qkv, qkv, c_col, c_row)


def _fox_layer(x, x_bf, w_in, b_f, w_o, ln_g, ln_b, batch, seq):
    hd = HEADS * HEAD_DIM
    scale = HEAD_DIM ** -0.5
    qkv = _matmul(x_bf, w_in.astype(BF16), n_out=3 * hd, tn=1024, scale=scale, scaled_blocks=hd // 1024)
    f_logit = _small_matmul(x, _pad_cols(w_in[:, 3 * hd:]), _pad_cols(b_f[None, :]))
    c_col, c_t = _fox_gate(f_logit, batch, seq)
    o = _fox_attn(qkv, c_col, c_t[:, :, None, :], batch, seq)
    return _mm_res_ln(o, w_o.astype(BF16), x, ln_g[None, :], ln_b[None, :])


def _t5_bucket_table(tq):
    a = np.arange(tq, dtype=np.int64)[:, None]
    b = np.arange(tq + T5_BAND_BACK, dtype=np.int64)[None, :]
    rel = (b - T5_BAND_BACK) - a
    nb = T5_BUCKETS // 2
    max_exact = nb // 2
    ret = np.where(rel > 0, nb, 0)
    n = np.abs(rel)
    nf = np.maximum(n, 1).astype(np.float32)
    large = max_exact + (np.log(nf / np.float32(max_exact)) / np.float32(math.log(T5_MAX_DISTANCE / max_exact))
                         * np.float32(nb - max_exact)).astype(np.int32)
    large = np.minimum(large, nb - 1)
    return (ret + np.where(n < max_exact, n, large)).astype(np.int32)


def _t5_band_kernel(t5_ref, bucket_ref, o_ref):
    bucket = bucket_ref[...]

    def per_head(h, carry):
        acc = jnp.zeros(bucket.shape, F32)
        for b in range(T5_BUCKETS):
            acc = jnp.where(bucket == b, t5_ref[b, h], acc)
        o_ref[h] = acc - t5_ref[T5_FAR_BUCKET, h]
        return carry

    lax.fori_loop(0, HEADS, per_head, 0)


def _t5_band(t5_bias, tq):
    bucket = jnp.asarray(_t5_bucket_table(tq))
    width = tq + T5_BAND_BACK
    return pl.pallas_call(
        _t5_band_kernel,
        name="t5_band",
        grid=(1,),
        in_specs=[pl.BlockSpec(memory_space=pltpu.SMEM),
                  pl.BlockSpec((tq, width), lambda i: (0, 0))],
        out_specs=pl.BlockSpec((HEADS, tq, width), lambda i: (0, 0, 0)),
        out_shape=jax.ShapeDtypeStruct((HEADS, tq, width), F32),
        compiler_params=_params(("arbitrary",)),
    )(t5_bias, bucket)


def _sortable_key(v):
    bits = pltpu.bitcast(v, I32)
    return bits ^ ((bits >> 31) & 0x7FFFFFFF)


def _kth_largest_key(key_ref, kv, k):
    rows = key_ref.shape[0]

    def count_ge(t):
        t_b = jnp.broadcast_to(t, (rows, LANES))
        acc = jnp.zeros((rows, LANES), I32)
        for c in range(kv // LANES):
            acc = acc + jnp.where(key_ref[:, c * LANES:(c + 1) * LANES] >= t_b, 1, 0)
        return jnp.sum(acc, axis=1, keepdims=True)

    t0 = jnp.where(count_ge(jnp.zeros((rows, 1), I32)) >= k, 0, jnp.iinfo(jnp.int32).min).astype(I32)

    def body(it, t):
        cand = t | (jnp.int32(1) << (30 - it))
        return jnp.where(count_ge(cand) >= k, cand, t)

    return lax.fori_loop(0, 31, body, t0)


def _dsa_index_kernel(qi_ref, tail_ref, o_ref, key_ref, *, tq, topk):
    seq = tail_ref.shape[0]
    k_idx = tail_ref[:, :IDX_DIM].astype(BF16)
    for i in range(seq // tq):
        kv = (i + 1) * tq
        rows = slice(i * tq, kv)
        w = tail_ref[rows, IDX_DIM:IDX_DIM + IDX_HEADS] * ((IDX_HEADS * IDX_DIM) ** -0.5)
        score = jnp.zeros((tq, kv), F32)
        for h in range(IDX_HEADS):
            logits = _nt_dot(qi_ref[rows, h * IDX_DIM:(h + 1) * IDX_DIM], k_idx[:kv, :])
            score = score + jnp.maximum(logits, 0.0) * w[:, h:h + 1]
        row = lax.broadcasted_iota(I32, (tq, kv), 0) + i * tq
        col = lax.broadcasted_iota(I32, (tq, kv), 1)
        admissible = (col // CHUNK) <= (row // CHUNK)
        if kv <= topk:
            keep = admissible
        else:
            key_ref[:, :kv] = _sortable_key(jnp.where(admissible, score, NEG_INF))
            thr = _kth_largest_key(key_ref, kv, topk)
            keep = admissible & (key_ref[:, :kv] >= thr)
        o_ref[rows, :kv] = jnp.where(keep, 0.0, NEG_INF).astype(o_ref.dtype)
        if kv < seq:
            o_ref[rows, kv:] = jnp.full((tq, seq - kv), NEG_INF, o_ref.dtype)


def _dsa_index(q_idx, tail, batch, seq, topk):
    tq = min(ATTN_TQ, seq)
    return pl.pallas_call(
        functools.partial(_dsa_index_kernel, tq=tq, topk=topk),
        name="dsa_index",
        grid=(batch,),
        in_specs=[pl.BlockSpec((seq, IDX_HEADS * IDX_DIM), lambda b: (b, 0)),
                  pl.BlockSpec((seq, LANES), lambda b: (b, 0))],
        out_specs=pl.BlockSpec((seq, seq), lambda b: (b, 0)),
        out_shape=jax.ShapeDtypeStruct((batch * seq, seq), BF16),
        scratch_shapes=[pltpu.VMEM((tq, seq), I32)],
        compiler_params=_params(("arbitrary",)),
    )(q_idx, tail)


def _dsa_attn_kernel(q_ref, k_ref, v_ref, sel_ref, band_ref, o_ref, *, tq):
    seq = k_ref.shape[0]
    width = band_ref.shape[1]
    for i in range(seq // tq):
        kv = (i + 1) * tq
        near = min(width, kv)
        far = kv - near
        dead = sel_ref[i * tq:kv, :kv] < 0.0
        s = _nt_dot(q_ref[i * tq:kv, :], k_ref[:kv, :])
        s_near = jnp.where(dead[:, far:], NEG_INF, s[:, far:] + band_ref[:, width - near:])
        m = jnp.max(s_near, axis=1, keepdims=True)
        if far:
            s_far = jnp.where(dead[:, :far], NEG_INF, s[:, :far])
            m = jnp.maximum(m, jnp.max(s_far, axis=1, keepdims=True))
        p = jnp.exp(s_near - m)
        l = jnp.sum(p, axis=1, keepdims=True)
        o = jnp.dot(p.astype(BF16), v_ref[far:kv, :], preferred_element_type=F32)
        if far:
            p = jnp.exp(s_far - m)
            l = l + jnp.sum(p, axis=1, keepdims=True)
            o = o + jnp.dot(p.astype(BF16), v_ref[:far, :], preferred_element_type=F32)
        o_ref[i * tq:kv, :] = (o / l).astype(o_ref.dtype)


def _dsa_attn(qkv, sel, band, batch, seq):
    tq = min(ATTN_TQ, seq)
    width = band.shape[2]
    return pl.pallas_call(
        functools.partial(_dsa_attn_kernel, tq=tq),
        name="dsa_attn",
        grid=(batch, HEADS),
        in_specs=[pl.BlockSpec((seq, HEAD_DIM), lambda b, h: (b, h)),
                  pl.BlockSpec((seq, HEAD_DIM), lambda b, h: (b, HEADS)),
                  pl.BlockSpec((seq, HEAD_DIM), lambda b, h: (b, HEADS + 1)),
                  pl.BlockSpec((seq, seq), lambda b, h: (b, 0)),
                  pl.BlockSpec((None, tq, width), lambda b, h: (h, 0, 0))],
        out_specs=pl.BlockSpec((seq, HEAD_DIM), lambda b, h: (b, h)),
        out_shape=jax.ShapeDtypeStruct((batch * seq, HEADS * HEAD_DIM), BF16),
        compiler_params=_params(("arbitrary", "arbitrary")),
    )(qkv, qkv, qkv, sel, band)


def _dsa_layer(x, x_bf, t5_bias, w_in, w_o, ln_g, ln_b, batch, seq):
    hd = HEADS * HEAD_DIM
    n_qkv = hd + 2 * HEAD_DIM
    n_qi = IDX_HEADS * IDX_DIM
    scale = HEAD_DIM ** -0.5
    w_bf = w_in.astype(BF16)
    qkv = _matmul(x_bf, w_bf, n_out=n_qkv, tn=256, scale=scale, scaled_blocks=hd // 256)
    q_idx = _matmul(x_bf, w_bf, n_out=n_qi, tn=256, col_block_offset=n_qkv // 256)
    tail = _small_matmul(x, _pad_cols(w_in[:, n_qkv + n_qi:]), jnp.zeros((1, LANES), F32))
    topk = min(IDX_TOPK_MAX, seq // 4)
    sel = _dsa_index(q_idx, tail, batch, seq, topk)
    band = _t5_band(t5_bias, min(ATTN_TQ, seq))
    o = _dsa_attn(qkv, sel, band, batch, seq)
    return _mm_res_ln(o, w_o.astype(BF16), x, ln_g[None, :], ln_b[None, :])


def _route_kernel(x_ref, w_ref, b_ref, o_ref, ot_ref, before_ref, cnt_ref, run_ref):
    i = pl.program_id(0)

    @pl.when(i == 0)
    def _():
        run_ref[...] = jnp.zeros(run_ref.shape, F32)

    tm = o_ref.shape[0]
    logit = _dot_precise(x_ref[...], w_ref[...]) + b_ref[...]
    lane = lax.broadcasted_iota(I32, logit.shape, 1)
    big = jnp.int32(LANES)

    def first_argmax(mask):
        top = jnp.max(jnp.where(mask, logit, -jnp.inf), axis=1, keepdims=True)
        idx = jnp.min(jnp.where(mask & (logit == top), lane, big), axis=1, keepdims=True)
        return top, idx

    is_group = lane < MOE_GROUPS
    g_top, g_sel = first_argmax(is_group)
    p_g = 1.0 / jnp.sum(jnp.where(is_group, jnp.exp(logit - g_top), 0.0), axis=1, keepdims=True)
    lo = MOE_GROUPS + g_sel * MOE_EXPERTS_PER_GROUP
    in_group = (lane >= lo) & (lane < lo + MOE_EXPERTS_PER_GROUP)
    v1, i1 = first_argmax(in_group)
    v2, i2 = first_argmax(in_group & (lane != i1))
    ex = jnp.exp(v2 - v1)
    w1 = 1.0 / (1.0 + ex)
    w2 = ex / (1.0 + ex)
    e1 = i1 - MOE_GROUPS
    e2 = i2 - MOE_GROUPS
    pick1 = lane == e1
    pick2 = lane == e2
    both = jnp.where(pick1 | pick2, 1.0, 0.0)
    earlier = lax.broadcasted_iota(I32, (tm, tm), 1) < lax.broadcasted_iota(I32, (tm, tm), 0)
    before = jnp.dot(jnp.where(earlier, 1.0, 0.0).astype(BF16), both.astype(BF16),
                     preferred_element_type=F32) + run_ref[...]
    r1 = jnp.sum(jnp.where(pick1, before, 0.0), axis=1, keepdims=True)
    r2 = jnp.sum(jnp.where(pick2, before, 0.0), axis=1, keepdims=True)
    before_ref[...] = run_ref[...]
    run_ref[...] = run_ref[...] + jnp.sum(both, axis=0, keepdims=True)
    cnt_ref[...] = run_ref[...]
    out = jnp.where(lane == 0, e1.astype(F32), 0.0)
    out = jnp.where(lane == 1, e2.astype(F32), out)
    out = jnp.where(lane == 2, w1 * p_g, out)
    out = jnp.where(lane == 3, w2 * p_g, out)
    out = jnp.where(lane == 4, r1, out)
    out = jnp.where(lane == 5, r2, out)
    o_ref[...] = out
    ot_ref[...] = out.T[:8, :]


def _route(x, w, b, tm):
    m, k = x.shape
    n_t = m // tm
    return pl.pallas_call(
        _route_kernel,
        name="moe_route",
        grid=(n_t,),
        in_specs=[pl.BlockSpec((tm, k), lambda i: (i, 0)),
                  pl.BlockSpec((k, LANES), lambda i: (0, 0)),
                  pl.BlockSpec((1, LANES), lambda i: (0, 0))],
        out_specs=[pl.BlockSpec((tm, LANES), lambda i: (i, 0)),
                   pl.BlockSpec((8, tm), lambda i: (0, i)),
                   pl.BlockSpec((None, 1, LANES), lambda i: (i, 0, 0)),
                   pl.BlockSpec((1, LANES), lambda i: (0, 0))],
        out_shape=[jax.ShapeDtypeStruct((m, LANES), F32), jax.ShapeDtypeStruct((8, m), F32),
                   jax.ShapeDtypeStruct((n_t, 1, LANES), F32), jax.ShapeDtypeStruct((1, LANES), F32)],
        scratch_shapes=[pltpu.VMEM((1, LANES), F32)],
        compiler_params=_params(("arbitrary",)),
    )(x, w, b)


def _dispatch_tables(before, counts, n_tokens, tm):
    n_t = n_tokens // tm
    before = before[:, 0, :MOE_EXPERTS].astype(I32)
    total = counts[0, :MOE_EXPERTS].astype(I32)
    after = jnp.concatenate([before[1:], total[None, :]], axis=0)
    length = after - before
    length_al = (length + ROW_ALIGN - 1) // ROW_ALIGN * ROW_ALIGN
    t_ids = jnp.arange(n_t, dtype=I32)
    e_ids = jnp.arange(MOE_EXPERTS, dtype=I32)
    before_al = jnp.sum(jnp.where((t_ids[None, :] < t_ids[:, None])[:, :, None], length_al[None, :, :], 0), axis=1)
    total_al = jnp.sum(length_al, axis=0)
    seg = (total_al + tm - 1) // tm * tm
    ends = jnp.sum(jnp.where(e_ids[None, :] <= e_ids[:, None], seg[None, :], 0), axis=1)
    starts = ends - seg
    off = jnp.sum(jnp.where((e_ids[None, :] < e_ids[:, None])[None, :, :], length_al[:, None, :], 0), axis=2)
    used = jnp.sum(length_al, axis=1)
    n_chunks = used // ROW_ALIGN
    dstart = starts[None, :] + before_al
    c_row = jnp.arange(TILE_CHUNKS, dtype=I32) * ROW_ALIGN
    c_exp = jnp.sum(((off + length_al)[:, None, :] <= c_row[None, :, None]).astype(I32), axis=2)
    c_exp = jnp.minimum(c_exp, MOE_EXPERTS - 1)
    onehot = c_exp[:, :, None] == e_ids[None, None, :]
    chunk_row = jnp.sum(jnp.where(onehot, (dstart - off)[:, None, :], 0), axis=2) + c_row[None, :]
    pos_tab = off - before
    n_rows = _sorted_rows(n_tokens, tm)
    n_tiles = n_rows // tm
    tile_start = jnp.arange(n_tiles, dtype=I32) * tm
    tile_expert = jnp.minimum(jnp.sum((ends[None, :] <= tile_start[:, None]).astype(I32), axis=1), MOE_EXPERTS - 1)
    tile_valid = (tile_start < ends[-1]).astype(I32)
    tile_first = jnp.concatenate([jnp.ones((1,), I32), (tile_expert[1:] != tile_expert[:-1]).astype(I32)])
    later = (e_ids[None, :] > e_ids[:, None]) & (total[None, :] > 0)
    next_used = jnp.min(jnp.where(later, e_ids[None, :], MOE_EXPERTS), axis=1)
    next_used = jnp.where(next_used == MOE_EXPERTS, -1, next_used).astype(I32)
    tile_next = jnp.sum(jnp.where(tile_expert[:, None] == e_ids[None, :], next_used[None, :], 0), axis=1)
    tail_first = starts + total_al
    tail_chunks = (seg - total_al) // ROW_ALIGN
    return dict(n_chunks=n_chunks.astype(I32), chunk_row=chunk_row.reshape(-1).astype(I32),
                pos_tab=pos_tab.reshape(-1).astype(I32), pos_rows=pos_tab.astype(F32),
                tile_expert=tile_expert.astype(I32), tile_valid=tile_valid, tile_first=tile_first,
                tile_next=tile_next.astype(I32), tail_first=tail_first.astype(I32),
                tail_chunks=tail_chunks.astype(I32))


def _sorted_rows(n_tokens, tm):
    n_t = n_tokens // tm
    bound = 2 * n_tokens + n_t * MOE_EXPERTS * (ROW_ALIGN - 1) + MOE_EXPERTS * (tm - ROW_ALIGN) + 1
    return (bound + tm - 1) // tm * tm


def _chunk_copy_out(buf, slot, c, row, hbm, sem):
    src = buf.at[slot, pl.ds(pl.multiple_of(c * ROW_ALIGN, ROW_ALIGN), ROW_ALIGN)]
    return pltpu.make_async_copy(src, hbm.at[pl.ds(pl.multiple_of(row, ROW_ALIGN), ROW_ALIGN)], sem.at[slot])


def _chunk_copy_in(hbm, row, buf, slot, c, sem):
    dst = buf.at[slot, pl.ds(pl.multiple_of(c * ROW_ALIGN, ROW_ALIGN), ROW_ALIGN)]
    return pltpu.make_async_copy(hbm.at[pl.ds(pl.multiple_of(row, ROW_ALIGN), ROW_ALIGN)], dst, sem.at[slot])


def _moe_dispatch_kernel(nch_ref, crow_ref, ptab_ref, tfirst_ref, tchunks_ref, tvalid_ref,
                         x_ref, rt_ref, xs_hbm, buf, zeros, sem, sem_z, *, tm, n_tiles):
    i = pl.program_id(0)
    n_t = pl.num_programs(0)
    slot = i % 2

    def wait_tile(tile, s):
        def body(c, carry):
            _chunk_copy_out(buf, s, 0, 0, xs_hbm, sem).wait()
            return carry
        lax.fori_loop(0, nch_ref[tile], body, 0)

    def zero_tail(e, k):
        return pltpu.make_async_copy(
            zeros.at[pl.ds(0, ROW_ALIGN)],
            xs_hbm.at[pl.ds(pl.multiple_of(tfirst_ref[e] + k * ROW_ALIGN, ROW_ALIGN), ROW_ALIGN)], sem_z.at[0])

    def zero_tile(t):
        return pltpu.make_async_copy(zeros, xs_hbm.at[pl.ds(pl.multiple_of(t * tm, tm), tm)], sem_z.at[1])

    def for_each_fill(tail_fn, tile_fn):
        def per_expert(e, carry):
            def per_chunk(k, c2):
                tail_fn(e, k)
                return c2
            lax.fori_loop(0, tchunks_ref[e], per_chunk, 0)
            return carry
        lax.fori_loop(0, MOE_EXPERTS, per_expert, 0)

        def per_tile(t, carry):
            @pl.when(tvalid_ref[t] == 0)
            def _():
                tile_fn(t)
            return carry
        lax.fori_loop(0, n_tiles, per_tile, 0)

    @pl.when(i == 0)
    def _():
        zeros[...] = jnp.zeros(zeros.shape, zeros.dtype)
        for_each_fill(lambda e, k: zero_tail(e, k).start(), lambda t: zero_tile(t).start())

    @pl.when(i >= 2)
    def _():
        wait_tile(i - 2, slot)

    pos1 = rt_ref[4:5, :].astype(I32)
    pos2 = rt_ref[5:6, :].astype(I32)
    e1 = rt_ref[0:1, :].astype(I32)
    e2 = rt_ref[1:2, :].astype(I32)
    for e in range(MOE_EXPERTS):
        shift = ptab_ref[i * MOE_EXPERTS + e]
        pos1 = pos1 + jnp.where(e1 == e, shift, 0)
        pos2 = pos2 + jnp.where(e2 == e, shift, 0)
    blk = 256
    for k in range(TILE_ROWS // blk):
        row = lax.broadcasted_iota(I32, (blk, tm), 0) + k * blk
        perm = jnp.where((row == pos1) | (row == pos2), 1.0, 0.0).astype(BF16)
        buf[slot, k * blk:(k + 1) * blk, :] = jnp.dot(perm, x_ref[...], preferred_element_type=F32).astype(BF16)

    def send(c, carry):
        _chunk_copy_out(buf, slot, c, crow_ref[i * TILE_CHUNKS + c], xs_hbm, sem).start()
        return carry
    lax.fori_loop(0, nch_ref[i], send, 0)

    @pl.when(i == n_t - 1)
    def _():
        @pl.when(i >= 1)
        def _():
            wait_tile(i - 1, 1 - slot)
        wait_tile(i, slot)
        for_each_fill(lambda e, k: zero_tail(e, k).wait(), lambda t: zero_tile(t).wait())


def _moe_dispatch(x_bf, route_t, tables, tm):
    m, d = x_bf.shape
    n_rows = _sorted_rows(m, tm)
    grid_spec = pltpu.PrefetchScalarGridSpec(
        num_scalar_prefetch=6,
        grid=(m // tm,),
        in_specs=[pl.BlockSpec((tm, d), lambda i, *_: (i, 0)),
                  pl.BlockSpec((8, tm), lambda i, *_: (0, i))],
        out_specs=pl.BlockSpec(memory_space=pl.ANY),
        scratch_shapes=[pltpu.VMEM((2, TILE_ROWS, d), BF16), pltpu.VMEM((tm, d), BF16),
                        pltpu.SemaphoreType.DMA((2,)), pltpu.SemaphoreType.DMA((2,))],
    )
    return pl.pallas_call(
        functools.partial(_moe_dispatch_kernel, tm=tm, n_tiles=n_rows // tm),
        name="moe_dispatch",
        grid_spec=grid_spec,
        out_shape=jax.ShapeDtypeStruct((n_rows, d), BF16),
        compiler_params=_params(("arbitrary",)),
    )(tables["n_chunks"], tables["chunk_row"], tables["pos_tab"], tables["tail_first"], tables["tail_chunks"],
      tables["tile_valid"], x_bf, route_t)


def _cast_rows(src_ref, dst_ref):
    rows = 32 * 8 * LANES // src_ref.shape[1]

    def body(i, carry):
        at = pl.ds(pl.multiple_of(i * rows, rows), rows)
        dst_ref[at, :] = src_ref[at, :].astype(dst_ref.dtype)
        return carry
    lax.fori_loop(0, src_ref.shape[0] // rows, body, 0, unroll=2)


def _moe_ffn_kernel(te_ref, tv_ref, tf_ref, tn_ref, x_ref, wg_hbm, wu_hbm, wd_hbm, o_ref,
                    wg_f, wu_f, wd_f, wg_bf, wu_bf, wd_bf, sem_w, *, layer):
    j = pl.program_id(0)
    valid = tv_ref[j] == 1

    def weight_copies(e):
        return (pltpu.make_async_copy(wg_hbm.at[layer, e], wg_f, sem_w.at[0]),
                pltpu.make_async_copy(wu_hbm.at[layer, e], wu_f, sem_w.at[1]),
                pltpu.make_async_copy(wd_hbm.at[layer, e], wd_f, sem_w.at[2]))

    @pl.when(j == 0)
    def _():
        for c in weight_copies(te_ref[0]):
            c.start()

    @pl.when(valid & (tf_ref[j] == 1))
    def _():
        for c in weight_copies(te_ref[j]):
            c.wait()
        _cast_rows(wg_f, wg_bf)
        _cast_rows(wu_f, wu_bf)
        _cast_rows(wd_f, wd_bf)

        @pl.when(tn_ref[j] >= 0)
        def _():
            for c in weight_copies(tn_ref[j]):
                c.start()

    @pl.when(valid)
    def _():
        xb = x_ref[...]
        hg = jnp.dot(xb, wg_bf[...], preferred_element_type=F32)
        hu = jnp.dot(xb, wu_bf[...], preferred_element_type=F32)
        act = (hg * (1.0 / (1.0 + jnp.exp(-hg))) * hu).astype(BF16)
        o_ref[...] = jnp.dot(act, wd_bf[...], preferred_element_type=F32).astype(o_ref.dtype)

    @pl.when(jnp.logical_not(valid))
    def _():
        o_ref[...] = jnp.zeros(o_ref.shape, o_ref.dtype)


def _moe_ffn(x_sorted, w_gate, w_up, w_down, layer, tables, tm):
    n_rows, d = x_sorted.shape
    f = w_gate.shape[3]
    any_spec = pl.BlockSpec(memory_space=pl.ANY)
    grid_spec = pltpu.PrefetchScalarGridSpec(
        num_scalar_prefetch=4,
        grid=(n_rows // tm,),
        in_specs=[pl.BlockSpec((tm, d), lambda j, *_: (j, 0)), any_spec, any_spec, any_spec],
        out_specs=pl.BlockSpec((tm, d), lambda j, *_: (j, 0)),
        scratch_shapes=[pltpu.VMEM((d, f), F32), pltpu.VMEM((d, f), F32), pltpu.VMEM((f, d), F32),
                        pltpu.VMEM((d, f), BF16), pltpu.VMEM((d, f), BF16), pltpu.VMEM((f, d), BF16),
                        pltpu.SemaphoreType.DMA((3,))],
    )
    return pl.pallas_call(
        functools.partial(_moe_ffn_kernel, layer=layer),
        name="moe_ffn",
        grid_spec=grid_spec,
        out_shape=jax.ShapeDtypeStruct((n_rows, d), BF16),
        compiler_params=_params(("arbitrary",)),
    )(tables["tile_expert"], tables["tile_valid"], tables["tile_first"], tables["tile_next"],
      x_sorted, w_gate, w_up, w_down)


def _moe_combine_kernel(nch_ref, crow_ref, ys_hbm, x_ref, r_ref, ptab_ref, g_ref, b_ref, o_ref, obf_ref,
                        buf, sem, *, tm):
    i = pl.program_id(0)
    n_t = pl.num_programs(0)
    slot = i % 2

    def fetch(tile, s):
        def body(c, carry):
            _chunk_copy_in(ys_hbm, crow_ref[tile * TILE_CHUNKS + c], buf, s, c, sem).start()
            return carry
        lax.fori_loop(0, nch_ref[tile], body, 0)

    @pl.when(i == 0)
    def _():
        buf[...] = jnp.zeros(buf.shape, buf.dtype)
        fetch(0, 0)

    @pl.when(i + 1 < n_t)
    def _():
        fetch(i + 1, 1 - slot)

    def arrived(c, carry):
        _chunk_copy_in(ys_hbm, 0, buf, slot, 0, sem).wait()
        return carry
    lax.fori_loop(0, nch_ref[i], arrived, 0)

    lane = lax.broadcasted_iota(I32, (tm, LANES), 1)
    shift = ptab_ref[...]
    e1 = r_ref[:, 0:1].astype(I32)
    e2 = r_ref[:, 1:2].astype(I32)
    pos1 = (r_ref[:, 4:5] + jnp.sum(jnp.where(lane == e1, shift, 0.0), axis=1, keepdims=True)).astype(I32)
    pos2 = (r_ref[:, 5:6] + jnp.sum(jnp.where(lane == e2, shift, 0.0), axis=1, keepdims=True)).astype(I32)
    col = lax.broadcasted_iota(I32, (tm, TILE_ROWS), 1)
    pick = jnp.where(col == pos1, r_ref[:, 2:3], 0.0) + jnp.where(col == pos2, r_ref[:, 3:4], 0.0)
    pick_hi = pick.astype(BF16)
    pick_lo = (pick - pick_hi.astype(F32)).astype(BF16)
    rows = buf[slot]
    y = jnp.dot(pick_lo, rows, preferred_element_type=F32) + jnp.dot(pick_hi, rows, preferred_element_type=F32)
    out = _layernorm_rows(DEEPNORM_ALPHA * x_ref[...] + y, g_ref[...], b_ref[...])
    o_ref[...] = out
    obf_ref[...] = out.astype(BF16)


def _moe_combine(y_sorted, x, route, tables, g, b, tm):
    m, d = x.shape
    grid_spec = pltpu.PrefetchScalarGridSpec(
        num_scalar_prefetch=2,
        grid=(m // tm,),
        in_specs=[pl.BlockSpec(memory_space=pl.ANY),
                  pl.BlockSpec((tm, d), lambda i, *_: (i, 0)),
                  pl.BlockSpec((tm, LANES), lambda i, *_: (i, 0)),
                  pl.BlockSpec((None, 1, LANES), lambda i, *_: (i, 0, 0)),
                  pl.BlockSpec((1, d), lambda i, *_: (0, 0)),
                  pl.BlockSpec((1, d), lambda i, *_: (0, 0))],
        out_specs=[pl.BlockSpec((tm, d), lambda i, *_: (i, 0)),
                   pl.BlockSpec((tm, d), lambda i, *_: (i, 0))],
        scratch_shapes=[pltpu.VMEM((2, TILE_ROWS, d), BF16), pltpu.SemaphoreType.DMA((2,))],
    )
    pos_rows = _pad_cols(tables["pos_rows"])[:, None, :]
    return pl.pallas_call(
        functools.partial(_moe_combine_kernel, tm=tm),
        name="moe_combine",
        grid_spec=grid_spec,
        out_shape=[jax.ShapeDtypeStruct((m, d), F32), jax.ShapeDtypeStruct((m, d), BF16)],
        compiler_params=_params(("arbitrary",)),
    )(tables["n_chunks"], tables["chunk_row"], y_sorted, x, route, pos_rows, g, b)


def _moe_layer(x, x_bf, w_group, b_group, w_router, b_router, w_gate, w_up, w_down, layer, ln_g, ln_b):
    n_tokens = x.shape[0]
    tm = min(MOE_TM, n_tokens)
    w_route = _pad_cols(jnp.concatenate([w_group, w_router], axis=1))
    b_route = _pad_cols(jnp.concatenate([b_group, b_router])[None, :])
    route, route_t, before, counts = _route(x, w_route, b_route, tm)
    tables = _dispatch_tables(before, counts, n_tokens, tm)
    x_sorted = _moe_dispatch(x_bf, route_t, tables, tm)
    y_sorted = _moe_ffn(x_sorted, w_gate, w_up, w_down, layer, tables, tm)
    return _moe_combine(y_sorted, x, route, tables, ln_g[None, :], ln_b[None, :], tm)


def kernel(x, t5_rel_bias, mla_w_in, mla_q_norm, mla_kv_norm, mla_w_q_up, mla_w_kv_up, mla_w_o, fox_w_in, fox_b_f, fox_w_o, dsa_w_in, dsa_w_o, ln_g, ln_b, moe_w_group, moe_b_group, moe_w_router, moe_b_router, moe_w_gate, moe_w_up, moe_w_down):
    batch, seq, d = x.shape
    x = x.reshape(batch * seq, d)
    x_bf = x.astype(BF16)
    for layer in range(DEPTH):
        kind = layer % N_MIXERS
        j = layer // N_MIXERS
        g0, b0 = ln_g[layer, 0], ln_b[layer, 0]
        if kind == 0:
            x, x_bf = _mla_layer(x, x_bf, mla_w_in[j], mla_q_norm[j], mla_kv_norm[j], mla_w_q_up[j],
                                 mla_w_kv_up[j], mla_w_o[j], g0, b0, batch, seq)
        elif kind == 1:
            x, x_bf = _fox_layer(x, x_bf, fox_w_in[j], fox_b_f[j], fox_w_o[j], g0, b0, batch, seq)
        else:
            x, x_bf = _dsa_layer(x, x_bf, t5_rel_bias, dsa_w_in[j], dsa_w_o[j], g0, b0, batch, seq)
        x, x_bf = _moe_layer(x, x_bf, moe_w_group[layer], moe_b_group[layer], moe_w_router[layer], moe_b_router[layer],
                             moe_w_gate, moe_w_up, moe_w_down, layer, ln_g[layer, 1], ln_b[layer, 1])
    return x.reshape(batch, seq, d)
```

```python
import functools
import math

import numpy as np
import jax
import jax.numpy as jnp
from jax import lax
from jax.experimental import pallas as pl
from jax.experimental.pallas import tpu as pltpu

F32 = jnp.float32
BF16 = jnp.bfloat16
I32 = jnp.int32

D_MODEL = 2048
DEPTH = 4
CHUNK = 64
N_MIXERS = 3
DEEPNORM_ALPHA = (2.0 * DEPTH) ** 0.25

HEADS = 16
HEAD_DIM = 128
MLA_NOPE_DIM = 128
MLA_ROPE_DIM = 64
MLA_Q_RANK = 512
MLA_KV_RANK = 512
MLA_QK_PAD = 256
ROPE_THETA = 10000.0

IDX_HEADS = 16
IDX_DIM = 64
IDX_TOPK_MAX = 256

T5_BUCKETS = 32
T5_MAX_DISTANCE = 128
T5_FAR_BUCKET = T5_BUCKETS // 2 - 1
T5_BAND_BACK = 128

MOE_GROUPS = 4
MOE_EXPERTS_PER_GROUP = 8
MOE_EXPERTS = MOE_GROUPS * MOE_EXPERTS_PER_GROUP
MOE_D_FF = 512

LN_EPS = 1e-5
RMS_EPS = 1e-6
NEG_INF = -1e30

LANES = 128
VMEM_LIMIT = 56 * 1024 * 1024

ATTN_TQ = 256
MOE_TM = 256
ROW_ALIGN = 16
TILE_ROWS = -(-(2 * MOE_TM + MOE_EXPERTS * (ROW_ALIGN - 1)) // 256) * 256
TILE_CHUNKS = TILE_ROWS // ROW_ALIGN


def _params(semantics, vmem=VMEM_LIMIT):
    return pltpu.CompilerParams(dimension_semantics=semantics, vmem_limit_bytes=vmem)


def _mm_kernel(a_ref, w_ref, o_ref, *, scale, scaled_blocks):
    acc = jnp.dot(a_ref[...], w_ref[...], preferred_element_type=F32)
    if scaled_blocks:
        acc = acc * jnp.where(pl.program_id(0) < scaled_blocks, scale, 1.0).astype(F32)
    o_ref[...] = acc.astype(o_ref.dtype)


def _matmul(a, w, *, n_out, tn, tm=1024, col_block_offset=0, scale=1.0, scaled_blocks=0, out_dtype=BF16):
    m, k = a.shape
    tm = min(tm, m)
    kern = functools.partial(_mm_kernel, scale=scale, scaled_blocks=scaled_blocks)
    return pl.pallas_call(
        kern,
        name="proj_mm",
        grid=(n_out // tn, m // tm),
        in_specs=[pl.BlockSpec((tm, k), lambda j, i: (i, 0)),
                  pl.BlockSpec((k, tn), lambda j, i: (0, j + col_block_offset))],
        out_specs=pl.BlockSpec((tm, tn), lambda j, i: (i, j)),
        out_shape=jax.ShapeDtypeStruct((m, n_out), out_dtype),
        compiler_params=_params(("arbitrary", "arbitrary")),
    )(a, w)


def _split3(v):
    hi = v.astype(BF16)
    r1 = v - hi.astype(F32)
    mid = r1.astype(BF16)
    lo = (r1 - mid.astype(F32)).astype(BF16)
    return hi, mid, lo


def _dot_precise(x, w):
    xh = x.astype(BF16)
    xm = (x - xh.astype(F32)).astype(BF16)
    wh = w.astype(BF16)
    wm = (w - wh.astype(F32)).astype(BF16)
    d = lambda p, q: jnp.dot(p, q, preferred_element_type=F32)
    return (d(xm, wh) + d(xh, wm)) + d(xh, wh)


def _small_mm_kernel(x_ref, w_ref, b_ref, o_ref):
    o_ref[...] = _dot_precise(x_ref[...], w_ref[...]) + b_ref[...]


def _small_matmul(x, w, b, tm=512):
    m, k = x.shape
    tm = min(tm, m)
    return pl.pallas_call(
        _small_mm_kernel,
        name="small_mm",
        grid=(m // tm,),
        in_specs=[pl.BlockSpec((tm, k), lambda i: (i, 0)),
                  pl.BlockSpec((k, LANES), lambda i: (0, 0)),
                  pl.BlockSpec((1, LANES), lambda i: (0, 0))],
        out_specs=pl.BlockSpec((tm, LANES), lambda i: (i, 0)),
        out_shape=jax.ShapeDtypeStruct((m, LANES), F32),
        compiler_params=_params(("arbitrary",)),
    )(x, w, b)


def _pad_cols(w, width=LANES):
    return jnp.pad(w, ((0, 0), (0, width - w.shape[1])))


def _layernorm_rows(z, g, b):
    mu = jnp.mean(z, axis=-1, keepdims=True)
    zc = z - mu
    var = jnp.mean(zc * zc, axis=-1, keepdims=True)
    return zc * lax.rsqrt(var + LN_EPS) * g + b


def _mm_res_ln_kernel(a_ref, w_ref, x_ref, g_ref, b_ref, o_ref, obf_ref):
    y = jnp.dot(a_ref[...], w_ref[...], preferred_element_type=F32)
    out = _layernorm_rows(DEEPNORM_ALPHA * x_ref[...] + y, g_ref[...], b_ref[...])
    o_ref[...] = out
    obf_ref[...] = out.astype(BF16)


def _mm_res_ln(a, w, x, g, b, tm=512):
    m, k = a.shape
    d = w.shape[1]
    tm = min(tm, m)
    return pl.pallas_call(
        _mm_res_ln_kernel,
        name="out_proj_res_ln",
        grid=(m // tm,),
        in_specs=[pl.BlockSpec((tm, k), lambda i: (i, 0)),
                  pl.BlockSpec((k, d), lambda i: (0, 0)),
                  pl.BlockSpec((tm, d), lambda i: (i, 0)),
                  pl.BlockSpec((1, d), lambda i: (0, 0)),
                  pl.BlockSpec((1, d), lambda i: (0, 0))],
        out_specs=[pl.BlockSpec((tm, d), lambda i: (i, 0)),
                   pl.BlockSpec((tm, d), lambda i: (i, 0))],
        out_shape=[jax.ShapeDtypeStruct((m, d), F32), jax.ShapeDtypeStruct((m, d), BF16)],
        compiler_params=_params(("arbitrary",)),
    )(a, w, x, g, b)


def _rope_tables(seq):
    half = MLA_ROPE_DIM // 2
    inv_freq = ROPE_THETA ** (-jnp.arange(half, dtype=F32) / half)
    ang = jnp.arange(seq, dtype=I32).astype(F32)[:, None] * inv_freq[None, :]
    cos, sin = jnp.cos(ang), jnp.sin(ang)
    z = jnp.zeros_like(cos)
    cos_t = jnp.concatenate([cos, cos, z, z], axis=1)
    sin_a = jnp.concatenate([-sin, z, z, z], axis=1)
    sin_b = jnp.concatenate([z, sin, z, z], axis=1)
    return cos_t, sin_a, sin_b


def _rope_lanes(r, cos_t, sin_a, sin_b):
    from_right = pltpu.roll(r, LANES - MLA_ROPE_DIM // 2, 1)
    from_left = pltpu.roll(r, MLA_ROPE_DIM // 2, 1)
    return r * cos_t + from_right * sin_a + from_left * sin_b


def _rms_rows(v, g):
    return v * lax.rsqrt(jnp.mean(v * v, axis=-1, keepdims=True) + RMS_EPS) * g


def _mla_in_kernel(x_ref, w_ref, qn_ref, kvn_ref, cos_ref, sina_ref, sinb_ref, cq_ref, ckv_ref, kr_ref):
    h = jnp.dot(x_ref[...], w_ref[...], preferred_element_type=F32)
    cq_ref[...] = _rms_rows(h[:, :MLA_Q_RANK], qn_ref[...]).astype(BF16)
    ckv_ref[...] = _rms_rows(h[:, MLA_Q_RANK:MLA_Q_RANK + MLA_KV_RANK], kvn_ref[...]).astype(BF16)
    r = h[:, MLA_Q_RANK + MLA_KV_RANK:]
    kr_ref[...] = _rope_lanes(r, cos_ref[...], sina_ref[...], sinb_ref[...]).astype(BF16)


def _mla_in(x_bf, w_in_pad, q_norm, kv_norm, tables, seq, tm=512):
    m, k = x_bf.shape
    tm = min(tm, seq)
    nw = w_in_pad.shape[1]
    per_seq = seq // tm
    tab_spec = pl.BlockSpec((tm, LANES), lambda i: (i % per_seq, 0))
    return pl.pallas_call(
        _mla_in_kernel,
        name="mla_in",
        grid=(m // tm,),
        in_specs=[pl.BlockSpec((tm, k), lambda i: (i, 0)),
                  pl.BlockSpec((k, nw), lambda i: (0, 0)),
                  pl.BlockSpec((1, MLA_Q_RANK), lambda i: (0, 0)),
                  pl.BlockSpec((1, MLA_KV_RANK), lambda i: (0, 0)),
                  tab_spec, tab_spec, tab_spec],
        out_specs=[pl.BlockSpec((tm, MLA_Q_RANK), lambda i: (i, 0)),
                   pl.BlockSpec((tm, MLA_KV_RANK), lambda i: (i, 0)),
                   pl.BlockSpec((tm, LANES), lambda i: (i, 0))],
        out_shape=[jax.ShapeDtypeStruct((m, MLA_Q_RANK), BF16),
                   jax.ShapeDtypeStruct((m, MLA_KV_RANK), BF16),
                   jax.ShapeDtypeStruct((m, LANES), BF16)],
        compiler_params=_params(("arbitrary",)),
    )(x_bf, w_in_pad, q_norm, kv_norm, *tables)


def _fill_v_aug(va_ref, v_ref):
    va_ref[:, :HEAD_DIM] = v_ref[...]
    va_ref[:, HEAD_DIM:] = jnp.ones((v_ref.shape[0], HEAD_DIM), BF16)


def _softmax_pv(pieces):
    m = None
    for s, _ in pieces:
        pm = jnp.max(s, axis=1, keepdims=True)
        m = pm if m is None else jnp.maximum(m, pm)
    acc = None
    for s, v in pieces:
        t = jnp.dot(jnp.exp((s - m).astype(BF16)), v, preferred_element_type=F32)
        acc = t if acc is None else acc + t
    return acc[:, :HEAD_DIM] / acc[:, HEAD_DIM:HEAD_DIM + 1]


def _nt_dot(q, k):
    return lax.dot_general(q, k, (((1,), (1,)), ((), ())), preferred_element_type=F32)


def _attend_tiles(n_tiles, scores, finish):
    s_next = scores(0)
    for i in range(n_tiles):
        s = s_next
        if i + 1 < n_tiles:
            s_next = scores(i + 1)
        finish(i, s)


def _mla_attn_kernel(q_ref, kn_ref, v_ref, kr_ref, cos_ref, sina_ref, sinb_ref, o_ref, qc_ref, kc_ref, va_ref,
                     *, tq):
    seq = q_ref.shape[0]
    r = q_ref[:, MLA_NOPE_DIM:].astype(F32)
    qc_ref[:, :MLA_NOPE_DIM] = q_ref[:, :MLA_NOPE_DIM]
    qc_ref[:, MLA_NOPE_DIM:] = _rope_lanes(r, cos_ref[...], sina_ref[...], sinb_ref[...]).astype(BF16)
    kc_ref[:, :MLA_NOPE_DIM] = kn_ref[...]
    kc_ref[:, MLA_NOPE_DIM:] = kr_ref[...]
    _fill_v_aug(va_ref, v_ref)
    row = lax.broadcasted_iota(I32, (tq, tq), 0)
    col = lax.broadcasted_iota(I32, (tq, tq), 1)
    visible = (col // CHUNK) <= (row // CHUNK)
    def scores(i):
        return _nt_dot(qc_ref[i * tq:(i + 1) * tq, :], kc_ref[:(i + 1) * tq, :])

    def finish(i, s):
        lo, kv = i * tq, (i + 1) * tq
        pieces = [(jnp.where(visible, s[:, lo:], NEG_INF), va_ref[lo:kv, :])]
        if lo:
            pieces.append((s[:, :lo], va_ref[:lo, :]))
        o_ref[lo:kv, :] = _softmax_pv(pieces).astype(o_ref.dtype)

    _attend_tiles(seq // tq, scores, finish)


def _mla_attn(q, kv, kr, tables, batch, seq):
    tq = min(ATTN_TQ, seq)
    kern = functools.partial(_mla_attn_kernel, tq=tq)
    tab_spec = pl.BlockSpec((seq, LANES), lambda b, h: (0, 0))
    return pl.pallas_call(
        kern,
        name="mla_attn",
        grid=(batch, HEADS),
        in_specs=[pl.BlockSpec((seq, MLA_QK_PAD), lambda b, h: (b, h)),
                  pl.BlockSpec((seq, HEAD_DIM), lambda b, h: (b, 2 * h)),
                  pl.BlockSpec((seq, HEAD_DIM), lambda b, h: (b, 2 * h + 1)),
                  pl.BlockSpec((seq, LANES), lambda b, h: (b, 0)),
                  tab_spec, tab_spec, tab_spec],
        out_specs=pl.BlockSpec((seq, HEAD_DIM), lambda b, h: (b, h)),
        out_shape=jax.ShapeDtypeStruct((batch * seq, HEADS * HEAD_DIM), BF16),
        scratch_shapes=[pltpu.VMEM((seq, MLA_QK_PAD), BF16), pltpu.VMEM((seq, MLA_QK_PAD), BF16),
                        pltpu.VMEM((seq, 2 * HEAD_DIM), BF16)],
        compiler_params=_params(("arbitrary", "arbitrary")),
    )(q, kv, kv, kr, *tables)


def _mla_q_up_layout(w_q_up):
    rank = w_q_up.shape[0]
    w = w_q_up.reshape(rank, HEADS, MLA_NOPE_DIM + MLA_ROPE_DIM)
    w = jnp.pad(w, ((0, 0), (0, 0), (0, MLA_QK_PAD - MLA_NOPE_DIM - MLA_ROPE_DIM)))
    return w.reshape(rank, HEADS * MLA_QK_PAD)


def _mla_layer(x, x_bf, w_in, q_norm, kv_norm, w_q_up, w_kv_up, w_o, ln_g, ln_b, batch, seq):
    tables = _rope_tables(seq)
    w_in_pad = jnp.pad(w_in, ((0, 0), (0, MLA_Q_RANK + MLA_KV_RANK + LANES - w_in.shape[1]))).astype(BF16)
    cq, ckv, kr = _mla_in(x_bf, w_in_pad, q_norm[None, :], kv_norm[None, :], tables, seq)
    scale = (MLA_NOPE_DIM + MLA_ROPE_DIM) ** -0.5
    nq = HEADS * MLA_QK_PAD
    q = _matmul(cq, _mla_q_up_layout(w_q_up).astype(BF16), n_out=nq, tn=1024, scale=scale, scaled_blocks=nq // 1024)
    kv = _matmul(ckv, w_kv_up.astype(BF16), n_out=w_kv_up.shape[1], tn=1024)
    o = _mla_attn(q, kv, kr, tables, batch, seq)
    return _mm_res_ln(o, w_o.astype(BF16), x, ln_g[None, :], ln_b[None, :])


def _log_sigmoid(z):
    return jnp.minimum(z, 0.0) - jnp.log1p(jnp.exp(-jnp.abs(z)))


def _fox_gate_kernel(f_ref, c_ref, *, blk):
    seq = f_ref.shape[0]
    tri = (lax.broadcasted_iota(I32, (blk, blk), 1) <= lax.broadcasted_iota(I32, (blk, blk), 0)).astype(BF16)
    carry = jnp.zeros((1, LANES), F32)
    for j in range(seq // blk):
        lf = _log_sigmoid(f_ref[j * blk:(j + 1) * blk, :])
        hi, mid, lo = _split3(lf)
        d = lambda p: jnp.dot(tri, p, preferred_element_type=F32)
        c = (d(lo) + d(mid)) + d(hi) + carry
        c_ref[j * blk:(j + 1) * blk, :] = c
        carry = c[blk - 1:blk, :]


def _fox_gate(f_logit, batch, seq):
    blk = min(256, seq)
    return pl.pallas_call(
        functools.partial(_fox_gate_kernel, blk=blk),
        name="fox_gate",
        grid=(batch,),
        in_specs=[pl.BlockSpec((seq, LANES), lambda b: (b, 0))],
        out_specs=pl.BlockSpec((seq, LANES), lambda b: (b, 0)),
        out_shape=jax.ShapeDtypeStruct((batch * seq, LANES), F32),
        compiler_params=_params(("arbitrary",)),
    )(f_logit)


def _fox_attn_kernel(q_ref, k_ref, v_ref, c_ref, o_ref, qa_ref, ka_ref, va_ref, *, tq):
    seq = q_ref.shape[0]
    h = pl.program_id(1)
    lane = lax.broadcasted_iota(I32, (seq, LANES), 1)
    c = jnp.sum(jnp.where(lane == h, c_ref[...], 0.0), axis=1, keepdims=True)
    terms = [t.astype(F32) for t in _split3(c)]
    q_side = jnp.where((lane >= 3) & (lane < 6), 1.0, 0.0)
    k_side = jnp.where(lane < 3, 1.0, 0.0)
    for j, t in enumerate(terms):
        q_side = jnp.where(lane == j, t, q_side)
        k_side = jnp.where(lane == 3 + j, -t, k_side)
    qa_ref[:, :HEAD_DIM] = q_ref[...]
    qa_ref[:, HEAD_DIM:] = q_side.astype(BF16)
    ka_ref[:, :HEAD_DIM] = k_ref[...]
    ka_ref[:, HEAD_DIM:] = k_side.astype(BF16)
    _fill_v_aug(va_ref, v_ref)
    row = lax.broadcasted_iota(I32, (tq, tq), 0)
    col = lax.broadcasted_iota(I32, (tq, tq), 1)
    def scores(i):
        return _nt_dot(qa_ref[i * tq:(i + 1) * tq, :], ka_ref[:(i + 1) * tq, :])

    def finish(i, s):
        lo, kv = i * tq, (i + 1) * tq
        pieces = [(jnp.where(col <= row, s[:, lo:], NEG_INF), va_ref[lo:kv, :])]
        if lo:
            pieces.append((s[:, :lo], va_ref[:lo, :]))
        o_ref[lo:kv, :] = _softmax_pv(pieces).astype(o_ref.dtype)

    _attend_tiles(seq // tq, scores, finish)


def _fox_attn(qkv, c_col, batch, seq):
    tq = min(ATTN_TQ, seq)
    wide = pltpu.VMEM((seq, 2 * HEAD_DIM), BF16)
    return pl.pallas_call(
        functools.partial(_fox_attn_kernel, tq=tq),
        name="fox_attn",
        grid=(batch, HEADS),
        in_specs=[pl.BlockSpec((seq, HEAD_DIM), lambda b, h: (b, h)),
                  pl.BlockSpec((seq, HEAD_DIM), lambda b, h: (b, HEADS + h)),
                  pl.BlockSpec((seq, HEAD_DIM), lambda b, h: (b, 2 * HEADS + h)),
                  pl.BlockSpec((seq, LANES), lambda b, h: (b, 0))],
        out_specs=pl.BlockSpec((seq, HEAD_DIM), lambda b, h: (b, h)),
        out_shape=jax.ShapeDtypeStruct((batch * seq, HEADS * HEAD_DIM), BF16),
        scratch_shapes=[wide, wide, wide],
        compiler_params=_params(("arbitrary", "arbitrary")),
    )(qkv, qkv, qkv, c_col)


def _fox_layer(x, x_bf, w_in, b_f, w_o, ln_g, ln_b, batch, seq):
    hd = HEADS * HEAD_DIM
    scale = HEAD_DIM ** -0.5
    qkv = _matmul(x_bf, w_in.astype(BF16), n_out=3 * hd, tn=1024, scale=scale, scaled_blocks=hd // 1024)
    f_logit = _small_matmul(x, _pad_cols(w_in[:, 3 * hd:]), _pad_cols(b_f[None, :]))
    o = _fox_attn(qkv, _fox_gate(f_logit, batch, seq), batch, seq)
    return _mm_res_ln(o, w_o.astype(BF16), x, ln_g[None, :], ln_b[None, :])


def _t5_bucket_table(tq):
    a = np.arange(tq, dtype=np.int64)[:, None]
    b = np.arange(tq + T5_BAND_BACK, dtype=np.int64)[None, :]
    rel = (b - T5_BAND_BACK) - a
    nb = T5_BUCKETS // 2
    max_exact = nb // 2
    ret = np.where(rel > 0, nb, 0)
    n = np.abs(rel)
    nf = np.maximum(n, 1).astype(np.float32)
    large = max_exact + (np.log(nf / np.float32(max_exact)) / np.float32(math.log(T5_MAX_DISTANCE / max_exact))
                         * np.float32(nb - max_exact)).astype(np.int32)
    large = np.minimum(large, nb - 1)
    return (ret + np.where(n < max_exact, n, large)).astype(np.int32)


def _t5_band_kernel(t5_ref, bucket_ref, o_ref):
    bucket = bucket_ref[...]

    def per_head(h, carry):
        acc = jnp.zeros(bucket.shape, F32)
        for b in range(T5_BUCKETS):
            acc = jnp.where(bucket == b, t5_ref[b, h], acc)
        o_ref[h] = acc - t5_ref[T5_FAR_BUCKET, h]
        return carry

    lax.fori_loop(0, HEADS, per_head, 0)


def _t5_band(t5_bias, tq):
    bucket = jnp.asarray(_t5_bucket_table(tq))
    width = tq + T5_BAND_BACK
    return pl.pallas_call(
        _t5_band_kernel,
        name="t5_band",
        grid=(1,),
        in_specs=[pl.BlockSpec(memory_space=pltpu.SMEM),
                  pl.BlockSpec((tq, width), lambda i: (0, 0))],
        out_specs=pl.BlockSpec((HEADS, tq, width), lambda i: (0, 0, 0)),
        out_shape=jax.ShapeDtypeStruct((HEADS, tq, width), F32),
        compiler_params=_params(("arbitrary",)),
    )(t5_bias, bucket)


def _sortable_key(v):
    bits = pltpu.bitcast(v, I32)
    return bits ^ ((bits >> 31) & 0x7FFFFFFF)


def _kth_largest_key(key_ref, kv, k):
    rows = key_ref.shape[0]

    def count_ge(t):
        t_b = jnp.broadcast_to(t, (rows, LANES))
        acc = jnp.zeros((rows, LANES), I32)
        for c in range(kv // LANES):
            acc = acc + jnp.where(key_ref[:, c * LANES:(c + 1) * LANES] >= t_b, 1, 0)
        return jnp.sum(acc, axis=1, keepdims=True)

    t0 = jnp.where(count_ge(jnp.zeros((rows, 1), I32)) >= k, 0, jnp.iinfo(jnp.int32).min).astype(I32)

    def body(it, t):
        cand = t | (jnp.int32(1) << (30 - it))
        return jnp.where(count_ge(cand) >= k, cand, t)

    return lax.fori_loop(0, 31, body, t0)


def _dsa_index_kernel(qi_ref, tail_ref, o_ref, key_ref, *, tq, topk):
    seq = tail_ref.shape[0]
    k_idx = tail_ref[:, :IDX_DIM].astype(BF16)
    for i in range(seq // tq):
        kv = (i + 1) * tq
        rows = slice(i * tq, kv)
        w = tail_ref[rows, IDX_DIM:IDX_DIM + IDX_HEADS] * ((IDX_HEADS * IDX_DIM) ** -0.5)
        score = jnp.zeros((tq, kv), F32)
        for h in range(IDX_HEADS):
            logits = _nt_dot(qi_ref[rows, h * IDX_DIM:(h + 1) * IDX_DIM], k_idx[:kv, :])
            score = score + jnp.maximum(logits, 0.0) * w[:, h:h + 1]
        row = lax.broadcasted_iota(I32, (tq, kv), 0) + i * tq
        col = lax.broadcasted_iota(I32, (tq, kv), 1)
        admissible = (col // CHUNK) <= (row // CHUNK)
        if kv <= topk:
            keep = admissible
        else:
            key_ref[:, :kv] = _sortable_key(jnp.where(admissible, score, NEG_INF))
            thr = _kth_largest_key(key_ref, kv, topk)
            keep = admissible & (key_ref[:, :kv] >= thr)
        o_ref[rows, :kv] = jnp.where(keep, 0.0, NEG_INF).astype(o_ref.dtype)
        if kv < seq:
            o_ref[rows, kv:] = jnp.full((tq, seq - kv), NEG_INF, o_ref.dtype)


def _dsa_index(q_idx, tail, batch, seq, topk):
    tq = min(ATTN_TQ, seq)
    return pl.pallas_call(
        functools.partial(_dsa_index_kernel, tq=tq, topk=topk),
        name="dsa_index",
        grid=(batch,),
        in_specs=[pl.BlockSpec((seq, IDX_HEADS * IDX_DIM), lambda b: (b, 0)),
                  pl.BlockSpec((seq, LANES), lambda b: (b, 0))],
        out_specs=pl.BlockSpec((seq, seq), lambda b: (b, 0)),
        out_shape=jax.ShapeDtypeStruct((batch * seq, seq), BF16),
        scratch_shapes=[pltpu.VMEM((tq, seq), I32)],
        compiler_params=_params(("arbitrary",)),
    )(q_idx, tail)


def _dsa_attn_kernel(q_ref, k_ref, v_ref, sel_ref, band_ref, o_ref, va_ref, *, tq):
    seq = k_ref.shape[0]
    width = band_ref.shape[1]
    _fill_v_aug(va_ref, v_ref)
    def scores(i):
        return _nt_dot(q_ref[i * tq:(i + 1) * tq, :], k_ref[:(i + 1) * tq, :])

    def finish(i, s):
        lo, kv = i * tq, (i + 1) * tq
        near = min(width, kv)
        far = kv - near
        s = s + sel_ref[lo:kv, :kv].astype(F32)
        pieces = [(s[:, far:] + band_ref[:, width - near:], va_ref[far:kv, :])]
        if far:
            pieces.append((s[:, :far], va_ref[:far, :]))
        o_ref[lo:kv, :] = _softmax_pv(pieces).astype(o_ref.dtype)

    _attend_tiles(seq // tq, scores, finish)


def _dsa_attn(qkv, sel, band, batch, seq):
    tq = min(ATTN_TQ, seq)
    width = band.shape[2]
    return pl.pallas_call(
        functools.partial(_dsa_attn_kernel, tq=tq),
        name="dsa_attn",
        grid=(batch, HEADS),
        in_specs=[pl.BlockSpec((seq, HEAD_DIM), lambda b, h: (b, h)),
                  pl.BlockSpec((seq, HEAD_DIM), lambda b, h: (b, HEADS)),
                  pl.BlockSpec((seq, HEAD_DIM), lambda b, h: (b, HEADS + 1)),
                  pl.BlockSpec((seq, seq), lambda b, h: (b, 0)),
                  pl.BlockSpec((None, tq, width), lambda b, h: (h, 0, 0))],
        out_specs=pl.BlockSpec((seq, HEAD_DIM), lambda b, h: (b, h)),
        out_shape=jax.ShapeDtypeStruct((batch * seq, HEADS * HEAD_DIM), BF16),
        scratch_shapes=[pltpu.VMEM((seq, 2 * HEAD_DIM), BF16)],
        compiler_params=_params(("arbitrary", "arbitrary")),
    )(qkv, qkv, qkv, sel, band)


def _dsa_layer(x, x_bf, t5_bias, w_in, w_o, ln_g, ln_b, batch, seq):
    hd = HEADS * HEAD_DIM
    n_qkv = hd + 2 * HEAD_DIM
    n_qi = IDX_HEADS * IDX_DIM
    scale = HEAD_DIM ** -0.5
    w_bf = w_in.astype(BF16)
    qkv = _matmul(x_bf, w_bf, n_out=n_qkv, tn=256, scale=scale, scaled_blocks=hd // 256)
    q_idx = _matmul(x_bf, w_bf, n_out=n_qi, tn=256, col_block_offset=n_qkv // 256)
    tail = _small_matmul(x, _pad_cols(w_in[:, n_qkv + n_qi:]), jnp.zeros((1, LANES), F32))
    topk = min(IDX_TOPK_MAX, seq // 4)
    sel = _dsa_index(q_idx, tail, batch, seq, topk)
    band = _t5_band(t5_bias, min(ATTN_TQ, seq))
    o = _dsa_attn(qkv, sel, band, batch, seq)
    return _mm_res_ln(o, w_o.astype(BF16), x, ln_g[None, :], ln_b[None, :])


def _route_kernel(x_ref, w_ref, b_ref, o_ref, ot_ref, before_ref, cnt_ref, run_ref):
    i = pl.program_id(0)

    @pl.when(i == 0)
    def _():
        run_ref[...] = jnp.zeros(run_ref.shape, F32)

    tm = o_ref.shape[0]
    logit = _dot_precise(x_ref[...], w_ref[...]) + b_ref[...]
    lane = lax.broadcasted_iota(I32, logit.shape, 1)
    big = jnp.int32(LANES)

    def first_argmax(mask):
        top = jnp.max(jnp.where(mask, logit, -jnp.inf), axis=1, keepdims=True)
        idx = jnp.min(jnp.where(mask & (logit == top), lane, big), axis=1, keepdims=True)
        return top, idx

    is_group = lane < MOE_GROUPS
    g_top, g_sel = first_argmax(is_group)
    p_g = 1.0 / jnp.sum(jnp.where(is_group, jnp.exp(logit - g_top), 0.0), axis=1, keepdims=True)
    lo = MOE_GROUPS + g_sel * MOE_EXPERTS_PER_GROUP
    in_group = (lane >= lo) & (lane < lo + MOE_EXPERTS_PER_GROUP)
    v1, i1 = first_argmax(in_group)
    v2, i2 = first_argmax(in_group & (lane != i1))
    ex = jnp.exp(v2 - v1)
    w1 = 1.0 / (1.0 + ex)
    w2 = ex / (1.0 + ex)
    e1 = i1 - MOE_GROUPS
    e2 = i2 - MOE_GROUPS
    pick1 = lane == e1
    pick2 = lane == e2
    both = jnp.where(pick1 | pick2, 1.0, 0.0)
    earlier = lax.broadcasted_iota(I32, (tm, tm), 1) < lax.broadcasted_iota(I32, (tm, tm), 0)
    before = jnp.dot(jnp.where(earlier, 1.0, 0.0).astype(BF16), both.astype(BF16),
                     preferred_element_type=F32) + run_ref[...]
    r1 = jnp.sum(jnp.where(pick1, before, 0.0), axis=1, keepdims=True)
    r2 = jnp.sum(jnp.where(pick2, before, 0.0), axis=1, keepdims=True)
    before_ref[...] = run_ref[...]
    run_ref[...] = run_ref[...] + jnp.sum(both, axis=0, keepdims=True)
    cnt_ref[...] = run_ref[...]
    out = jnp.where(lane == 0, e1.astype(F32), 0.0)
    out = jnp.where(lane == 1, e2.astype(F32), out)
    out = jnp.where(lane == 2, w1 * p_g, out)
    out = jnp.where(lane == 3, w2 * p_g, out)
    out = jnp.where(lane == 4, r1, out)
    out = jnp.where(lane == 5, r2, out)
    o_ref[...] = out
    ot_ref[...] = out.T[:8, :]


def _route(x, w, b, tm):
    m, k = x.shape
    n_t = m // tm
    return pl.pallas_call(
        _route_kernel,
        name="moe_route",
        grid=(n_t,),
        in_specs=[pl.BlockSpec((tm, k), lambda i: (i, 0)),
                  pl.BlockSpec((k, LANES), lambda i: (0, 0)),
                  pl.BlockSpec((1, LANES), lambda i: (0, 0))],
        out_specs=[pl.BlockSpec((tm, LANES), lambda i: (i, 0)),
                   pl.BlockSpec((8, tm), lambda i: (0, i)),
                   pl.BlockSpec((None, 1, LANES), lambda i: (i, 0, 0)),
                   pl.BlockSpec((1, LANES), lambda i: (0, 0))],
        out_shape=[jax.ShapeDtypeStruct((m, LANES), F32), jax.ShapeDtypeStruct((8, m), F32),
                   jax.ShapeDtypeStruct((n_t, 1, LANES), F32), jax.ShapeDtypeStruct((1, LANES), F32)],
        scratch_shapes=[pltpu.VMEM((1, LANES), F32)],
        compiler_params=_params(("arbitrary",)),
    )(x, w, b)


def _dispatch_tables(before, counts, n_tokens, tm):
    n_t = n_tokens // tm
    before = before[:, 0, :MOE_EXPERTS].astype(I32)
    total = counts[0, :MOE_EXPERTS].astype(I32)
    after = jnp.concatenate([before[1:], total[None, :]], axis=0)
    length = after - before
    length_al = (length + ROW_ALIGN - 1) // ROW_ALIGN * ROW_ALIGN
    t_ids = jnp.arange(n_t, dtype=I32)
    e_ids = jnp.arange(MOE_EXPERTS, dtype=I32)
    before_al = jnp.sum(jnp.where((t_ids[None, :] < t_ids[:, None])[:, :, None], length_al[None, :, :], 0), axis=1)
    total_al = jnp.sum(length_al, axis=0)
    seg = (total_al + tm - 1) // tm * tm
    ends = jnp.sum(jnp.where(e_ids[None, :] <= e_ids[:, None], seg[None, :], 0), axis=1)
    starts = ends - seg
    off = jnp.sum(jnp.where((e_ids[None, :] < e_ids[:, None])[None, :, :], length_al[:, None, :], 0), axis=2)
    used = jnp.sum(length_al, axis=1)
    n_chunks = used // ROW_ALIGN
    dstart = starts[None, :] + before_al
    c_row = jnp.arange(TILE_CHUNKS, dtype=I32) * ROW_ALIGN
    c_exp = jnp.sum(((off + length_al)[:, None, :] <= c_row[None, :, None]).astype(I32), axis=2)
    c_exp = jnp.minimum(c_exp, MOE_EXPERTS - 1)
    onehot = c_exp[:, :, None] == e_ids[None, None, :]
    chunk_row = jnp.sum(jnp.where(onehot, (dstart - off)[:, None, :], 0), axis=2) + c_row[None, :]
    pos_tab = off - before
    n_rows = _sorted_rows(n_tokens, tm)
    n_tiles = n_rows // tm
    tile_start = jnp.arange(n_tiles, dtype=I32) * tm
    tile_expert = jnp.minimum(jnp.sum((ends[None, :] <= tile_start[:, None]).astype(I32), axis=1), MOE_EXPERTS - 1)
    tile_valid = (tile_start < ends[-1]).astype(I32)
    tile_first = jnp.concatenate([jnp.ones((1,), I32), (tile_expert[1:] != tile_expert[:-1]).astype(I32)])
    later = (e_ids[None, :] > e_ids[:, None]) & (total[None, :] > 0)
    next_used = jnp.min(jnp.where(later, e_ids[None, :], MOE_EXPERTS), axis=1)
    next_used = jnp.where(next_used == MOE_EXPERTS, -1, next_used).astype(I32)
    tile_next = jnp.sum(jnp.where(tile_expert[:, None] == e_ids[None, :], next_used[None, :], 0), axis=1)
    tail_first = starts + total_al
    tail_chunks = (seg - total_al) // ROW_ALIGN
    return dict(n_chunks=n_chunks.astype(I32), chunk_row=chunk_row.reshape(-1).astype(I32),
                pos_tab=pos_tab.reshape(-1).astype(I32), pos_rows=pos_tab.astype(F32),
                tile_expert=tile_expert.astype(I32), tile_valid=tile_valid, tile_first=tile_first,
                tile_next=tile_next.astype(I32), n_valid=jnp.sum(tile_valid)[None].astype(I32),
                tail_first=tail_first.astype(I32),
                tail_chunks=tail_chunks.astype(I32))


def _sorted_rows(n_tokens, tm):
    n_t = n_tokens // tm
    bound = 2 * n_tokens + n_t * MOE_EXPERTS * (ROW_ALIGN - 1) + MOE_EXPERTS * (tm - ROW_ALIGN) + 1
    return (bound + tm - 1) // tm * tm


def _chunk_copy_out(buf, slot, c, row, hbm, sem):
    src = buf.at[slot, pl.ds(pl.multiple_of(c * ROW_ALIGN, ROW_ALIGN), ROW_ALIGN)]
    return pltpu.make_async_copy(src, hbm.at[pl.ds(pl.multiple_of(row, ROW_ALIGN), ROW_ALIGN)], sem.at[slot])


def _chunk_copy_in(hbm, row, buf, slot, c, sem):
    dst = buf.at[slot, pl.ds(pl.multiple_of(c * ROW_ALIGN, ROW_ALIGN), ROW_ALIGN)]
    return pltpu.make_async_copy(hbm.at[pl.ds(pl.multiple_of(row, ROW_ALIGN), ROW_ALIGN)], dst, sem.at[slot])


def _moe_dispatch_kernel(nch_ref, crow_ref, ptab_ref, tfirst_ref, tchunks_ref, tvalid_ref,
                         x_ref, rt_ref, xs_hbm, buf, zeros, sem, sem_z, *, tm, n_tiles):
    i = pl.program_id(0)
    n_t = pl.num_programs(0)
    slot = i % 2

    def wait_tile(tile, s):
        def body(c, carry):
            _chunk_copy_out(buf, s, 0, 0, xs_hbm, sem).wait()
            return carry
        lax.fori_loop(0, nch_ref[tile], body, 0)

    def zero_tail(e, k):
        return pltpu.make_async_copy(
            zeros.at[pl.ds(0, ROW_ALIGN)],
            xs_hbm.at[pl.ds(pl.multiple_of(tfirst_ref[e] + k * ROW_ALIGN, ROW_ALIGN), ROW_ALIGN)], sem_z.at[0])

    def zero_tile(t):
        return pltpu.make_async_copy(zeros, xs_hbm.at[pl.ds(pl.multiple_of(t * tm, tm), tm)], sem_z.at[1])

    def for_each_fill(tail_fn, tile_fn):
        def per_expert(e, carry):
            def per_chunk(k, c2):
                tail_fn(e, k)
                return c2
            lax.fori_loop(0, tchunks_ref[e], per_chunk, 0)
            return carry
        lax.fori_loop(0, MOE_EXPERTS, per_expert, 0)

        def per_tile(t, carry):
            @pl.when(tvalid_ref[t] == 0)
            def _():
                tile_fn(t)
            return carry
        lax.fori_loop(0, n_tiles, per_tile, 0)

    @pl.when(i == 0)
    def _():
        zeros[...] = jnp.zeros(zeros.shape, zeros.dtype)
        for_each_fill(lambda e, k: zero_tail(e, k).start(), lambda t: zero_tile(t).start())

    @pl.when(i >= 2)
    def _():
        wait_tile(i - 2, slot)

    pos1 = rt_ref[4:5, :].astype(I32)
    pos2 = rt_ref[5:6, :].astype(I32)
    e1 = rt_ref[0:1, :].astype(I32)
    e2 = rt_ref[1:2, :].astype(I32)
    for e in range(MOE_EXPERTS):
        shift = ptab_ref[i * MOE_EXPERTS + e]
        pos1 = pos1 + jnp.where(e1 == e, shift, 0)
        pos2 = pos2 + jnp.where(e2 == e, shift, 0)
    blk = 256
    for k in range(TILE_ROWS // blk):
        row = lax.broadcasted_iota(I32, (blk, tm), 0) + k * blk
        perm = jnp.where((row == pos1) | (row == pos2), 1.0, 0.0).astype(BF16)
        buf[slot, k * blk:(k + 1) * blk, :] = jnp.dot(perm, x_ref[...], preferred_element_type=F32).astype(BF16)

    def send(c, carry):
        _chunk_copy_out(buf, slot, c, crow_ref[i * TILE_CHUNKS + c], xs_hbm, sem).start()
        return carry
    lax.fori_loop(0, nch_ref[i], send, 0)

    @pl.when(i == n_t - 1)
    def _():
        @pl.when(i >= 1)
        def _():
            wait_tile(i - 1, 1 - slot)
        wait_tile(i, slot)
        for_each_fill(lambda e, k: zero_tail(e, k).wait(), lambda t: zero_tile(t).wait())


def _moe_dispatch(x_bf, route_t, tables, tm):
    m, d = x_bf.shape
    n_rows = _sorted_rows(m, tm)
    grid_spec = pltpu.PrefetchScalarGridSpec(
        num_scalar_prefetch=6,
        grid=(m // tm,),
        in_specs=[pl.BlockSpec((tm, d), lambda i, *_: (i, 0)),
                  pl.BlockSpec((8, tm), lambda i, *_: (0, i))],
        out_specs=pl.BlockSpec(memory_space=pl.ANY),
        scratch_shapes=[pltpu.VMEM((2, TILE_ROWS, d), BF16), pltpu.VMEM((tm, d), BF16),
                        pltpu.SemaphoreType.DMA((2,)), pltpu.SemaphoreType.DMA((2,))],
    )
    return pl.pallas_call(
        functools.partial(_moe_dispatch_kernel, tm=tm, n_tiles=n_rows // tm),
        name="moe_dispatch",
        grid_spec=grid_spec,
        out_shape=jax.ShapeDtypeStruct((n_rows, d), BF16),
        compiler_params=_params(("arbitrary",)),
    )(tables["n_chunks"], tables["chunk_row"], tables["pos_tab"], tables["tail_first"], tables["tail_chunks"],
      tables["tile_valid"], x_bf, route_t)


def _cast_rows(src_ref, dst_ref):
    rows = 32 * 8 * LANES // src_ref.shape[1]

    def body(i, carry):
        at = pl.ds(pl.multiple_of(i * rows, rows), rows)
        dst_ref[at, :] = src_ref[at, :].astype(dst_ref.dtype)
        return carry
    lax.fori_loop(0, src_ref.shape[0] // rows, body, 0, unroll=2)


def _moe_ffn_kernel(te_ref, tv_ref, tf_ref, tn_ref, nv_ref, x_ref, wg_hbm, wu_hbm, wd_hbm, o_ref,
                    wg_f, wu_f, wd_f, wg_bf, wu_bf, wd_bf, sem_w, *, layer):
    j = pl.program_id(0)
    valid = tv_ref[j] == 1

    def weight_copies(e):
        return (pltpu.make_async_copy(wg_hbm.at[layer, e], wg_f, sem_w.at[0]),
                pltpu.make_async_copy(wu_hbm.at[layer, e], wu_f, sem_w.at[1]),
                pltpu.make_async_copy(wd_hbm.at[layer, e], wd_f, sem_w.at[2]))

    @pl.when(j == 0)
    def _():
        for c in weight_copies(te_ref[0]):
            c.start()

    @pl.when(valid & (tf_ref[j] == 1))
    def _():
        for c in weight_copies(te_ref[j]):
            c.wait()
        _cast_rows(wg_f, wg_bf)
        _cast_rows(wu_f, wu_bf)
        _cast_rows(wd_f, wd_bf)

        @pl.when(tn_ref[j] >= 0)
        def _():
            for c in weight_copies(tn_ref[j]):
                c.start()

    @pl.when(valid)
    def _():
        xb = x_ref[...]
        hg = jnp.dot(xb, wg_bf[...], preferred_element_type=F32)
        hu = jnp.dot(xb, wu_bf[...], preferred_element_type=F32)
        act = (hg * (1.0 / (1.0 + jnp.exp(-hg))) * hu).astype(BF16)
        o_ref[...] = jnp.dot(act, wd_bf[...], preferred_element_type=F32).astype(o_ref.dtype)

    @pl.when(jnp.logical_not(valid))
    def _():
        o_ref[...] = jnp.zeros(o_ref.shape, o_ref.dtype)


def _moe_ffn(x_sorted, w_gate, w_up, w_down, layer, tables, tm):
    n_rows, d = x_sorted.shape
    f = w_gate.shape[3]
    any_spec = pl.BlockSpec(memory_space=pl.ANY)
    grid_spec = pltpu.PrefetchScalarGridSpec(
        num_scalar_prefetch=5,
        grid=(n_rows // tm,),
        in_specs=[pl.BlockSpec((tm, d), lambda j, te, tv, tf, tn, nv: (jnp.minimum(j, nv[0] - 1), 0)),
                  any_spec, any_spec, any_spec],
        out_specs=pl.BlockSpec((tm, d), lambda j, *_: (j, 0)),
        scratch_shapes=[pltpu.VMEM((d, f), F32), pltpu.VMEM((d, f), F32), pltpu.VMEM((f, d), F32),
                        pltpu.VMEM((d, f), BF16), pltpu.VMEM((d, f), BF16), pltpu.VMEM((f, d), BF16),
                        pltpu.SemaphoreType.DMA((3,))],
    )
    return pl.pallas_call(
        functools.partial(_moe_ffn_kernel, layer=layer),
        name="moe_ffn",
        grid_spec=grid_spec,
        out_shape=jax.ShapeDtypeStruct((n_rows, d), BF16),
        compiler_params=_params(("arbitrary",)),
    )(tables["tile_expert"], tables["tile_valid"], tables["tile_first"], tables["tile_next"], tables["n_valid"],
      x_sorted, w_gate, w_up, w_down)


def _moe_combine_kernel(nch_ref, crow_ref, ys_hbm, x_ref, r_ref, ptab_ref, g_ref, b_ref, o_ref, obf_ref,
                        buf, sem, *, tm):
    i = pl.program_id(0)
    n_t = pl.num_programs(0)
    slot = i % 2

    def fetch(tile, s):
        def body(c, carry):
            _chunk_copy_in(ys_hbm, crow_ref[tile * TILE_CHUNKS + c], buf, s, c, sem).start()
            return carry
        lax.fori_loop(0, nch_ref[tile], body, 0)

    @pl.when(i == 0)
    def _():
        buf[...] = jnp.zeros(buf.shape, buf.dtype)
        fetch(0, 0)

    @pl.when(i + 1 < n_t)
    def _():
        fetch(i + 1, 1 - slot)

    def arrived(c, carry):
        _chunk_copy_in(ys_hbm, 0, buf, slot, 0, sem).wait()
        return carry
    lax.fori_loop(0, nch_ref[i], arrived, 0)

    lane = lax.broadcasted_iota(I32, (tm, LANES), 1)
    shift = ptab_ref[...]
    e1 = r_ref[:, 0:1].astype(I32)
    e2 = r_ref[:, 1:2].astype(I32)
    pos1 = (r_ref[:, 4:5] + jnp.sum(jnp.where(lane == e1, shift, 0.0), axis=1, keepdims=True)).astype(I32)
    pos2 = (r_ref[:, 5:6] + jnp.sum(jnp.where(lane == e2, shift, 0.0), axis=1, keepdims=True)).astype(I32)
    col = lax.broadcasted_iota(I32, (tm, TILE_ROWS), 1)
    pick = jnp.where(col == pos1, r_ref[:, 2:3], 0.0) + jnp.where(col == pos2, r_ref[:, 3:4], 0.0)
    pick_hi = pick.astype(BF16)
    pick_lo = (pick - pick_hi.astype(F32)).astype(BF16)
    rows = buf[slot]
    y = jnp.dot(pick_lo, rows, preferred_element_type=F32) + jnp.dot(pick_hi, rows, preferred_element_type=F32)
    out = _layernorm_rows(DEEPNORM_ALPHA * x_ref[...] + y, g_ref[...], b_ref[...])
    o_ref[...] = out
    obf_ref[...] = out.astype(BF16)


def _moe_combine(y_sorted, x, route, tables, g, b, tm):
    m, d = x.shape
    grid_spec = pltpu.PrefetchScalarGridSpec(
        num_scalar_prefetch=2,
        grid=(m // tm,),
        in_specs=[pl.BlockSpec(memory_space=pl.ANY),
                  pl.BlockSpec((tm, d), lambda i, *_: (i, 0)),
                  pl.BlockSpec((tm, LANES), lambda i, *_: (i, 0)),
                  pl.BlockSpec((None, 1, LANES), lambda i, *_: (i, 0, 0)),
                  pl.BlockSpec((1, d), lambda i, *_: (0, 0)),
                  pl.BlockSpec((1, d), lambda i, *_: (0, 0))],
        out_specs=[pl.BlockSpec((tm, d), lambda i, *_: (i, 0)),
                   pl.BlockSpec((tm, d), lambda i, *_: (i, 0))],
        scratch_shapes=[pltpu.VMEM((2, TILE_ROWS, d), BF16), pltpu.SemaphoreType.DMA((2,))],
    )
    pos_rows = _pad_cols(tables["pos_rows"])[:, None, :]
    return pl.pallas_call(
        functools.partial(_moe_combine_kernel, tm=tm),
        name="moe_combine",
        grid_spec=grid_spec,
        out_shape=[jax.ShapeDtypeStruct((m, d), F32), jax.ShapeDtypeStruct((m, d), BF16)],
        compiler_params=_params(("arbitrary",)),
    )(tables["n_chunks"], tables["chunk_row"], y_sorted, x, route, pos_rows, g, b)


def _moe_layer(x, x_bf, w_group, b_group, w_router, b_router, w_gate, w_up, w_down, layer, ln_g, ln_b):
    n_tokens = x.shape[0]
    tm = min(MOE_TM, n_tokens)
    w_route = _pad_cols(jnp.concatenate([w_group, w_router], axis=1))
    b_route = _pad_cols(jnp.concatenate([b_group, b_router])[None, :])
    route, route_t, before, counts = _route(x, w_route, b_route, tm)
    tables = _dispatch_tables(before, counts, n_tokens, tm)
    x_sorted = _moe_dispatch(x_bf, route_t, tables, tm)
    y_sorted = _moe_ffn(x_sorted, w_gate, w_up, w_down, layer, tables, tm)
    return _moe_combine(y_sorted, x, route, tables, ln_g[None, :], ln_b[None, :], tm)


def kernel(x, t5_rel_bias, mla_w_in, mla_q_norm, mla_kv_norm, mla_w_q_up, mla_w_kv_up, mla_w_o, fox_w_in, fox_b_f, fox_w_o, dsa_w_in, dsa_w_o, ln_g, ln_b, moe_w_group, moe_b_group, moe_w_router, moe_b_router, moe_w_gate, moe_w_up, moe_w_down):
    batch, seq, d = x.shape
    x = x.reshape(batch * seq, d)
    x_bf = x.astype(BF16)
    for layer in range(DEPTH):
        kind = layer % N_MIXERS
        j = layer // N_MIXERS
        g0, b0 = ln_g[layer, 0], ln_b[layer, 0]
        if kind == 0:
            x, x_bf = _mla_layer(x, x_bf, mla_w_in[j], mla_q_norm[j], mla_kv_norm[j], mla_w_q_up[j],
                                 mla_w_kv_up[j], mla_w_o[j], g0, b0, batch, seq)
        elif kind == 1:
            x, x_bf = _fox_layer(x, x_bf, fox_w_in[j], fox_b_f[j], fox_w_o[j], g0, b0, batch, seq)
        else:
            x, x_bf = _dsa_layer(x, x_bf, t5_rel_bias, dsa_w_in[j], dsa_w_o[j], g0, b0, batch, seq)
        x, x_bf = _moe_layer(x, x_bf, moe_w_group[layer], moe_b_group[layer], moe_w_router[layer], moe_b_router[layer],
                             moe_w_gate, moe_w_up, moe_w_down, layer, ln_g[layer, 1], ln_b[layer, 1])
    return x.reshape(batch, seq, d)
```

```python
import functools
import math

import numpy as np
import jax
import jax.numpy as jnp
from jax import lax
from jax.experimental import pallas as pl
from jax.experimental.pallas import tpu as pltpu

F32 = jnp.float32
BF16 = jnp.bfloat16
I32 = jnp.int32

D_MODEL = 2048
DEPTH = 4
CHUNK = 64
N_MIXERS = 3
DEEPNORM_ALPHA = (2.0 * DEPTH) ** 0.25

HEADS = 16
HEAD_DIM = 128
MLA_NOPE_DIM = 128
MLA_ROPE_DIM = 64
MLA_Q_RANK = 512
MLA_KV_RANK = 512
MLA_QK_PAD = 256
ROPE_THETA = 10000.0

IDX_HEADS = 16
IDX_DIM = 64
IDX_TOPK_MAX = 256

T5_BUCKETS = 32
T5_MAX_DISTANCE = 128
T5_FAR_BUCKET = T5_BUCKETS // 2 - 1
T5_BAND_BACK = 128

MOE_GROUPS = 4
MOE_EXPERTS_PER_GROUP = 8
MOE_EXPERTS = MOE_GROUPS * MOE_EXPERTS_PER_GROUP
MOE_D_FF = 512

LN_EPS = 1e-5
RMS_EPS = 1e-6
NEG_INF = -1e30

LANES = 128
VMEM_LIMIT = 56 * 1024 * 1024

ATTN_TQ = 256
MOE_TM = 256
ROW_ALIGN = 16
TILE_ROWS = -(-(2 * MOE_TM + MOE_EXPERTS * (ROW_ALIGN - 1)) // 256) * 256
TILE_CHUNKS = TILE_ROWS // ROW_ALIGN
SORTED_WIDTH = D_MODEL + LANES


def _params(semantics, vmem=VMEM_LIMIT):
    return pltpu.CompilerParams(dimension_semantics=semantics, vmem_limit_bytes=vmem)


def _mm_kernel(a_ref, w_ref, o_ref, *, scale, scaled_cols):
    acc = jnp.dot(a_ref[...], w_ref[...], preferred_element_type=F32)
    if scaled_cols:
        tn = o_ref.shape[1]
        col = pl.program_id(0) * tn + lax.broadcasted_iota(I32, (1, tn), 1)
        acc = acc * jnp.where(col < scaled_cols, scale, 1.0)
    o_ref[...] = acc.astype(o_ref.dtype)


def _matmul(a, w, *, n_out, tn, tm=1024, scale=1.0, scaled_cols=0, out_dtype=BF16):
    m, k = a.shape
    tm = min(tm, m)
    kern = functools.partial(_mm_kernel, scale=scale, scaled_cols=scaled_cols)
    return pl.pallas_call(
        kern,
        name="proj_mm",
        grid=(n_out // tn, m // tm),
        in_specs=[pl.BlockSpec((tm, k), lambda j, i: (i, 0)),
                  pl.BlockSpec((k, tn), lambda j, i: (0, j))],
        out_specs=pl.BlockSpec((tm, tn), lambda j, i: (i, j)),
        out_shape=jax.ShapeDtypeStruct((m, n_out), out_dtype),
        compiler_params=_params(("arbitrary", "arbitrary")),
    )(a, w)


def _split3(v):
    hi = v.astype(BF16)
    r1 = v - hi.astype(F32)
    mid = r1.astype(BF16)
    lo = (r1 - mid.astype(F32)).astype(BF16)
    return hi, mid, lo


def _dot_precise(x, w):
    xh = x.astype(BF16)
    xm = (x - xh.astype(F32)).astype(BF16)
    wh = w.astype(BF16)
    wm = (w - wh.astype(F32)).astype(BF16)
    d = lambda p, q: jnp.dot(p, q, preferred_element_type=F32)
    return (d(xm, wh) + d(xh, wm)) + d(xh, wh)


def _small_mm_kernel(x_ref, w_ref, b_ref, o_ref):
    o_ref[...] = _dot_precise(x_ref[...], w_ref[...]) + b_ref[...]


def _small_matmul(x, w, b, tm=512):
    m, k = x.shape
    tm = min(tm, m)
    return pl.pallas_call(
        _small_mm_kernel,
        name="small_mm",
        grid=(m // tm,),
        in_specs=[pl.BlockSpec((tm, k), lambda i: (i, 0)),
                  pl.BlockSpec((k, LANES), lambda i: (0, 0)),
                  pl.BlockSpec((1, LANES), lambda i: (0, 0))],
        out_specs=pl.BlockSpec((tm, LANES), lambda i: (i, 0)),
        out_shape=jax.ShapeDtypeStruct((m, LANES), F32),
        compiler_params=_params(("arbitrary",)),
    )(x, w, b)


def _pad_cols(w, width=LANES):
    return jnp.pad(w, ((0, 0), (0, width - w.shape[1])))


def _layernorm_rows(z, g, b):
    mu = jnp.mean(z, axis=-1, keepdims=True)
    zc = z - mu
    var = jnp.mean(zc * zc, axis=-1, keepdims=True)
    return zc * lax.rsqrt(var + LN_EPS) * g + b


def _mm_res_ln_kernel(a_ref, w_ref, x_ref, g_ref, b_ref, o_ref, obf_ref):
    y = jnp.dot(a_ref[...], w_ref[...], preferred_element_type=F32)
    out = _layernorm_rows(DEEPNORM_ALPHA * x_ref[...] + y, g_ref[...], b_ref[...])
    o_ref[...] = out
    obf_ref[...] = out.astype(BF16)


def _mm_res_ln(a, w, x, g, b, tm=512):
    m, k = a.shape
    d = w.shape[1]
    tm = min(tm, m)
    return pl.pallas_call(
        _mm_res_ln_kernel,
        name="out_proj_res_ln",
        grid=(m // tm,),
        in_specs=[pl.BlockSpec((tm, k), lambda i: (i, 0)),
                  pl.BlockSpec((k, d), lambda i: (0, 0)),
                  pl.BlockSpec((tm, d), lambda i: (i, 0)),
                  pl.BlockSpec((1, d), lambda i: (0, 0)),
                  pl.BlockSpec((1, d), lambda i: (0, 0))],
        out_specs=[pl.BlockSpec((tm, d), lambda i: (i, 0)),
                   pl.BlockSpec((tm, d), lambda i: (i, 0))],
        out_shape=[jax.ShapeDtypeStruct((m, d), F32), jax.ShapeDtypeStruct((m, d), BF16)],
        compiler_params=_params(("arbitrary",)),
    )(a, w, x, g, b)


def _rope_tables(seq):
    half = MLA_ROPE_DIM // 2
    inv_freq = ROPE_THETA ** (-jnp.arange(half, dtype=F32) / half)
    ang = jnp.arange(seq, dtype=I32).astype(F32)[:, None] * inv_freq[None, :]
    cos, sin = jnp.cos(ang), jnp.sin(ang)
    z = jnp.zeros_like(cos)
    cos_t = jnp.concatenate([cos, cos, z, z], axis=1)
    sin_a = jnp.concatenate([-sin, z, z, z], axis=1)
    sin_b = jnp.concatenate([z, sin, z, z], axis=1)
    return cos_t, sin_a, sin_b


def _rope_lanes(r, cos_t, sin_a, sin_b):
    from_right = pltpu.roll(r, LANES - MLA_ROPE_DIM // 2, 1)
    from_left = pltpu.roll(r, MLA_ROPE_DIM // 2, 1)
    return r * cos_t + from_right * sin_a + from_left * sin_b


def _rms_rows(v, g):
    return v * lax.rsqrt(jnp.mean(v * v, axis=-1, keepdims=True) + RMS_EPS) * g


def _mla_in_kernel(x_ref, w_ref, qn_ref, kvn_ref, cos_ref, sina_ref, sinb_ref, cq_ref, ckv_ref, kr_ref):
    h = jnp.dot(x_ref[...], w_ref[...], preferred_element_type=F32)
    cq_ref[...] = _rms_rows(h[:, :MLA_Q_RANK], qn_ref[...]).astype(BF16)
    ckv_ref[...] = _rms_rows(h[:, MLA_Q_RANK:MLA_Q_RANK + MLA_KV_RANK], kvn_ref[...]).astype(BF16)
    r = h[:, MLA_Q_RANK + MLA_KV_RANK:]
    kr_ref[...] = _rope_lanes(r, cos_ref[...], sina_ref[...], sinb_ref[...]).astype(BF16)


def _mla_in(x_bf, w_in_pad, q_norm, kv_norm, tables, seq, tm=512):
    m, k = x_bf.shape
    tm = min(tm, seq)
    nw = w_in_pad.shape[1]
    per_seq = seq // tm
    tab_spec = pl.BlockSpec((tm, LANES), lambda i: (i % per_seq, 0))
    return pl.pallas_call(
        _mla_in_kernel,
        name="mla_in",
        grid=(m // tm,),
        in_specs=[pl.BlockSpec((tm, k), lambda i: (i, 0)),
                  pl.BlockSpec((k, nw), lambda i: (0, 0)),
                  pl.BlockSpec((1, MLA_Q_RANK), lambda i: (0, 0)),
                  pl.BlockSpec((1, MLA_KV_RANK), lambda i: (0, 0)),
                  tab_spec, tab_spec, tab_spec],
        out_specs=[pl.BlockSpec((tm, MLA_Q_RANK), lambda i: (i, 0)),
                   pl.BlockSpec((tm, MLA_KV_RANK), lambda i: (i, 0)),
                   pl.BlockSpec((tm, LANES), lambda i: (i, 0))],
        out_shape=[jax.ShapeDtypeStruct((m, MLA_Q_RANK), BF16),
                   jax.ShapeDtypeStruct((m, MLA_KV_RANK), BF16),
                   jax.ShapeDtypeStruct((m, LANES), BF16)],
        compiler_params=_params(("arbitrary",)),
    )(x_bf, w_in_pad, q_norm, kv_norm, *tables)


def _fill_v_aug(va_ref, v_ref):
    va_ref[:, :HEAD_DIM] = v_ref[...]
    va_ref[:, HEAD_DIM:] = jnp.ones((v_ref.shape[0], HEAD_DIM), BF16)


def _softmax_pv(pieces):
    m = None
    for s, _ in pieces:
        pm = jnp.max(s, axis=1, keepdims=True)
        m = pm if m is None else jnp.maximum(m, pm)
    acc = None
    for s, v in pieces:
        t = jnp.dot(jnp.exp((s - m).astype(BF16)), v, preferred_element_type=F32)
        acc = t if acc is None else acc + t
    return acc[:, :HEAD_DIM] / acc[:, HEAD_DIM:HEAD_DIM + 1]


def _nt_dot(q, k):
    return lax.dot_general(q, k, (((1,), (1,)), ((), ())), preferred_element_type=F32)


def _attend_tiles(n_tiles, scores, finish):
    s_next = scores(0)
    for i in range(n_tiles):
        s = s_next
        if i + 1 < n_tiles:
            s_next = scores(i + 1)
        finish(i, s)


def _mla_attn_kernel(q_ref, kn_ref, v_ref, kr_ref, cos_ref, sina_ref, sinb_ref, o_ref, qc_ref, kc_ref, va_ref,
                     *, tq):
    seq = q_ref.shape[0]
    r = q_ref[:, MLA_NOPE_DIM:].astype(F32)
    qc_ref[:, :MLA_NOPE_DIM] = q_ref[:, :MLA_NOPE_DIM]
    qc_ref[:, MLA_NOPE_DIM:] = _rope_lanes(r, cos_ref[...], sina_ref[...], sinb_ref[...]).astype(BF16)
    kc_ref[:, :MLA_NOPE_DIM] = kn_ref[...]
    kc_ref[:, MLA_NOPE_DIM:] = kr_ref[...]
    _fill_v_aug(va_ref, v_ref)
    row = lax.broadcasted_iota(I32, (tq, tq), 0)
    col = lax.broadcasted_iota(I32, (tq, tq), 1)
    visible = (col // CHUNK) <= (row // CHUNK)
    def scores(i):
        return _nt_dot(qc_ref[i * tq:(i + 1) * tq, :], kc_ref[:(i + 1) * tq, :])

    def finish(i, s):
        lo, kv = i * tq, (i + 1) * tq
        pieces = [(jnp.where(visible, s[:, lo:], NEG_INF), va_ref[lo:kv, :])]
        if lo:
            pieces.append((s[:, :lo], va_ref[:lo, :]))
        o_ref[lo:kv, :] = _softmax_pv(pieces).astype(o_ref.dtype)

    _attend_tiles(seq // tq, scores, finish)


def _mla_attn(q, kv, kr, tables, batch, seq):
    tq = min(ATTN_TQ, seq)
    kern = functools.partial(_mla_attn_kernel, tq=tq)
    tab_spec = pl.BlockSpec((seq, LANES), lambda b, h: (0, 0))
    return pl.pallas_call(
        kern,
        name="mla_attn",
        grid=(batch, HEADS),
        in_specs=[pl.BlockSpec((seq, MLA_QK_PAD), lambda b, h: (b, h)),
                  pl.BlockSpec((seq, HEAD_DIM), lambda b, h: (b, 2 * h)),
                  pl.BlockSpec((seq, HEAD_DIM), lambda b, h: (b, 2 * h + 1)),
                  pl.BlockSpec((seq, LANES), lambda b, h: (b, 0)),
                  tab_spec, tab_spec, tab_spec],
        out_specs=pl.BlockSpec((seq, HEAD_DIM), lambda b, h: (b, h)),
        out_shape=jax.ShapeDtypeStruct((batch * seq, HEADS * HEAD_DIM), BF16),
        scratch_shapes=[pltpu.VMEM((seq, MLA_QK_PAD), BF16), pltpu.VMEM((seq, MLA_QK_PAD), BF16),
                        pltpu.VMEM((seq, 2 * HEAD_DIM), BF16)],
        compiler_params=_params(("arbitrary", "arbitrary")),
    )(q, kv, kv, kr, *tables)


def _mla_q_up_layout(w_q_up):
    rank = w_q_up.shape[0]
    w = w_q_up.reshape(rank, HEADS, MLA_NOPE_DIM + MLA_ROPE_DIM)
    w = jnp.pad(w, ((0, 0), (0, 0), (0, MLA_QK_PAD - MLA_NOPE_DIM - MLA_ROPE_DIM)))
    return w.reshape(rank, HEADS * MLA_QK_PAD)


def _mla_layer(x, x_bf, w_in, q_norm, kv_norm, w_q_up, w_kv_up, w_o, ln_g, ln_b, batch, seq):
    tables = _rope_tables(seq)
    w_in_pad = jnp.pad(w_in, ((0, 0), (0, MLA_Q_RANK + MLA_KV_RANK + LANES - w_in.shape[1]))).astype(BF16)
    cq, ckv, kr = _mla_in(x_bf, w_in_pad, q_norm[None, :], kv_norm[None, :], tables, seq)
    scale = (MLA_NOPE_DIM + MLA_ROPE_DIM) ** -0.5
    nq = HEADS * MLA_QK_PAD
    q = _matmul(cq, _mla_q_up_layout(w_q_up).astype(BF16), n_out=nq, tn=1024, scale=scale, scaled_cols=nq)
    kv = _matmul(ckv, w_kv_up.astype(BF16), n_out=w_kv_up.shape[1], tn=1024)
    o = _mla_attn(q, kv, kr, tables, batch, seq)
    return _mm_res_ln(o, w_o.astype(BF16), x, ln_g[None, :], ln_b[None, :])


def _log_sigmoid(z):
    return jnp.minimum(z, 0.0) - jnp.log1p(jnp.exp(-jnp.abs(z)))


def _fox_gate_kernel(f_ref, c_ref, *, blk):
    seq = f_ref.shape[0]
    tri = (lax.broadcasted_iota(I32, (blk, blk), 1) <= lax.broadcasted_iota(I32, (blk, blk), 0)).astype(BF16)
    carry = jnp.zeros((1, LANES), F32)
    for j in range(seq // blk):
        lf = _log_sigmoid(f_ref[j * blk:(j + 1) * blk, :])
        hi, mid, lo = _split3(lf)
        d = lambda p: jnp.dot(tri, p, preferred_element_type=F32)
        c = (d(lo) + d(mid)) + d(hi) + carry
        c_ref[j * blk:(j + 1) * blk, :] = c
        carry = c[blk - 1:blk, :]


def _fox_gate(f_logit, batch, seq):
    blk = min(256, seq)
    return pl.pallas_call(
        functools.partial(_fox_gate_kernel, blk=blk),
        name="fox_gate",
        grid=(batch,),
        in_specs=[pl.BlockSpec((seq, LANES), lambda b: (b, 0))],
        out_specs=pl.BlockSpec((seq, LANES), lambda b: (b, 0)),
        out_shape=jax.ShapeDtypeStruct((batch * seq, LANES), F32),
        compiler_params=_params(("arbitrary",)),
    )(f_logit)


def _fox_attn_kernel(q_ref, k_ref, v_ref, c_ref, o_ref, qa_ref, ka_ref, va_ref, *, tq):
    seq = q_ref.shape[0]
    h = pl.program_id(1)
    lane = lax.broadcasted_iota(I32, (seq, LANES), 1)
    c = jnp.sum(jnp.where(lane == h, c_ref[...], 0.0), axis=1, keepdims=True)
    terms = [t.astype(F32) for t in _split3(c)]
    q_side = jnp.where((lane >= 3) & (lane < 6), 1.0, 0.0)
    k_side = jnp.where(lane < 3, 1.0, 0.0)
    for j, t in enumerate(terms):
        q_side = jnp.where(lane == j, t, q_side)
        k_side = jnp.where(lane == 3 + j, -t, k_side)
    qa_ref[:, :HEAD_DIM] = q_ref[...]
    qa_ref[:, HEAD_DIM:] = q_side.astype(BF16)
    ka_ref[:, :HEAD_DIM] = k_ref[...]
    ka_ref[:, HEAD_DIM:] = k_side.astype(BF16)
    _fill_v_aug(va_ref, v_ref)
    row = lax.broadcasted_iota(I32, (tq, tq), 0)
    col = lax.broadcasted_iota(I32, (tq, tq), 1)
    def scores(i):
        return _nt_dot(qa_ref[i * tq:(i + 1) * tq, :], ka_ref[:(i + 1) * tq, :])

    def finish(i, s):
        lo, kv = i * tq, (i + 1) * tq
        pieces = [(jnp.where(col <= row, s[:, lo:], NEG_INF), va_ref[lo:kv, :])]
        if lo:
            pieces.append((s[:, :lo], va_ref[:lo, :]))
        o_ref[lo:kv, :] = _softmax_pv(pieces).astype(o_ref.dtype)

    _attend_tiles(seq // tq, scores, finish)


def _fox_attn(qkv, c_col, batch, seq):
    tq = min(ATTN_TQ, seq)
    wide = pltpu.VMEM((seq, 2 * HEAD_DIM), BF16)
    return pl.pallas_call(
        functools.partial(_fox_attn_kernel, tq=tq),
        name="fox_attn",
        grid=(batch, HEADS),
        in_specs=[pl.BlockSpec((seq, HEAD_DIM), lambda b, h: (b, h)),
                  pl.BlockSpec((seq, HEAD_DIM), lambda b, h: (b, HEADS + h)),
                  pl.BlockSpec((seq, HEAD_DIM), lambda b, h: (b, 2 * HEADS + h)),
                  pl.BlockSpec((seq, LANES), lambda b, h: (b, 0))],
        out_specs=pl.BlockSpec((seq, HEAD_DIM), lambda b, h: (b, h)),
        out_shape=jax.ShapeDtypeStruct((batch * seq, HEADS * HEAD_DIM), BF16),
        scratch_shapes=[wide, wide, wide],
        compiler_params=_params(("arbitrary", "arbitrary")),
    )(qkv, qkv, qkv, c_col)


def _fox_layer(x, x_bf, w_in, b_f, w_o, ln_g, ln_b, batch, seq):
    hd = HEADS * HEAD_DIM
    scale = HEAD_DIM ** -0.5
    qkv = _matmul(x_bf, w_in.astype(BF16), n_out=3 * hd, tn=1024, scale=scale, scaled_cols=hd)
    f_logit = _small_matmul(x, _pad_cols(w_in[:, 3 * hd:]), _pad_cols(b_f[None, :]))
    o = _fox_attn(qkv, _fox_gate(f_logit, batch, seq), batch, seq)
    return _mm_res_ln(o, w_o.astype(BF16), x, ln_g[None, :], ln_b[None, :])


def _t5_bucket_table(tq):
    a = np.arange(tq, dtype=np.int64)[:, None]
    b = np.arange(tq + T5_BAND_BACK, dtype=np.int64)[None, :]
    rel = (b - T5_BAND_BACK) - a
    nb = T5_BUCKETS // 2
    max_exact = nb // 2
    ret = np.where(rel > 0, nb, 0)
    n = np.abs(rel)
    nf = np.maximum(n, 1).astype(np.float32)
    large = max_exact + (np.log(nf / np.float32(max_exact)) / np.float32(math.log(T5_MAX_DISTANCE / max_exact))
                         * np.float32(nb - max_exact)).astype(np.int32)
    large = np.minimum(large, nb - 1)
    return (ret + np.where(n < max_exact, n, large)).astype(np.int32)


def _t5_band_kernel(t5_ref, bucket_ref, o_ref):
    bucket = bucket_ref[...]

    def per_head(h, carry):
        acc = jnp.zeros(bucket.shape, F32)
        for b in range(T5_BUCKETS):
            acc = jnp.where(bucket == b, t5_ref[b, h], acc)
        o_ref[h] = acc - t5_ref[T5_FAR_BUCKET, h]
        return carry

    lax.fori_loop(0, HEADS, per_head, 0)


def _t5_band(t5_bias, tq):
    bucket = jnp.asarray(_t5_bucket_table(tq))
    width = tq + T5_BAND_BACK
    return pl.pallas_call(
        _t5_band_kernel,
        name="t5_band",
        grid=(1,),
        in_specs=[pl.BlockSpec(memory_space=pltpu.SMEM),
                  pl.BlockSpec((tq, width), lambda i: (0, 0))],
        out_specs=pl.BlockSpec((HEADS, tq, width), lambda i: (0, 0, 0)),
        out_shape=jax.ShapeDtypeStruct((HEADS, tq, width), F32),
        compiler_params=_params(("arbitrary",)),
    )(t5_bias, bucket)


def _sortable_key(v):
    bits = pltpu.bitcast(v, I32)
    return bits ^ ((bits >> 31) & 0x7FFFFFFF)


def _kth_largest_key(key_ref, kv, k):
    rows = key_ref.shape[0]

    def count_ge(t):
        t_b = jnp.broadcast_to(t, (rows, LANES))
        acc = jnp.zeros((rows, LANES), I32)
        for c in range(kv // LANES):
            acc = acc + jnp.where(key_ref[:, c * LANES:(c + 1) * LANES] >= t_b, 1, 0)
        return jnp.sum(acc, axis=1, keepdims=True)

    t0 = jnp.where(count_ge(jnp.zeros((rows, 1), I32)) >= k, 0, jnp.iinfo(jnp.int32).min).astype(I32)

    def body(it, t):
        cand = t | (jnp.int32(1) << (30 - it))
        return jnp.where(count_ge(cand) >= k, cand, t)

    return lax.fori_loop(0, 31, body, t0)


def _dsa_index_kernel(qi_ref, tail_ref, o_ref, key_ref, *, tq, topk):
    seq = tail_ref.shape[0]
    k_idx = tail_ref[:, :IDX_DIM].astype(BF16)
    for i in range(seq // tq):
        kv = (i + 1) * tq
        rows = slice(i * tq, kv)
        w = tail_ref[rows, IDX_DIM:IDX_DIM + IDX_HEADS] * ((IDX_HEADS * IDX_DIM) ** -0.5)
        score = jnp.zeros((tq, kv), F32)
        for h in range(IDX_HEADS):
            logits = _nt_dot(qi_ref[rows, h * IDX_DIM:(h + 1) * IDX_DIM], k_idx[:kv, :])
            score = score + jnp.maximum(logits, 0.0) * w[:, h:h + 1]
        row = lax.broadcasted_iota(I32, (tq, kv), 0) + i * tq
        col = lax.broadcasted_iota(I32, (tq, kv), 1)
        admissible = (col // CHUNK) <= (row // CHUNK)
        if kv <= topk:
            keep = admissible
        else:
            key_ref[:, :kv] = _sortable_key(jnp.where(admissible, score, NEG_INF))
            thr = _kth_largest_key(key_ref, kv, topk)
            keep = admissible & (key_ref[:, :kv] >= thr)
        o_ref[rows, :kv] = jnp.where(keep, 0.0, NEG_INF).astype(o_ref.dtype)
        if kv < seq:
            o_ref[rows, kv:] = jnp.full((tq, seq - kv), NEG_INF, o_ref.dtype)


def _dsa_index(q_idx, tail, batch, seq, topk):
    tq = min(ATTN_TQ, seq)
    return pl.pallas_call(
        functools.partial(_dsa_index_kernel, tq=tq, topk=topk),
        name="dsa_index",
        grid=(batch,),
        in_specs=[pl.BlockSpec((seq, IDX_HEADS * IDX_DIM), lambda b: (b, 0)),
                  pl.BlockSpec((seq, LANES), lambda b: (b, 0))],
        out_specs=pl.BlockSpec((seq, seq), lambda b: (b, 0)),
        out_shape=jax.ShapeDtypeStruct((batch * seq, seq), BF16),
        scratch_shapes=[pltpu.VMEM((tq, seq), I32)],
        compiler_params=_params(("arbitrary",)),
    )(q_idx, tail)


def _dsa_attn_kernel(q_ref, k_ref, v_ref, sel_ref, band_ref, o_ref, va_ref, *, tq):
    seq = k_ref.shape[0]
    width = band_ref.shape[1]
    _fill_v_aug(va_ref, v_ref)
    def scores(i):
        return _nt_dot(q_ref[i * tq:(i + 1) * tq, :], k_ref[:(i + 1) * tq, :])

    def finish(i, s):
        lo, kv = i * tq, (i + 1) * tq
        near = min(width, kv)
        far = kv - near
        s = s + sel_ref[lo:kv, :kv].astype(F32)
        pieces = [(s[:, far:] + band_ref[:, width - near:], va_ref[far:kv, :])]
        if far:
            pieces.append((s[:, :far], va_ref[:far, :]))
        o_ref[lo:kv, :] = _softmax_pv(pieces).astype(o_ref.dtype)

    _attend_tiles(seq // tq, scores, finish)


def _dsa_attn(qkv, sel, band, batch, seq):
    tq = min(ATTN_TQ, seq)
    width = band.shape[2]
    return pl.pallas_call(
        functools.partial(_dsa_attn_kernel, tq=tq),
        name="dsa_attn",
        grid=(batch, HEADS),
        in_specs=[pl.BlockSpec((seq, HEAD_DIM), lambda b, h: (b, h)),
                  pl.BlockSpec((seq, HEAD_DIM), lambda b, h: (b, HEADS)),
                  pl.BlockSpec((seq, HEAD_DIM), lambda b, h: (b, HEADS + 1)),
                  pl.BlockSpec((seq, seq), lambda b, h: (b, 0)),
                  pl.BlockSpec((None, tq, width), lambda b, h: (h, 0, 0))],
        out_specs=pl.BlockSpec((seq, HEAD_DIM), lambda b, h: (b, h)),
        out_shape=jax.ShapeDtypeStruct((batch * seq, HEADS * HEAD_DIM), BF16),
        scratch_shapes=[pltpu.VMEM((seq, 2 * HEAD_DIM), BF16)],
        compiler_params=_params(("arbitrary", "arbitrary")),
    )(qkv, qkv, qkv, sel, band)


def _dsa_layer(x, x_bf, t5_bias, w_in, w_o, ln_g, ln_b, batch, seq):
    hd = HEADS * HEAD_DIM
    n_qkv = hd + 2 * HEAD_DIM
    n_qi = IDX_HEADS * IDX_DIM
    scale = HEAD_DIM ** -0.5
    w_bf = w_in.astype(BF16)
    qkv = _matmul(x_bf, w_bf, n_out=n_qkv, tn=n_qkv // 3, scale=scale, scaled_cols=hd)
    q_idx = _matmul(x_bf, w_bf[:, n_qkv:n_qkv + n_qi], n_out=n_qi, tn=n_qi)
    tail = _small_matmul(x, _pad_cols(w_in[:, n_qkv + n_qi:]), jnp.zeros((1, LANES), F32))
    topk = min(IDX_TOPK_MAX, seq // 4)
    sel = _dsa_index(q_idx, tail, batch, seq, topk)
    band = _t5_band(t5_bias, min(ATTN_TQ, seq))
    o = _dsa_attn(qkv, sel, band, batch, seq)
    return _mm_res_ln(o, w_o.astype(BF16), x, ln_g[None, :], ln_b[None, :])


def _route_kernel(x_ref, w_ref, b_ref, o_ref, ot_ref, before_ref, cnt_ref, run_ref):
    i = pl.program_id(0)

    @pl.when(i == 0)
    def _():
        run_ref[...] = jnp.zeros(run_ref.shape, F32)

    tm = o_ref.shape[0]
    logit = _dot_precise(x_ref[...], w_ref[...]) + b_ref[...]
    lane = lax.broadcasted_iota(I32, logit.shape, 1)
    big = jnp.int32(LANES)

    def first_argmax(mask):
        top = jnp.max(jnp.where(mask, logit, -jnp.inf), axis=1, keepdims=True)
        idx = jnp.min(jnp.where(mask & (logit == top), lane, big), axis=1, keepdims=True)
        return top, idx

    is_group = lane < MOE_GROUPS
    g_top, g_sel = first_argmax(is_group)
    p_g = 1.0 / jnp.sum(jnp.where(is_group, jnp.exp(logit - g_top), 0.0), axis=1, keepdims=True)
    lo = MOE_GROUPS + g_sel * MOE_EXPERTS_PER_GROUP
    in_group = (lane >= lo) & (lane < lo + MOE_EXPERTS_PER_GROUP)
    v1, i1 = first_argmax(in_group)
    v2, i2 = first_argmax(in_group & (lane != i1))
    ex = jnp.exp(v2 - v1)
    w1 = 1.0 / (1.0 + ex)
    w2 = ex / (1.0 + ex)
    e1 = i1 - MOE_GROUPS
    e2 = i2 - MOE_GROUPS
    pick1 = lane == e1
    pick2 = lane == e2
    both = jnp.where(pick1 | pick2, 1.0, 0.0)
    earlier = lax.broadcasted_iota(I32, (tm, tm), 1) < lax.broadcasted_iota(I32, (tm, tm), 0)
    before = jnp.dot(jnp.where(earlier, 1.0, 0.0).astype(BF16), both.astype(BF16),
                     preferred_element_type=F32) + run_ref[...]
    r1 = jnp.sum(jnp.where(pick1, before, 0.0), axis=1, keepdims=True)
    r2 = jnp.sum(jnp.where(pick2, before, 0.0), axis=1, keepdims=True)
    before_ref[...] = run_ref[...]
    run_ref[...] = run_ref[...] + jnp.sum(both, axis=0, keepdims=True)
    cnt_ref[...] = run_ref[...]
    out = jnp.where(lane == 0, e1.astype(F32), 0.0)
    out = jnp.where(lane == 1, e2.astype(F32), out)
    out = jnp.where(lane == 2, w1 * p_g, out)
    out = jnp.where(lane == 3, w2 * p_g, out)
    out = jnp.where(lane == 4, r1, out)
    out = jnp.where(lane == 5, r2, out)
    o_ref[...] = out
    ot_ref[...] = out.T[:8, :]


def _route(x, w, b, tm):
    m, k = x.shape
    n_t = m // tm
    return pl.pallas_call(
        _route_kernel,
        name="moe_route",
        grid=(n_t,),
        in_specs=[pl.BlockSpec((tm, k), lambda i: (i, 0)),
                  pl.BlockSpec((k, LANES), lambda i: (0, 0)),
                  pl.BlockSpec((1, LANES), lambda i: (0, 0))],
        out_specs=[pl.BlockSpec((tm, LANES), lambda i: (i, 0)),
                   pl.BlockSpec((8, tm), lambda i: (0, i)),
                   pl.BlockSpec((None, 1, LANES), lambda i: (i, 0, 0)),
                   pl.BlockSpec((1, LANES), lambda i: (0, 0))],
        out_shape=[jax.ShapeDtypeStruct((m, LANES), F32), jax.ShapeDtypeStruct((8, m), F32),
                   jax.ShapeDtypeStruct((n_t, 1, LANES), F32), jax.ShapeDtypeStruct((1, LANES), F32)],
        scratch_shapes=[pltpu.VMEM((1, LANES), F32)],
        compiler_params=_params(("arbitrary",)),
    )(x, w, b)


def _dispatch_tables(before, counts, n_tokens, tm):
    n_t = n_tokens // tm
    before = before[:, 0, :MOE_EXPERTS].astype(I32)
    total = counts[0, :MOE_EXPERTS].astype(I32)
    after = jnp.concatenate([before[1:], total[None, :]], axis=0)
    length = after - before
    length_al = (length + ROW_ALIGN - 1) // ROW_ALIGN * ROW_ALIGN
    t_ids = jnp.arange(n_t, dtype=I32)
    e_ids = jnp.arange(MOE_EXPERTS, dtype=I32)
    before_al = jnp.sum(jnp.where((t_ids[None, :] < t_ids[:, None])[:, :, None], length_al[None, :, :], 0), axis=1)
    total_al = jnp.sum(length_al, axis=0)
    seg = (total_al + tm - 1) // tm * tm
    ends = jnp.sum(jnp.where(e_ids[None, :] <= e_ids[:, None], seg[None, :], 0), axis=1)
    starts = ends - seg
    off = jnp.sum(jnp.where((e_ids[None, :] < e_ids[:, None])[None, :, :], length_al[:, None, :], 0), axis=2)
    used = jnp.sum(length_al, axis=1)
    n_chunks = used // ROW_ALIGN
    dstart = starts[None, :] + before_al
    c_row = jnp.arange(TILE_CHUNKS, dtype=I32) * ROW_ALIGN
    c_exp = jnp.sum(((off + length_al)[:, None, :] <= c_row[None, :, None]).astype(I32), axis=2)
    c_exp = jnp.minimum(c_exp, MOE_EXPERTS - 1)
    onehot = c_exp[:, :, None] == e_ids[None, None, :]
    chunk_row = jnp.sum(jnp.where(onehot, (dstart - off)[:, None, :], 0), axis=2) + c_row[None, :]
    pos_tab = off - before
    n_rows = _sorted_rows(n_tokens, tm)
    n_tiles = n_rows // tm
    tile_start = jnp.arange(n_tiles, dtype=I32) * tm
    tile_expert = jnp.minimum(jnp.sum((ends[None, :] <= tile_start[:, None]).astype(I32), axis=1), MOE_EXPERTS - 1)
    tile_valid = (tile_start < ends[-1]).astype(I32)
    tile_first = jnp.concatenate([jnp.ones((1,), I32), (tile_expert[1:] != tile_expert[:-1]).astype(I32)])
    later = (e_ids[None, :] > e_ids[:, None]) & (total[None, :] > 0)
    next_used = jnp.min(jnp.where(later, e_ids[None, :], MOE_EXPERTS), axis=1)
    next_used = jnp.where(next_used == MOE_EXPERTS, -1, next_used).astype(I32)
    tile_next = jnp.sum(jnp.where(tile_expert[:, None] == e_ids[None, :], next_used[None, :], 0), axis=1)
    tail_first = starts + total_al
    tail_chunks = (seg - total_al) // ROW_ALIGN
    return dict(n_chunks=n_chunks.astype(I32), chunk_row=chunk_row.reshape(-1).astype(I32),
                pos_tab=pos_tab.reshape(-1).astype(I32), pos_rows=pos_tab.astype(F32),
                tile_expert=tile_expert.astype(I32), tile_valid=tile_valid, tile_first=tile_first,
                tile_next=tile_next.astype(I32), n_valid=jnp.sum(tile_valid)[None].astype(I32),
                tail_first=tail_first.astype(I32),
                tail_chunks=tail_chunks.astype(I32))


def _sorted_rows(n_tokens, tm):
    n_t = n_tokens // tm
    bound = 2 * n_tokens + n_t * MOE_EXPERTS * (ROW_ALIGN - 1) + MOE_EXPERTS * (tm - ROW_ALIGN) + 1
    return (bound + tm - 1) // tm * tm


def _chunk_copy_out(buf, slot, c, row, hbm, sem):
    src = buf.at[slot, pl.ds(pl.multiple_of(c * ROW_ALIGN, ROW_ALIGN), ROW_ALIGN)]
    return pltpu.make_async_copy(src, hbm.at[pl.ds(pl.multiple_of(row, ROW_ALIGN), ROW_ALIGN)], sem.at[slot])


def _chunk_copy_in(hbm, row, buf, slot, c, sem):
    dst = buf.at[slot, pl.ds(pl.multiple_of(c * ROW_ALIGN, ROW_ALIGN), ROW_ALIGN)]
    return pltpu.make_async_copy(hbm.at[pl.ds(pl.multiple_of(row, ROW_ALIGN), ROW_ALIGN)], dst, sem.at[slot])


def _moe_dispatch_kernel(nch_ref, crow_ref, ptab_ref, tfirst_ref, tchunks_ref, tvalid_ref,
                         x_ref, rt_ref, r_ref, xs_hbm, buf, zeros, sem, sem_z, *, tm, n_tiles):
    i = pl.program_id(0)
    n_t = pl.num_programs(0)
    slot = i % 2

    def wait_tile(tile, s):
        def body(c, carry):
            _chunk_copy_out(buf, s, 0, 0, xs_hbm, sem).wait()
            return carry
        lax.fori_loop(0, nch_ref[tile], body, 0)

    def zero_tail(e, k):
        return pltpu.make_async_copy(
            zeros.at[pl.ds(0, ROW_ALIGN)],
            xs_hbm.at[pl.ds(pl.multiple_of(tfirst_ref[e] + k * ROW_ALIGN, ROW_ALIGN), ROW_ALIGN)], sem_z.at[0])

    def zero_tile(t):
        return pltpu.make_async_copy(zeros, xs_hbm.at[pl.ds(pl.multiple_of(t * tm, tm), tm)], sem_z.at[1])

    def for_each_fill(tail_fn, tile_fn):
        def per_expert(e, carry):
            def per_chunk(k, c2):
                tail_fn(e, k)
                return c2
            lax.fori_loop(0, tchunks_ref[e], per_chunk, 0)
            return carry
        lax.fori_loop(0, MOE_EXPERTS, per_expert, 0)

        def per_tile(t, carry):
            @pl.when(tvalid_ref[t] == 0)
            def _():
                tile_fn(t)
            return carry
        lax.fori_loop(0, n_tiles, per_tile, 0)

    @pl.when(i == 0)
    def _():
        zeros[...] = jnp.zeros(zeros.shape, zeros.dtype)
        for_each_fill(lambda e, k: zero_tail(e, k).start(), lambda t: zero_tile(t).start())

    @pl.when(i >= 2)
    def _():
        wait_tile(i - 2, slot)

    pos1 = rt_ref[4:5, :].astype(I32)
    pos2 = rt_ref[5:6, :].astype(I32)
    e1 = rt_ref[0:1, :].astype(I32)
    e2 = rt_ref[1:2, :].astype(I32)
    for e in range(MOE_EXPERTS):
        shift = ptab_ref[i * MOE_EXPERTS + e]
        pos1 = pos1 + jnp.where(e1 == e, shift, 0)
        pos2 = pos2 + jnp.where(e2 == e, shift, 0)
    lane = lax.broadcasted_iota(I32, (tm, LANES), 1)

    def gate_terms(g):
        hi = g.astype(BF16).astype(F32)
        return jnp.where(lane == 0, hi, jnp.where(lane == 1, g - hi, 0.0)).astype(BF16)

    gate1 = gate_terms(r_ref[:, 2:3])
    gate2 = gate_terms(r_ref[:, 3:4])
    blk = 256
    for k in range(TILE_ROWS // blk):
        row = lax.broadcasted_iota(I32, (blk, tm), 0) + k * blk
        pick1 = jnp.where(row == pos1, 1.0, 0.0).astype(BF16)
        pick2 = jnp.where(row == pos2, 1.0, 0.0).astype(BF16)
        at = slice(k * blk, (k + 1) * blk)
        buf[slot, at, :D_MODEL] = jnp.dot(pick1 + pick2, x_ref[...], preferred_element_type=F32).astype(BF16)
        buf[slot, at, D_MODEL:] = (jnp.dot(pick1, gate1, preferred_element_type=F32)
                                   + jnp.dot(pick2, gate2, preferred_element_type=F32)).astype(BF16)

    def send(c, carry):
        _chunk_copy_out(buf, slot, c, crow_ref[i * TILE_CHUNKS + c], xs_hbm, sem).start()
        return carry
    lax.fori_loop(0, nch_ref[i], send, 0)

    @pl.when(i == n_t - 1)
    def _():
        @pl.when(i >= 1)
        def _():
            wait_tile(i - 1, 1 - slot)
        wait_tile(i, slot)
        for_each_fill(lambda e, k: zero_tail(e, k).wait(), lambda t: zero_tile(t).wait())


def _moe_dispatch(x_bf, route_t, route, tables, tm):
    m, d = x_bf.shape
    n_rows = _sorted_rows(m, tm)
    grid_spec = pltpu.PrefetchScalarGridSpec(
        num_scalar_prefetch=6,
        grid=(m // tm,),
        in_specs=[pl.BlockSpec((tm, d), lambda i, *_: (i, 0)),
                  pl.BlockSpec((8, tm), lambda i, *_: (0, i)),
                  pl.BlockSpec((tm, LANES), lambda i, *_: (i, 0))],
        out_specs=pl.BlockSpec(memory_space=pl.ANY),
        scratch_shapes=[pltpu.VMEM((2, TILE_ROWS, SORTED_WIDTH), BF16), pltpu.VMEM((tm, SORTED_WIDTH), BF16),
                        pltpu.SemaphoreType.DMA((2,)), pltpu.SemaphoreType.DMA((2,))],
    )
    return pl.pallas_call(
        functools.partial(_moe_dispatch_kernel, tm=tm, n_tiles=n_rows // tm),
        name="moe_dispatch",
        grid_spec=grid_spec,
        out_shape=jax.ShapeDtypeStruct((n_rows, SORTED_WIDTH), BF16),
        compiler_params=_params(("arbitrary",)),
    )(tables["n_chunks"], tables["chunk_row"], tables["pos_tab"], tables["tail_first"], tables["tail_chunks"],
      tables["tile_valid"], x_bf, route_t, route)


def _cast_rows(src_ref, dst_ref):
    rows = 32 * 8 * LANES // src_ref.shape[1]

    def body(i, carry):
        at = pl.ds(pl.multiple_of(i * rows, rows), rows)
        dst_ref[at, :] = src_ref[at, :].astype(dst_ref.dtype)
        return carry
    lax.fori_loop(0, src_ref.shape[0] // rows, body, 0, unroll=2)


def _moe_ffn_kernel(te_ref, tv_ref, tf_ref, tn_ref, nv_ref, x_ref, wg_hbm, wu_hbm, wd_hbm, o_ref,
                    wg_f, wu_f, wd_f, wg_bf, wu_bf, wd_bf, sem_w, *, layer):
    j = pl.program_id(0)
    valid = tv_ref[j] == 1

    def weight_copies(e):
        return (pltpu.make_async_copy(wg_hbm.at[layer, e], wg_f, sem_w.at[0]),
                pltpu.make_async_copy(wu_hbm.at[layer, e], wu_f, sem_w.at[1]),
                pltpu.make_async_copy(wd_hbm.at[layer, e], wd_f, sem_w.at[2]))

    @pl.when(j == 0)
    def _():
        for c in weight_copies(te_ref[0]):
            c.start()

    @pl.when(valid & (tf_ref[j] == 1))
    def _():
        for c in weight_copies(te_ref[j]):
            c.wait()
        _cast_rows(wg_f, wg_bf)
        _cast_rows(wu_f, wu_bf)
        _cast_rows(wd_f, wd_bf)

        @pl.when(tn_ref[j] >= 0)
        def _():
            for c in weight_copies(tn_ref[j]):
                c.start()

    @pl.when(valid)
    def _():
        xb = x_ref[:, :D_MODEL]
        gate = x_ref[:, D_MODEL:D_MODEL + 1].astype(F32) + x_ref[:, D_MODEL + 1:D_MODEL + 2].astype(F32)
        hg = jnp.dot(xb, wg_bf[...], preferred_element_type=F32)
        hu = jnp.dot(xb, wu_bf[...], preferred_element_type=F32)
        act = (hg * (1.0 / (1.0 + jnp.exp(-hg))) * hu).astype(BF16)
        o_ref[...] = (jnp.dot(act, wd_bf[...], preferred_element_type=F32) * gate).astype(o_ref.dtype)

    @pl.when(jnp.logical_not(valid))
    def _():
        o_ref[...] = jnp.zeros(o_ref.shape, o_ref.dtype)


def _moe_ffn(x_sorted, w_gate, w_up, w_down, layer, tables, tm):
    n_rows, width = x_sorted.shape
    d, f = w_gate.shape[2], w_gate.shape[3]
    any_spec = pl.BlockSpec(memory_space=pl.ANY)
    grid_spec = pltpu.PrefetchScalarGridSpec(
        num_scalar_prefetch=5,
        grid=(n_rows // tm,),
        in_specs=[pl.BlockSpec((tm, width), lambda j, te, tv, tf, tn, nv: (jnp.minimum(j, nv[0] - 1), 0)),
                  any_spec, any_spec, any_spec],
        out_specs=pl.BlockSpec((tm, d), lambda j, *_: (j, 0)),
        scratch_shapes=[pltpu.VMEM((d, f), F32), pltpu.VMEM((d, f), F32), pltpu.VMEM((f, d), F32),
                        pltpu.VMEM((d, f), BF16), pltpu.VMEM((d, f), BF16), pltpu.VMEM((f, d), BF16),
                        pltpu.SemaphoreType.DMA((3,))],
    )
    return pl.pallas_call(
        functools.partial(_moe_ffn_kernel, layer=layer),
        name="moe_ffn",
        grid_spec=grid_spec,
        out_shape=jax.ShapeDtypeStruct((n_rows, d), BF16),
        compiler_params=_params(("arbitrary",)),
    )(tables["tile_expert"], tables["tile_valid"], tables["tile_first"], tables["tile_next"], tables["n_valid"],
      x_sorted, w_gate, w_up, w_down)


def _moe_combine_kernel(nch_ref, crow_ref, ys_hbm, x_ref, r_ref, ptab_ref, g_ref, b_ref, o_ref, obf_ref,
                        buf, sem, *, tm):
    i = pl.program_id(0)
    n_t = pl.num_programs(0)
    slot = i % 2

    def fetch(tile, s):
        def body(c, carry):
            _chunk_copy_in(ys_hbm, crow_ref[tile * TILE_CHUNKS + c], buf, s, c, sem).start()
            return carry
        lax.fori_loop(0, nch_ref[tile], body, 0)

    @pl.when(i == 0)
    def _():
        buf[...] = jnp.zeros(buf.shape, buf.dtype)
        fetch(0, 0)

    @pl.when(i + 1 < n_t)
    def _():
        fetch(i + 1, 1 - slot)

    def arrived(c, carry):
        _chunk_copy_in(ys_hbm, 0, buf, slot, 0, sem).wait()
        return carry
    lax.fori_loop(0, nch_ref[i], arrived, 0)

    lane = lax.broadcasted_iota(I32, (tm, LANES), 1)
    shift = ptab_ref[...]
    e1 = r_ref[:, 0:1].astype(I32)
    e2 = r_ref[:, 1:2].astype(I32)
    pos1 = (r_ref[:, 4:5] + jnp.sum(jnp.where(lane == e1, shift, 0.0), axis=1, keepdims=True)).astype(I32)
    pos2 = (r_ref[:, 5:6] + jnp.sum(jnp.where(lane == e2, shift, 0.0), axis=1, keepdims=True)).astype(I32)
    col = lax.broadcasted_iota(I32, (tm, TILE_ROWS), 1)
    pick = jnp.where((col == pos1) | (col == pos2), 1.0, 0.0).astype(BF16)
    y = jnp.dot(pick, buf[slot], preferred_element_type=F32)
    out = _layernorm_rows(DEEPNORM_ALPHA * x_ref[...] + y, g_ref[...], b_ref[...])
    o_ref[...] = out
    obf_ref[...] = out.astype(BF16)


def _moe_combine(y_sorted, x, route, tables, g, b, tm):
    m, d = x.shape
    grid_spec = pltpu.PrefetchScalarGridSpec(
        num_scalar_prefetch=2,
        grid=(m // tm,),
        in_specs=[pl.BlockSpec(memory_space=pl.ANY),
                  pl.BlockSpec((tm, d), lambda i, *_: (i, 0)),
                  pl.BlockSpec((tm, LANES), lambda i, *_: (i, 0)),
                  pl.BlockSpec((None, 1, LANES), lambda i, *_: (i, 0, 0)),
                  pl.BlockSpec((1, d), lambda i, *_: (0, 0)),
                  pl.BlockSpec((1, d), lambda i, *_: (0, 0))],
        out_specs=[pl.BlockSpec((tm, d), lambda i, *_: (i, 0)),
                   pl.BlockSpec((tm, d), lambda i, *_: (i, 0))],
        scratch_shapes=[pltpu.VMEM((2, TILE_ROWS, d), BF16), pltpu.SemaphoreType.DMA((2,))],
    )
    pos_rows = _pad_cols(tables["pos_rows"])[:, None, :]
    return pl.pallas_call(
        functools.partial(_moe_combine_kernel, tm=tm),
        name="moe_combine",
        grid_spec=grid_spec,
        out_shape=[jax.ShapeDtypeStruct((m, d), F32), jax.ShapeDtypeStruct((m, d), BF16)],
        compiler_params=_params(("arbitrary",)),
    )(tables["n_chunks"], tables["chunk_row"], y_sorted, x, route, pos_rows, g, b)


def _moe_layer(x, x_bf, w_group, b_group, w_router, b_router, w_gate, w_up, w_down, layer, ln_g, ln_b):
    n_tokens = x.shape[0]
    tm = min(MOE_TM, n_tokens)
    w_route = _pad_cols(jnp.concatenate([w_group, w_router], axis=1))
    b_route = _pad_cols(jnp.concatenate([b_group, b_router])[None, :])
    route, route_t, before, counts = _route(x, w_route, b_route, tm)
    tables = _dispatch_tables(before, counts, n_tokens, tm)
    x_sorted = _moe_dispatch(x_bf, route_t, route, tables, tm)
    y_sorted = _moe_ffn(x_sorted, w_gate, w_up, w_down, layer, tables, tm)
    return _moe_combine(y_sorted, x, route, tables, ln_g[None, :], ln_b[None, :], tm)


def kernel(x, t5_rel_bias, mla_w_in, mla_q_norm, mla_kv_norm, mla_w_q_up, mla_w_kv_up, mla_w_o, fox_w_in, fox_b_f, fox_w_o, dsa_w_in, dsa_w_o, ln_g, ln_b, moe_w_group, moe_b_group, moe_w_router, moe_b_router, moe_w_gate, moe_w_up, moe_w_down):
    batch, seq, d = x.shape
    x = x.reshape(batch * seq, d)
    x_bf = x.astype(BF16)
    for layer in range(DEPTH):
        kind = layer % N_MIXERS
        j = layer // N_MIXERS
        g0, b0 = ln_g[layer, 0], ln_b[layer, 0]
        if kind == 0:
            x, x_bf = _mla_layer(x, x_bf, mla_w_in[j], mla_q_norm[j], mla_kv_norm[j], mla_w_q_up[j],
                                 mla_w_kv_up[j], mla_w_o[j], g0, b0, batch, seq)
        elif kind == 1:
            x, x_bf = _fox_layer(x, x_bf, fox_w_in[j], fox_b_f[j], fox_w_o[j], g0, b0, batch, seq)
        else:
            x, x_bf = _dsa_layer(x, x_bf, t5_rel_bias, dsa_w_in[j], dsa_w_o[j], g0, b0, batch, seq)
        x, x_bf = _moe_layer(x, x_bf, moe_w_group[layer], moe_b_group[layer], moe_w_router[layer], moe_b_router[layer],
                             moe_w_gate, moe_w_up, moe_w_down, layer, ln_g[layer, 1], ln_b[layer, 1])
    return x.reshape(batch, seq, d)
```

```python
import functools
import math

import numpy as np
import jax
import jax.numpy as jnp
from jax import lax
from jax.experimental import pallas as pl
from jax.experimental.pallas import tpu as pltpu

F32 = jnp.float32
BF16 = jnp.bfloat16
I32 = jnp.int32

D_MODEL = 2048
DEPTH = 4
CHUNK = 64
N_MIXERS = 3
DEEPNORM_ALPHA = (2.0 * DEPTH) ** 0.25

HEADS = 16
HEAD_DIM = 128
MLA_NOPE_DIM = 128
MLA_ROPE_DIM = 64
MLA_Q_RANK = 512
MLA_KV_RANK = 512
MLA_QK_PAD = 256
ROPE_THETA = 10000.0

IDX_HEADS = 16
IDX_DIM = 64
IDX_TOPK_MAX = 256

T5_BUCKETS = 32
T5_MAX_DISTANCE = 128
T5_FAR_BUCKET = T5_BUCKETS // 2 - 1
T5_BAND_BACK = 128

MOE_GROUPS = 4
MOE_EXPERTS_PER_GROUP = 8
MOE_EXPERTS = MOE_GROUPS * MOE_EXPERTS_PER_GROUP
MOE_D_FF = 512

LN_EPS = 1e-5
RMS_EPS = 1e-6
NEG_INF = -1e30

LANES = 128
VMEM_LIMIT = 56 * 1024 * 1024

ATTN_TQ = 256
MOE_TM = 256
ROW_ALIGN = 16
TILE_ROWS = -(-(2 * MOE_TM + MOE_EXPERTS * (ROW_ALIGN - 1)) // 256) * 256
TILE_CHUNKS = TILE_ROWS // ROW_ALIGN
SORTED_WIDTH = D_MODEL + LANES


def _params(semantics, vmem=VMEM_LIMIT):
    return pltpu.CompilerParams(dimension_semantics=semantics, vmem_limit_bytes=vmem)


def _cast_rows(src_ref, dst_ref):
    rows = 16
    while 2 * rows * src_ref.shape[1] <= 32 * 8 * LANES and src_ref.shape[0] % (2 * rows) == 0:
        rows *= 2

    def body(i, carry):
        at = pl.ds(pl.multiple_of(i * rows, rows), rows)
        dst_ref[at, :] = src_ref[at, :].astype(dst_ref.dtype)
        return carry
    lax.fori_loop(0, src_ref.shape[0] // rows, body, 0, unroll=2)


def _mm_kernel(a_ref, w_ref, o_ref, w_bf, *, scale, scaled_cols):
    @pl.when(pl.program_id(1) == 0)
    def _():
        _cast_rows(w_ref, w_bf)

    acc = jnp.dot(a_ref[...], w_bf[...], preferred_element_type=F32)
    if scaled_cols:
        tn = o_ref.shape[1]
        col = pl.program_id(0) * tn + lax.broadcasted_iota(I32, (1, tn), 1)
        acc = acc * jnp.where(col < scaled_cols, scale, 1.0)
    o_ref[...] = acc.astype(o_ref.dtype)


def _matmul(a, w, *, n_out, tn, tm=1024, scale=1.0, scaled_cols=0, out_dtype=BF16):
    m, k = a.shape
    tm = min(tm, m)
    kern = functools.partial(_mm_kernel, scale=scale, scaled_cols=scaled_cols)
    return pl.pallas_call(
        kern,
        name="proj_mm",
        grid=(n_out // tn, m // tm),
        in_specs=[pl.BlockSpec((tm, k), lambda j, i: (i, 0)),
                  pl.BlockSpec((k, tn), lambda j, i: (0, j))],
        out_specs=pl.BlockSpec((tm, tn), lambda j, i: (i, j)),
        out_shape=jax.ShapeDtypeStruct((m, n_out), out_dtype),
        scratch_shapes=[pltpu.VMEM((k, tn), BF16)],
        compiler_params=_params(("arbitrary", "arbitrary")),
    )(a, w)


def _split3(v):
    hi = v.astype(BF16)
    r1 = v - hi.astype(F32)
    mid = r1.astype(BF16)
    lo = (r1 - mid.astype(F32)).astype(BF16)
    return hi, mid, lo


def _dot_precise(x, w):
    xh = x.astype(BF16)
    xm = (x - xh.astype(F32)).astype(BF16)
    wh = w.astype(BF16)
    wm = (w - wh.astype(F32)).astype(BF16)
    d = lambda p, q: jnp.dot(p, q, preferred_element_type=F32)
    return (d(xm, wh) + d(xh, wm)) + d(xh, wh)


def _small_mm_kernel(x_ref, w_ref, b_ref, o_ref):
    o_ref[...] = _dot_precise(x_ref[...], w_ref[...]) + b_ref[...]


def _small_matmul(x, w, b, tm=512):
    m, k = x.shape
    tm = min(tm, m)
    return pl.pallas_call(
        _small_mm_kernel,
        name="small_mm",
        grid=(m // tm,),
        in_specs=[pl.BlockSpec((tm, k), lambda i: (i, 0)),
                  pl.BlockSpec((k, LANES), lambda i: (0, 0)),
                  pl.BlockSpec((1, LANES), lambda i: (0, 0))],
        out_specs=pl.BlockSpec((tm, LANES), lambda i: (i, 0)),
        out_shape=jax.ShapeDtypeStruct((m, LANES), F32),
        compiler_params=_params(("arbitrary",)),
    )(x, w, b)


def _pad_cols(w, width=LANES):
    return jnp.pad(w, ((0, 0), (0, width - w.shape[1])))


def _layernorm_rows(z, g, b):
    mu = jnp.mean(z, axis=-1, keepdims=True)
    zc = z - mu
    var = jnp.mean(zc * zc, axis=-1, keepdims=True)
    return zc * lax.rsqrt(var + LN_EPS) * g + b


def _mm_res_ln_kernel(a_ref, w_ref, x_ref, g_ref, b_ref, o_ref, obf_ref):
    y = jnp.dot(a_ref[...], w_ref[...], preferred_element_type=F32)
    out = _layernorm_rows(DEEPNORM_ALPHA * x_ref[...] + y, g_ref[...], b_ref[...])
    o_ref[...] = out
    obf_ref[...] = out.astype(BF16)


def _mm_res_ln(a, w, x, g, b, tm=512):
    m, k = a.shape
    d = w.shape[1]
    tm = min(tm, m)
    return pl.pallas_call(
        _mm_res_ln_kernel,
        name="out_proj_res_ln",
        grid=(m // tm,),
        in_specs=[pl.BlockSpec((tm, k), lambda i: (i, 0)),
                  pl.BlockSpec((k, d), lambda i: (0, 0)),
                  pl.BlockSpec((tm, d), lambda i: (i, 0)),
                  pl.BlockSpec((1, d), lambda i: (0, 0)),
                  pl.BlockSpec((1, d), lambda i: (0, 0))],
        out_specs=[pl.BlockSpec((tm, d), lambda i: (i, 0)),
                   pl.BlockSpec((tm, d), lambda i: (i, 0))],
        out_shape=[jax.ShapeDtypeStruct((m, d), F32), jax.ShapeDtypeStruct((m, d), BF16)],
        compiler_params=_params(("arbitrary",)),
    )(a, w, x, g, b)


def _rope_tables(seq):
    half = MLA_ROPE_DIM // 2
    inv_freq = ROPE_THETA ** (-jnp.arange(half, dtype=F32) / half)
    ang = jnp.arange(seq, dtype=I32).astype(F32)[:, None] * inv_freq[None, :]
    cos, sin = jnp.cos(ang), jnp.sin(ang)
    z = jnp.zeros_like(cos)
    cos_t = jnp.concatenate([cos, cos, z, z], axis=1)
    sin_a = jnp.concatenate([-sin, z, z, z], axis=1)
    sin_b = jnp.concatenate([z, sin, z, z], axis=1)
    return cos_t, sin_a, sin_b


def _rope_lanes(r, cos_t, sin_a, sin_b):
    from_right = pltpu.roll(r, LANES - MLA_ROPE_DIM // 2, 1)
    from_left = pltpu.roll(r, MLA_ROPE_DIM // 2, 1)
    return r * cos_t + from_right * sin_a + from_left * sin_b


def _rms_rows(v, g):
    return v * lax.rsqrt(jnp.mean(v * v, axis=-1, keepdims=True) + RMS_EPS) * g


def _mla_in_kernel(x_ref, w_ref, qn_ref, kvn_ref, cos_ref, sina_ref, sinb_ref, cq_ref, ckv_ref, kr_ref):
    h = jnp.dot(x_ref[...], w_ref[...], preferred_element_type=F32)
    cq_ref[...] = _rms_rows(h[:, :MLA_Q_RANK], qn_ref[...]).astype(BF16)
    ckv_ref[...] = _rms_rows(h[:, MLA_Q_RANK:MLA_Q_RANK + MLA_KV_RANK], kvn_ref[...]).astype(BF16)
    r = h[:, MLA_Q_RANK + MLA_KV_RANK:]
    kr_ref[...] = _rope_lanes(r, cos_ref[...], sina_ref[...], sinb_ref[...]).astype(BF16)


def _mla_in(x_bf, w_in_pad, q_norm, kv_norm, tables, seq, tm=512):
    m, k = x_bf.shape
    tm = min(tm, seq)
    nw = w_in_pad.shape[1]
    per_seq = seq // tm
    tab_spec = pl.BlockSpec((tm, LANES), lambda i: (i % per_seq, 0))
    return pl.pallas_call(
        _mla_in_kernel,
        name="mla_in",
        grid=(m // tm,),
        in_specs=[pl.BlockSpec((tm, k), lambda i: (i, 0)),
                  pl.BlockSpec((k, nw), lambda i: (0, 0)),
                  pl.BlockSpec((1, MLA_Q_RANK), lambda i: (0, 0)),
                  pl.BlockSpec((1, MLA_KV_RANK), lambda i: (0, 0)),
                  tab_spec, tab_spec, tab_spec],
        out_specs=[pl.BlockSpec((tm, MLA_Q_RANK), lambda i: (i, 0)),
                   pl.BlockSpec((tm, MLA_KV_RANK), lambda i: (i, 0)),
                   pl.BlockSpec((tm, LANES), lambda i: (i, 0))],
        out_shape=[jax.ShapeDtypeStruct((m, MLA_Q_RANK), BF16),
                   jax.ShapeDtypeStruct((m, MLA_KV_RANK), BF16),
                   jax.ShapeDtypeStruct((m, LANES), BF16)],
        compiler_params=_params(("arbitrary",)),
    )(x_bf, w_in_pad, q_norm, kv_norm, *tables)


def _fill_v_aug(va_ref, v_ref):
    va_ref[:, :HEAD_DIM] = v_ref[...]
    va_ref[:, HEAD_DIM:] = jnp.ones((v_ref.shape[0], HEAD_DIM), BF16)


def _softmax_pv(pieces):
    m = None
    for s, _ in pieces:
        pm = jnp.max(s, axis=1, keepdims=True)
        m = pm if m is None else jnp.maximum(m, pm)
    acc = None
    for s, v in pieces:
        t = jnp.dot(jnp.exp((s - m).astype(BF16)), v, preferred_element_type=F32)
        acc = t if acc is None else acc + t
    return acc[:, :HEAD_DIM] / acc[:, HEAD_DIM:HEAD_DIM + 1]


def _nt_dot(q, k):
    return lax.dot_general(q, k, (((1,), (1,)), ((), ())), preferred_element_type=F32)


def _attend_tiles(n_tiles, scores, finish):
    s_next = scores(0)
    for i in range(n_tiles):
        s = s_next
        if i + 1 < n_tiles:
            s_next = scores(i + 1)
        finish(i, s)


def _mla_q_up_kernel(c_ref, w_ref, cos_ref, sina_ref, sinb_ref, o_ref, w_bf, *, scale):
    @pl.when(pl.program_id(1) == 0)
    def _():
        _cast_rows(w_ref, w_bf)

    acc = jnp.dot(c_ref[...], w_bf[...], preferred_element_type=F32) * scale
    for h in range(o_ref.shape[1] // MLA_QK_PAD):
        at = h * MLA_QK_PAD
        o_ref[:, at:at + MLA_NOPE_DIM] = acc[:, at:at + MLA_NOPE_DIM].astype(BF16)
        r = acc[:, at + MLA_NOPE_DIM:at + MLA_QK_PAD]
        o_ref[:, at + MLA_NOPE_DIM:at + MLA_QK_PAD] = _rope_lanes(
            r, cos_ref[...], sina_ref[...], sinb_ref[...]).astype(BF16)


def _mla_kv_up_kernel(c_ref, w_ref, kr_ref, k_ref, v_ref, w_bf):
    @pl.when(pl.program_id(1) == 0)
    def _():
        _cast_rows(w_ref, w_bf)

    acc = jnp.dot(c_ref[...], w_bf[...], preferred_element_type=F32)
    ones = jnp.ones((acc.shape[0], HEAD_DIM), BF16)
    for h in range(k_ref.shape[1] // MLA_QK_PAD):
        at = h * MLA_QK_PAD
        k_ref[:, at:at + MLA_NOPE_DIM] = acc[:, at:at + MLA_NOPE_DIM].astype(BF16)
        k_ref[:, at + MLA_NOPE_DIM:at + MLA_QK_PAD] = kr_ref[...]
        v_ref[:, at:at + HEAD_DIM] = acc[:, at + MLA_NOPE_DIM:at + MLA_QK_PAD].astype(BF16)
        v_ref[:, at + HEAD_DIM:at + 2 * HEAD_DIM] = ones


def _mla_up(cq, ckv, kr, w_q_up, w_kv_up, tables, seq, scale, tm=1024, tn=1024):
    m, rank = cq.shape
    tm = min(tm, seq)
    per_seq = seq // tm
    n = HEADS * MLA_QK_PAD
    tab_spec = pl.BlockSpec((tm, LANES), lambda j, i: (i % per_seq, 0))
    row_spec = pl.BlockSpec((tm, rank), lambda j, i: (i, 0))
    w_spec = pl.BlockSpec((rank, tn), lambda j, i: (0, j))
    out_spec = pl.BlockSpec((tm, tn), lambda j, i: (i, j))
    out = jax.ShapeDtypeStruct((m, n), BF16)
    q = pl.pallas_call(
        functools.partial(_mla_q_up_kernel, scale=scale),
        name="mla_q_up",
        grid=(n // tn, m // tm),
        in_specs=[row_spec, w_spec, tab_spec, tab_spec, tab_spec],
        out_specs=out_spec,
        out_shape=out,
        scratch_shapes=[pltpu.VMEM((rank, tn), BF16)],
        compiler_params=_params(("arbitrary", "arbitrary")),
    )(cq, w_q_up, *tables)
    k, v = pl.pallas_call(
        _mla_kv_up_kernel,
        name="mla_kv_up",
        grid=(n // tn, m // tm),
        in_specs=[row_spec, w_spec, pl.BlockSpec((tm, LANES), lambda j, i: (i, 0))],
        out_specs=[out_spec, out_spec],
        out_shape=[out, out],
        scratch_shapes=[pltpu.VMEM((rank, tn), BF16)],
        compiler_params=_params(("arbitrary", "arbitrary")),
    )(ckv, w_kv_up, kr)
    return q, k, v


def _mla_attn_kernel(q_ref, k_ref, v_ref, o_ref, *, tq):
    seq = q_ref.shape[0]
    row = lax.broadcasted_iota(I32, (tq, tq), 0)
    col = lax.broadcasted_iota(I32, (tq, tq), 1)
    visible = (col // CHUNK) <= (row // CHUNK)

    def scores(i):
        return _nt_dot(q_ref[i * tq:(i + 1) * tq, :], k_ref[:(i + 1) * tq, :])

    def finish(i, s):
        lo, kv = i * tq, (i + 1) * tq
        pieces = [(jnp.where(visible, s[:, lo:], NEG_INF), v_ref[lo:kv, :])]
        if lo:
            pieces.append((s[:, :lo], v_ref[:lo, :]))
        o_ref[lo:kv, :] = _softmax_pv(pieces).astype(o_ref.dtype)

    _attend_tiles(seq // tq, scores, finish)


def _mla_attn(q, k, v, batch, seq):
    tq = min(ATTN_TQ, seq)
    head_spec = pl.BlockSpec((seq, MLA_QK_PAD), lambda b, h: (b, h))
    return pl.pallas_call(
        functools.partial(_mla_attn_kernel, tq=tq),
        name="mla_attn",
        grid=(batch, HEADS),
        in_specs=[head_spec, head_spec, head_spec],
        out_specs=pl.BlockSpec((seq, HEAD_DIM), lambda b, h: (b, h)),
        out_shape=jax.ShapeDtypeStruct((batch * seq, HEADS * HEAD_DIM), BF16),
        compiler_params=_params(("arbitrary", "arbitrary")),
    )(q, k, v)


def _mla_q_up_layout(w_q_up):
    rank = w_q_up.shape[0]
    w = w_q_up.reshape(rank, HEADS, MLA_NOPE_DIM + MLA_ROPE_DIM)
    w = jnp.pad(w, ((0, 0), (0, 0), (0, MLA_QK_PAD - MLA_NOPE_DIM - MLA_ROPE_DIM)))
    return w.reshape(rank, HEADS * MLA_QK_PAD)


def _mla_layer(x, x_bf, w_in, q_norm, kv_norm, w_q_up, w_kv_up, w_o, ln_g, ln_b, batch, seq):
    tables = _rope_tables(seq)
    w_in_pad = jnp.pad(w_in, ((0, 0), (0, MLA_Q_RANK + MLA_KV_RANK + LANES - w_in.shape[1]))).astype(BF16)
    cq, ckv, kr = _mla_in(x_bf, w_in_pad, q_norm[None, :], kv_norm[None, :], tables, seq)
    scale = (MLA_NOPE_DIM + MLA_ROPE_DIM) ** -0.5
    q, k, v = _mla_up(cq, ckv, kr, _mla_q_up_layout(w_q_up), w_kv_up, tables, seq, scale)
    o = _mla_attn(q, k, v, batch, seq)
    return _mm_res_ln(o, w_o.astype(BF16), x, ln_g[None, :], ln_b[None, :])


def _log_sigmoid(z):
    return jnp.minimum(z, 0.0) - jnp.log1p(jnp.exp(-jnp.abs(z)))


def _fox_gate_kernel(f_ref, c_ref, *, blk):
    seq = f_ref.shape[0]
    tri = (lax.broadcasted_iota(I32, (blk, blk), 1) <= lax.broadcasted_iota(I32, (blk, blk), 0)).astype(BF16)
    carry = jnp.zeros((1, LANES), F32)
    for j in range(seq // blk):
        lf = _log_sigmoid(f_ref[j * blk:(j + 1) * blk, :])
        hi, mid, lo = _split3(lf)
        d = lambda p: jnp.dot(tri, p, preferred_element_type=F32)
        c = (d(lo) + d(mid)) + d(hi) + carry
        c_ref[j * blk:(j + 1) * blk, :] = c
        carry = c[blk - 1:blk, :]


def _fox_gate(f_logit, batch, seq):
    blk = min(256, seq)
    return pl.pallas_call(
        functools.partial(_fox_gate_kernel, blk=blk),
        name="fox_gate",
        grid=(batch,),
        in_specs=[pl.BlockSpec((seq, LANES), lambda b: (b, 0))],
        out_specs=pl.BlockSpec((seq, LANES), lambda b: (b, 0)),
        out_shape=jax.ShapeDtypeStruct((batch * seq, LANES), F32),
        compiler_params=_params(("arbitrary",)),
    )(f_logit)


def _fox_attn_kernel(q_ref, k_ref, v_ref, c_ref, o_ref, qa_ref, ka_ref, va_ref, *, tq):
    seq = q_ref.shape[0]
    h = pl.program_id(1)
    lane = lax.broadcasted_iota(I32, (seq, LANES), 1)
    c = jnp.sum(jnp.where(lane == h, c_ref[...], 0.0), axis=1, keepdims=True)
    terms = [t.astype(F32) for t in _split3(c)]
    q_side = jnp.where((lane >= 3) & (lane < 6), 1.0, 0.0)
    k_side = jnp.where(lane < 3, 1.0, 0.0)
    for j, t in enumerate(terms):
        q_side = jnp.where(lane == j, t, q_side)
        k_side = jnp.where(lane == 3 + j, -t, k_side)
    qa_ref[:, :HEAD_DIM] = q_ref[...]
    qa_ref[:, HEAD_DIM:] = q_side.astype(BF16)
    ka_ref[:, :HEAD_DIM] = k_ref[...]
    ka_ref[:, HEAD_DIM:] = k_side.astype(BF16)
    _fill_v_aug(va_ref, v_ref)
    row = lax.broadcasted_iota(I32, (tq, tq), 0)
    col = lax.broadcasted_iota(I32, (tq, tq), 1)
    def scores(i):
        return _nt_dot(qa_ref[i * tq:(i + 1) * tq, :], ka_ref[:(i + 1) * tq, :])

    def finish(i, s):
        lo, kv = i * tq, (i + 1) * tq
        pieces = [(jnp.where(col <= row, s[:, lo:], NEG_INF), va_ref[lo:kv, :])]
        if lo:
            pieces.append((s[:, :lo], va_ref[:lo, :]))
        o_ref[lo:kv, :] = _softmax_pv(pieces).astype(o_ref.dtype)

    _attend_tiles(seq // tq, scores, finish)


def _fox_attn(qkv, c_col, batch, seq):
    tq = min(ATTN_TQ, seq)
    wide = pltpu.VMEM((seq, 2 * HEAD_DIM), BF16)
    return pl.pallas_call(
        functools.partial(_fox_attn_kernel, tq=tq),
        name="fox_attn",
        grid=(batch, HEADS),
        in_specs=[pl.BlockSpec((seq, HEAD_DIM), lambda b, h: (b, h)),
                  pl.BlockSpec((seq, HEAD_DIM), lambda b, h: (b, HEADS + h)),
                  pl.BlockSpec((seq, HEAD_DIM), lambda b, h: (b, 2 * HEADS + h)),
                  pl.BlockSpec((seq, LANES), lambda b, h: (b, 0))],
        out_specs=pl.BlockSpec((seq, HEAD_DIM), lambda b, h: (b, h)),
        out_shape=jax.ShapeDtypeStruct((batch * seq, HEADS * HEAD_DIM), BF16),
        scratch_shapes=[wide, wide, wide],
        compiler_params=_params(("arbitrary", "arbitrary")),
    )(qkv, qkv, qkv, c_col)


def _fox_layer(x, x_bf, w_in, b_f, w_o, ln_g, ln_b, batch, seq):
    hd = HEADS * HEAD_DIM
    scale = HEAD_DIM ** -0.5
    qkv = _matmul(x_bf, w_in, n_out=3 * hd, tn=1024, scale=scale, scaled_cols=hd)
    f_logit = _small_matmul(x, _pad_cols(w_in[:, 3 * hd:]), _pad_cols(b_f[None, :]))
    o = _fox_attn(qkv, _fox_gate(f_logit, batch, seq), batch, seq)
    return _mm_res_ln(o, w_o.astype(BF16), x, ln_g[None, :], ln_b[None, :])


def _t5_bucket_table(tq):
    a = np.arange(tq, dtype=np.int64)[:, None]
    b = np.arange(tq + T5_BAND_BACK, dtype=np.int64)[None, :]
    rel = (b - T5_BAND_BACK) - a
    nb = T5_BUCKETS // 2
    max_exact = nb // 2
    ret = np.where(rel > 0, nb, 0)
    n = np.abs(rel)
    nf = np.maximum(n, 1).astype(np.float32)
    large = max_exact + (np.log(nf / np.float32(max_exact)) / np.float32(math.log(T5_MAX_DISTANCE / max_exact))
                         * np.float32(nb - max_exact)).astype(np.int32)
    large = np.minimum(large, nb - 1)
    return (ret + np.where(n < max_exact, n, large)).astype(np.int32)


def _t5_band_kernel(t5_ref, bucket_ref, o_ref):
    bucket = bucket_ref[...]

    def per_head(h, carry):
        acc = jnp.zeros(bucket.shape, F32)
        for b in range(T5_BUCKETS):
            acc = jnp.where(bucket == b, t5_ref[b, h], acc)
        o_ref[h] = acc - t5_ref[T5_FAR_BUCKET, h]
        return carry

    lax.fori_loop(0, HEADS, per_head, 0)


def _t5_band(t5_bias, tq):
    bucket = jnp.asarray(_t5_bucket_table(tq))
    width = tq + T5_BAND_BACK
    return pl.pallas_call(
        _t5_band_kernel,
        name="t5_band",
        grid=(1,),
        in_specs=[pl.BlockSpec(memory_space=pltpu.SMEM),
                  pl.BlockSpec((tq, width), lambda i: (0, 0))],
        out_specs=pl.BlockSpec((HEADS, tq, width), lambda i: (0, 0, 0)),
        out_shape=jax.ShapeDtypeStruct((HEADS, tq, width), F32),
        compiler_params=_params(("arbitrary",)),
    )(t5_bias, bucket)


def _sortable_key(v):
    bits = pltpu.bitcast(v, I32)
    return bits ^ ((bits >> 31) & 0x7FFFFFFF)


def _kth_largest_key(key_ref, kv, k):
    rows = key_ref.shape[0]

    def count_ge(t):
        t_b = jnp.broadcast_to(t, (rows, LANES))
        acc = jnp.zeros((rows, LANES), I32)
        for c in range(kv // LANES):
            acc = acc + jnp.where(key_ref[:, c * LANES:(c + 1) * LANES] >= t_b, 1, 0)
        return jnp.sum(acc, axis=1, keepdims=True)

    t0 = jnp.where(count_ge(jnp.zeros((rows, 1), I32)) >= k, 0, jnp.iinfo(jnp.int32).min).astype(I32)

    def body(it, t):
        cand = t | (jnp.int32(1) << (30 - it))
        return jnp.where(count_ge(cand) >= k, cand, t)

    return lax.fori_loop(0, 31, body, t0)


def _dsa_index_kernel(qi_ref, tail_ref, o_ref, key_ref, *, tq, topk):
    seq = tail_ref.shape[0]
    k_idx = tail_ref[:, :IDX_DIM].astype(BF16)
    for i in range(seq // tq):
        kv = (i + 1) * tq
        rows = slice(i * tq, kv)
        w = tail_ref[rows, IDX_DIM:IDX_DIM + IDX_HEADS] * ((IDX_HEADS * IDX_DIM) ** -0.5)
        score = jnp.zeros((tq, kv), F32)
        for h in range(IDX_HEADS):
            logits = _nt_dot(qi_ref[rows, h * IDX_DIM:(h + 1) * IDX_DIM], k_idx[:kv, :])
            score = score + jnp.maximum(logits, 0.0) * w[:, h:h + 1]
        row = lax.broadcasted_iota(I32, (tq, kv), 0) + i * tq
        col = lax.broadcasted_iota(I32, (tq, kv), 1)
        admissible = (col // CHUNK) <= (row // CHUNK)
        if kv <= topk:
            keep = admissible
        else:
            key_ref[:, :kv] = _sortable_key(jnp.where(admissible, score, NEG_INF))
            thr = _kth_largest_key(key_ref, kv, topk)
            keep = admissible & (key_ref[:, :kv] >= thr)
        o_ref[rows, :kv] = jnp.where(keep, 0.0, NEG_INF).astype(o_ref.dtype)
        if kv < seq:
            o_ref[rows, kv:] = jnp.full((tq, seq - kv), NEG_INF, o_ref.dtype)


def _dsa_index(q_idx, tail, batch, seq, topk):
    tq = min(ATTN_TQ, seq)
    return pl.pallas_call(
        functools.partial(_dsa_index_kernel, tq=tq, topk=topk),
        name="dsa_index",
        grid=(batch,),
        in_specs=[pl.BlockSpec((seq, IDX_HEADS * IDX_DIM), lambda b: (b, 0)),
                  pl.BlockSpec((seq, LANES), lambda b: (b, 0))],
        out_specs=pl.BlockSpec((seq, seq), lambda b: (b, 0)),
        out_shape=jax.ShapeDtypeStruct((batch * seq, seq), BF16),
        scratch_shapes=[pltpu.VMEM((tq, seq), I32)],
        compiler_params=_params(("arbitrary",)),
    )(q_idx, tail)


def _dsa_attn_kernel(q_ref, k_ref, v_ref, sel_ref, band_ref, o_ref, va_ref, *, tq):
    seq = k_ref.shape[0]
    width = band_ref.shape[1]
    _fill_v_aug(va_ref, v_ref)
    def scores(i):
        return _nt_dot(q_ref[i * tq:(i + 1) * tq, :], k_ref[:(i + 1) * tq, :])

    def finish(i, s):
        lo, kv = i * tq, (i + 1) * tq
        near = min(width, kv)
        far = kv - near
        s = s + sel_ref[lo:kv, :kv].astype(F32)
        pieces = [(s[:, far:] + band_ref[:, width - near:], va_ref[far:kv, :])]
        if far:
            pieces.append((s[:, :far], va_ref[:far, :]))
        o_ref[lo:kv, :] = _softmax_pv(pieces).astype(o_ref.dtype)

    _attend_tiles(seq // tq, scores, finish)


def _dsa_attn(qkv, sel, band, batch, seq):
    tq = min(ATTN_TQ, seq)
    width = band.shape[2]
    return pl.pallas_call(
        functools.partial(_dsa_attn_kernel, tq=tq),
        name="dsa_attn",
        grid=(batch, HEADS),
        in_specs=[pl.BlockSpec((seq, HEAD_DIM), lambda b, h: (b, h)),
                  pl.BlockSpec((seq, HEAD_DIM), lambda b, h: (b, HEADS)),
                  pl.BlockSpec((seq, HEAD_DIM), lambda b, h: (b, HEADS + 1)),
                  pl.BlockSpec((seq, seq), lambda b, h: (b, 0)),
                  pl.BlockSpec((None, tq, width), lambda b, h: (h, 0, 0))],
        out_specs=pl.BlockSpec((seq, HEAD_DIM), lambda b, h: (b, h)),
        out_shape=jax.ShapeDtypeStruct((batch * seq, HEADS * HEAD_DIM), BF16),
        scratch_shapes=[pltpu.VMEM((seq, 2 * HEAD_DIM), BF16)],
        compiler_params=_params(("arbitrary", "arbitrary")),
    )(qkv, qkv, qkv, sel, band)


def _dsa_layer(x, x_bf, t5_bias, w_in, w_o, ln_g, ln_b, batch, seq):
    hd = HEADS * HEAD_DIM
    n_qkv = hd + 2 * HEAD_DIM
    n_qi = IDX_HEADS * IDX_DIM
    scale = HEAD_DIM ** -0.5
    qkv = _matmul(x_bf, w_in, n_out=n_qkv, tn=n_qkv // 3, scale=scale, scaled_cols=hd)
    q_idx = _matmul(x_bf, w_in[:, n_qkv:n_qkv + n_qi], n_out=n_qi, tn=n_qi)
    tail = _small_matmul(x, _pad_cols(w_in[:, n_qkv + n_qi:]), jnp.zeros((1, LANES), F32))
    topk = min(IDX_TOPK_MAX, seq // 4)
    sel = _dsa_index(q_idx, tail, batch, seq, topk)
    band = _t5_band(t5_bias, min(ATTN_TQ, seq))
    o = _dsa_attn(qkv, sel, band, batch, seq)
    return _mm_res_ln(o, w_o.astype(BF16), x, ln_g[None, :], ln_b[None, :])


def _route_kernel(x_ref, w_ref, b_ref, o_ref, ot_ref, before_ref, cnt_ref, run_ref):
    i = pl.program_id(0)

    @pl.when(i == 0)
    def _():
        run_ref[...] = jnp.zeros(run_ref.shape, F32)

    tm = o_ref.shape[0]
    logit = _dot_precise(x_ref[...], w_ref[...]) + b_ref[...]
    lane = lax.broadcasted_iota(I32, logit.shape, 1)
    big = jnp.int32(LANES)

    def first_argmax(mask):
        top = jnp.max(jnp.where(mask, logit, -jnp.inf), axis=1, keepdims=True)
        idx = jnp.min(jnp.where(mask & (logit == top), lane, big), axis=1, keepdims=True)
        return top, idx

    is_group = lane < MOE_GROUPS
    g_top, g_sel = first_argmax(is_group)
    p_g = 1.0 / jnp.sum(jnp.where(is_group, jnp.exp(logit - g_top), 0.0), axis=1, keepdims=True)
    lo = MOE_GROUPS + g_sel * MOE_EXPERTS_PER_GROUP
    in_group = (lane >= lo) & (lane < lo + MOE_EXPERTS_PER_GROUP)
    v1, i1 = first_argmax(in_group)
    v2, i2 = first_argmax(in_group & (lane != i1))
    ex = jnp.exp(v2 - v1)
    w1 = 1.0 / (1.0 + ex)
    w2 = ex / (1.0 + ex)
    e1 = i1 - MOE_GROUPS
    e2 = i2 - MOE_GROUPS
    pick1 = lane == e1
    pick2 = lane == e2
    both = jnp.where(pick1 | pick2, 1.0, 0.0)
    earlier = lax.broadcasted_iota(I32, (tm, tm), 1) < lax.broadcasted_iota(I32, (tm, tm), 0)
    before = jnp.dot(jnp.where(earlier, 1.0, 0.0).astype(BF16), both.astype(BF16),
                     preferred_element_type=F32) + run_ref[...]
    r1 = jnp.sum(jnp.where(pick1, before, 0.0), axis=1, keepdims=True)
    r2 = jnp.sum(jnp.where(pick2, before, 0.0), axis=1, keepdims=True)
    before_ref[...] = run_ref[...]
    run_ref[...] = run_ref[...] + jnp.sum(both, axis=0, keepdims=True)
    cnt_ref[...] = run_ref[...]
    out = jnp.where(lane == 0, e1.astype(F32), 0.0)
    out = jnp.where(lane == 1, e2.astype(F32), out)
    out = jnp.where(lane == 2, w1 * p_g, out)
    out = jnp.where(lane == 3, w2 * p_g, out)
    out = jnp.where(lane == 4, r1, out)
    out = jnp.where(lane == 5, r2, out)
    o_ref[...] = out
    ot_ref[...] = out.T[:8, :]


def _route(x, w, b, tm):
    m, k = x.shape
    n_t = m // tm
    return pl.pallas_call(
        _route_kernel,
        name="moe_route",
        grid=(n_t,),
        in_specs=[pl.BlockSpec((tm, k), lambda i: (i, 0)),
                  pl.BlockSpec((k, LANES), lambda i: (0, 0)),
                  pl.BlockSpec((1, LANES), lambda i: (0, 0))],
        out_specs=[pl.BlockSpec((tm, LANES), lambda i: (i, 0)),
                   pl.BlockSpec((8, tm), lambda i: (0, i)),
                   pl.BlockSpec((None, 1, LANES), lambda i: (i, 0, 0)),
                   pl.BlockSpec((1, LANES), lambda i: (0, 0))],
        out_shape=[jax.ShapeDtypeStruct((m, LANES), F32), jax.ShapeDtypeStruct((8, m), F32),
                   jax.ShapeDtypeStruct((n_t, 1, LANES), F32), jax.ShapeDtypeStruct((1, LANES), F32)],
        scratch_shapes=[pltpu.VMEM((1, LANES), F32)],
        compiler_params=_params(("arbitrary",)),
    )(x, w, b)


def _dispatch_tables(before, counts, n_tokens, tm):
    n_t = n_tokens // tm
    before = before[:, 0, :MOE_EXPERTS].astype(I32)
    total = counts[0, :MOE_EXPERTS].astype(I32)
    after = jnp.concatenate([before[1:], total[None, :]], axis=0)
    length = after - before
    length_al = (length + ROW_ALIGN - 1) // ROW_ALIGN * ROW_ALIGN
    t_ids = jnp.arange(n_t, dtype=I32)
    e_ids = jnp.arange(MOE_EXPERTS, dtype=I32)
    before_al = jnp.sum(jnp.where((t_ids[None, :] < t_ids[:, None])[:, :, None], length_al[None, :, :], 0), axis=1)
    total_al = jnp.sum(length_al, axis=0)
    seg = (total_al + tm - 1) // tm * tm
    ends = jnp.sum(jnp.where(e_ids[None, :] <= e_ids[:, None], seg[None, :], 0), axis=1)
    starts = ends - seg
    off = jnp.sum(jnp.where((e_ids[None, :] < e_ids[:, None])[None, :, :], length_al[:, None, :], 0), axis=2)
    used = jnp.sum(length_al, axis=1)
    n_chunks = used // ROW_ALIGN
    dstart = starts[None, :] + before_al
    c_row = jnp.arange(TILE_CHUNKS, dtype=I32) * ROW_ALIGN
    c_exp = jnp.sum(((off + length_al)[:, None, :] <= c_row[None, :, None]).astype(I32), axis=2)
    c_exp = jnp.minimum(c_exp, MOE_EXPERTS - 1)
    onehot = c_exp[:, :, None] == e_ids[None, None, :]
    chunk_row = jnp.sum(jnp.where(onehot, (dstart - off)[:, None, :], 0), axis=2) + c_row[None, :]
    pos_tab = off - before
    n_rows = _sorted_rows(n_tokens, tm)
    n_tiles = n_rows // tm
    tile_start = jnp.arange(n_tiles, dtype=I32) * tm
    tile_expert = jnp.minimum(jnp.sum((ends[None, :] <= tile_start[:, None]).astype(I32), axis=1), MOE_EXPERTS - 1)
    tile_valid = (tile_start < ends[-1]).astype(I32)
    tile_first = jnp.concatenate([jnp.ones((1,), I32), (tile_expert[1:] != tile_expert[:-1]).astype(I32)])
    later = (e_ids[None, :] > e_ids[:, None]) & (total[None, :] > 0)
    next_used = jnp.min(jnp.where(later, e_ids[None, :], MOE_EXPERTS), axis=1)
    next_used = jnp.where(next_used == MOE_EXPERTS, -1, next_used).astype(I32)
    tile_next = jnp.sum(jnp.where(tile_expert[:, None] == e_ids[None, :], next_used[None, :], 0), axis=1)
    tail_first = starts + total_al
    tail_chunks = (seg - total_al) // ROW_ALIGN
    return dict(n_chunks=n_chunks.astype(I32), chunk_row=chunk_row.reshape(-1).astype(I32),
                pos_tab=pos_tab.reshape(-1).astype(I32), pos_rows=pos_tab.astype(F32),
                tile_expert=tile_expert.astype(I32), tile_valid=tile_valid, tile_first=tile_first,
                tile_next=tile_next.astype(I32), n_valid=jnp.sum(tile_valid)[None].astype(I32),
                tail_first=tail_first.astype(I32),
                tail_chunks=tail_chunks.astype(I32))


def _sorted_rows(n_tokens, tm):
    n_t = n_tokens // tm
    bound = 2 * n_tokens + n_t * MOE_EXPERTS * (ROW_ALIGN - 1) + MOE_EXPERTS * (tm - ROW_ALIGN) + 1
    return (bound + tm - 1) // tm * tm


def _chunk_copy_out(buf, slot, c, row, hbm, sem):
    src = buf.at[slot, pl.ds(pl.multiple_of(c * ROW_ALIGN, ROW_ALIGN), ROW_ALIGN)]
    return pltpu.make_async_copy(src, hbm.at[pl.ds(pl.multiple_of(row, ROW_ALIGN), ROW_ALIGN)], sem.at[slot])


def _chunk_copy_in(hbm, row, buf, slot, c, sem):
    dst = buf.at[slot, pl.ds(pl.multiple_of(c * ROW_ALIGN, ROW_ALIGN), ROW_ALIGN)]
    return pltpu.make_async_copy(hbm.at[pl.ds(pl.multiple_of(row, ROW_ALIGN), ROW_ALIGN)], dst, sem.at[slot])


def _moe_dispatch_kernel(nch_ref, crow_ref, ptab_ref, tfirst_ref, tchunks_ref, tvalid_ref,
                         x_ref, rt_ref, r_ref, xs_hbm, buf, zeros, sem, sem_z, *, tm, n_tiles):
    i = pl.program_id(0)
    n_t = pl.num_programs(0)
    slot = i % 2

    def wait_tile(tile, s):
        def body(c, carry):
            _chunk_copy_out(buf, s, 0, 0, xs_hbm, sem).wait()
            return carry
        lax.fori_loop(0, nch_ref[tile], body, 0)

    def zero_tail(e, k):
        return pltpu.make_async_copy(
            zeros.at[pl.ds(0, ROW_ALIGN)],
            xs_hbm.at[pl.ds(pl.multiple_of(tfirst_ref[e] + k * ROW_ALIGN, ROW_ALIGN), ROW_ALIGN)], sem_z.at[0])

    def zero_tile(t):
        return pltpu.make_async_copy(zeros, xs_hbm.at[pl.ds(pl.multiple_of(t * tm, tm), tm)], sem_z.at[1])

    def for_each_fill(tail_fn, tile_fn):
        def per_expert(e, carry):
            def per_chunk(k, c2):
                tail_fn(e, k)
                return c2
            lax.fori_loop(0, tchunks_ref[e], per_chunk, 0)
            return carry
        lax.fori_loop(0, MOE_EXPERTS, per_expert, 0)

        def per_tile(t, carry):
            @pl.when(tvalid_ref[t] == 0)
            def _():
                tile_fn(t)
            return carry
        lax.fori_loop(0, n_tiles, per_tile, 0)

    @pl.when(i == 0)
    def _():
        zeros[...] = jnp.zeros(zeros.shape, zeros.dtype)
        for_each_fill(lambda e, k: zero_tail(e, k).start(), lambda t: zero_tile(t).start())

    @pl.when(i >= 2)
    def _():
        wait_tile(i - 2, slot)

    pos1 = rt_ref[4:5, :].astype(I32)
    pos2 = rt_ref[5:6, :].astype(I32)
    e1 = rt_ref[0:1, :].astype(I32)
    e2 = rt_ref[1:2, :].astype(I32)
    for e in range(MOE_EXPERTS):
        shift = ptab_ref[i * MOE_EXPERTS + e]
        pos1 = pos1 + jnp.where(e1 == e, shift, 0)
        pos2 = pos2 + jnp.where(e2 == e, shift, 0)
    lane = lax.broadcasted_iota(I32, (tm, LANES), 1)

    def gate_terms(g):
        hi = g.astype(BF16).astype(F32)
        return jnp.where(lane == 0, hi, jnp.where(lane == 1, g - hi, 0.0)).astype(BF16)

    gate1 = gate_terms(r_ref[:, 2:3])
    gate2 = gate_terms(r_ref[:, 3:4])
    blk = 256
    for k in range(TILE_ROWS // blk):
        row = lax.broadcasted_iota(I32, (blk, tm), 0) + k * blk
        pick1 = jnp.where(row == pos1, 1.0, 0.0).astype(BF16)
        pick2 = jnp.where(row == pos2, 1.0, 0.0).astype(BF16)
        at = slice(k * blk, (k + 1) * blk)
        buf[slot, at, :D_MODEL] = jnp.dot(pick1 + pick2, x_ref[...], preferred_element_type=F32).astype(BF16)
        buf[slot, at, D_MODEL:] = (jnp.dot(pick1, gate1, preferred_element_type=F32)
                                   + jnp.dot(pick2, gate2, preferred_element_type=F32)).astype(BF16)

    def send(c, carry):
        _chunk_copy_out(buf, slot, c, crow_ref[i * TILE_CHUNKS + c], xs_hbm, sem).start()
        return carry
    lax.fori_loop(0, nch_ref[i], send, 0)

    @pl.when(i == n_t - 1)
    def _():
        @pl.when(i >= 1)
        def _():
            wait_tile(i - 1, 1 - slot)
        wait_tile(i, slot)
        for_each_fill(lambda e, k: zero_tail(e, k).wait(), lambda t: zero_tile(t).wait())


def _moe_dispatch(x_bf, route_t, route, tables, tm):
    m, d = x_bf.shape
    n_rows = _sorted_rows(m, tm)
    grid_spec = pltpu.PrefetchScalarGridSpec(
        num_scalar_prefetch=6,
        grid=(m // tm,),
        in_specs=[pl.BlockSpec((tm, d), lambda i, *_: (i, 0)),
                  pl.BlockSpec((8, tm), lambda i, *_: (0, i)),
                  pl.BlockSpec((tm, LANES), lambda i, *_: (i, 0))],
        out_specs=pl.BlockSpec(memory_space=pl.ANY),
        scratch_shapes=[pltpu.VMEM((2, TILE_ROWS, SORTED_WIDTH), BF16), pltpu.VMEM((tm, SORTED_WIDTH), BF16),
                        pltpu.SemaphoreType.DMA((2,)), pltpu.SemaphoreType.DMA((2,))],
    )
    return pl.pallas_call(
        functools.partial(_moe_dispatch_kernel, tm=tm, n_tiles=n_rows // tm),
        name="moe_dispatch",
        grid_spec=grid_spec,
        out_shape=jax.ShapeDtypeStruct((n_rows, SORTED_WIDTH), BF16),
        compiler_params=_params(("arbitrary",)),
    )(tables["n_chunks"], tables["chunk_row"], tables["pos_tab"], tables["tail_first"], tables["tail_chunks"],
      tables["tile_valid"], x_bf, route_t, route)


def _moe_ffn_kernel(te_ref, tv_ref, tf_ref, tn_ref, nv_ref, x_ref, wg_hbm, wu_hbm, wd_hbm, o_ref,
                    wg_f, wu_f, wd_f, wg_bf, wu_bf, wd_bf, sem_w, *, layer):
    j = pl.program_id(0)
    valid = tv_ref[j] == 1

    def weight_copies(e):
        return (pltpu.make_async_copy(wg_hbm.at[layer, e], wg_f, sem_w.at[0]),
                pltpu.make_async_copy(wu_hbm.at[layer, e], wu_f, sem_w.at[1]),
                pltpu.make_async_copy(wd_hbm.at[layer, e], wd_f, sem_w.at[2]))

    @pl.when(j == 0)
    def _():
        for c in weight_copies(te_ref[0]):
            c.start()

    @pl.when(valid & (tf_ref[j] == 1))
    def _():
        for c in weight_copies(te_ref[j]):
            c.wait()
        _cast_rows(wg_f, wg_bf)
        _cast_rows(wu_f, wu_bf)
        _cast_rows(wd_f, wd_bf)

        @pl.when(tn_ref[j] >= 0)
        def _():
            for c in weight_copies(tn_ref[j]):
                c.start()

    @pl.when(valid)
    def _():
        xb = x_ref[:, :D_MODEL]
        gate = x_ref[:, D_MODEL:D_MODEL + 1].astype(F32) + x_ref[:, D_MODEL + 1:D_MODEL + 2].astype(F32)
        hg = jnp.dot(xb, wg_bf[...], preferred_element_type=F32)
        hu = jnp.dot(xb, wu_bf[...], preferred_element_type=F32)
        act = (hg * (1.0 / (1.0 + jnp.exp(-hg))) * hu).astype(BF16)
        o_ref[...] = (jnp.dot(act, wd_bf[...], preferred_element_type=F32) * gate).astype(o_ref.dtype)

    @pl.when(jnp.logical_not(valid))
    def _():
        o_ref[...] = jnp.zeros(o_ref.shape, o_ref.dtype)


def _moe_ffn(x_sorted, w_gate, w_up, w_down, layer, tables, tm):
    n_rows, width = x_sorted.shape
    d, f = w_gate.shape[2], w_gate.shape[3]
    any_spec = pl.BlockSpec(memory_space=pl.ANY)
    grid_spec = pltpu.PrefetchScalarGridSpec(
        num_scalar_prefetch=5,
        grid=(n_rows // tm,),
        in_specs=[pl.BlockSpec((tm, width), lambda j, te, tv, tf, tn, nv: (jnp.minimum(j, nv[0] - 1), 0)),
                  any_spec, any_spec, any_spec],
        out_specs=pl.BlockSpec((tm, d), lambda j, *_: (j, 0)),
        scratch_shapes=[pltpu.VMEM((d, f), F32), pltpu.VMEM((d, f), F32), pltpu.VMEM((f, d), F32),
                        pltpu.VMEM((d, f), BF16), pltpu.VMEM((d, f), BF16), pltpu.VMEM((f, d), BF16),
                        pltpu.SemaphoreType.DMA((3,))],
    )
    return pl.pallas_call(
        functools.partial(_moe_ffn_kernel, layer=layer),
        name="moe_ffn",
        grid_spec=grid_spec,
        out_shape=jax.ShapeDtypeStruct((n_rows, d), BF16),
        compiler_params=_params(("arbitrary",)),
    )(tables["tile_expert"], tables["tile_valid"], tables["tile_first"], tables["tile_next"], tables["n_valid"],
      x_sorted, w_gate, w_up, w_down)


def _moe_combine_kernel(nch_ref, crow_ref, ys_hbm, x_ref, r_ref, ptab_ref, g_ref, b_ref, o_ref, obf_ref,
                        buf, sem, *, tm):
    i = pl.program_id(0)
    n_t = pl.num_programs(0)
    slot = i % 2

    def fetch(tile, s):
        def body(c, carry):
            _chunk_copy_in(ys_hbm, crow_ref[tile * TILE_CHUNKS + c], buf, s, c, sem).start()
            return carry
        lax.fori_loop(0, nch_ref[tile], body, 0)

    @pl.when(i == 0)
    def _():
        buf[...] = jnp.zeros(buf.shape, buf.dtype)
        fetch(0, 0)

    @pl.when(i + 1 < n_t)
    def _():
        fetch(i + 1, 1 - slot)

    def arrived(c, carry):
        _chunk_copy_in(ys_hbm, 0, buf, slot, 0, sem).wait()
        return carry
    lax.fori_loop(0, nch_ref[i], arrived, 0)

    lane = lax.broadcasted_iota(I32, (tm, LANES), 1)
    shift = ptab_ref[...]
    e1 = r_ref[:, 0:1].astype(I32)
    e2 = r_ref[:, 1:2].astype(I32)
    pos1 = (r_ref[:, 4:5] + jnp.sum(jnp.where(lane == e1, shift, 0.0), axis=1, keepdims=True)).astype(I32)
    pos2 = (r_ref[:, 5:6] + jnp.sum(jnp.where(lane == e2, shift, 0.0), axis=1, keepdims=True)).astype(I32)
    col = lax.broadcasted_iota(I32, (tm, TILE_ROWS), 1)
    pick = jnp.where((col == pos1) | (col == pos2), 1.0, 0.0).astype(BF16)
    y = jnp.dot(pick, buf[slot], preferred_element_type=F32)
    out = _layernorm_rows(DEEPNORM_ALPHA * x_ref[...] + y, g_ref[...], b_ref[...])
    o_ref[...] = out
    obf_ref[...] = out.astype(BF16)


def _moe_combine(y_sorted, x, route, tables, g, b, tm):
    m, d = x.shape
    grid_spec = pltpu.PrefetchScalarGridSpec(
        num_scalar_prefetch=2,
        grid=(m // tm,),
        in_specs=[pl.BlockSpec(memory_space=pl.ANY),
                  pl.BlockSpec((tm, d), lambda i, *_: (i, 0)),
                  pl.BlockSpec((tm, LANES), lambda i, *_: (i, 0)),
                  pl.BlockSpec((None, 1, LANES), lambda i, *_: (i, 0, 0)),
                  pl.BlockSpec((1, d), lambda i, *_: (0, 0)),
                  pl.BlockSpec((1, d), lambda i, *_: (0, 0))],
        out_specs=[pl.BlockSpec((tm, d), lambda i, *_: (i, 0)),
                   pl.BlockSpec((tm, d), lambda i, *_: (i, 0))],
        scratch_shapes=[pltpu.VMEM((2, TILE_ROWS, d), BF16), pltpu.SemaphoreType.DMA((2,))],
    )
    pos_rows = _pad_cols(tables["pos_rows"])[:, None, :]
    return pl.pallas_call(
        functools.partial(_moe_combine_kernel, tm=tm),
        name="moe_combine",
        grid_spec=grid_spec,
        out_shape=[jax.ShapeDtypeStruct((m, d), F32), jax.ShapeDtypeStruct((m, d), BF16)],
        compiler_params=_params(("arbitrary",)),
    )(tables["n_chunks"], tables["chunk_row"], y_sorted, x, route, pos_rows, g, b)


def _moe_layer(x, x_bf, w_group, b_group, w_router, b_router, w_gate, w_up, w_down, layer, ln_g, ln_b):
    n_tokens = x.shape[0]
    tm = min(MOE_TM, n_tokens)
    w_route = _pad_cols(jnp.concatenate([w_group, w_router], axis=1))
    b_route = _pad_cols(jnp.concatenate([b_group, b_router])[None, :])
    route, route_t, before, counts = _route(x, w_route, b_route, tm)
    tables = _dispatch_tables(before, counts, n_tokens, tm)
    x_sorted = _moe_dispatch(x_bf, route_t, route, tables, tm)
    y_sorted = _moe_ffn(x_sorted, w_gate, w_up, w_down, layer, tables, tm)
    return _moe_combine(y_sorted, x, route, tables, ln_g[None, :], ln_b[None, :], tm)


def kernel(x, t5_rel_bias, mla_w_in, mla_q_norm, mla_kv_norm, mla_w_q_up, mla_w_kv_up, mla_w_o, fox_w_in, fox_b_f, fox_w_o, dsa_w_in, dsa_w_o, ln_g, ln_b, moe_w_group, moe_b_group, moe_w_router, moe_b_router, moe_w_gate, moe_w_up, moe_w_down):
    batch, seq, d = x.shape
    x = x.reshape(batch * seq, d)
    x_bf = x.astype(BF16)
    for layer in range(DEPTH):
        kind = layer % N_MIXERS
        j = layer // N_MIXERS
        g0, b0 = ln_g[layer, 0], ln_b[layer, 0]
        if kind == 0:
            x, x_bf = _mla_layer(x, x_bf, mla_w_in[j], mla_q_norm[j], mla_kv_norm[j], mla_w_q_up[j],
                                 mla_w_kv_up[j], mla_w_o[j], g0, b0, batch, seq)
        elif kind == 1:
            x, x_bf = _fox_layer(x, x_bf, fox_w_in[j], fox_b_f[j], fox_w_o[j], g0, b0, batch, seq)
        else:
            x, x_bf = _dsa_layer(x, x_bf, t5_rel_bias, dsa_w_in[j], dsa_w_o[j], g0, b0, batch, seq)
        x, x_bf = _moe_layer(x, x_bf, moe_w_group[layer], moe_b_group[layer], moe_w_router[layer], moe_b_router[layer],
                             moe_w_gate, moe_w_up, moe_w_down, layer, ln_g[layer, 1], ln_b[layer, 1])
    return x.reshape(batch, seq, d)
```

```python
import functools
import math

import numpy as np
import jax
import jax.numpy as jnp
from jax import lax
from jax.experimental import pallas as pl
from jax.experimental.pallas import tpu as pltpu

F32 = jnp.float32
BF16 = jnp.bfloat16
I32 = jnp.int32

D_MODEL = 2048
DEPTH = 4
CHUNK = 64
N_MIXERS = 3
DEEPNORM_ALPHA = (2.0 * DEPTH) ** 0.25

HEADS = 16
HEAD_DIM = 128
MLA_NOPE_DIM = 128
MLA_ROPE_DIM = 64
MLA_Q_RANK = 512
MLA_KV_RANK = 512
MLA_QK_PAD = 256
ROPE_THETA = 10000.0

IDX_HEADS = 16
IDX_DIM = 64
IDX_TOPK_MAX = 256

T5_BUCKETS = 32
T5_MAX_DISTANCE = 128
T5_FAR_BUCKET = T5_BUCKETS // 2 - 1
T5_BAND_BACK = 128

MOE_GROUPS = 4
MOE_EXPERTS_PER_GROUP = 8
MOE_EXPERTS = MOE_GROUPS * MOE_EXPERTS_PER_GROUP
MOE_D_FF = 512

LN_EPS = 1e-5
RMS_EPS = 1e-6
NEG_INF = -1e30

LANES = 128
VMEM_LIMIT = 56 * 1024 * 1024

ATTN_TQ = 256
MOE_TM = 256
ROW_ALIGN = 16
TILE_ROWS = -(-(2 * MOE_TM + MOE_EXPERTS * (ROW_ALIGN - 1)) // 256) * 256
TILE_CHUNKS = TILE_ROWS // ROW_ALIGN
SORTED_WIDTH = D_MODEL + LANES


def _params(semantics, vmem=VMEM_LIMIT):
    return pltpu.CompilerParams(dimension_semantics=semantics, vmem_limit_bytes=vmem)


def _cast_rows(src_ref, dst_ref):
    rows = 16
    while 2 * rows * src_ref.shape[1] <= 32 * 8 * LANES and src_ref.shape[0] % (2 * rows) == 0:
        rows *= 2

    def body(i, carry):
        at = pl.ds(pl.multiple_of(i * rows, rows), rows)
        dst_ref[at, :] = src_ref[at, :].astype(dst_ref.dtype)
        return carry
    lax.fori_loop(0, src_ref.shape[0] // rows, body, 0, unroll=2)


def _mm_kernel(a_ref, w_ref, o_ref, w_bf, *, scale, scaled_cols):
    @pl.when(pl.program_id(1) == 0)
    def _():
        _cast_rows(w_ref, w_bf)

    acc = jnp.dot(a_ref[...], w_bf[...], preferred_element_type=F32)
    if scaled_cols:
        tn = o_ref.shape[1]
        col = pl.program_id(0) * tn + lax.broadcasted_iota(I32, (1, tn), 1)
        acc = acc * jnp.where(col < scaled_cols, scale, 1.0)
    o_ref[...] = acc.astype(o_ref.dtype)


def _matmul(a, w, *, n_out, tn, tm=1024, scale=1.0, scaled_cols=0, out_dtype=BF16):
    m, k = a.shape
    tm = min(tm, m)
    kern = functools.partial(_mm_kernel, scale=scale, scaled_cols=scaled_cols)
    return pl.pallas_call(
        kern,
        name="proj_mm",
        grid=(n_out // tn, m // tm),
        in_specs=[pl.BlockSpec((tm, k), lambda j, i: (i, 0)),
                  pl.BlockSpec((k, tn), lambda j, i: (0, j))],
        out_specs=pl.BlockSpec((tm, tn), lambda j, i: (i, j)),
        out_shape=jax.ShapeDtypeStruct((m, n_out), out_dtype),
        scratch_shapes=[pltpu.VMEM((k, tn), BF16)],
        compiler_params=_params(("arbitrary", "arbitrary")),
    )(a, w)


def _split3(v):
    hi = v.astype(BF16)
    r1 = v - hi.astype(F32)
    mid = r1.astype(BF16)
    lo = (r1 - mid.astype(F32)).astype(BF16)
    return hi, mid, lo


def _dot_precise(x, w):
    xh = x.astype(BF16)
    xm = (x - xh.astype(F32)).astype(BF16)
    wh = w.astype(BF16)
    wm = (w - wh.astype(F32)).astype(BF16)
    d = lambda p, q: jnp.dot(p, q, preferred_element_type=F32)
    return (d(xm, wh) + d(xh, wm)) + d(xh, wh)


def _small_mm_kernel(x_ref, w_ref, b_ref, o_ref):
    o_ref[...] = _dot_precise(x_ref[...], w_ref[...]) + b_ref[...]


def _small_matmul(x, w, b, tm=512):
    m, k = x.shape
    tm = min(tm, m)
    return pl.pallas_call(
        _small_mm_kernel,
        name="small_mm",
        grid=(m // tm,),
        in_specs=[pl.BlockSpec((tm, k), lambda i: (i, 0)),
                  pl.BlockSpec((k, LANES), lambda i: (0, 0)),
                  pl.BlockSpec((1, LANES), lambda i: (0, 0))],
        out_specs=pl.BlockSpec((tm, LANES), lambda i: (i, 0)),
        out_shape=jax.ShapeDtypeStruct((m, LANES), F32),
        compiler_params=_params(("arbitrary",)),
    )(x, w, b)


def _pad_cols(w, width=LANES):
    return jnp.pad(w, ((0, 0), (0, width - w.shape[1])))


def _layernorm_rows(z, g, b):
    mu = jnp.mean(z, axis=-1, keepdims=True)
    zc = z - mu
    var = jnp.mean(zc * zc, axis=-1, keepdims=True)
    return zc * lax.rsqrt(var + LN_EPS) * g + b


def _mm_res_ln_kernel(a_ref, w_ref, x_ref, g_ref, b_ref, o_ref, obf_ref):
    y = jnp.dot(a_ref[...], w_ref[...], preferred_element_type=F32)
    out = _layernorm_rows(DEEPNORM_ALPHA * x_ref[...] + y, g_ref[...], b_ref[...])
    o_ref[...] = out
    obf_ref[...] = out.astype(BF16)


def _mm_res_ln(a, w, x, g, b, tm=512):
    m, k = a.shape
    d = w.shape[1]
    tm = min(tm, m)
    return pl.pallas_call(
        _mm_res_ln_kernel,
        name="out_proj_res_ln",
        grid=(m // tm,),
        in_specs=[pl.BlockSpec((tm, k), lambda i: (i, 0)),
                  pl.BlockSpec((k, d), lambda i: (0, 0)),
                  pl.BlockSpec((tm, d), lambda i: (i, 0)),
                  pl.BlockSpec((1, d), lambda i: (0, 0)),
                  pl.BlockSpec((1, d), lambda i: (0, 0))],
        out_specs=[pl.BlockSpec((tm, d), lambda i: (i, 0)),
                   pl.BlockSpec((tm, d), lambda i: (i, 0))],
        out_shape=[jax.ShapeDtypeStruct((m, d), F32), jax.ShapeDtypeStruct((m, d), BF16)],
        compiler_params=_params(("arbitrary",)),
    )(a, w, x, g, b)


def _rope_tables(seq):
    half = MLA_ROPE_DIM // 2
    inv_freq = ROPE_THETA ** (-jnp.arange(half, dtype=F32) / half)
    ang = jnp.arange(seq, dtype=I32).astype(F32)[:, None] * inv_freq[None, :]
    cos, sin = jnp.cos(ang), jnp.sin(ang)
    z = jnp.zeros_like(cos)
    cos_t = jnp.concatenate([cos, cos, z, z], axis=1)
    sin_a = jnp.concatenate([-sin, z, z, z], axis=1)
    sin_b = jnp.concatenate([z, sin, z, z], axis=1)
    return cos_t, sin_a, sin_b


def _rope_lanes(r, cos_t, sin_a, sin_b):
    from_right = pltpu.roll(r, LANES - MLA_ROPE_DIM // 2, 1)
    from_left = pltpu.roll(r, MLA_ROPE_DIM // 2, 1)
    return r * cos_t + from_right * sin_a + from_left * sin_b


def _rms_rows(v, g):
    return v * lax.rsqrt(jnp.mean(v * v, axis=-1, keepdims=True) + RMS_EPS) * g


def _mla_in_kernel(x_ref, w_ref, qn_ref, kvn_ref, cos_ref, sina_ref, sinb_ref, cq_ref, ckv_ref, kr_ref):
    h = jnp.dot(x_ref[...], w_ref[...], preferred_element_type=F32)
    cq_ref[...] = _rms_rows(h[:, :MLA_Q_RANK], qn_ref[...]).astype(BF16)
    ckv_ref[...] = _rms_rows(h[:, MLA_Q_RANK:MLA_Q_RANK + MLA_KV_RANK], kvn_ref[...]).astype(BF16)
    r = h[:, MLA_Q_RANK + MLA_KV_RANK:]
    kr_ref[...] = _rope_lanes(r, cos_ref[...], sina_ref[...], sinb_ref[...]).astype(BF16)


def _mla_in(x_bf, w_in_pad, q_norm, kv_norm, tables, seq, tm=512):
    m, k = x_bf.shape
    tm = min(tm, seq)
    nw = w_in_pad.shape[1]
    per_seq = seq // tm
    tab_spec = pl.BlockSpec((tm, LANES), lambda i: (i % per_seq, 0))
    return pl.pallas_call(
        _mla_in_kernel,
        name="mla_in",
        grid=(m // tm,),
        in_specs=[pl.BlockSpec((tm, k), lambda i: (i, 0)),
                  pl.BlockSpec((k, nw), lambda i: (0, 0)),
                  pl.BlockSpec((1, MLA_Q_RANK), lambda i: (0, 0)),
                  pl.BlockSpec((1, MLA_KV_RANK), lambda i: (0, 0)),
                  tab_spec, tab_spec, tab_spec],
        out_specs=[pl.BlockSpec((tm, MLA_Q_RANK), lambda i: (i, 0)),
                   pl.BlockSpec((tm, MLA_KV_RANK), lambda i: (i, 0)),
                   pl.BlockSpec((tm, LANES), lambda i: (i, 0))],
        out_shape=[jax.ShapeDtypeStruct((m, MLA_Q_RANK), BF16),
                   jax.ShapeDtypeStruct((m, MLA_KV_RANK), BF16),
                   jax.ShapeDtypeStruct((m, LANES), BF16)],
        compiler_params=_params(("arbitrary",)),
    )(x_bf, w_in_pad, q_norm, kv_norm, *tables)


def _fill_v_aug(va_ref, v_ref):
    va_ref[:, :HEAD_DIM] = v_ref[...]
    va_ref[:, HEAD_DIM:] = jnp.ones((v_ref.shape[0], HEAD_DIM), BF16)


def _softmax_pv(pieces):
    m = None
    for s, _ in pieces:
        pm = jnp.max(s, axis=1, keepdims=True)
        m = pm if m is None else jnp.maximum(m, pm)
    acc = None
    for s, v in pieces:
        t = jnp.dot(jnp.exp((s - m).astype(BF16)), v, preferred_element_type=F32)
        acc = t if acc is None else acc + t
    return acc[:, :HEAD_DIM] / acc[:, HEAD_DIM:HEAD_DIM + 1]


def _nt_dot(q, k):
    return lax.dot_general(q, k, (((1,), (1,)), ((), ())), preferred_element_type=F32)


def _attend_tiles(n_tiles, scores, finish):
    s_next = scores(0)
    for i in range(n_tiles):
        s = s_next
        if i + 1 < n_tiles:
            s_next = scores(i + 1)
        finish(i, s)


def _mla_q_up_kernel(c_ref, w_ref, cos_ref, sina_ref, sinb_ref, o_ref, w_bf, *, scale):
    @pl.when(pl.program_id(1) == 0)
    def _():
        _cast_rows(w_ref, w_bf)

    acc = jnp.dot(c_ref[...], w_bf[...], preferred_element_type=F32) * scale
    for h in range(o_ref.shape[1] // MLA_QK_PAD):
        at = h * MLA_QK_PAD
        o_ref[:, at:at + MLA_NOPE_DIM] = acc[:, at:at + MLA_NOPE_DIM].astype(BF16)
        r = acc[:, at + MLA_NOPE_DIM:at + MLA_QK_PAD]
        o_ref[:, at + MLA_NOPE_DIM:at + MLA_QK_PAD] = _rope_lanes(
            r, cos_ref[...], sina_ref[...], sinb_ref[...]).astype(BF16)


def _mla_kv_up_kernel(c_ref, w_ref, kr_ref, k_ref, v_ref, w_bf):
    @pl.when(pl.program_id(1) == 0)
    def _():
        _cast_rows(w_ref, w_bf)

    acc = jnp.dot(c_ref[...], w_bf[...], preferred_element_type=F32)
    ones = jnp.ones((acc.shape[0], HEAD_DIM), BF16)
    for h in range(k_ref.shape[1] // MLA_QK_PAD):
        at = h * MLA_QK_PAD
        k_ref[:, at:at + MLA_NOPE_DIM] = acc[:, at:at + MLA_NOPE_DIM].astype(BF16)
        k_ref[:, at + MLA_NOPE_DIM:at + MLA_QK_PAD] = kr_ref[...]
        v_ref[:, at:at + HEAD_DIM] = acc[:, at + MLA_NOPE_DIM:at + MLA_QK_PAD].astype(BF16)
        v_ref[:, at + HEAD_DIM:at + 2 * HEAD_DIM] = ones


def _mla_up(cq, ckv, kr, w_q_up, w_kv_up, tables, seq, scale, tm=1024, tn=1024):
    m, rank = cq.shape
    tm = min(tm, seq)
    per_seq = seq // tm
    n = HEADS * MLA_QK_PAD
    tab_spec = pl.BlockSpec((tm, LANES), lambda j, i: (i % per_seq, 0))
    row_spec = pl.BlockSpec((tm, rank), lambda j, i: (i, 0))
    w_spec = pl.BlockSpec((rank, tn), lambda j, i: (0, j))
    out_spec = pl.BlockSpec((tm, tn), lambda j, i: (i, j))
    out = jax.ShapeDtypeStruct((m, n), BF16)
    q = pl.pallas_call(
        functools.partial(_mla_q_up_kernel, scale=scale),
        name="mla_q_up",
        grid=(n // tn, m // tm),
        in_specs=[row_spec, w_spec, tab_spec, tab_spec, tab_spec],
        out_specs=out_spec,
        out_shape=out,
        scratch_shapes=[pltpu.VMEM((rank, tn), BF16)],
        compiler_params=_params(("arbitrary", "arbitrary")),
    )(cq, w_q_up, *tables)
    k, v = pl.pallas_call(
        _mla_kv_up_kernel,
        name="mla_kv_up",
        grid=(n // tn, m // tm),
        in_specs=[row_spec, w_spec, pl.BlockSpec((tm, LANES), lambda j, i: (i, 0))],
        out_specs=[out_spec, out_spec],
        out_shape=[out, out],
        scratch_shapes=[pltpu.VMEM((rank, tn), BF16)],
        compiler_params=_params(("arbitrary", "arbitrary")),
    )(ckv, w_kv_up, kr)
    return q, k, v


def _mla_attn_kernel(q_ref, k_ref, v_ref, o_ref, *, tq):
    seq = q_ref.shape[0]
    row = lax.broadcasted_iota(I32, (tq, tq), 0)
    col = lax.broadcasted_iota(I32, (tq, tq), 1)
    visible = (col // CHUNK) <= (row // CHUNK)

    def scores(i):
        return _nt_dot(q_ref[i * tq:(i + 1) * tq, :], k_ref[:(i + 1) * tq, :])

    def finish(i, s):
        lo, kv = i * tq, (i + 1) * tq
        pieces = [(jnp.where(visible, s[:, lo:], NEG_INF), v_ref[lo:kv, :])]
        if lo:
            pieces.append((s[:, :lo], v_ref[:lo, :]))
        o_ref[lo:kv, :] = _softmax_pv(pieces).astype(o_ref.dtype)

    _attend_tiles(seq // tq, scores, finish)


def _mla_attn(q, k, v, batch, seq):
    tq = min(ATTN_TQ, seq)
    head_spec = pl.BlockSpec((seq, MLA_QK_PAD), lambda b, h: (b, h))
    return pl.pallas_call(
        functools.partial(_mla_attn_kernel, tq=tq),
        name="mla_attn",
        grid=(batch, HEADS),
        in_specs=[head_spec, head_spec, head_spec],
        out_specs=pl.BlockSpec((seq, HEAD_DIM), lambda b, h: (b, h)),
        out_shape=jax.ShapeDtypeStruct((batch * seq, HEADS * HEAD_DIM), BF16),
        compiler_params=_params(("arbitrary", "arbitrary")),
    )(q, k, v)


def _mla_q_up_layout(w_q_up):
    rank = w_q_up.shape[0]
    w = w_q_up.reshape(rank, HEADS, MLA_NOPE_DIM + MLA_ROPE_DIM)
    w = jnp.pad(w, ((0, 0), (0, 0), (0, MLA_QK_PAD - MLA_NOPE_DIM - MLA_ROPE_DIM)))
    return w.reshape(rank, HEADS * MLA_QK_PAD)


def _mla_layer(x, x_bf, w_in, q_norm, kv_norm, w_q_up, w_kv_up, w_o, ln_g, ln_b, batch, seq):
    tables = _rope_tables(seq)
    w_in_pad = jnp.pad(w_in, ((0, 0), (0, MLA_Q_RANK + MLA_KV_RANK + LANES - w_in.shape[1]))).astype(BF16)
    cq, ckv, kr = _mla_in(x_bf, w_in_pad, q_norm[None, :], kv_norm[None, :], tables, seq)
    scale = (MLA_NOPE_DIM + MLA_ROPE_DIM) ** -0.5
    q, k, v = _mla_up(cq, ckv, kr, _mla_q_up_layout(w_q_up), w_kv_up, tables, seq, scale)
    o = _mla_attn(q, k, v, batch, seq)
    return _mm_res_ln(o, w_o.astype(BF16), x, ln_g[None, :], ln_b[None, :])


def _log_sigmoid(z):
    return jnp.minimum(z, 0.0) - jnp.log1p(jnp.exp(-jnp.abs(z)))


def _fox_gate_kernel(f_ref, c_ref, *, blk):
    seq = f_ref.shape[0]
    tri = (lax.broadcasted_iota(I32, (blk, blk), 1) <= lax.broadcasted_iota(I32, (blk, blk), 0)).astype(BF16)
    carry = jnp.zeros((1, LANES), F32)
    for j in range(seq // blk):
        lf = _log_sigmoid(f_ref[j * blk:(j + 1) * blk, :])
        hi, mid, lo = _split3(lf)
        d = lambda p: jnp.dot(tri, p, preferred_element_type=F32)
        c = (d(lo) + d(mid)) + d(hi) + carry
        c_ref[j * blk:(j + 1) * blk, :] = c
        carry = c[blk - 1:blk, :]


def _fox_gate(f_logit, batch, seq):
    blk = min(256, seq)
    return pl.pallas_call(
        functools.partial(_fox_gate_kernel, blk=blk),
        name="fox_gate",
        grid=(batch,),
        in_specs=[pl.BlockSpec((seq, LANES), lambda b: (b, 0))],
        out_specs=pl.BlockSpec((seq, LANES), lambda b: (b, 0)),
        out_shape=jax.ShapeDtypeStruct((batch * seq, LANES), F32),
        compiler_params=_params(("arbitrary",)),
    )(f_logit)


def _fox_attn_kernel(q_ref, k_ref, v_ref, c_ref, o_ref, qs_ref, ks_ref, *, tq):
    seq = q_ref.shape[0]
    h = pl.program_id(1)
    lane = lax.broadcasted_iota(I32, (seq, LANES), 1)
    c = jnp.sum(jnp.where(lane == h, c_ref[...], 0.0), axis=1, keepdims=True)
    terms = [t.astype(F32) for t in _split3(c)]
    q_side = jnp.where((lane >= 3) & (lane < 6), 1.0, 0.0)
    k_side = jnp.where(lane < 3, 1.0, 0.0)
    for j, t in enumerate(terms):
        q_side = jnp.where(lane == j, t, q_side)
        k_side = jnp.where(lane == 3 + j, -t, k_side)
    qs_ref[...] = q_side.astype(BF16)
    ks_ref[...] = k_side.astype(BF16)
    row = lax.broadcasted_iota(I32, (tq, tq), 0)
    col = lax.broadcasted_iota(I32, (tq, tq), 1)

    def wide(ref, side_ref, rows):
        return jnp.concatenate([ref[rows, :], side_ref[rows, :]], axis=1)

    def values(rows):
        v = v_ref[rows, :]
        return jnp.concatenate([v, jnp.ones(v.shape, BF16)], axis=1)

    def scores(i):
        return _nt_dot(wide(q_ref, qs_ref, slice(i * tq, (i + 1) * tq)), wide(k_ref, ks_ref, slice(0, (i + 1) * tq)))

    def finish(i, s):
        lo, kv = i * tq, (i + 1) * tq
        pieces = [(jnp.where(col <= row, s[:, lo:], NEG_INF), values(slice(lo, kv)))]
        if lo:
            pieces.append((s[:, :lo], values(slice(0, lo))))
        o_ref[lo:kv, :] = _softmax_pv(pieces).astype(o_ref.dtype)

    _attend_tiles(seq // tq, scores, finish)


def _fox_attn(qkv, c_col, batch, seq):
    tq = min(ATTN_TQ, seq)
    side = pltpu.VMEM((seq, LANES), BF16)
    return pl.pallas_call(
        functools.partial(_fox_attn_kernel, tq=tq),
        name="fox_attn",
        grid=(batch, HEADS),
        in_specs=[pl.BlockSpec((seq, HEAD_DIM), lambda b, h: (b, h)),
                  pl.BlockSpec((seq, HEAD_DIM), lambda b, h: (b, HEADS + h)),
                  pl.BlockSpec((seq, HEAD_DIM), lambda b, h: (b, 2 * HEADS + h)),
                  pl.BlockSpec((seq, LANES), lambda b, h: (b, 0))],
        out_specs=pl.BlockSpec((seq, HEAD_DIM), lambda b, h: (b, h)),
        out_shape=jax.ShapeDtypeStruct((batch * seq, HEADS * HEAD_DIM), BF16),
        scratch_shapes=[side, side],
        compiler_params=_params(("arbitrary", "arbitrary")),
    )(qkv, qkv, qkv, c_col)


def _fox_layer(x, x_bf, w_in, b_f, w_o, ln_g, ln_b, batch, seq):
    hd = HEADS * HEAD_DIM
    scale = HEAD_DIM ** -0.5
    qkv = _matmul(x_bf, w_in, n_out=3 * hd, tn=1024, scale=scale, scaled_cols=hd)
    f_logit = _small_matmul(x, _pad_cols(w_in[:, 3 * hd:]), _pad_cols(b_f[None, :]))
    o = _fox_attn(qkv, _fox_gate(f_logit, batch, seq), batch, seq)
    return _mm_res_ln(o, w_o.astype(BF16), x, ln_g[None, :], ln_b[None, :])


def _t5_bucket_table(tq):
    a = np.arange(tq, dtype=np.int64)[:, None]
    b = np.arange(tq + T5_BAND_BACK, dtype=np.int64)[None, :]
    rel = (b - T5_BAND_BACK) - a
    nb = T5_BUCKETS // 2
    max_exact = nb // 2
    ret = np.where(rel > 0, nb, 0)
    n = np.abs(rel)
    nf = np.maximum(n, 1).astype(np.float32)
    large = max_exact + (np.log(nf / np.float32(max_exact)) / np.float32(math.log(T5_MAX_DISTANCE / max_exact))
                         * np.float32(nb - max_exact)).astype(np.int32)
    large = np.minimum(large, nb - 1)
    return (ret + np.where(n < max_exact, n, large)).astype(np.int32)


def _t5_band_kernel(t5_ref, bucket_ref, o_ref):
    bucket = bucket_ref[...]

    def per_head(h, carry):
        acc = jnp.zeros(bucket.shape, F32)
        for b in range(T5_BUCKETS):
            acc = jnp.where(bucket == b, t5_ref[b, h], acc)
        o_ref[h] = acc - t5_ref[T5_FAR_BUCKET, h]
        return carry

    lax.fori_loop(0, HEADS, per_head, 0)


def _t5_band(t5_bias, tq):
    bucket = jnp.asarray(_t5_bucket_table(tq))
    width = tq + T5_BAND_BACK
    return pl.pallas_call(
        _t5_band_kernel,
        name="t5_band",
        grid=(1,),
        in_specs=[pl.BlockSpec(memory_space=pltpu.SMEM),
                  pl.BlockSpec((tq, width), lambda i: (0, 0))],
        out_specs=pl.BlockSpec((HEADS, tq, width), lambda i: (0, 0, 0)),
        out_shape=jax.ShapeDtypeStruct((HEADS, tq, width), F32),
        compiler_params=_params(("arbitrary",)),
    )(t5_bias, bucket)


def _sortable_key(v):
    bits = pltpu.bitcast(v, I32)
    return bits ^ ((bits >> 31) & 0x7FFFFFFF)


def _kth_largest_key(key_ref, kv, k):
    rows = key_ref.shape[0]

    def count_ge(t):
        t_b = jnp.broadcast_to(t, (rows, LANES))
        acc = jnp.zeros((rows, LANES), I32)
        for c in range(kv // LANES):
            acc = acc + jnp.where(key_ref[:, c * LANES:(c + 1) * LANES] >= t_b, 1, 0)
        return jnp.sum(acc, axis=1, keepdims=True)

    t0 = jnp.where(count_ge(jnp.zeros((rows, 1), I32)) >= k, 0, jnp.iinfo(jnp.int32).min).astype(I32)

    def body(it, t):
        cand = t | (jnp.int32(1) << (30 - it))
        return jnp.where(count_ge(cand) >= k, cand, t)

    return lax.fori_loop(0, 31, body, t0)


def _dsa_index_kernel(qi_ref, tail_ref, o_ref, key_ref, *, tq, topk):
    seq = tail_ref.shape[0]
    k_idx = tail_ref[:, :IDX_DIM].astype(BF16)
    for i in range(seq // tq):
        kv = (i + 1) * tq
        rows = slice(i * tq, kv)
        w = tail_ref[rows, IDX_DIM:IDX_DIM + IDX_HEADS] * ((IDX_HEADS * IDX_DIM) ** -0.5)
        score = jnp.zeros((tq, kv), F32)
        for h in range(IDX_HEADS):
            logits = _nt_dot(qi_ref[rows, h * IDX_DIM:(h + 1) * IDX_DIM], k_idx[:kv, :])
            score = score + jnp.maximum(logits, 0.0) * w[:, h:h + 1]
        row = lax.broadcasted_iota(I32, (tq, kv), 0) + i * tq
        col = lax.broadcasted_iota(I32, (tq, kv), 1)
        admissible = (col // CHUNK) <= (row // CHUNK)
        if kv <= topk:
            keep = admissible
        else:
            key_ref[:, :kv] = _sortable_key(jnp.where(admissible, score, NEG_INF))
            thr = _kth_largest_key(key_ref, kv, topk)
            keep = admissible & (key_ref[:, :kv] >= thr)
        o_ref[rows, :kv] = jnp.where(keep, 0.0, NEG_INF).astype(o_ref.dtype)
        if kv < seq:
            o_ref[rows, kv:] = jnp.full((tq, seq - kv), NEG_INF, o_ref.dtype)


def _dsa_index(q_idx, tail, batch, seq, topk):
    tq = min(ATTN_TQ, seq)
    return pl.pallas_call(
        functools.partial(_dsa_index_kernel, tq=tq, topk=topk),
        name="dsa_index",
        grid=(batch,),
        in_specs=[pl.BlockSpec((seq, IDX_HEADS * IDX_DIM), lambda b: (b, 0)),
                  pl.BlockSpec((seq, LANES), lambda b: (b, 0))],
        out_specs=pl.BlockSpec((seq, seq), lambda b: (b, 0)),
        out_shape=jax.ShapeDtypeStruct((batch * seq, seq), BF16),
        scratch_shapes=[pltpu.VMEM((tq, seq), I32)],
        compiler_params=_params(("arbitrary",)),
    )(q_idx, tail)


def _dsa_attn_kernel(q_ref, k_ref, v_ref, sel_ref, band_ref, o_ref, va_ref, *, tq):
    seq = k_ref.shape[0]
    width = band_ref.shape[1]
    _fill_v_aug(va_ref, v_ref)
    def scores(i):
        return _nt_dot(q_ref[i * tq:(i + 1) * tq, :], k_ref[:(i + 1) * tq, :])

    def finish(i, s):
        lo, kv = i * tq, (i + 1) * tq
        near = min(width, kv)
        far = kv - near
        s = s + sel_ref[lo:kv, :kv].astype(F32)
        pieces = [(s[:, far:] + band_ref[:, width - near:], va_ref[far:kv, :])]
        if far:
            pieces.append((s[:, :far], va_ref[:far, :]))
        o_ref[lo:kv, :] = _softmax_pv(pieces).astype(o_ref.dtype)

    _attend_tiles(seq // tq, scores, finish)


def _dsa_attn(qkv, sel, band, batch, seq):
    tq = min(ATTN_TQ, seq)
    width = band.shape[2]
    return pl.pallas_call(
        functools.partial(_dsa_attn_kernel, tq=tq),
        name="dsa_attn",
        grid=(batch, HEADS),
        in_specs=[pl.BlockSpec((seq, HEAD_DIM), lambda b, h: (b, h)),
                  pl.BlockSpec((seq, HEAD_DIM), lambda b, h: (b, HEADS)),
                  pl.BlockSpec((seq, HEAD_DIM), lambda b, h: (b, HEADS + 1)),
                  pl.BlockSpec((seq, seq), lambda b, h: (b, 0)),
                  pl.BlockSpec((None, tq, width), lambda b, h: (h, 0, 0))],
        out_specs=pl.BlockSpec((seq, HEAD_DIM), lambda b, h: (b, h)),
        out_shape=jax.ShapeDtypeStruct((batch * seq, HEADS * HEAD_DIM), BF16),
        scratch_shapes=[pltpu.VMEM((seq, 2 * HEAD_DIM), BF16)],
        compiler_params=_params(("arbitrary", "arbitrary")),
    )(qkv, qkv, qkv, sel, band)


def _dsa_layer(x, x_bf, t5_bias, w_in, w_o, ln_g, ln_b, batch, seq):
    hd = HEADS * HEAD_DIM
    n_qkv = hd + 2 * HEAD_DIM
    n_qi = IDX_HEADS * IDX_DIM
    scale = HEAD_DIM ** -0.5
    qkv = _matmul(x_bf, w_in, n_out=n_qkv, tn=n_qkv // 3, scale=scale, scaled_cols=hd)
    q_idx = _matmul(x_bf, w_in[:, n_qkv:n_qkv + n_qi], n_out=n_qi, tn=n_qi)
    tail = _small_matmul(x, _pad_cols(w_in[:, n_qkv + n_qi:]), jnp.zeros((1, LANES), F32))
    topk = min(IDX_TOPK_MAX, seq // 4)
    sel = _dsa_index(q_idx, tail, batch, seq, topk)
    band = _t5_band(t5_bias, min(ATTN_TQ, seq))
    o = _dsa_attn(qkv, sel, band, batch, seq)
    return _mm_res_ln(o, w_o.astype(BF16), x, ln_g[None, :], ln_b[None, :])


def _route_kernel(x_ref, w_ref, b_ref, o_ref, ot_ref, before_ref, cnt_ref, run_ref):
    i = pl.program_id(0)

    @pl.when(i == 0)
    def _():
        run_ref[...] = jnp.zeros(run_ref.shape, F32)

    tm = o_ref.shape[0]
    logit = _dot_precise(x_ref[...], w_ref[...]) + b_ref[...]
    lane = lax.broadcasted_iota(I32, logit.shape, 1)
    big = jnp.int32(LANES)

    def first_argmax(mask):
        top = jnp.max(jnp.where(mask, logit, -jnp.inf), axis=1, keepdims=True)
        idx = jnp.min(jnp.where(mask & (logit == top), lane, big), axis=1, keepdims=True)
        return top, idx

    is_group = lane < MOE_GROUPS
    g_top, g_sel = first_argmax(is_group)
    p_g = 1.0 / jnp.sum(jnp.where(is_group, jnp.exp(logit - g_top), 0.0), axis=1, keepdims=True)
    lo = MOE_GROUPS + g_sel * MOE_EXPERTS_PER_GROUP
    in_group = (lane >= lo) & (lane < lo + MOE_EXPERTS_PER_GROUP)
    v1, i1 = first_argmax(in_group)
    v2, i2 = first_argmax(in_group & (lane != i1))
    ex = jnp.exp(v2 - v1)
    w1 = 1.0 / (1.0 + ex)
    w2 = ex / (1.0 + ex)
    e1 = i1 - MOE_GROUPS
    e2 = i2 - MOE_GROUPS
    pick1 = lane == e1
    pick2 = lane == e2
    both = jnp.where(pick1 | pick2, 1.0, 0.0)
    earlier = lax.broadcasted_iota(I32, (tm, tm), 1) < lax.broadcasted_iota(I32, (tm, tm), 0)
    before = jnp.dot(jnp.where(earlier, 1.0, 0.0).astype(BF16), both.astype(BF16),
                     preferred_element_type=F32) + run_ref[...]
    r1 = jnp.sum(jnp.where(pick1, before, 0.0), axis=1, keepdims=True)
    r2 = jnp.sum(jnp.where(pick2, before, 0.0), axis=1, keepdims=True)
    before_ref[...] = run_ref[...]
    run_ref[...] = run_ref[...] + jnp.sum(both, axis=0, keepdims=True)
    cnt_ref[...] = run_ref[...]
    out = jnp.where(lane == 0, e1.astype(F32), 0.0)
    out = jnp.where(lane == 1, e2.astype(F32), out)
    out = jnp.where(lane == 2, w1 * p_g, out)
    out = jnp.where(lane == 3, w2 * p_g, out)
    out = jnp.where(lane == 4, r1, out)
    out = jnp.where(lane == 5, r2, out)
    o_ref[...] = out
    ot_ref[...] = out.T[:8, :]


def _route(x, w, b, tm):
    m, k = x.shape
    n_t = m // tm
    return pl.pallas_call(
        _route_kernel,
        name="moe_route",
        grid=(n_t,),
        in_specs=[pl.BlockSpec((tm, k), lambda i: (i, 0)),
                  pl.BlockSpec((k, LANES), lambda i: (0, 0)),
                  pl.BlockSpec((1, LANES), lambda i: (0, 0))],
        out_specs=[pl.BlockSpec((tm, LANES), lambda i: (i, 0)),
                   pl.BlockSpec((8, tm), lambda i: (0, i)),
                   pl.BlockSpec((None, 1, LANES), lambda i: (i, 0, 0)),
                   pl.BlockSpec((1, LANES), lambda i: (0, 0))],
        out_shape=[jax.ShapeDtypeStruct((m, LANES), F32), jax.ShapeDtypeStruct((8, m), F32),
                   jax.ShapeDtypeStruct((n_t, 1, LANES), F32), jax.ShapeDtypeStruct((1, LANES), F32)],
        scratch_shapes=[pltpu.VMEM((1, LANES), F32)],
        compiler_params=_params(("arbitrary",)),
    )(x, w, b)


def _dispatch_tables(before, counts, n_tokens, tm):
    n_t = n_tokens // tm
    before = before[:, 0, :MOE_EXPERTS].astype(I32)
    total = counts[0, :MOE_EXPERTS].astype(I32)
    after = jnp.concatenate([before[1:], total[None, :]], axis=0)
    length = after - before
    length_al = (length + ROW_ALIGN - 1) // ROW_ALIGN * ROW_ALIGN
    t_ids = jnp.arange(n_t, dtype=I32)
    e_ids = jnp.arange(MOE_EXPERTS, dtype=I32)
    before_al = jnp.sum(jnp.where((t_ids[None, :] < t_ids[:, None])[:, :, None], length_al[None, :, :], 0), axis=1)
    total_al = jnp.sum(length_al, axis=0)
    seg = (total_al + tm - 1) // tm * tm
    ends = jnp.sum(jnp.where(e_ids[None, :] <= e_ids[:, None], seg[None, :], 0), axis=1)
    starts = ends - seg
    off = jnp.sum(jnp.where((e_ids[None, :] < e_ids[:, None])[None, :, :], length_al[:, None, :], 0), axis=2)
    used = jnp.sum(length_al, axis=1)
    n_chunks = used // ROW_ALIGN
    dstart = starts[None, :] + before_al
    c_row = jnp.arange(TILE_CHUNKS, dtype=I32) * ROW_ALIGN
    c_exp = jnp.sum(((off + length_al)[:, None, :] <= c_row[None, :, None]).astype(I32), axis=2)
    c_exp = jnp.minimum(c_exp, MOE_EXPERTS - 1)
    onehot = c_exp[:, :, None] == e_ids[None, None, :]
    chunk_row = jnp.sum(jnp.where(onehot, (dstart - off)[:, None, :], 0), axis=2) + c_row[None, :]
    pos_tab = off - before
    n_rows = _sorted_rows(n_tokens, tm)
    n_tiles = n_rows // tm
    tile_start = jnp.arange(n_tiles, dtype=I32) * tm
    tile_expert = jnp.minimum(jnp.sum((ends[None, :] <= tile_start[:, None]).astype(I32), axis=1), MOE_EXPERTS - 1)
    tile_valid = (tile_start < ends[-1]).astype(I32)
    tile_first = jnp.concatenate([jnp.ones((1,), I32), (tile_expert[1:] != tile_expert[:-1]).astype(I32)])
    j_ids = jnp.arange(n_tiles, dtype=I32)
    tile_slot = (jnp.sum(jnp.where(j_ids[None, :] <= j_ids[:, None], tile_first[None, :], 0), axis=1) - 1) % 2
    later = (e_ids[None, :] > e_ids[:, None]) & (total[None, :] > 0)
    next_used = jnp.min(jnp.where(later, e_ids[None, :], MOE_EXPERTS), axis=1)
    next_used = jnp.where(next_used == MOE_EXPERTS, -1, next_used).astype(I32)
    tile_next = jnp.sum(jnp.where(tile_expert[:, None] == e_ids[None, :], next_used[None, :], 0), axis=1)
    tail_first = starts + total_al
    tail_chunks = (seg - total_al) // ROW_ALIGN
    return dict(n_chunks=n_chunks.astype(I32), chunk_row=chunk_row.reshape(-1).astype(I32),
                pos_tab=pos_tab.reshape(-1).astype(I32), pos_rows=pos_tab.astype(F32),
                tile_expert=tile_expert.astype(I32), tile_valid=tile_valid, tile_first=tile_first,
                tile_next=tile_next.astype(I32), n_valid=jnp.sum(tile_valid)[None].astype(I32),
                tile_slot=tile_slot.astype(I32),
                tail_first=tail_first.astype(I32),
                tail_chunks=tail_chunks.astype(I32))


def _sorted_rows(n_tokens, tm):
    n_t = n_tokens // tm
    bound = 2 * n_tokens + n_t * MOE_EXPERTS * (ROW_ALIGN - 1) + MOE_EXPERTS * (tm - ROW_ALIGN) + 1
    return (bound + tm - 1) // tm * tm


def _chunk_copy_out(buf, slot, c, row, hbm, sem):
    src = buf.at[slot, pl.ds(pl.multiple_of(c * ROW_ALIGN, ROW_ALIGN), ROW_ALIGN)]
    return pltpu.make_async_copy(src, hbm.at[pl.ds(pl.multiple_of(row, ROW_ALIGN), ROW_ALIGN)], sem.at[slot])


def _chunk_copy_in(hbm, row, buf, slot, c, sem):
    dst = buf.at[slot, pl.ds(pl.multiple_of(c * ROW_ALIGN, ROW_ALIGN), ROW_ALIGN)]
    src = hbm.at[pl.ds(pl.multiple_of(row, ROW_ALIGN), ROW_ALIGN), pl.ds(0, buf.shape[2])]
    return pltpu.make_async_copy(src, dst, sem.at[slot])


def _moe_dispatch_kernel(nch_ref, crow_ref, ptab_ref, tfirst_ref, tchunks_ref, tvalid_ref,
                         x_ref, rt_ref, r_ref, xs_hbm, buf, zeros, sem, sem_z, *, tm, n_tiles):
    i = pl.program_id(0)
    n_t = pl.num_programs(0)
    slot = i % 2

    def wait_tile(tile, s):
        def body(c, carry):
            _chunk_copy_out(buf, s, 0, 0, xs_hbm, sem).wait()
            return carry
        lax.fori_loop(0, nch_ref[tile], body, 0)

    def zero_tail(e, k):
        return pltpu.make_async_copy(
            zeros.at[pl.ds(0, ROW_ALIGN)],
            xs_hbm.at[pl.ds(pl.multiple_of(tfirst_ref[e] + k * ROW_ALIGN, ROW_ALIGN), ROW_ALIGN)], sem_z.at[0])

    def zero_tile(t):
        return pltpu.make_async_copy(zeros, xs_hbm.at[pl.ds(pl.multiple_of(t * tm, tm), tm)], sem_z.at[1])

    def for_each_fill(tail_fn, tile_fn):
        def per_expert(e, carry):
            def per_chunk(k, c2):
                tail_fn(e, k)
                return c2
            lax.fori_loop(0, tchunks_ref[e], per_chunk, 0)
            return carry
        lax.fori_loop(0, MOE_EXPERTS, per_expert, 0)

        def per_tile(t, carry):
            @pl.when(tvalid_ref[t] == 0)
            def _():
                tile_fn(t)
            return carry
        lax.fori_loop(0, n_tiles, per_tile, 0)

    @pl.when(i == 0)
    def _():
        zeros[...] = jnp.zeros(zeros.shape, zeros.dtype)
        for_each_fill(lambda e, k: zero_tail(e, k).start(), lambda t: zero_tile(t).start())

    @pl.when(i >= 2)
    def _():
        wait_tile(i - 2, slot)

    pos1 = rt_ref[4:5, :].astype(I32)
    pos2 = rt_ref[5:6, :].astype(I32)
    e1 = rt_ref[0:1, :].astype(I32)
    e2 = rt_ref[1:2, :].astype(I32)
    for e in range(MOE_EXPERTS):
        shift = ptab_ref[i * MOE_EXPERTS + e]
        pos1 = pos1 + jnp.where(e1 == e, shift, 0)
        pos2 = pos2 + jnp.where(e2 == e, shift, 0)
    lane = lax.broadcasted_iota(I32, (tm, LANES), 1)

    def gate_terms(g):
        hi = g.astype(BF16).astype(F32)
        return jnp.where(lane == 0, hi, jnp.where(lane == 1, g - hi, 0.0)).astype(BF16)

    gate1 = gate_terms(r_ref[:, 2:3])
    gate2 = gate_terms(r_ref[:, 3:4])
    blk = 256
    for k in range(TILE_ROWS // blk):
        row = lax.broadcasted_iota(I32, (blk, tm), 0) + k * blk
        pick1 = jnp.where(row == pos1, 1.0, 0.0).astype(BF16)
        pick2 = jnp.where(row == pos2, 1.0, 0.0).astype(BF16)
        at = slice(k * blk, (k + 1) * blk)
        buf[slot, at, :D_MODEL] = jnp.dot(pick1 + pick2, x_ref[...], preferred_element_type=F32).astype(BF16)
        buf[slot, at, D_MODEL:] = (jnp.dot(pick1, gate1, preferred_element_type=F32)
                                   + jnp.dot(pick2, gate2, preferred_element_type=F32)).astype(BF16)

    def send(c, carry):
        _chunk_copy_out(buf, slot, c, crow_ref[i * TILE_CHUNKS + c], xs_hbm, sem).start()
        return carry
    lax.fori_loop(0, nch_ref[i], send, 0)

    @pl.when(i == n_t - 1)
    def _():
        @pl.when(i >= 1)
        def _():
            wait_tile(i - 1, 1 - slot)
        wait_tile(i, slot)
        for_each_fill(lambda e, k: zero_tail(e, k).wait(), lambda t: zero_tile(t).wait())


def _moe_dispatch(x_bf, route_t, route, tables, tm):
    m, d = x_bf.shape
    n_rows = _sorted_rows(m, tm)
    grid_spec = pltpu.PrefetchScalarGridSpec(
        num_scalar_prefetch=6,
        grid=(m // tm,),
        in_specs=[pl.BlockSpec((tm, d), lambda i, *_: (i, 0)),
                  pl.BlockSpec((8, tm), lambda i, *_: (0, i)),
                  pl.BlockSpec((tm, LANES), lambda i, *_: (i, 0))],
        out_specs=pl.BlockSpec(memory_space=pl.ANY),
        scratch_shapes=[pltpu.VMEM((2, TILE_ROWS, SORTED_WIDTH), BF16), pltpu.VMEM((tm, SORTED_WIDTH), BF16),
                        pltpu.SemaphoreType.DMA((2,)), pltpu.SemaphoreType.DMA((2,))],
    )
    return pl.pallas_call(
        functools.partial(_moe_dispatch_kernel, tm=tm, n_tiles=n_rows // tm),
        name="moe_dispatch",
        grid_spec=grid_spec,
        out_shape=jax.ShapeDtypeStruct((n_rows, SORTED_WIDTH), BF16),
        compiler_params=_params(("arbitrary",)),
    )(tables["n_chunks"], tables["chunk_row"], tables["pos_tab"], tables["tail_first"], tables["tail_chunks"],
      tables["tile_valid"], x_bf, route_t, route)


def _moe_ffn_kernel(te_ref, tv_ref, tf_ref, tn_ref, nv_ref, ts_ref, x_ref, wg_hbm, wu_hbm, wd_hbm, o_ref,
                    wg_f, wu_f, wd_f, wg_bf, wu_bf, wd_bf, sem_w, *, layer):
    j = pl.program_id(0)
    valid = tv_ref[j] == 1
    slot = ts_ref[j]

    def weight_copies(e, s):
        return (pltpu.make_async_copy(wg_hbm.at[layer, e], wg_f.at[s], sem_w.at[s, 0]),
                pltpu.make_async_copy(wu_hbm.at[layer, e], wu_f.at[s], sem_w.at[s, 1]),
                pltpu.make_async_copy(wd_hbm.at[layer, e], wd_f.at[s], sem_w.at[s, 2]))

    @pl.when(j == 0)
    def _():
        for c in weight_copies(te_ref[0], slot):
            c.start()

    @pl.when(valid & (tf_ref[j] == 1))
    def _():
        @pl.when(tn_ref[j] >= 0)
        def _():
            for c in weight_copies(tn_ref[j], 1 - slot):
                c.start()

        for c in weight_copies(te_ref[j], slot):
            c.wait()
        _cast_rows(wg_f.at[slot], wg_bf)
        _cast_rows(wu_f.at[slot], wu_bf)
        _cast_rows(wd_f.at[slot], wd_bf)

    @pl.when(valid)
    def _():
        xb = x_ref[:, :D_MODEL]
        gate = x_ref[:, D_MODEL:D_MODEL + 1].astype(F32) + x_ref[:, D_MODEL + 1:D_MODEL + 2].astype(F32)
        hg = jnp.dot(xb, wg_bf[...], preferred_element_type=F32)
        hu = jnp.dot(xb, wu_bf[...], preferred_element_type=F32)
        act = (hg * (1.0 / (1.0 + jnp.exp(-hg))) * hu).astype(BF16)
        o_ref[...] = (jnp.dot(act, wd_bf[...], preferred_element_type=F32) * gate).astype(o_ref.dtype)


def _moe_ffn(x_sorted, w_gate, w_up, w_down, layer, tables, tm):
    n_rows, width = x_sorted.shape
    d, f = w_gate.shape[2], w_gate.shape[3]
    any_spec = pl.BlockSpec(memory_space=pl.ANY)
    grid_spec = pltpu.PrefetchScalarGridSpec(
        num_scalar_prefetch=6,
        grid=(n_rows // tm,),
        in_specs=[pl.BlockSpec((tm, width), lambda j, te, tv, tf, tn, nv, ts: (jnp.minimum(j, nv[0] - 1), 0)),
                  any_spec, any_spec, any_spec],
        out_specs=pl.BlockSpec((tm, d), lambda j, te, tv, tf, tn, nv, ts: (jnp.minimum(j, nv[0] - 1), 0)),
        scratch_shapes=[pltpu.VMEM((2, d, f), F32), pltpu.VMEM((2, d, f), F32), pltpu.VMEM((2, f, d), F32),
                        pltpu.VMEM((d, f), BF16), pltpu.VMEM((d, f), BF16), pltpu.VMEM((f, d), BF16),
                        pltpu.SemaphoreType.DMA((2, 3))],
    )
    return pl.pallas_call(
        functools.partial(_moe_ffn_kernel, layer=layer),
        name="moe_ffn",
        grid_spec=grid_spec,
        out_shape=jax.ShapeDtypeStruct((n_rows, width), BF16),
        input_output_aliases={6: 0},
        compiler_params=_params(("arbitrary",)),
    )(tables["tile_expert"], tables["tile_valid"], tables["tile_first"], tables["tile_next"], tables["n_valid"],
      tables["tile_slot"], x_sorted, w_gate, w_up, w_down)


def _moe_combine_kernel(nch_ref, crow_ref, ys_hbm, x_ref, r_ref, ptab_ref, g_ref, b_ref, o_ref, obf_ref,
                        buf, sem, *, tm):
    i = pl.program_id(0)
    n_t = pl.num_programs(0)
    slot = i % 2

    def fetch(tile, s):
        def body(c, carry):
            _chunk_copy_in(ys_hbm, crow_ref[tile * TILE_CHUNKS + c], buf, s, c, sem).start()
            return carry
        lax.fori_loop(0, nch_ref[tile], body, 0)

    @pl.when(i == 0)
    def _():
        buf[...] = jnp.zeros(buf.shape, buf.dtype)
        fetch(0, 0)

    @pl.when(i + 1 < n_t)
    def _():
        fetch(i + 1, 1 - slot)

    def arrived(c, carry):
        _chunk_copy_in(ys_hbm, 0, buf, slot, 0, sem).wait()
        return carry
    lax.fori_loop(0, nch_ref[i], arrived, 0)

    lane = lax.broadcasted_iota(I32, (tm, LANES), 1)
    shift = ptab_ref[...]
    e1 = r_ref[:, 0:1].astype(I32)
    e2 = r_ref[:, 1:2].astype(I32)
    pos1 = (r_ref[:, 4:5] + jnp.sum(jnp.where(lane == e1, shift, 0.0), axis=1, keepdims=True)).astype(I32)
    pos2 = (r_ref[:, 5:6] + jnp.sum(jnp.where(lane == e2, shift, 0.0), axis=1, keepdims=True)).astype(I32)
    col = lax.broadcasted_iota(I32, (tm, TILE_ROWS), 1)
    pick = jnp.where((col == pos1) | (col == pos2), 1.0, 0.0).astype(BF16)
    y = jnp.dot(pick, buf[slot], preferred_element_type=F32)
    out = _layernorm_rows(DEEPNORM_ALPHA * x_ref[...] + y, g_ref[...], b_ref[...])
    o_ref[...] = out
    obf_ref[...] = out.astype(BF16)


def _moe_combine(y_sorted, x, route, tables, g, b, tm):
    m, d = x.shape
    grid_spec = pltpu.PrefetchScalarGridSpec(
        num_scalar_prefetch=2,
        grid=(m // tm,),
        in_specs=[pl.BlockSpec(memory_space=pl.ANY),
                  pl.BlockSpec((tm, d), lambda i, *_: (i, 0)),
                  pl.BlockSpec((tm, LANES), lambda i, *_: (i, 0)),
                  pl.BlockSpec((None, 1, LANES), lambda i, *_: (i, 0, 0)),
                  pl.BlockSpec((1, d), lambda i, *_: (0, 0)),
                  pl.BlockSpec((1, d), lambda i, *_: (0, 0))],
        out_specs=[pl.BlockSpec((tm, d), lambda i, *_: (i, 0)),
                   pl.BlockSpec((tm, d), lambda i, *_: (i, 0))],
        scratch_shapes=[pltpu.VMEM((2, TILE_ROWS, d), BF16), pltpu.SemaphoreType.DMA((2,))],
    )
    pos_rows = _pad_cols(tables["pos_rows"])[:, None, :]
    return pl.pallas_call(
        functools.partial(_moe_combine_kernel, tm=tm),
        name="moe_combine",
        grid_spec=grid_spec,
        out_shape=[jax.ShapeDtypeStruct((m, d), F32), jax.ShapeDtypeStruct((m, d), BF16)],
        compiler_params=_params(("arbitrary",)),
    )(tables["n_chunks"], tables["chunk_row"], y_sorted, x, route, pos_rows, g, b)


def _moe_layer(x, x_bf, w_group, b_group, w_router, b_router, w_gate, w_up, w_down, layer, ln_g, ln_b):
    n_tokens = x.shape[0]
    tm = min(MOE_TM, n_tokens)
    w_route = _pad_cols(jnp.concatenate([w_group, w_router], axis=1))
    b_route = _pad_cols(jnp.concatenate([b_group, b_router])[None, :])
    route, route_t, before, counts = _route(x, w_route, b_route, tm)
    tables = _dispatch_tables(before, counts, n_tokens, tm)
    x_sorted = _moe_dispatch(x_bf, route_t, route, tables, tm)
    y_sorted = _moe_ffn(x_sorted, w_gate, w_up, w_down, layer, tables, tm)
    return _moe_combine(y_sorted, x, route, tables, ln_g[None, :], ln_b[None, :], tm)


def kernel(x, t5_rel_bias, mla_w_in, mla_q_norm, mla_kv_norm, mla_w_q_up, mla_w_kv_up, mla_w_o, fox_w_in, fox_b_f, fox_w_o, dsa_w_in, dsa_w_o, ln_g, ln_b, moe_w_group, moe_b_group, moe_w_router, moe_b_router, moe_w_gate, moe_w_up, moe_w_down):
    batch, seq, d = x.shape
    x = x.reshape(batch * seq, d)
    x_bf = x.astype(BF16)
    for layer in range(DEPTH):
        kind = layer % N_MIXERS
        j = layer // N_MIXERS
        g0, b0 = ln_g[layer, 0], ln_b[layer, 0]
        if kind == 0:
            x, x_bf = _mla_layer(x, x_bf, mla_w_in[j], mla_q_norm[j], mla_kv_norm[j], mla_w_q_up[j],
                                 mla_w_kv_up[j], mla_w_o[j], g0, b0, batch, seq)
        elif kind == 1:
            x, x_bf = _fox_layer(x, x_bf, fox_w_in[j], fox_b_f[j], fox_w_o[j], g0, b0, batch, seq)
        else:
            x, x_bf = _dsa_layer(x, x_bf, t5_rel_bias, dsa_w_in[j], dsa_w_o[j], g0, b0, batch, seq)
        x, x_bf = _moe_layer(x, x_bf, moe_w_group[layer], moe_b_group[layer], moe_w_router[layer], moe_b_router[layer],
                             moe_w_gate, moe_w_up, moe_w_down, layer, ln_g[layer, 1], ln_b[layer, 1])
    return x.reshape(batch, seq, d)
```

```python
import functools
import math

import numpy as np
import jax
import jax.numpy as jnp
from jax import lax
from jax.experimental import pallas as pl
from jax.experimental.pallas import tpu as pltpu

F32 = jnp.float32
BF16 = jnp.bfloat16
I32 = jnp.int32

D_MODEL = 2048
DEPTH = 4
CHUNK = 64
N_MIXERS = 3
DEEPNORM_ALPHA = (2.0 * DEPTH) ** 0.25

HEADS = 16
HEAD_DIM = 128
MLA_NOPE_DIM = 128
MLA_ROPE_DIM = 64
MLA_Q_RANK = 512
MLA_KV_RANK = 512
MLA_QK_PAD = 256
ROPE_THETA = 10000.0

IDX_HEADS = 16
IDX_DIM = 64
IDX_TOPK_MAX = 256

T5_BUCKETS = 32
T5_MAX_DISTANCE = 128
T5_FAR_BUCKET = T5_BUCKETS // 2 - 1
T5_BAND_BACK = 128

MOE_GROUPS = 4
MOE_EXPERTS_PER_GROUP = 8
MOE_EXPERTS = MOE_GROUPS * MOE_EXPERTS_PER_GROUP
MOE_D_FF = 512

LN_EPS = 1e-5
RMS_EPS = 1e-6
NEG_INF = -1e30

LANES = 128
VMEM_LIMIT = 56 * 1024 * 1024

ATTN_TQ = 256
MOE_TM = 256
ROW_ALIGN = 16
TILE_ROWS = -(-(2 * MOE_TM + MOE_EXPERTS * (ROW_ALIGN - 1)) // 256) * 256
TILE_CHUNKS = TILE_ROWS // ROW_ALIGN
SORTED_WIDTH = D_MODEL + LANES


def _params(semantics, vmem=VMEM_LIMIT):
    return pltpu.CompilerParams(dimension_semantics=semantics, vmem_limit_bytes=vmem)


def _cast_rows(src_ref, dst_ref):
    rows = 16
    while 2 * rows * src_ref.shape[1] <= 32 * 8 * LANES and src_ref.shape[0] % (2 * rows) == 0:
        rows *= 2

    def body(i, carry):
        at = pl.ds(pl.multiple_of(i * rows, rows), rows)
        dst_ref[at, :] = src_ref[at, :].astype(dst_ref.dtype)
        return carry
    lax.fori_loop(0, src_ref.shape[0] // rows, body, 0, unroll=2)


def _mm_kernel(a_ref, w_ref, o_ref, w_bf, *, scale, scaled_cols):
    @pl.when(pl.program_id(1) == 0)
    def _():
        _cast_rows(w_ref, w_bf)

    acc = jnp.dot(a_ref[...], w_bf[...], preferred_element_type=F32)
    if scaled_cols:
        tn = o_ref.shape[1]
        col = pl.program_id(0) * tn + lax.broadcasted_iota(I32, (1, tn), 1)
        acc = acc * jnp.where(col < scaled_cols, scale, 1.0)
    o_ref[...] = acc.astype(o_ref.dtype)


def _matmul(a, w, layer, *, n_out, tn, tm=1024, scale=1.0, scaled_cols=0, out_dtype=BF16):
    m, k = a.shape
    tm = min(tm, m)
    kern = functools.partial(_mm_kernel, scale=scale, scaled_cols=scaled_cols)
    return pl.pallas_call(
        kern,
        name="proj_mm",
        grid=(n_out // tn, m // tm),
        in_specs=[pl.BlockSpec((tm, k), lambda j, i: (i, 0)),
                  pl.BlockSpec((None, k, tn), lambda j, i: (layer, 0, j))],
        out_specs=pl.BlockSpec((tm, tn), lambda j, i: (i, j)),
        out_shape=jax.ShapeDtypeStruct((m, n_out), out_dtype),
        scratch_shapes=[pltpu.VMEM((k, tn), BF16)],
        compiler_params=_params(("arbitrary", "arbitrary")),
    )(a, w)


def _split3(v):
    hi = v.astype(BF16)
    r1 = v - hi.astype(F32)
    mid = r1.astype(BF16)
    lo = (r1 - mid.astype(F32)).astype(BF16)
    return hi, mid, lo


def _dot_precise(x, w):
    xh = x.astype(BF16)
    xm = (x - xh.astype(F32)).astype(BF16)
    wh = w.astype(BF16)
    wm = (w - wh.astype(F32)).astype(BF16)
    d = lambda p, q: jnp.dot(p, q, preferred_element_type=F32)
    return (d(xm, wh) + d(xh, wm)) + d(xh, wh)


def _small_mm_kernel(x_ref, w_ref, b_ref, o_ref):
    o_ref[...] = _dot_precise(x_ref[...], w_ref[...]) + b_ref[...]


def _small_matmul(x, w, b, tm=512):
    m, k = x.shape
    tm = min(tm, m)
    return pl.pallas_call(
        _small_mm_kernel,
        name="small_mm",
        grid=(m // tm,),
        in_specs=[pl.BlockSpec((tm, k), lambda i: (i, 0)),
                  pl.BlockSpec((k, LANES), lambda i: (0, 0)),
                  pl.BlockSpec((1, LANES), lambda i: (0, 0))],
        out_specs=pl.BlockSpec((tm, LANES), lambda i: (i, 0)),
        out_shape=jax.ShapeDtypeStruct((m, LANES), F32),
        compiler_params=_params(("arbitrary",)),
    )(x, w, b)


def _pad_cols(w, width=LANES):
    return jnp.pad(w, ((0, 0), (0, width - w.shape[1])))


def _layernorm_rows(z, g, b):
    mu = jnp.mean(z, axis=-1, keepdims=True)
    zc = z - mu
    var = jnp.mean(zc * zc, axis=-1, keepdims=True)
    return zc * lax.rsqrt(var + LN_EPS) * g + b


def _route_tile(x, w, b, run_ref):
    tm = x.shape[0]
    logit = _dot_precise(x, w) + b
    lane = lax.broadcasted_iota(I32, logit.shape, 1)
    big = jnp.int32(LANES)

    def first_argmax(mask):
        top = jnp.max(jnp.where(mask, logit, -jnp.inf), axis=1, keepdims=True)
        idx = jnp.min(jnp.where(mask & (logit == top), lane, big), axis=1, keepdims=True)
        return top, idx

    is_group = lane < MOE_GROUPS
    g_top, g_sel = first_argmax(is_group)
    p_g = 1.0 / jnp.sum(jnp.where(is_group, jnp.exp(logit - g_top), 0.0), axis=1, keepdims=True)
    lo = MOE_GROUPS + g_sel * MOE_EXPERTS_PER_GROUP
    in_group = (lane >= lo) & (lane < lo + MOE_EXPERTS_PER_GROUP)
    v1, i1 = first_argmax(in_group)
    v2, i2 = first_argmax(in_group & (lane != i1))
    ex = jnp.exp(v2 - v1)
    w1 = 1.0 / (1.0 + ex)
    w2 = ex / (1.0 + ex)
    e1 = i1 - MOE_GROUPS
    e2 = i2 - MOE_GROUPS
    pick1 = lane == e1
    pick2 = lane == e2
    both = jnp.where(pick1 | pick2, 1.0, 0.0)
    earlier = lax.broadcasted_iota(I32, (tm, tm), 1) < lax.broadcasted_iota(I32, (tm, tm), 0)
    start = run_ref[...]
    before = jnp.dot(jnp.where(earlier, 1.0, 0.0).astype(BF16), both.astype(BF16),
                     preferred_element_type=F32) + start
    r1 = jnp.sum(jnp.where(pick1, before, 0.0), axis=1, keepdims=True)
    r2 = jnp.sum(jnp.where(pick2, before, 0.0), axis=1, keepdims=True)
    run_ref[...] = start + jnp.sum(both, axis=0, keepdims=True)
    out = jnp.where(lane == 0, e1.astype(F32), 0.0)
    out = jnp.where(lane == 1, e2.astype(F32), out)
    out = jnp.where(lane == 2, w1 * p_g, out)
    out = jnp.where(lane == 3, w2 * p_g, out)
    out = jnp.where(lane == 4, r1, out)
    out = jnp.where(lane == 5, r2, out)
    return out, start


def _mm_res_ln_kernel(a_ref, w_ref, x_ref, g_ref, b_ref, wr_ref, br_ref,
                      o_ref, obf_ref, r_ref, rt_ref, before_ref, cnt_ref, run_ref, *, sub):
    @pl.when(pl.program_id(0) == 0)
    def _():
        run_ref[...] = jnp.zeros(run_ref.shape, F32)

    y = jnp.dot(a_ref[...], w_ref[...], preferred_element_type=F32)
    out = _layernorm_rows(DEEPNORM_ALPHA * x_ref[...] + y, g_ref[...], b_ref[...])
    o_ref[...] = out
    obf_ref[...] = out.astype(BF16)
    for k in range(out.shape[0] // sub):
        rows = slice(k * sub, (k + 1) * sub)
        slab, before = _route_tile(out[rows, :], wr_ref[...], br_ref[...], run_ref)
        r_ref[rows, :] = slab
        rt_ref[:, rows] = slab.T[:8, :]
        before_ref[k] = before
    cnt_ref[...] = run_ref[...]


def _mm_res_ln(a, w, x, g, b, w_route, b_route, tm=512):
    m, k = a.shape
    d = w.shape[1]
    tm = min(tm, m)
    sub = min(MOE_TM, tm)
    per = tm // sub
    return pl.pallas_call(
        functools.partial(_mm_res_ln_kernel, sub=sub),
        name="out_proj_res_ln",
        grid=(m // tm,),
        in_specs=[pl.BlockSpec((tm, k), lambda i: (i, 0)),
                  pl.BlockSpec((k, d), lambda i: (0, 0)),
                  pl.BlockSpec((tm, d), lambda i: (i, 0)),
                  pl.BlockSpec((1, d), lambda i: (0, 0)),
                  pl.BlockSpec((1, d), lambda i: (0, 0)),
                  pl.BlockSpec((d, LANES), lambda i: (0, 0)),
                  pl.BlockSpec((1, LANES), lambda i: (0, 0))],
        out_specs=[pl.BlockSpec((tm, d), lambda i: (i, 0)),
                   pl.BlockSpec((tm, d), lambda i: (i, 0)),
                   pl.BlockSpec((tm, LANES), lambda i: (i, 0)),
                   pl.BlockSpec((8, tm), lambda i: (0, i)),
                   pl.BlockSpec((per, 1, LANES), lambda i: (i, 0, 0)),
                   pl.BlockSpec((1, LANES), lambda i: (0, 0))],
        out_shape=[jax.ShapeDtypeStruct((m, d), F32), jax.ShapeDtypeStruct((m, d), BF16),
                   jax.ShapeDtypeStruct((m, LANES), F32), jax.ShapeDtypeStruct((8, m), F32),
                   jax.ShapeDtypeStruct((m // sub, 1, LANES), F32), jax.ShapeDtypeStruct((1, LANES), F32)],
        scratch_shapes=[pltpu.VMEM((1, LANES), F32)],
        compiler_params=_params(("arbitrary",)),
    )(a, w, x, g, b, w_route, b_route)


def _rope_tables(seq):
    half = MLA_ROPE_DIM // 2
    inv_freq = ROPE_THETA ** (-jnp.arange(half, dtype=F32) / half)
    ang = jnp.arange(seq, dtype=I32).astype(F32)[:, None] * inv_freq[None, :]
    cos, sin = jnp.cos(ang), jnp.sin(ang)
    z = jnp.zeros_like(cos)
    cos_t = jnp.concatenate([cos, cos, z, z], axis=1)
    sin_a = jnp.concatenate([-sin, z, z, z], axis=1)
    sin_b = jnp.concatenate([z, sin, z, z], axis=1)
    return cos_t, sin_a, sin_b


def _rope_lanes(r, cos_t, sin_a, sin_b):
    from_right = pltpu.roll(r, LANES - MLA_ROPE_DIM // 2, 1)
    from_left = pltpu.roll(r, MLA_ROPE_DIM // 2, 1)
    return r * cos_t + from_right * sin_a + from_left * sin_b


def _rms_rows(v, g):
    return v * lax.rsqrt(jnp.mean(v * v, axis=-1, keepdims=True) + RMS_EPS) * g


def _mla_in_kernel(x_ref, w_ref, qn_ref, kvn_ref, cos_ref, sina_ref, sinb_ref, cq_ref, ckv_ref, kr_ref):
    h = jnp.dot(x_ref[...], w_ref[...], preferred_element_type=F32)
    cq_ref[...] = _rms_rows(h[:, :MLA_Q_RANK], qn_ref[...]).astype(BF16)
    ckv_ref[...] = _rms_rows(h[:, MLA_Q_RANK:MLA_Q_RANK + MLA_KV_RANK], kvn_ref[...]).astype(BF16)
    r = h[:, MLA_Q_RANK + MLA_KV_RANK:]
    kr_ref[...] = _rope_lanes(r, cos_ref[...], sina_ref[...], sinb_ref[...]).astype(BF16)


def _mla_in(x_bf, w_in_pad, q_norm, kv_norm, tables, seq, tm=512):
    m, k = x_bf.shape
    tm = min(tm, seq)
    nw = w_in_pad.shape[1]
    per_seq = seq // tm
    tab_spec = pl.BlockSpec((tm, LANES), lambda i: (i % per_seq, 0))
    return pl.pallas_call(
        _mla_in_kernel,
        name="mla_in",
        grid=(m // tm,),
        in_specs=[pl.BlockSpec((tm, k), lambda i: (i, 0)),
                  pl.BlockSpec((k, nw), lambda i: (0, 0)),
                  pl.BlockSpec((1, MLA_Q_RANK), lambda i: (0, 0)),
                  pl.BlockSpec((1, MLA_KV_RANK), lambda i: (0, 0)),
                  tab_spec, tab_spec, tab_spec],
        out_specs=[pl.BlockSpec((tm, MLA_Q_RANK), lambda i: (i, 0)),
                   pl.BlockSpec((tm, MLA_KV_RANK), lambda i: (i, 0)),
                   pl.BlockSpec((tm, LANES), lambda i: (i, 0))],
        out_shape=[jax.ShapeDtypeStruct((m, MLA_Q_RANK), BF16),
                   jax.ShapeDtypeStruct((m, MLA_KV_RANK), BF16),
                   jax.ShapeDtypeStruct((m, LANES), BF16)],
        compiler_params=_params(("arbitrary",)),
    )(x_bf, w_in_pad, q_norm, kv_norm, *tables)


def _fill_v_aug(va_ref, v_ref):
    va_ref[:, :HEAD_DIM] = v_ref[...]
    va_ref[:, HEAD_DIM:] = jnp.ones((v_ref.shape[0], HEAD_DIM), BF16)


def _softmax_pv(pieces):
    m = None
    for s, _ in pieces:
        pm = jnp.max(s, axis=1, keepdims=True)
        m = pm if m is None else jnp.maximum(m, pm)
    acc = None
    for s, v in pieces:
        t = jnp.dot(jnp.exp((s - m).astype(BF16)), v, preferred_element_type=F32)
        acc = t if acc is None else acc + t
    return acc[:, :HEAD_DIM] / acc[:, HEAD_DIM:HEAD_DIM + 1]


def _nt_dot(q, k):
    return lax.dot_general(q, k, (((1,), (1,)), ((), ())), preferred_element_type=F32)


def _attend_tiles(n_tiles, scores, finish):
    s_next = scores(0)
    for i in range(n_tiles):
        s = s_next
        if i + 1 < n_tiles:
            s_next = scores(i + 1)
        finish(i, s)


def _mla_q_up_kernel(c_ref, w_ref, cos_ref, sina_ref, sinb_ref, o_ref, w_bf, *, scale):
    @pl.when(pl.program_id(1) == 0)
    def _():
        _cast_rows(w_ref, w_bf)

    acc = jnp.dot(c_ref[...], w_bf[...], preferred_element_type=F32) * scale
    for h in range(o_ref.shape[1] // MLA_QK_PAD):
        at = h * MLA_QK_PAD
        o_ref[:, at:at + MLA_NOPE_DIM] = acc[:, at:at + MLA_NOPE_DIM].astype(BF16)
        r = acc[:, at + MLA_NOPE_DIM:at + MLA_QK_PAD]
        o_ref[:, at + MLA_NOPE_DIM:at + MLA_QK_PAD] = _rope_lanes(
            r, cos_ref[...], sina_ref[...], sinb_ref[...]).astype(BF16)


def _mla_kv_up_kernel(c_ref, w_ref, kr_ref, k_ref, v_ref, w_bf):
    @pl.when(pl.program_id(1) == 0)
    def _():
        _cast_rows(w_ref, w_bf)

    acc = jnp.dot(c_ref[...], w_bf[...], preferred_element_type=F32)
    ones = jnp.ones((acc.shape[0], HEAD_DIM), BF16)
    for h in range(k_ref.shape[1] // MLA_QK_PAD):
        at = h * MLA_QK_PAD
        k_ref[:, at:at + MLA_NOPE_DIM] = acc[:, at:at + MLA_NOPE_DIM].astype(BF16)
        k_ref[:, at + MLA_NOPE_DIM:at + MLA_QK_PAD] = kr_ref[...]
        v_ref[:, at:at + HEAD_DIM] = acc[:, at + MLA_NOPE_DIM:at + MLA_QK_PAD].astype(BF16)
        v_ref[:, at + HEAD_DIM:at + 2 * HEAD_DIM] = ones


def _mla_up(cq, ckv, kr, w_q_up, w_kv_up, j, tables, seq, scale, tm=1024, tn=1024):
    m, rank = cq.shape
    tm = min(tm, seq)
    per_seq = seq // tm
    n = HEADS * MLA_QK_PAD
    tab_spec = pl.BlockSpec((tm, LANES), lambda j, i: (i % per_seq, 0))
    row_spec = pl.BlockSpec((tm, rank), lambda j, i: (i, 0))
    w_spec = pl.BlockSpec((rank, tn), lambda j, i: (0, j))
    out_spec = pl.BlockSpec((tm, tn), lambda j, i: (i, j))
    out = jax.ShapeDtypeStruct((m, n), BF16)
    q = pl.pallas_call(
        functools.partial(_mla_q_up_kernel, scale=scale),
        name="mla_q_up",
        grid=(n // tn, m // tm),
        in_specs=[row_spec, w_spec, tab_spec, tab_spec, tab_spec],
        out_specs=out_spec,
        out_shape=out,
        scratch_shapes=[pltpu.VMEM((rank, tn), BF16)],
        compiler_params=_params(("arbitrary", "arbitrary")),
    )(cq, w_q_up, *tables)
    k, v = pl.pallas_call(
        _mla_kv_up_kernel,
        name="mla_kv_up",
        grid=(n // tn, m // tm),
        in_specs=[row_spec, pl.BlockSpec((None, rank, tn), lambda c, i: (j, 0, c)),
                  pl.BlockSpec((tm, LANES), lambda c, i: (i, 0))],
        out_specs=[out_spec, out_spec],
        out_shape=[out, out],
        scratch_shapes=[pltpu.VMEM((rank, tn), BF16)],
        compiler_params=_params(("arbitrary", "arbitrary")),
    )(ckv, w_kv_up, kr)
    return q, k, v


def _mla_attn_kernel(q_ref, k_ref, v_ref, o_ref, *, tq):
    seq = q_ref.shape[0]
    row = lax.broadcasted_iota(I32, (tq, tq), 0)
    col = lax.broadcasted_iota(I32, (tq, tq), 1)
    visible = (col // CHUNK) <= (row // CHUNK)

    def scores(i):
        return _nt_dot(q_ref[i * tq:(i + 1) * tq, :], k_ref[:(i + 1) * tq, :])

    def finish(i, s):
        lo, kv = i * tq, (i + 1) * tq
        pieces = [(jnp.where(visible, s[:, lo:], NEG_INF), v_ref[lo:kv, :])]
        if lo:
            pieces.append((s[:, :lo], v_ref[:lo, :]))
        o_ref[lo:kv, :] = _softmax_pv(pieces).astype(o_ref.dtype)

    _attend_tiles(seq // tq, scores, finish)


def _mla_attn(q, k, v, batch, seq):
    tq = min(ATTN_TQ, seq)
    head_spec = pl.BlockSpec((seq, MLA_QK_PAD), lambda b, h: (b, h))
    return pl.pallas_call(
        functools.partial(_mla_attn_kernel, tq=tq),
        name="mla_attn",
        grid=(batch, HEADS),
        in_specs=[head_spec, head_spec, head_spec],
        out_specs=pl.BlockSpec((seq, HEAD_DIM), lambda b, h: (b, h)),
        out_shape=jax.ShapeDtypeStruct((batch * seq, HEADS * HEAD_DIM), BF16),
        compiler_params=_params(("arbitrary", "arbitrary")),
    )(q, k, v)


def _mla_q_up_layout(w_q_up):
    rank = w_q_up.shape[0]
    w = w_q_up.reshape(rank, HEADS, MLA_NOPE_DIM + MLA_ROPE_DIM)
    w = jnp.pad(w, ((0, 0), (0, 0), (0, MLA_QK_PAD - MLA_NOPE_DIM - MLA_ROPE_DIM)))
    return w.reshape(rank, HEADS * MLA_QK_PAD)


def _mla_layer(x, x_bf, w_in, q_norm, kv_norm, w_q_up, w_kv_up, j, w_o, ln_g, ln_b, route_wb, batch, seq):
    tables = _rope_tables(seq)
    w_in_pad = jnp.pad(w_in, ((0, 0), (0, MLA_Q_RANK + MLA_KV_RANK + LANES - w_in.shape[1]))).astype(BF16)
    cq, ckv, kr = _mla_in(x_bf, w_in_pad, q_norm[None, :], kv_norm[None, :], tables, seq)
    scale = (MLA_NOPE_DIM + MLA_ROPE_DIM) ** -0.5
    q, k, v = _mla_up(cq, ckv, kr, _mla_q_up_layout(w_q_up), w_kv_up, j, tables, seq, scale)
    o = _mla_attn(q, k, v, batch, seq)
    return _mm_res_ln(o, w_o.astype(BF16), x, ln_g[None, :], ln_b[None, :], *route_wb)


def _log_sigmoid(z):
    return jnp.minimum(z, 0.0) - jnp.log1p(jnp.exp(-jnp.abs(z)))


def _fox_gate_kernel(f_ref, c_ref, *, blk):
    seq = f_ref.shape[0]
    tri = (lax.broadcasted_iota(I32, (blk, blk), 1) <= lax.broadcasted_iota(I32, (blk, blk), 0)).astype(BF16)
    carry = jnp.zeros((1, LANES), F32)
    for j in range(seq // blk):
        lf = _log_sigmoid(f_ref[j * blk:(j + 1) * blk, :])
        hi, mid, lo = _split3(lf)
        d = lambda p: jnp.dot(tri, p, preferred_element_type=F32)
        c = (d(lo) + d(mid)) + d(hi) + carry
        c_ref[j * blk:(j + 1) * blk, :] = c
        carry = c[blk - 1:blk, :]


def _fox_gate(f_logit, batch, seq):
    blk = min(256, seq)
    return pl.pallas_call(
        functools.partial(_fox_gate_kernel, blk=blk),
        name="fox_gate",
        grid=(batch,),
        in_specs=[pl.BlockSpec((seq, LANES), lambda b: (b, 0))],
        out_specs=pl.BlockSpec((seq, LANES), lambda b: (b, 0)),
        out_shape=jax.ShapeDtypeStruct((batch * seq, LANES), F32),
        compiler_params=_params(("arbitrary",)),
    )(f_logit)


def _fox_attn_kernel(q_ref, k_ref, v_ref, c_ref, o_ref, qs_ref, ks_ref, *, tq):
    seq = q_ref.shape[0]
    h = pl.program_id(1)
    lane = lax.broadcasted_iota(I32, (seq, LANES), 1)
    c = jnp.sum(jnp.where(lane == h, c_ref[...], 0.0), axis=1, keepdims=True)
    terms = [t.astype(F32) for t in _split3(c)]
    q_side = jnp.where((lane >= 3) & (lane < 6), 1.0, 0.0)
    k_side = jnp.where(lane < 3, 1.0, 0.0)
    for j, t in enumerate(terms):
        q_side = jnp.where(lane == j, t, q_side)
        k_side = jnp.where(lane == 3 + j, -t, k_side)
    qs_ref[...] = q_side.astype(BF16)
    ks_ref[...] = k_side.astype(BF16)
    row = lax.broadcasted_iota(I32, (tq, tq), 0)
    col = lax.broadcasted_iota(I32, (tq, tq), 1)

    def wide(ref, side_ref, rows):
        return jnp.concatenate([ref[rows, :], side_ref[rows, :]], axis=1)

    def values(rows):
        v = v_ref[rows, :]
        return jnp.concatenate([v, jnp.ones(v.shape, BF16)], axis=1)

    def scores(i):
        return _nt_dot(wide(q_ref, qs_ref, slice(i * tq, (i + 1) * tq)), wide(k_ref, ks_ref, slice(0, (i + 1) * tq)))

    def finish(i, s):
        lo, kv = i * tq, (i + 1) * tq
        pieces = [(jnp.where(col <= row, s[:, lo:], NEG_INF), values(slice(lo, kv)))]
        if lo:
            pieces.append((s[:, :lo], values(slice(0, lo))))
        o_ref[lo:kv, :] = _softmax_pv(pieces).astype(o_ref.dtype)

    _attend_tiles(seq // tq, scores, finish)


def _fox_attn(qkv, c_col, batch, seq):
    tq = min(ATTN_TQ, seq)
    side = pltpu.VMEM((seq, LANES), BF16)
    return pl.pallas_call(
        functools.partial(_fox_attn_kernel, tq=tq),
        name="fox_attn",
        grid=(batch, HEADS),
        in_specs=[pl.BlockSpec((seq, HEAD_DIM), lambda b, h: (b, h)),
                  pl.BlockSpec((seq, HEAD_DIM), lambda b, h: (b, HEADS + h)),
                  pl.BlockSpec((seq, HEAD_DIM), lambda b, h: (b, 2 * HEADS + h)),
                  pl.BlockSpec((seq, LANES), lambda b, h: (b, 0))],
        out_specs=pl.BlockSpec((seq, HEAD_DIM), lambda b, h: (b, h)),
        out_shape=jax.ShapeDtypeStruct((batch * seq, HEADS * HEAD_DIM), BF16),
        scratch_shapes=[side, side],
        compiler_params=_params(("arbitrary", "arbitrary")),
    )(qkv, qkv, qkv, c_col)


def _fox_layer(x, x_bf, w_in, j, b_f, w_o, ln_g, ln_b, route_wb, batch, seq):
    hd = HEADS * HEAD_DIM
    scale = HEAD_DIM ** -0.5
    qkv = _matmul(x_bf, w_in, j, n_out=3 * hd, tn=1024, scale=scale, scaled_cols=hd)
    f_logit = _small_matmul(x, _pad_cols(w_in[j, :, 3 * hd:]), _pad_cols(b_f[None, :]))
    o = _fox_attn(qkv, _fox_gate(f_logit, batch, seq), batch, seq)
    return _mm_res_ln(o, w_o.astype(BF16), x, ln_g[None, :], ln_b[None, :], *route_wb)


def _t5_bucket_table(tq):
    a = np.arange(tq, dtype=np.int64)[:, None]
    b = np.arange(tq + T5_BAND_BACK, dtype=np.int64)[None, :]
    rel = (b - T5_BAND_BACK) - a
    nb = T5_BUCKETS // 2
    max_exact = nb // 2
    ret = np.where(rel > 0, nb, 0)
    n = np.abs(rel)
    nf = np.maximum(n, 1).astype(np.float32)
    large = max_exact + (np.log(nf / np.float32(max_exact)) / np.float32(math.log(T5_MAX_DISTANCE / max_exact))
                         * np.float32(nb - max_exact)).astype(np.int32)
    large = np.minimum(large, nb - 1)
    return (ret + np.where(n < max_exact, n, large)).astype(np.int32)


def _t5_band_kernel(t5_ref, bucket_ref, o_ref):
    bucket = bucket_ref[...]

    def per_head(h, carry):
        acc = jnp.zeros(bucket.shape, F32)
        for b in range(T5_BUCKETS):
            acc = jnp.where(bucket == b, t5_ref[b, h], acc)
        o_ref[h] = acc - t5_ref[T5_FAR_BUCKET, h]
        return carry

    lax.fori_loop(0, HEADS, per_head, 0)


def _t5_band(t5_bias, tq):
    bucket = jnp.asarray(_t5_bucket_table(tq))
    width = tq + T5_BAND_BACK
    return pl.pallas_call(
        _t5_band_kernel,
        name="t5_band",
        grid=(1,),
        in_specs=[pl.BlockSpec(memory_space=pltpu.SMEM),
                  pl.BlockSpec((tq, width), lambda i: (0, 0))],
        out_specs=pl.BlockSpec((HEADS, tq, width), lambda i: (0, 0, 0)),
        out_shape=jax.ShapeDtypeStruct((HEADS, tq, width), F32),
        compiler_params=_params(("arbitrary",)),
    )(t5_bias, bucket)


def _sortable_key(v):
    bits = pltpu.bitcast(v, I32)
    return bits ^ ((bits >> 31) & 0x7FFFFFFF)


def _kth_largest_key(key_ref, kv, k):
    rows = key_ref.shape[0]

    def count_ge(t):
        t_b = jnp.broadcast_to(t, (rows, LANES))
        acc = jnp.zeros((rows, LANES), I32)
        for c in range(kv // LANES):
            acc = acc + jnp.where(key_ref[:, c * LANES:(c + 1) * LANES] >= t_b, 1, 0)
        return jnp.sum(acc, axis=1, keepdims=True)

    t0 = jnp.where(count_ge(jnp.zeros((rows, 1), I32)) >= k, 0, jnp.iinfo(jnp.int32).min).astype(I32)

    def body(it, t):
        cand = t | (jnp.int32(1) << (30 - it))
        return jnp.where(count_ge(cand) >= k, cand, t)

    return lax.fori_loop(0, 31, body, t0)


def _dsa_index_kernel(qi_ref, tail_ref, o_ref, key_ref, *, tq, topk):
    seq = tail_ref.shape[0]
    k_idx = tail_ref[:, :IDX_DIM].astype(BF16)
    for i in range(seq // tq):
        kv = (i + 1) * tq
        rows = slice(i * tq, kv)
        w = tail_ref[rows, IDX_DIM:IDX_DIM + IDX_HEADS] * ((IDX_HEADS * IDX_DIM) ** -0.5)
        score = jnp.zeros((tq, kv), F32)
        for h in range(IDX_HEADS):
            logits = _nt_dot(qi_ref[rows, h * IDX_DIM:(h + 1) * IDX_DIM], k_idx[:kv, :])
            score = score + jnp.maximum(logits, 0.0) * w[:, h:h + 1]
        row = lax.broadcasted_iota(I32, (tq, kv), 0) + i * tq
        col = lax.broadcasted_iota(I32, (tq, kv), 1)
        admissible = (col // CHUNK) <= (row // CHUNK)
        if kv <= topk:
            keep = admissible
        else:
            key_ref[:, :kv] = _sortable_key(jnp.where(admissible, score, NEG_INF))
            thr = _kth_largest_key(key_ref, kv, topk)
            keep = admissible & (key_ref[:, :kv] >= thr)
        o_ref[rows, :kv] = jnp.where(keep, 0.0, NEG_INF).astype(o_ref.dtype)
        if kv < seq:
            o_ref[rows, kv:] = jnp.full((tq, seq - kv), NEG_INF, o_ref.dtype)


def _dsa_index(q_idx, tail, batch, seq, topk):
    tq = min(ATTN_TQ, seq)
    return pl.pallas_call(
        functools.partial(_dsa_index_kernel, tq=tq, topk=topk),
        name="dsa_index",
        grid=(batch,),
        in_specs=[pl.BlockSpec((seq, IDX_HEADS * IDX_DIM), lambda b: (b, 0)),
                  pl.BlockSpec((seq, LANES), lambda b: (b, 0))],
        out_specs=pl.BlockSpec((seq, seq), lambda b: (b, 0)),
        out_shape=jax.ShapeDtypeStruct((batch * seq, seq), BF16),
        scratch_shapes=[pltpu.VMEM((tq, seq), I32)],
        compiler_params=_params(("arbitrary",)),
    )(q_idx, tail)


def _dsa_attn_kernel(q_ref, k_ref, v_ref, sel_ref, band_ref, o_ref, va_ref, *, tq):
    seq = k_ref.shape[0]
    width = band_ref.shape[1]
    _fill_v_aug(va_ref, v_ref)
    def scores(i):
        return _nt_dot(q_ref[i * tq:(i + 1) * tq, :], k_ref[:(i + 1) * tq, :])

    def finish(i, s):
        lo, kv = i * tq, (i + 1) * tq
        near = min(width, kv)
        far = kv - near
        s = s + sel_ref[lo:kv, :kv].astype(F32)
        pieces = [(s[:, far:] + band_ref[:, width - near:], va_ref[far:kv, :])]
        if far:
            pieces.append((s[:, :far], va_ref[:far, :]))
        o_ref[lo:kv, :] = _softmax_pv(pieces).astype(o_ref.dtype)

    _attend_tiles(seq // tq, scores, finish)


def _dsa_attn(qkv, sel, band, batch, seq):
    tq = min(ATTN_TQ, seq)
    width = band.shape[2]
    return pl.pallas_call(
        functools.partial(_dsa_attn_kernel, tq=tq),
        name="dsa_attn",
        grid=(batch, HEADS),
        in_specs=[pl.BlockSpec((seq, HEAD_DIM), lambda b, h: (b, h)),
                  pl.BlockSpec((seq, HEAD_DIM), lambda b, h: (b, HEADS)),
                  pl.BlockSpec((seq, HEAD_DIM), lambda b, h: (b, HEADS + 1)),
                  pl.BlockSpec((seq, seq), lambda b, h: (b, 0)),
                  pl.BlockSpec((None, tq, width), lambda b, h: (h, 0, 0))],
        out_specs=pl.BlockSpec((seq, HEAD_DIM), lambda b, h: (b, h)),
        out_shape=jax.ShapeDtypeStruct((batch * seq, HEADS * HEAD_DIM), BF16),
        scratch_shapes=[pltpu.VMEM((seq, 2 * HEAD_DIM), BF16)],
        compiler_params=_params(("arbitrary", "arbitrary")),
    )(qkv, qkv, qkv, sel, band)


def _dsa_layer(x, x_bf, t5_bias, w_in, j, w_o, ln_g, ln_b, route_wb, batch, seq):
    hd = HEADS * HEAD_DIM
    n_qkv = hd + 2 * HEAD_DIM
    n_qi = IDX_HEADS * IDX_DIM
    scale = HEAD_DIM ** -0.5
    qkv = _matmul(x_bf, w_in, j, n_out=n_qkv, tn=n_qkv // 3, scale=scale, scaled_cols=hd)
    q_idx = _matmul(x_bf, w_in[j:j + 1, :, n_qkv:n_qkv + n_qi], 0, n_out=n_qi, tn=n_qi)
    tail = _small_matmul(x, _pad_cols(w_in[j, :, n_qkv + n_qi:]), jnp.zeros((1, LANES), F32))
    topk = min(IDX_TOPK_MAX, seq // 4)
    sel = _dsa_index(q_idx, tail, batch, seq, topk)
    band = _t5_band(t5_bias, min(ATTN_TQ, seq))
    o = _dsa_attn(qkv, sel, band, batch, seq)
    return _mm_res_ln(o, w_o.astype(BF16), x, ln_g[None, :], ln_b[None, :], *route_wb)


def _dispatch_tables(before, counts, n_tokens, tm):
    n_t = n_tokens // tm
    before = before[:, 0, :MOE_EXPERTS].astype(I32)
    total = counts[0, :MOE_EXPERTS].astype(I32)
    after = jnp.concatenate([before[1:], total[None, :]], axis=0)
    length = after - before
    length_al = (length + ROW_ALIGN - 1) // ROW_ALIGN * ROW_ALIGN
    t_ids = jnp.arange(n_t, dtype=I32)
    e_ids = jnp.arange(MOE_EXPERTS, dtype=I32)
    before_al = jnp.sum(jnp.where((t_ids[None, :] < t_ids[:, None])[:, :, None], length_al[None, :, :], 0), axis=1)
    total_al = jnp.sum(length_al, axis=0)
    seg = (total_al + tm - 1) // tm * tm
    ends = jnp.sum(jnp.where(e_ids[None, :] <= e_ids[:, None], seg[None, :], 0), axis=1)
    starts = ends - seg
    off = jnp.sum(jnp.where((e_ids[None, :] < e_ids[:, None])[None, :, :], length_al[:, None, :], 0), axis=2)
    used = jnp.sum(length_al, axis=1)
    n_chunks = used // ROW_ALIGN
    dstart = starts[None, :] + before_al
    c_row = jnp.arange(TILE_CHUNKS, dtype=I32) * ROW_ALIGN
    c_exp = jnp.sum(((off + length_al)[:, None, :] <= c_row[None, :, None]).astype(I32), axis=2)
    c_exp = jnp.minimum(c_exp, MOE_EXPERTS - 1)
    onehot = c_exp[:, :, None] == e_ids[None, None, :]
    chunk_row = jnp.sum(jnp.where(onehot, (dstart - off)[:, None, :], 0), axis=2) + c_row[None, :]
    pos_tab = off - before
    n_rows = _sorted_rows(n_tokens, tm)
    n_tiles = n_rows // tm
    tile_start = jnp.arange(n_tiles, dtype=I32) * tm
    tile_expert = jnp.minimum(jnp.sum((ends[None, :] <= tile_start[:, None]).astype(I32), axis=1), MOE_EXPERTS - 1)
    tile_valid = (tile_start < ends[-1]).astype(I32)
    tile_first = jnp.concatenate([jnp.ones((1,), I32), (tile_expert[1:] != tile_expert[:-1]).astype(I32)])
    j_ids = jnp.arange(n_tiles, dtype=I32)
    tile_slot = (jnp.sum(jnp.where(j_ids[None, :] <= j_ids[:, None], tile_first[None, :], 0), axis=1) - 1) % 2
    later = (e_ids[None, :] > e_ids[:, None]) & (total[None, :] > 0)
    next_used = jnp.min(jnp.where(later, e_ids[None, :], MOE_EXPERTS), axis=1)
    next_used = jnp.where(next_used == MOE_EXPERTS, -1, next_used).astype(I32)
    tile_next = jnp.sum(jnp.where(tile_expert[:, None] == e_ids[None, :], next_used[None, :], 0), axis=1)
    tail_first = starts + total_al
    tail_chunks = (seg - total_al) // ROW_ALIGN
    return dict(n_chunks=n_chunks.astype(I32), chunk_row=chunk_row.reshape(-1).astype(I32),
                pos_tab=pos_tab.reshape(-1).astype(I32), pos_rows=pos_tab.astype(F32),
                tile_expert=tile_expert.astype(I32), tile_valid=tile_valid, tile_first=tile_first,
                tile_next=tile_next.astype(I32), n_valid=jnp.sum(tile_valid)[None].astype(I32),
                tile_slot=tile_slot.astype(I32),
                tail_first=tail_first.astype(I32),
                tail_chunks=tail_chunks.astype(I32))


def _sorted_rows(n_tokens, tm):
    n_t = n_tokens // tm
    bound = 2 * n_tokens + n_t * MOE_EXPERTS * (ROW_ALIGN - 1) + MOE_EXPERTS * (tm - ROW_ALIGN) + 1
    return (bound + tm - 1) // tm * tm


def _chunk_copy_out(buf, slot, c, row, hbm, sem):
    src = buf.at[slot, pl.ds(pl.multiple_of(c * ROW_ALIGN, ROW_ALIGN), ROW_ALIGN)]
    return pltpu.make_async_copy(src, hbm.at[pl.ds(pl.multiple_of(row, ROW_ALIGN), ROW_ALIGN)], sem.at[slot])


def _chunk_copy_in(hbm, row, buf, slot, c, sem):
    dst = buf.at[slot, pl.ds(pl.multiple_of(c * ROW_ALIGN, ROW_ALIGN), ROW_ALIGN)]
    src = hbm.at[pl.ds(pl.multiple_of(row, ROW_ALIGN), ROW_ALIGN), pl.ds(0, buf.shape[2])]
    return pltpu.make_async_copy(src, dst, sem.at[slot])


def _moe_dispatch_kernel(nch_ref, crow_ref, ptab_ref, tfirst_ref, tchunks_ref, tvalid_ref,
                         x_ref, rt_ref, r_ref, xs_hbm, buf, zeros, sem, sem_z, *, tm, n_tiles):
    i = pl.program_id(0)
    n_t = pl.num_programs(0)
    slot = i % 2

    def wait_tile(tile, s):
        def body(c, carry):
            _chunk_copy_out(buf, s, 0, 0, xs_hbm, sem).wait()
            return carry
        lax.fori_loop(0, nch_ref[tile], body, 0)

    def zero_tail(e, k):
        return pltpu.make_async_copy(
            zeros.at[pl.ds(0, ROW_ALIGN)],
            xs_hbm.at[pl.ds(pl.multiple_of(tfirst_ref[e] + k * ROW_ALIGN, ROW_ALIGN), ROW_ALIGN)], sem_z.at[0])

    def zero_tile(t):
        return pltpu.make_async_copy(zeros, xs_hbm.at[pl.ds(pl.multiple_of(t * tm, tm), tm)], sem_z.at[1])

    def for_each_fill(tail_fn, tile_fn):
        def per_expert(e, carry):
            def per_chunk(k, c2):
                tail_fn(e, k)
                return c2
            lax.fori_loop(0, tchunks_ref[e], per_chunk, 0)
            return carry
        lax.fori_loop(0, MOE_EXPERTS, per_expert, 0)

        def per_tile(t, carry):
            @pl.when(tvalid_ref[t] == 0)
            def _():
                tile_fn(t)
            return carry
        lax.fori_loop(0, n_tiles, per_tile, 0)

    @pl.when(i == 0)
    def _():
        zeros[...] = jnp.zeros(zeros.shape, zeros.dtype)
        for_each_fill(lambda e, k: zero_tail(e, k).start(), lambda t: zero_tile(t).start())

    @pl.when(i >= 2)
    def _():
        wait_tile(i - 2, slot)

    pos1 = rt_ref[4:5, :].astype(I32)
    pos2 = rt_ref[5:6, :].astype(I32)
    e1 = rt_ref[0:1, :].astype(I32)
    e2 = rt_ref[1:2, :].astype(I32)
    for e in range(MOE_EXPERTS):
        shift = ptab_ref[i * MOE_EXPERTS + e]
        pos1 = pos1 + jnp.where(e1 == e, shift, 0)
        pos2 = pos2 + jnp.where(e2 == e, shift, 0)
    lane = lax.broadcasted_iota(I32, (tm, LANES), 1)

    def gate_terms(g):
        hi = g.astype(BF16).astype(F32)
        return jnp.where(lane == 0, hi, jnp.where(lane == 1, g - hi, 0.0)).astype(BF16)

    gate1 = gate_terms(r_ref[:, 2:3])
    gate2 = gate_terms(r_ref[:, 3:4])
    blk = 256
    for k in range(TILE_ROWS // blk):
        row = lax.broadcasted_iota(I32, (blk, tm), 0) + k * blk
        pick1 = jnp.where(row == pos1, 1.0, 0.0).astype(BF16)
        pick2 = jnp.where(row == pos2, 1.0, 0.0).astype(BF16)
        at = slice(k * blk, (k + 1) * blk)
        buf[slot, at, :D_MODEL] = jnp.dot(pick1 + pick2, x_ref[...], preferred_element_type=F32).astype(BF16)
        buf[slot, at, D_MODEL:] = (jnp.dot(pick1, gate1, preferred_element_type=F32)
                                   + jnp.dot(pick2, gate2, preferred_element_type=F32)).astype(BF16)

    def send(c, carry):
        _chunk_copy_out(buf, slot, c, crow_ref[i * TILE_CHUNKS + c], xs_hbm, sem).start()
        return carry
    lax.fori_loop(0, nch_ref[i], send, 0)

    @pl.when(i == n_t - 1)
    def _():
        @pl.when(i >= 1)
        def _():
            wait_tile(i - 1, 1 - slot)
        wait_tile(i, slot)
        for_each_fill(lambda e, k: zero_tail(e, k).wait(), lambda t: zero_tile(t).wait())


def _moe_dispatch(x_bf, route_t, route, tables, tm):
    m, d = x_bf.shape
    n_rows = _sorted_rows(m, tm)
    grid_spec = pltpu.PrefetchScalarGridSpec(
        num_scalar_prefetch=6,
        grid=(m // tm,),
        in_specs=[pl.BlockSpec((tm, d), lambda i, *_: (i, 0)),
                  pl.BlockSpec((8, tm), lambda i, *_: (0, i)),
                  pl.BlockSpec((tm, LANES), lambda i, *_: (i, 0))],
        out_specs=pl.BlockSpec(memory_space=pl.ANY),
        scratch_shapes=[pltpu.VMEM((2, TILE_ROWS, SORTED_WIDTH), BF16), pltpu.VMEM((tm, SORTED_WIDTH), BF16),
                        pltpu.SemaphoreType.DMA((2,)), pltpu.SemaphoreType.DMA((2,))],
    )
    return pl.pallas_call(
        functools.partial(_moe_dispatch_kernel, tm=tm, n_tiles=n_rows // tm),
        name="moe_dispatch",
        grid_spec=grid_spec,
        out_shape=jax.ShapeDtypeStruct((n_rows, SORTED_WIDTH), BF16),
        compiler_params=_params(("arbitrary",)),
    )(tables["n_chunks"], tables["chunk_row"], tables["pos_tab"], tables["tail_first"], tables["tail_chunks"],
      tables["tile_valid"], x_bf, route_t, route)


def _moe_ffn_kernel(te_ref, tv_ref, tf_ref, tn_ref, nv_ref, ts_ref, x_ref, wg_hbm, wu_hbm, wd_hbm, o_ref,
                    wg_f, wu_f, wd_f, wg_bf, wu_bf, wd_bf, sem_w, *, layer):
    j = pl.program_id(0)
    valid = tv_ref[j] == 1
    slot = ts_ref[j]

    def weight_copies(e, s):
        return (pltpu.make_async_copy(wg_hbm.at[layer, e], wg_f.at[s], sem_w.at[s, 0]),
                pltpu.make_async_copy(wu_hbm.at[layer, e], wu_f.at[s], sem_w.at[s, 1]),
                pltpu.make_async_copy(wd_hbm.at[layer, e], wd_f.at[s], sem_w.at[s, 2]))

    @pl.when(j == 0)
    def _():
        for c in weight_copies(te_ref[0], slot):
            c.start()

    @pl.when(valid & (tf_ref[j] == 1))
    def _():
        @pl.when(tn_ref[j] >= 0)
        def _():
            for c in weight_copies(tn_ref[j], 1 - slot):
                c.start()

        for c in weight_copies(te_ref[j], slot):
            c.wait()
        _cast_rows(wg_f.at[slot], wg_bf)
        _cast_rows(wu_f.at[slot], wu_bf)
        _cast_rows(wd_f.at[slot], wd_bf)

    @pl.when(valid)
    def _():
        xb = x_ref[:, :D_MODEL]
        gate = x_ref[:, D_MODEL:D_MODEL + 1].astype(F32) + x_ref[:, D_MODEL + 1:D_MODEL + 2].astype(F32)
        hg = jnp.dot(xb, wg_bf[...], preferred_element_type=F32)
        hu = jnp.dot(xb, wu_bf[...], preferred_element_type=F32)
        act = (hg * (1.0 / (1.0 + jnp.exp(-hg))) * hu).astype(BF16)
        o_ref[...] = (jnp.dot(act, wd_bf[...], preferred_element_type=F32) * gate).astype(o_ref.dtype)


def _moe_ffn(x_sorted, w_gate, w_up, w_down, layer, tables, tm):
    n_rows, width = x_sorted.shape
    d, f = w_gate.shape[2], w_gate.shape[3]
    any_spec = pl.BlockSpec(memory_space=pl.ANY)
    grid_spec = pltpu.PrefetchScalarGridSpec(
        num_scalar_prefetch=6,
        grid=(n_rows // tm,),
        in_specs=[pl.BlockSpec((tm, width), lambda j, te, tv, tf, tn, nv, ts: (jnp.minimum(j, nv[0] - 1), 0)),
                  any_spec, any_spec, any_spec],
        out_specs=pl.BlockSpec((tm, d), lambda j, te, tv, tf, tn, nv, ts: (jnp.minimum(j, nv[0] - 1), 0)),
        scratch_shapes=[pltpu.VMEM((2, d, f), F32), pltpu.VMEM((2, d, f), F32), pltpu.VMEM((2, f, d), F32),
                        pltpu.VMEM((d, f), BF16), pltpu.VMEM((d, f), BF16), pltpu.VMEM((f, d), BF16),
                        pltpu.SemaphoreType.DMA((2, 3))],
    )
    return pl.pallas_call(
        functools.partial(_moe_ffn_kernel, layer=layer),
        name="moe_ffn",
        grid_spec=grid_spec,
        out_shape=jax.ShapeDtypeStruct((n_rows, width), BF16),
        input_output_aliases={6: 0},
        compiler_params=_params(("arbitrary",)),
    )(tables["tile_expert"], tables["tile_valid"], tables["tile_first"], tables["tile_next"], tables["n_valid"],
      tables["tile_slot"], x_sorted, w_gate, w_up, w_down)


def _moe_combine_kernel(nch_ref, crow_ref, ys_hbm, x_ref, r_ref, ptab_ref, g_ref, b_ref, o_ref, obf_ref,
                        buf, sem, *, tm):
    i = pl.program_id(0)
    n_t = pl.num_programs(0)
    slot = i % 2

    def fetch(tile, s):
        def body(c, carry):
            _chunk_copy_in(ys_hbm, crow_ref[tile * TILE_CHUNKS + c], buf, s, c, sem).start()
            return carry
        lax.fori_loop(0, nch_ref[tile], body, 0)

    @pl.when(i == 0)
    def _():
        buf[...] = jnp.zeros(buf.shape, buf.dtype)
        fetch(0, 0)

    @pl.when(i + 1 < n_t)
    def _():
        fetch(i + 1, 1 - slot)

    def arrived(c, carry):
        _chunk_copy_in(ys_hbm, 0, buf, slot, 0, sem).wait()
        return carry
    lax.fori_loop(0, nch_ref[i], arrived, 0)

    lane = lax.broadcasted_iota(I32, (tm, LANES), 1)
    shift = ptab_ref[...]
    e1 = r_ref[:, 0:1].astype(I32)
    e2 = r_ref[:, 1:2].astype(I32)
    pos1 = (r_ref[:, 4:5] + jnp.sum(jnp.where(lane == e1, shift, 0.0), axis=1, keepdims=True)).astype(I32)
    pos2 = (r_ref[:, 5:6] + jnp.sum(jnp.where(lane == e2, shift, 0.0), axis=1, keepdims=True)).astype(I32)
    col = lax.broadcasted_iota(I32, (tm, TILE_ROWS), 1)
    pick = jnp.where((col == pos1) | (col == pos2), 1.0, 0.0).astype(BF16)
    y = jnp.dot(pick, buf[slot], preferred_element_type=F32)
    out = _layernorm_rows(DEEPNORM_ALPHA * x_ref[...] + y, g_ref[...], b_ref[...])
    o_ref[...] = out
    obf_ref[...] = out.astype(BF16)


def _moe_combine(y_sorted, x, route, tables, g, b, tm):
    m, d = x.shape
    grid_spec = pltpu.PrefetchScalarGridSpec(
        num_scalar_prefetch=2,
        grid=(m // tm,),
        in_specs=[pl.BlockSpec(memory_space=pl.ANY),
                  pl.BlockSpec((tm, d), lambda i, *_: (i, 0)),
                  pl.BlockSpec((tm, LANES), lambda i, *_: (i, 0)),
                  pl.BlockSpec((None, 1, LANES), lambda i, *_: (i, 0, 0)),
                  pl.BlockSpec((1, d), lambda i, *_: (0, 0)),
                  pl.BlockSpec((1, d), lambda i, *_: (0, 0))],
        out_specs=[pl.BlockSpec((tm, d), lambda i, *_: (i, 0)),
                   pl.BlockSpec((tm, d), lambda i, *_: (i, 0))],
        scratch_shapes=[pltpu.VMEM((2, TILE_ROWS, d), BF16), pltpu.SemaphoreType.DMA((2,))],
    )
    pos_rows = _pad_cols(tables["pos_rows"])[:, None, :]
    return pl.pallas_call(
        functools.partial(_moe_combine_kernel, tm=tm),
        name="moe_combine",
        grid_spec=grid_spec,
        out_shape=[jax.ShapeDtypeStruct((m, d), F32), jax.ShapeDtypeStruct((m, d), BF16)],
        compiler_params=_params(("arbitrary",)),
    )(tables["n_chunks"], tables["chunk_row"], y_sorted, x, route, pos_rows, g, b)


def _route_weights(w_group, b_group, w_router, b_router):
    return (_pad_cols(jnp.concatenate([w_group, w_router], axis=1)),
            _pad_cols(jnp.concatenate([b_group, b_router])[None, :]))


def _moe_layer(x, x_bf, route, route_t, before, counts, w_gate, w_up, w_down, layer, ln_g, ln_b):
    n_tokens = x.shape[0]
    tm = min(MOE_TM, n_tokens)
    tables = _dispatch_tables(before, counts, n_tokens, tm)
    x_sorted = _moe_dispatch(x_bf, route_t, route, tables, tm)
    y_sorted = _moe_ffn(x_sorted, w_gate, w_up, w_down, layer, tables, tm)
    return _moe_combine(y_sorted, x, route, tables, ln_g[None, :], ln_b[None, :], tm)


def kernel(x, t5_rel_bias, mla_w_in, mla_q_norm, mla_kv_norm, mla_w_q_up, mla_w_kv_up, mla_w_o, fox_w_in, fox_b_f, fox_w_o, dsa_w_in, dsa_w_o, ln_g, ln_b, moe_w_group, moe_b_group, moe_w_router, moe_b_router, moe_w_gate, moe_w_up, moe_w_down):
    batch, seq, d = x.shape
    x = x.reshape(batch * seq, d)
    x_bf = x.astype(BF16)
    for layer in range(DEPTH):
        kind = layer % N_MIXERS
        j = layer // N_MIXERS
        g0, b0 = ln_g[layer, 0], ln_b[layer, 0]
        route_wb = _route_weights(moe_w_group[layer], moe_b_group[layer], moe_w_router[layer], moe_b_router[layer])
        if kind == 0:
            mixed = _mla_layer(x, x_bf, mla_w_in[j], mla_q_norm[j], mla_kv_norm[j], mla_w_q_up[j],
                               mla_w_kv_up, j, mla_w_o[j], g0, b0, route_wb, batch, seq)
        elif kind == 1:
            mixed = _fox_layer(x, x_bf, fox_w_in, j, fox_b_f[j], fox_w_o[j], g0, b0, route_wb, batch, seq)
        else:
            mixed = _dsa_layer(x, x_bf, t5_rel_bias, dsa_w_in, j, dsa_w_o[j], g0, b0, route_wb, batch, seq)
        x, x_bf = _moe_layer(*mixed, moe_w_gate, moe_w_up, moe_w_down, layer, ln_g[layer, 1], ln_b[layer, 1])
    return x.reshape(batch, seq, d)
```

```python
import functools
import math

import numpy as np
import jax
import jax.numpy as jnp
from jax import lax
from jax.experimental import pallas as pl
from jax.experimental.pallas import tpu as pltpu

F32 = jnp.float32
BF16 = jnp.bfloat16
I32 = jnp.int32

D_MODEL = 2048
DEPTH = 4
CHUNK = 64
N_MIXERS = 3
DEEPNORM_ALPHA = (2.0 * DEPTH) ** 0.25

HEADS = 16
HEAD_DIM = 128
MLA_NOPE_DIM = 128
MLA_ROPE_DIM = 64
MLA_Q_RANK = 512
MLA_KV_RANK = 512
MLA_QK_PAD = 256
ROPE_THETA = 10000.0

IDX_HEADS = 16
IDX_DIM = 64
IDX_TOPK_MAX = 256

T5_BUCKETS = 32
T5_MAX_DISTANCE = 128
T5_FAR_BUCKET = T5_BUCKETS // 2 - 1
T5_BAND_BACK = 128

MOE_GROUPS = 4
MOE_EXPERTS_PER_GROUP = 8
MOE_EXPERTS = MOE_GROUPS * MOE_EXPERTS_PER_GROUP
MOE_D_FF = 512

LN_EPS = 1e-5
RMS_EPS = 1e-6
NEG_INF = -1e30

LANES = 128
VMEM_LIMIT = 56 * 1024 * 1024

ATTN_TQ = 256
MOE_TM = 256
ROW_ALIGN = 16
TILE_ROWS = -(-(2 * MOE_TM + MOE_EXPERTS * (ROW_ALIGN - 1)) // 256) * 256
TILE_CHUNKS = TILE_ROWS // ROW_ALIGN
SORTED_WIDTH = D_MODEL + LANES


def _params(semantics, vmem=VMEM_LIMIT):
    return pltpu.CompilerParams(dimension_semantics=semantics, vmem_limit_bytes=vmem)


def _cast_rows(src_ref, dst_ref):
    rows = 16
    while 2 * rows * src_ref.shape[1] <= 32 * 8 * LANES and src_ref.shape[0] % (2 * rows) == 0:
        rows *= 2

    def body(i, carry):
        at = pl.ds(pl.multiple_of(i * rows, rows), rows)
        dst_ref[at, :] = src_ref[at, :].astype(dst_ref.dtype)
        return carry
    lax.fori_loop(0, src_ref.shape[0] // rows, body, 0, unroll=2)


def _mm_kernel(a_ref, w_ref, o_ref, w_bf, *, scale, scaled_cols):
    @pl.when(pl.program_id(1) == 0)
    def _():
        _cast_rows(w_ref, w_bf)

    acc = jnp.dot(a_ref[...], w_bf[...], preferred_element_type=F32)
    if scaled_cols:
        tn = o_ref.shape[1]
        col = pl.program_id(0) * tn + lax.broadcasted_iota(I32, (1, tn), 1)
        acc = acc * jnp.where(col < scaled_cols, scale, 1.0)
    o_ref[...] = acc.astype(o_ref.dtype)


def _matmul(a, w, layer, *, n_out, tn, tm=1024, scale=1.0, scaled_cols=0, out_dtype=BF16):
    m, k = a.shape
    tm = min(tm, m)
    kern = functools.partial(_mm_kernel, scale=scale, scaled_cols=scaled_cols)
    return pl.pallas_call(
        kern,
        name="proj_mm",
        grid=(n_out // tn, m // tm),
        in_specs=[pl.BlockSpec((tm, k), lambda j, i: (i, 0)),
                  pl.BlockSpec((None, k, tn), lambda j, i: (layer, 0, j))],
        out_specs=pl.BlockSpec((tm, tn), lambda j, i: (i, j)),
        out_shape=jax.ShapeDtypeStruct((m, n_out), out_dtype),
        scratch_shapes=[pltpu.VMEM((k, tn), BF16)],
        compiler_params=_params(("arbitrary", "arbitrary")),
    )(a, w)


def _split3(v):
    hi = v.astype(BF16)
    r1 = v - hi.astype(F32)
    mid = r1.astype(BF16)
    lo = (r1 - mid.astype(F32)).astype(BF16)
    return hi, mid, lo


def _dot_precise(x, w):
    xh = x.astype(BF16)
    xm = (x - xh.astype(F32)).astype(BF16)
    wh = w.astype(BF16)
    wm = (w - wh.astype(F32)).astype(BF16)
    d = lambda p, q: jnp.dot(p, q, preferred_element_type=F32)
    return (d(xm, wh) + d(xh, wm)) + d(xh, wh)


def _small_mm_kernel(x_ref, w_ref, b_ref, o_ref):
    o_ref[...] = _dot_precise(x_ref[...], w_ref[...]) + b_ref[...]


def _small_matmul(x, w, b, tm=512):
    m, k = x.shape
    tm = min(tm, m)
    return pl.pallas_call(
        _small_mm_kernel,
        name="small_mm",
        grid=(m // tm,),
        in_specs=[pl.BlockSpec((tm, k), lambda i: (i, 0)),
                  pl.BlockSpec((k, LANES), lambda i: (0, 0)),
                  pl.BlockSpec((1, LANES), lambda i: (0, 0))],
        out_specs=pl.BlockSpec((tm, LANES), lambda i: (i, 0)),
        out_shape=jax.ShapeDtypeStruct((m, LANES), F32),
        compiler_params=_params(("arbitrary",)),
    )(x, w, b)


def _rows_spec(arr, tm):
    d = arr.shape[-1]
    if arr.ndim == 2:
        return pl.BlockSpec((tm, d), lambda i: (i, 0))
    per = arr.shape[1] // tm
    return pl.BlockSpec((None, tm, d), lambda i: (i // per, i % per, 0))


def _pad_cols(w, width=LANES):
    return jnp.pad(w, ((0, 0), (0, width - w.shape[1])))


def _layernorm_rows(z, g, b):
    mu = jnp.mean(z, axis=-1, keepdims=True)
    zc = z - mu
    var = jnp.mean(zc * zc, axis=-1, keepdims=True)
    return zc * lax.rsqrt(var + LN_EPS) * g + b


def _route_tile(x, w_hi, w_mid, b, run_ref):
    tm = x.shape[0]
    x_hi = x.astype(BF16)
    x_mid = (x - x_hi.astype(F32)).astype(BF16)
    d = lambda p, q: jnp.dot(p, q, preferred_element_type=F32)
    logit = (d(x_mid, w_hi) + d(x_hi, w_mid)) + d(x_hi, w_hi) + b
    lane = lax.broadcasted_iota(I32, logit.shape, 1)
    big = jnp.int32(LANES)

    def first_argmax(mask):
        top = jnp.max(jnp.where(mask, logit, -jnp.inf), axis=1, keepdims=True)
        idx = jnp.min(jnp.where(mask & (logit == top), lane, big), axis=1, keepdims=True)
        return top, idx

    is_group = lane < MOE_GROUPS
    g_top, g_sel = first_argmax(is_group)
    p_g = 1.0 / jnp.sum(jnp.where(is_group, jnp.exp(logit - g_top), 0.0), axis=1, keepdims=True)
    lo = MOE_GROUPS + g_sel * MOE_EXPERTS_PER_GROUP
    in_group = (lane >= lo) & (lane < lo + MOE_EXPERTS_PER_GROUP)
    v1, i1 = first_argmax(in_group)
    v2, i2 = first_argmax(in_group & (lane != i1))
    ex = jnp.exp(v2 - v1)
    w1 = 1.0 / (1.0 + ex)
    w2 = ex / (1.0 + ex)
    e1 = i1 - MOE_GROUPS
    e2 = i2 - MOE_GROUPS
    pick1 = lane == e1
    pick2 = lane == e2
    both = jnp.where(pick1 | pick2, 1.0, 0.0)
    earlier = lax.broadcasted_iota(I32, (tm, tm), 1) < lax.broadcasted_iota(I32, (tm, tm), 0)
    start = run_ref[...]
    before = jnp.dot(jnp.where(earlier, 1.0, 0.0).astype(BF16), both.astype(BF16),
                     preferred_element_type=F32) + start
    r1 = jnp.sum(jnp.where(pick1, before, 0.0), axis=1, keepdims=True)
    r2 = jnp.sum(jnp.where(pick2, before, 0.0), axis=1, keepdims=True)
    run_ref[...] = start + jnp.sum(both, axis=0, keepdims=True)
    out = jnp.where(lane == 0, e1.astype(F32), 0.0)
    out = jnp.where(lane == 1, e2.astype(F32), out)
    out = jnp.where(lane == 2, w1 * p_g, out)
    out = jnp.where(lane == 3, w2 * p_g, out)
    out = jnp.where(lane == 4, r1, out)
    out = jnp.where(lane == 5, r2, out)
    return out, start


def _mm_res_ln_kernel(a_ref, w_ref, x_ref, g_ref, b_ref, wrh_ref, wrm_ref, br_ref,
                      o_ref, obf_ref, r_ref, rt_ref, before_ref, cnt_ref, run_ref, *, sub):
    @pl.when(pl.program_id(0) == 0)
    def _():
        run_ref[...] = jnp.zeros(run_ref.shape, F32)

    y = jnp.dot(a_ref[...], w_ref[...], preferred_element_type=F32)
    out = _layernorm_rows(DEEPNORM_ALPHA * x_ref[...] + y, g_ref[...], b_ref[...])
    o_ref[...] = out
    obf_ref[...] = out.astype(BF16)
    for k in range(out.shape[0] // sub):
        rows = slice(k * sub, (k + 1) * sub)
        slab, before = _route_tile(out[rows, :], wrh_ref[...], wrm_ref[...], br_ref[...], run_ref)
        r_ref[rows, :] = slab
        rt_ref[:, rows] = slab.T[:8, :]
        before_ref[k] = before
    cnt_ref[...] = run_ref[...]


def _mm_res_ln(a, w, x, g, b, w_route_hi, w_route_mid, b_route, tm=512):
    m, k = a.shape
    d = w.shape[1]
    tm = min(tm, m)
    sub = min(MOE_TM, tm)
    per = tm // sub
    return pl.pallas_call(
        functools.partial(_mm_res_ln_kernel, sub=sub),
        name="out_proj_res_ln",
        grid=(m // tm,),
        in_specs=[pl.BlockSpec((tm, k), lambda i: (i, 0)),
                  pl.BlockSpec((k, d), lambda i: (0, 0)),
                  _rows_spec(x, tm),
                  pl.BlockSpec((1, d), lambda i: (0, 0)),
                  pl.BlockSpec((1, d), lambda i: (0, 0)),
                  pl.BlockSpec((d, LANES), lambda i: (0, 0)),
                  pl.BlockSpec((d, LANES), lambda i: (0, 0)),
                  pl.BlockSpec((1, LANES), lambda i: (0, 0))],
        out_specs=[pl.BlockSpec((tm, d), lambda i: (i, 0)),
                   pl.BlockSpec((tm, d), lambda i: (i, 0)),
                   pl.BlockSpec((tm, LANES), lambda i: (i, 0)),
                   pl.BlockSpec((8, tm), lambda i: (0, i)),
                   pl.BlockSpec((per, 1, LANES), lambda i: (i, 0, 0)),
                   pl.BlockSpec((1, LANES), lambda i: (0, 0))],
        out_shape=[jax.ShapeDtypeStruct((m, d), F32), jax.ShapeDtypeStruct((m, d), BF16),
                   jax.ShapeDtypeStruct((m, LANES), F32), jax.ShapeDtypeStruct((8, m), F32),
                   jax.ShapeDtypeStruct((m // sub, 1, LANES), F32), jax.ShapeDtypeStruct((1, LANES), F32)],
        scratch_shapes=[pltpu.VMEM((1, LANES), F32)],
        compiler_params=_params(("arbitrary",)),
    )(a, w, x, g, b, w_route_hi, w_route_mid, b_route)


def _rope_tables(seq):
    half = MLA_ROPE_DIM // 2
    inv_freq = ROPE_THETA ** (-jnp.arange(half, dtype=F32) / half)
    ang = jnp.arange(seq, dtype=I32).astype(F32)[:, None] * inv_freq[None, :]
    cos, sin = jnp.cos(ang), jnp.sin(ang)
    z = jnp.zeros_like(cos)
    cos_t = jnp.concatenate([cos, cos, z, z], axis=1)
    sin_a = jnp.concatenate([-sin, z, z, z], axis=1)
    sin_b = jnp.concatenate([z, sin, z, z], axis=1)
    return cos_t, sin_a, sin_b


def _rope_lanes(r, cos_t, sin_a, sin_b):
    from_right = pltpu.roll(r, LANES - MLA_ROPE_DIM // 2, 1)
    from_left = pltpu.roll(r, MLA_ROPE_DIM // 2, 1)
    return r * cos_t + from_right * sin_a + from_left * sin_b


def _rms_rows(v, g):
    return v * lax.rsqrt(jnp.mean(v * v, axis=-1, keepdims=True) + RMS_EPS) * g


def _mla_in_kernel(x_ref, w_ref, qn_ref, kvn_ref, cos_ref, sina_ref, sinb_ref, cq_ref, ckv_ref, kr_ref):
    h = jnp.dot(x_ref[...].astype(BF16), w_ref[...], preferred_element_type=F32)
    cq_ref[...] = _rms_rows(h[:, :MLA_Q_RANK], qn_ref[...]).astype(BF16)
    ckv_ref[...] = _rms_rows(h[:, MLA_Q_RANK:MLA_Q_RANK + MLA_KV_RANK], kvn_ref[...]).astype(BF16)
    r = h[:, MLA_Q_RANK + MLA_KV_RANK:]
    kr_ref[...] = _rope_lanes(r, cos_ref[...], sina_ref[...], sinb_ref[...]).astype(BF16)


def _mla_in(x_bf, w_in_pad, q_norm, kv_norm, tables, seq, tm=512):
    m, k = math.prod(x_bf.shape[:-1]), x_bf.shape[-1]
    tm = min(tm, seq)
    nw = w_in_pad.shape[1]
    per_seq = seq // tm
    tab_spec = pl.BlockSpec((tm, LANES), lambda i: (i % per_seq, 0))
    return pl.pallas_call(
        _mla_in_kernel,
        name="mla_in",
        grid=(m // tm,),
        in_specs=[_rows_spec(x_bf, tm),
                  pl.BlockSpec((k, nw), lambda i: (0, 0)),
                  pl.BlockSpec((1, MLA_Q_RANK), lambda i: (0, 0)),
                  pl.BlockSpec((1, MLA_KV_RANK), lambda i: (0, 0)),
                  tab_spec, tab_spec, tab_spec],
        out_specs=[pl.BlockSpec((tm, MLA_Q_RANK), lambda i: (i, 0)),
                   pl.BlockSpec((tm, MLA_KV_RANK), lambda i: (i, 0)),
                   pl.BlockSpec((tm, LANES), lambda i: (i, 0))],
        out_shape=[jax.ShapeDtypeStruct((m, MLA_Q_RANK), BF16),
                   jax.ShapeDtypeStruct((m, MLA_KV_RANK), BF16),
                   jax.ShapeDtypeStruct((m, LANES), BF16)],
        compiler_params=_params(("arbitrary",)),
    )(x_bf, w_in_pad, q_norm, kv_norm, *tables)


def _fill_v_aug(va_ref, v_ref):
    va_ref[:, :HEAD_DIM] = v_ref[...]
    va_ref[:, HEAD_DIM:] = jnp.ones((v_ref.shape[0], HEAD_DIM), BF16)


def _softmax_pv(pieces):
    m = None
    for s, _ in pieces:
        pm = jnp.max(s, axis=1, keepdims=True)
        m = pm if m is None else jnp.maximum(m, pm)
    acc = None
    for s, v in pieces:
        t = jnp.dot(jnp.exp((s - m).astype(BF16)), v, preferred_element_type=F32)
        acc = t if acc is None else acc + t
    return acc[:, :HEAD_DIM] / acc[:, HEAD_DIM:HEAD_DIM + 1]


def _nt_dot(q, k):
    return lax.dot_general(q, k, (((1,), (1,)), ((), ())), preferred_element_type=F32)


def _attend_tiles(n_tiles, scores, finish):
    s_next = scores(0)
    for i in range(n_tiles):
        s = s_next
        if i + 1 < n_tiles:
            s_next = scores(i + 1)
        finish(i, s)


def _mla_q_up_kernel(c_ref, w_ref, cos_ref, sina_ref, sinb_ref, o_ref, w_bf, *, scale):
    @pl.when(pl.program_id(1) == 0)
    def _():
        _cast_rows(w_ref, w_bf)

    acc = jnp.dot(c_ref[...], w_bf[...], preferred_element_type=F32) * scale
    for h in range(o_ref.shape[1] // MLA_QK_PAD):
        at = h * MLA_QK_PAD
        o_ref[:, at:at + MLA_NOPE_DIM] = acc[:, at:at + MLA_NOPE_DIM].astype(BF16)
        r = acc[:, at + MLA_NOPE_DIM:at + MLA_QK_PAD]
        o_ref[:, at + MLA_NOPE_DIM:at + MLA_QK_PAD] = _rope_lanes(
            r, cos_ref[...], sina_ref[...], sinb_ref[...]).astype(BF16)


def _mla_kv_up_kernel(c_ref, w_ref, kr_ref, k_ref, v_ref, w_bf):
    @pl.when(pl.program_id(1) == 0)
    def _():
        _cast_rows(w_ref, w_bf)

    acc = jnp.dot(c_ref[...], w_bf[...], preferred_element_type=F32)
    ones = jnp.ones((acc.shape[0], HEAD_DIM), BF16)
    for h in range(k_ref.shape[1] // MLA_QK_PAD):
        at = h * MLA_QK_PAD
        k_ref[:, at:at + MLA_NOPE_DIM] = acc[:, at:at + MLA_NOPE_DIM].astype(BF16)
        k_ref[:, at + MLA_NOPE_DIM:at + MLA_QK_PAD] = kr_ref[...]
        v_ref[:, at:at + HEAD_DIM] = acc[:, at + MLA_NOPE_DIM:at + MLA_QK_PAD].astype(BF16)
        v_ref[:, at + HEAD_DIM:at + 2 * HEAD_DIM] = ones


def _mla_up(cq, ckv, kr, w_q_up, w_kv_up, j, tables, seq, scale, tm=1024, tn=1024):
    m, rank = cq.shape
    tm = min(tm, seq)
    per_seq = seq // tm
    n = HEADS * MLA_QK_PAD
    tab_spec = pl.BlockSpec((tm, LANES), lambda j, i: (i % per_seq, 0))
    row_spec = pl.BlockSpec((tm, rank), lambda j, i: (i, 0))
    w_spec = pl.BlockSpec((rank, tn), lambda j, i: (0, j))
    out_spec = pl.BlockSpec((tm, tn), lambda j, i: (i, j))
    out = jax.ShapeDtypeStruct((m, n), BF16)
    q = pl.pallas_call(
        functools.partial(_mla_q_up_kernel, scale=scale),
        name="mla_q_up",
        grid=(n // tn, m // tm),
        in_specs=[row_spec, w_spec, tab_spec, tab_spec, tab_spec],
        out_specs=out_spec,
        out_shape=out,
        scratch_shapes=[pltpu.VMEM((rank, tn), BF16)],
        compiler_params=_params(("arbitrary", "arbitrary")),
    )(cq, w_q_up, *tables)
    k, v = pl.pallas_call(
        _mla_kv_up_kernel,
        name="mla_kv_up",
        grid=(n // tn, m // tm),
        in_specs=[row_spec, pl.BlockSpec((None, rank, tn), lambda c, i: (j, 0, c)),
                  pl.BlockSpec((tm, LANES), lambda c, i: (i, 0))],
        out_specs=[out_spec, out_spec],
        out_shape=[out, out],
        scratch_shapes=[pltpu.VMEM((rank, tn), BF16)],
        compiler_params=_params(("arbitrary", "arbitrary")),
    )(ckv, w_kv_up, kr)
    return q, k, v


def _mla_attn_kernel(q_ref, k_ref, v_ref, o_ref, *, tq):
    seq = q_ref.shape[0]
    row = lax.broadcasted_iota(I32, (tq, tq), 0)
    col = lax.broadcasted_iota(I32, (tq, tq), 1)
    visible = (col // CHUNK) <= (row // CHUNK)

    def scores(i):
        return _nt_dot(q_ref[i * tq:(i + 1) * tq, :], k_ref[:(i + 1) * tq, :])

    def finish(i, s):
        lo, kv = i * tq, (i + 1) * tq
        pieces = [(jnp.where(visible, s[:, lo:], NEG_INF), v_ref[lo:kv, :])]
        if lo:
            pieces.append((s[:, :lo], v_ref[:lo, :]))
        o_ref[lo:kv, :] = _softmax_pv(pieces).astype(o_ref.dtype)

    _attend_tiles(seq // tq, scores, finish)


def _mla_attn(q, k, v, batch, seq):
    tq = min(ATTN_TQ, seq)
    head_spec = pl.BlockSpec((seq, MLA_QK_PAD), lambda b, h: (b, h))
    return pl.pallas_call(
        functools.partial(_mla_attn_kernel, tq=tq),
        name="mla_attn",
        grid=(batch, HEADS),
        in_specs=[head_spec, head_spec, head_spec],
        out_specs=pl.BlockSpec((seq, HEAD_DIM), lambda b, h: (b, h)),
        out_shape=jax.ShapeDtypeStruct((batch * seq, HEADS * HEAD_DIM), BF16),
        compiler_params=_params(("arbitrary", "arbitrary")),
    )(q, k, v)


def _mla_q_up_layout(w_q_up):
    rank = w_q_up.shape[0]
    w = w_q_up.reshape(rank, HEADS, MLA_NOPE_DIM + MLA_ROPE_DIM)
    w = jnp.pad(w, ((0, 0), (0, 0), (0, MLA_QK_PAD - MLA_NOPE_DIM - MLA_ROPE_DIM)))
    return w.reshape(rank, HEADS * MLA_QK_PAD)


def _mla_layer(x, x_bf, w_in, q_norm, kv_norm, w_q_up, w_kv_up, j, w_o, ln_g, ln_b, route_wb, batch, seq):
    tables = _rope_tables(seq)
    w_in_pad = jnp.pad(w_in, ((0, 0), (0, MLA_Q_RANK + MLA_KV_RANK + LANES - w_in.shape[1]))).astype(BF16)
    cq, ckv, kr = _mla_in(x_bf, w_in_pad, q_norm[None, :], kv_norm[None, :], tables, seq)
    scale = (MLA_NOPE_DIM + MLA_ROPE_DIM) ** -0.5
    q, k, v = _mla_up(cq, ckv, kr, _mla_q_up_layout(w_q_up), w_kv_up, j, tables, seq, scale)
    o = _mla_attn(q, k, v, batch, seq)
    return _mm_res_ln(o, w_o.astype(BF16), x, ln_g[None, :], ln_b[None, :], *route_wb)


def _log_sigmoid(z):
    return jnp.minimum(z, 0.0) - jnp.log1p(jnp.exp(-jnp.abs(z)))


def _fox_gate_kernel(f_ref, c_ref, *, blk):
    seq = f_ref.shape[0]
    tri = (lax.broadcasted_iota(I32, (blk, blk), 1) <= lax.broadcasted_iota(I32, (blk, blk), 0)).astype(BF16)
    carry = jnp.zeros((1, LANES), F32)
    for j in range(seq // blk):
        lf = _log_sigmoid(f_ref[j * blk:(j + 1) * blk, :])
        hi, mid, lo = _split3(lf)
        d = lambda p: jnp.dot(tri, p, preferred_element_type=F32)
        c = (d(lo) + d(mid)) + d(hi) + carry
        c_ref[j * blk:(j + 1) * blk, :] = c
        carry = c[blk - 1:blk, :]


def _fox_gate(f_logit, batch, seq):
    blk = min(256, seq)
    return pl.pallas_call(
        functools.partial(_fox_gate_kernel, blk=blk),
        name="fox_gate",
        grid=(batch,),
        in_specs=[pl.BlockSpec((seq, LANES), lambda b: (b, 0))],
        out_specs=pl.BlockSpec((seq, LANES), lambda b: (b, 0)),
        out_shape=jax.ShapeDtypeStruct((batch * seq, LANES), F32),
        compiler_params=_params(("arbitrary",)),
    )(f_logit)


def _fox_attn_kernel(q_ref, k_ref, v_ref, c_ref, o_ref, qs_ref, ks_ref, *, tq):
    seq = q_ref.shape[0]
    h = pl.program_id(1)
    lane = lax.broadcasted_iota(I32, (seq, LANES), 1)
    c = jnp.sum(jnp.where(lane == h, c_ref[...], 0.0), axis=1, keepdims=True)
    terms = [t.astype(F32) for t in _split3(c)]
    q_side = jnp.where((lane >= 3) & (lane < 6), 1.0, 0.0)
    k_side = jnp.where(lane < 3, 1.0, 0.0)
    for j, t in enumerate(terms):
        q_side = jnp.where(lane == j, t, q_side)
        k_side = jnp.where(lane == 3 + j, -t, k_side)
    qs_ref[...] = q_side.astype(BF16)
    ks_ref[...] = k_side.astype(BF16)
    row = lax.broadcasted_iota(I32, (tq, tq), 0)
    col = lax.broadcasted_iota(I32, (tq, tq), 1)

    def wide(ref, side_ref, rows):
        return jnp.concatenate([ref[rows, :], side_ref[rows, :]], axis=1)

    def values(rows):
        v = v_ref[rows, :]
        return jnp.concatenate([v, jnp.ones(v.shape, BF16)], axis=1)

    def scores(i):
        return _nt_dot(wide(q_ref, qs_ref, slice(i * tq, (i + 1) * tq)), wide(k_ref, ks_ref, slice(0, (i + 1) * tq)))

    def finish(i, s):
        lo, kv = i * tq, (i + 1) * tq
        pieces = [(jnp.where(col <= row, s[:, lo:], NEG_INF), values(slice(lo, kv)))]
        if lo:
            pieces.append((s[:, :lo], values(slice(0, lo))))
        o_ref[lo:kv, :] = _softmax_pv(pieces).astype(o_ref.dtype)

    _attend_tiles(seq // tq, scores, finish)


def _fox_attn(qkv, c_col, batch, seq):
    tq = min(ATTN_TQ, seq)
    side = pltpu.VMEM((seq, LANES), BF16)
    return pl.pallas_call(
        functools.partial(_fox_attn_kernel, tq=tq),
        name="fox_attn",
        grid=(batch, HEADS),
        in_specs=[pl.BlockSpec((seq, HEAD_DIM), lambda b, h: (b, h)),
                  pl.BlockSpec((seq, HEAD_DIM), lambda b, h: (b, HEADS + h)),
                  pl.BlockSpec((seq, HEAD_DIM), lambda b, h: (b, 2 * HEADS + h)),
                  pl.BlockSpec((seq, LANES), lambda b, h: (b, 0))],
        out_specs=pl.BlockSpec((seq, HEAD_DIM), lambda b, h: (b, h)),
        out_shape=jax.ShapeDtypeStruct((batch * seq, HEADS * HEAD_DIM), BF16),
        scratch_shapes=[side, side],
        compiler_params=_params(("arbitrary", "arbitrary")),
    )(qkv, qkv, qkv, c_col)


def _fox_layer(x, x_bf, w_in, j, b_f, w_o, ln_g, ln_b, route_wb, batch, seq):
    hd = HEADS * HEAD_DIM
    scale = HEAD_DIM ** -0.5
    qkv = _matmul(x_bf, w_in, j, n_out=3 * hd, tn=1024, scale=scale, scaled_cols=hd)
    f_logit = _small_matmul(x, _pad_cols(w_in[j, :, 3 * hd:]), _pad_cols(b_f[None, :]))
    o = _fox_attn(qkv, _fox_gate(f_logit, batch, seq), batch, seq)
    return _mm_res_ln(o, w_o.astype(BF16), x, ln_g[None, :], ln_b[None, :], *route_wb)


def _t5_bucket_table(tq):
    a = np.arange(tq, dtype=np.int64)[:, None]
    b = np.arange(tq + T5_BAND_BACK, dtype=np.int64)[None, :]
    rel = (b - T5_BAND_BACK) - a
    nb = T5_BUCKETS // 2
    max_exact = nb // 2
    ret = np.where(rel > 0, nb, 0)
    n = np.abs(rel)
    nf = np.maximum(n, 1).astype(np.float32)
    large = max_exact + (np.log(nf / np.float32(max_exact)) / np.float32(math.log(T5_MAX_DISTANCE / max_exact))
                         * np.float32(nb - max_exact)).astype(np.int32)
    large = np.minimum(large, nb - 1)
    return (ret + np.where(n < max_exact, n, large)).astype(np.int32)


def _t5_band_kernel(t5_ref, bucket_ref, o_ref):
    bucket = bucket_ref[...]

    def per_head(h, carry):
        acc = jnp.zeros(bucket.shape, F32)
        for b in range(T5_BUCKETS):
            acc = jnp.where(bucket == b, t5_ref[b, h], acc)
        o_ref[h] = acc - t5_ref[T5_FAR_BUCKET, h]
        return carry

    lax.fori_loop(0, HEADS, per_head, 0)


def _t5_band(t5_bias, tq):
    bucket = jnp.asarray(_t5_bucket_table(tq))
    width = tq + T5_BAND_BACK
    return pl.pallas_call(
        _t5_band_kernel,
        name="t5_band",
        grid=(1,),
        in_specs=[pl.BlockSpec(memory_space=pltpu.SMEM),
                  pl.BlockSpec((tq, width), lambda i: (0, 0))],
        out_specs=pl.BlockSpec((HEADS, tq, width), lambda i: (0, 0, 0)),
        out_shape=jax.ShapeDtypeStruct((HEADS, tq, width), F32),
        compiler_params=_params(("arbitrary",)),
    )(t5_bias, bucket)


def _sortable_key(v):
    bits = pltpu.bitcast(v, I32)
    return bits ^ ((bits >> 31) & 0x7FFFFFFF)


def _kth_largest_key(key_ref, kv, k):
    rows = key_ref.shape[0]

    def count_ge(t):
        t_b = jnp.broadcast_to(t, (rows, LANES))
        acc = jnp.zeros((rows, LANES), I32)
        for c in range(kv // LANES):
            acc = acc + jnp.where(key_ref[:, c * LANES:(c + 1) * LANES] >= t_b, 1, 0)
        return jnp.sum(acc, axis=1, keepdims=True)

    t0 = jnp.where(count_ge(jnp.zeros((rows, 1), I32)) >= k, 0, jnp.iinfo(jnp.int32).min).astype(I32)

    def body(it, t):
        cand = t | (jnp.int32(1) << (30 - it))
        return jnp.where(count_ge(cand) >= k, cand, t)

    return lax.fori_loop(0, 31, body, t0)


def _dsa_index_kernel(qi_ref, tail_ref, o_ref, key_ref, *, tq, topk):
    seq = tail_ref.shape[0]
    k_idx = tail_ref[:, :IDX_DIM].astype(BF16)
    for i in range(seq // tq):
        kv = (i + 1) * tq
        rows = slice(i * tq, kv)
        w = tail_ref[rows, IDX_DIM:IDX_DIM + IDX_HEADS] * ((IDX_HEADS * IDX_DIM) ** -0.5)
        score = jnp.zeros((tq, kv), F32)
        for h in range(IDX_HEADS):
            logits = _nt_dot(qi_ref[rows, h * IDX_DIM:(h + 1) * IDX_DIM], k_idx[:kv, :])
            score = score + jnp.maximum(logits, 0.0) * w[:, h:h + 1]
        row = lax.broadcasted_iota(I32, (tq, kv), 0) + i * tq
        col = lax.broadcasted_iota(I32, (tq, kv), 1)
        admissible = (col // CHUNK) <= (row // CHUNK)
        if kv <= topk:
            keep = admissible
        else:
            key_ref[:, :kv] = _sortable_key(jnp.where(admissible, score, NEG_INF))
            thr = _kth_largest_key(key_ref, kv, topk)
            keep = admissible & (key_ref[:, :kv] >= thr)
        o_ref[rows, :kv] = jnp.where(keep, 0.0, NEG_INF).astype(o_ref.dtype)
        if kv < seq:
            o_ref[rows, kv:] = jnp.full((tq, seq - kv), NEG_INF, o_ref.dtype)


def _dsa_index(q_idx, tail, batch, seq, topk):
    tq = min(ATTN_TQ, seq)
    return pl.pallas_call(
        functools.partial(_dsa_index_kernel, tq=tq, topk=topk),
        name="dsa_index",
        grid=(batch,),
        in_specs=[pl.BlockSpec((seq, IDX_HEADS * IDX_DIM), lambda b: (b, 0)),
                  pl.BlockSpec((seq, LANES), lambda b: (b, 0))],
        out_specs=pl.BlockSpec((seq, seq), lambda b: (b, 0)),
        out_shape=jax.ShapeDtypeStruct((batch * seq, seq), BF16),
        scratch_shapes=[pltpu.VMEM((tq, seq), I32)],
        compiler_params=_params(("arbitrary",)),
    )(q_idx, tail)


def _dsa_attn_kernel(q_ref, k_ref, v_ref, sel_ref, band_ref, o_ref, va_ref, *, tq):
    seq = k_ref.shape[0]
    width = band_ref.shape[1]
    _fill_v_aug(va_ref, v_ref)
    def scores(i):
        return _nt_dot(q_ref[i * tq:(i + 1) * tq, :], k_ref[:(i + 1) * tq, :])

    def finish(i, s):
        lo, kv = i * tq, (i + 1) * tq
        near = min(width, kv)
        far = kv - near
        s = s + sel_ref[lo:kv, :kv].astype(F32)
        pieces = [(s[:, far:] + band_ref[:, width - near:], va_ref[far:kv, :])]
        if far:
            pieces.append((s[:, :far], va_ref[:far, :]))
        o_ref[lo:kv, :] = _softmax_pv(pieces).astype(o_ref.dtype)

    _attend_tiles(seq // tq, scores, finish)


def _dsa_attn(qkv, sel, band, batch, seq):
    tq = min(ATTN_TQ, seq)
    width = band.shape[2]
    return pl.pallas_call(
        functools.partial(_dsa_attn_kernel, tq=tq),
        name="dsa_attn",
        grid=(batch, HEADS),
        in_specs=[pl.BlockSpec((seq, HEAD_DIM), lambda b, h: (b, h)),
                  pl.BlockSpec((seq, HEAD_DIM), lambda b, h: (b, HEADS)),
                  pl.BlockSpec((seq, HEAD_DIM), lambda b, h: (b, HEADS + 1)),
                  pl.BlockSpec((seq, seq), lambda b, h: (b, 0)),
                  pl.BlockSpec((None, tq, width), lambda b, h: (h, 0, 0))],
        out_specs=pl.BlockSpec((seq, HEAD_DIM), lambda b, h: (b, h)),
        out_shape=jax.ShapeDtypeStruct((batch * seq, HEADS * HEAD_DIM), BF16),
        scratch_shapes=[pltpu.VMEM((seq, 2 * HEAD_DIM), BF16)],
        compiler_params=_params(("arbitrary", "arbitrary")),
    )(qkv, qkv, qkv, sel, band)


def _dsa_layer(x, x_bf, t5_bias, w_in, j, w_o, ln_g, ln_b, route_wb, batch, seq):
    hd = HEADS * HEAD_DIM
    n_qkv = hd + 2 * HEAD_DIM
    n_qi = IDX_HEADS * IDX_DIM
    scale = HEAD_DIM ** -0.5
    qkv = _matmul(x_bf, w_in, j, n_out=n_qkv, tn=n_qkv // 3, scale=scale, scaled_cols=hd)
    q_idx = _matmul(x_bf, w_in[j:j + 1, :, n_qkv:n_qkv + n_qi], 0, n_out=n_qi, tn=n_qi)
    tail = _small_matmul(x, _pad_cols(w_in[j, :, n_qkv + n_qi:]), jnp.zeros((1, LANES), F32))
    topk = min(IDX_TOPK_MAX, seq // 4)
    sel = _dsa_index(q_idx, tail, batch, seq, topk)
    band = _t5_band(t5_bias, min(ATTN_TQ, seq))
    o = _dsa_attn(qkv, sel, band, batch, seq)
    return _mm_res_ln(o, w_o.astype(BF16), x, ln_g[None, :], ln_b[None, :], *route_wb)


def _dispatch_tables(before, counts, n_tokens, tm):
    n_t = n_tokens // tm
    before = before[:, 0, :MOE_EXPERTS].astype(I32)
    total = counts[0, :MOE_EXPERTS].astype(I32)
    after = jnp.concatenate([before[1:], total[None, :]], axis=0)
    length = after - before
    length_al = (length + ROW_ALIGN - 1) // ROW_ALIGN * ROW_ALIGN
    t_ids = jnp.arange(n_t, dtype=I32)
    e_ids = jnp.arange(MOE_EXPERTS, dtype=I32)
    before_al = jnp.sum(jnp.where((t_ids[None, :] < t_ids[:, None])[:, :, None], length_al[None, :, :], 0), axis=1)
    total_al = jnp.sum(length_al, axis=0)
    seg = (total_al + tm - 1) // tm * tm
    ends = jnp.sum(jnp.where(e_ids[None, :] <= e_ids[:, None], seg[None, :], 0), axis=1)
    starts = ends - seg
    off = jnp.sum(jnp.where((e_ids[None, :] < e_ids[:, None])[None, :, :], length_al[:, None, :], 0), axis=2)
    used = jnp.sum(length_al, axis=1)
    n_chunks = used // ROW_ALIGN
    dstart = starts[None, :] + before_al
    c_row = jnp.arange(TILE_CHUNKS, dtype=I32) * ROW_ALIGN
    c_exp = jnp.sum(((off + length_al)[:, None, :] <= c_row[None, :, None]).astype(I32), axis=2)
    c_exp = jnp.minimum(c_exp, MOE_EXPERTS - 1)
    onehot = c_exp[:, :, None] == e_ids[None, None, :]
    chunk_row = jnp.sum(jnp.where(onehot, (dstart - off)[:, None, :], 0), axis=2) + c_row[None, :]
    pos_tab = off - before
    n_rows = _sorted_rows(n_tokens, tm)
    n_tiles = n_rows // tm
    tile_start = jnp.arange(n_tiles, dtype=I32) * tm
    tile_expert = jnp.minimum(jnp.sum((ends[None, :] <= tile_start[:, None]).astype(I32), axis=1), MOE_EXPERTS - 1)
    tile_valid = (tile_start < ends[-1]).astype(I32)
    tile_first = jnp.concatenate([jnp.ones((1,), I32), (tile_expert[1:] != tile_expert[:-1]).astype(I32)])
    j_ids = jnp.arange(n_tiles, dtype=I32)
    tile_slot = (jnp.sum(jnp.where(j_ids[None, :] <= j_ids[:, None], tile_first[None, :], 0), axis=1) - 1) % 2
    later = (e_ids[None, :] > e_ids[:, None]) & (total[None, :] > 0)
    next_used = jnp.min(jnp.where(later, e_ids[None, :], MOE_EXPERTS), axis=1)
    next_used = jnp.where(next_used == MOE_EXPERTS, -1, next_used).astype(I32)
    tile_next = jnp.sum(jnp.where(tile_expert[:, None] == e_ids[None, :], next_used[None, :], 0), axis=1)
    tail_first = starts + total_al
    tail_chunks = (seg - total_al) // ROW_ALIGN
    return dict(n_chunks=n_chunks.astype(I32), chunk_row=chunk_row.reshape(-1).astype(I32),
                pos_tab=pos_tab.reshape(-1).astype(I32), pos_rows=pos_tab.astype(F32),
                tile_expert=tile_expert.astype(I32), tile_valid=tile_valid, tile_first=tile_first,
                tile_next=tile_next.astype(I32), n_valid=jnp.sum(tile_valid)[None].astype(I32),
                tile_slot=tile_slot.astype(I32),
                tail_first=tail_first.astype(I32),
                tail_chunks=tail_chunks.astype(I32))


def _sorted_rows(n_tokens, tm):
    n_t = n_tokens // tm
    bound = 2 * n_tokens + n_t * MOE_EXPERTS * (ROW_ALIGN - 1) + MOE_EXPERTS * (tm - ROW_ALIGN) + 1
    return (bound + tm - 1) // tm * tm


def _chunk_copy_out(buf, slot, c, row, hbm, sem):
    src = buf.at[slot, pl.ds(pl.multiple_of(c * ROW_ALIGN, ROW_ALIGN), ROW_ALIGN)]
    return pltpu.make_async_copy(src, hbm.at[pl.ds(pl.multiple_of(row, ROW_ALIGN), ROW_ALIGN)], sem.at[slot])


def _chunk_copy_in(hbm, row, buf, slot, c, sem):
    dst = buf.at[slot, pl.ds(pl.multiple_of(c * ROW_ALIGN, ROW_ALIGN), ROW_ALIGN)]
    src = hbm.at[pl.ds(pl.multiple_of(row, ROW_ALIGN), ROW_ALIGN), pl.ds(0, buf.shape[2])]
    return pltpu.make_async_copy(src, dst, sem.at[slot])


def _moe_dispatch_kernel(nch_ref, crow_ref, ptab_ref, tfirst_ref, tchunks_ref, tvalid_ref,
                         x_ref, rt_ref, r_ref, xs_hbm, buf, zeros, sem, sem_z, *, tm, n_tiles):
    i = pl.program_id(0)
    n_t = pl.num_programs(0)
    slot = i % 2

    def wait_tile(tile, s):
        def body(c, carry):
            _chunk_copy_out(buf, s, 0, 0, xs_hbm, sem).wait()
            return carry
        lax.fori_loop(0, nch_ref[tile], body, 0)

    def zero_tail(e, k):
        return pltpu.make_async_copy(
            zeros.at[pl.ds(0, ROW_ALIGN)],
            xs_hbm.at[pl.ds(pl.multiple_of(tfirst_ref[e] + k * ROW_ALIGN, ROW_ALIGN), ROW_ALIGN)], sem_z.at[0])

    def zero_tile(t):
        return pltpu.make_async_copy(zeros, xs_hbm.at[pl.ds(pl.multiple_of(t * tm, tm), tm)], sem_z.at[1])

    def for_each_fill(tail_fn, tile_fn):
        def per_expert(e, carry):
            def per_chunk(k, c2):
                tail_fn(e, k)
                return c2
            lax.fori_loop(0, tchunks_ref[e], per_chunk, 0)
            return carry
        lax.fori_loop(0, MOE_EXPERTS, per_expert, 0)

        def per_tile(t, carry):
            @pl.when(tvalid_ref[t] == 0)
            def _():
                tile_fn(t)
            return carry
        lax.fori_loop(0, n_tiles, per_tile, 0)

    @pl.when(i == 0)
    def _():
        zeros[...] = jnp.zeros(zeros.shape, zeros.dtype)
        for_each_fill(lambda e, k: zero_tail(e, k).start(), lambda t: zero_tile(t).start())

    @pl.when(i >= 2)
    def _():
        wait_tile(i - 2, slot)

    pos1 = rt_ref[4:5, :].astype(I32)
    pos2 = rt_ref[5:6, :].astype(I32)
    e1 = rt_ref[0:1, :].astype(I32)
    e2 = rt_ref[1:2, :].astype(I32)
    for e in range(MOE_EXPERTS):
        shift = ptab_ref[i * MOE_EXPERTS + e]
        pos1 = pos1 + jnp.where(e1 == e, shift, 0)
        pos2 = pos2 + jnp.where(e2 == e, shift, 0)
    lane = lax.broadcasted_iota(I32, (tm, LANES), 1)

    def gate_terms(g):
        hi = g.astype(BF16).astype(F32)
        return jnp.where(lane == 0, hi, jnp.where(lane == 1, g - hi, 0.0)).astype(BF16)

    gate1 = gate_terms(r_ref[:, 2:3])
    gate2 = gate_terms(r_ref[:, 3:4])
    blk = 256
    for k in range(TILE_ROWS // blk):
        row = lax.broadcasted_iota(I32, (blk, tm), 0) + k * blk
        pick1 = jnp.where(row == pos1, 1.0, 0.0).astype(BF16)
        pick2 = jnp.where(row == pos2, 1.0, 0.0).astype(BF16)
        at = slice(k * blk, (k + 1) * blk)
        buf[slot, at, :D_MODEL] = jnp.dot(pick1 + pick2, x_ref[...], preferred_element_type=F32).astype(BF16)
        buf[slot, at, D_MODEL:] = (jnp.dot(pick1, gate1, preferred_element_type=F32)
                                   + jnp.dot(pick2, gate2, preferred_element_type=F32)).astype(BF16)

    def send(c, carry):
        _chunk_copy_out(buf, slot, c, crow_ref[i * TILE_CHUNKS + c], xs_hbm, sem).start()
        return carry
    lax.fori_loop(0, nch_ref[i], send, 0)

    @pl.when(i == n_t - 1)
    def _():
        @pl.when(i >= 1)
        def _():
            wait_tile(i - 1, 1 - slot)
        wait_tile(i, slot)
        for_each_fill(lambda e, k: zero_tail(e, k).wait(), lambda t: zero_tile(t).wait())


def _moe_dispatch(x_bf, route_t, route, tables, tm):
    m, d = x_bf.shape
    n_rows = _sorted_rows(m, tm)
    grid_spec = pltpu.PrefetchScalarGridSpec(
        num_scalar_prefetch=6,
        grid=(m // tm,),
        in_specs=[pl.BlockSpec((tm, d), lambda i, *_: (i, 0)),
                  pl.BlockSpec((8, tm), lambda i, *_: (0, i)),
                  pl.BlockSpec((tm, LANES), lambda i, *_: (i, 0))],
        out_specs=pl.BlockSpec(memory_space=pl.ANY),
        scratch_shapes=[pltpu.VMEM((2, TILE_ROWS, SORTED_WIDTH), BF16), pltpu.VMEM((tm, SORTED_WIDTH), BF16),
                        pltpu.SemaphoreType.DMA((2,)), pltpu.SemaphoreType.DMA((2,))],
    )
    return pl.pallas_call(
        functools.partial(_moe_dispatch_kernel, tm=tm, n_tiles=n_rows // tm),
        name="moe_dispatch",
        grid_spec=grid_spec,
        out_shape=jax.ShapeDtypeStruct((n_rows, SORTED_WIDTH), BF16),
        compiler_params=_params(("arbitrary",)),
    )(tables["n_chunks"], tables["chunk_row"], tables["pos_tab"], tables["tail_first"], tables["tail_chunks"],
      tables["tile_valid"], x_bf, route_t, route)


def _moe_ffn_kernel(te_ref, tv_ref, tf_ref, tn_ref, nv_ref, ts_ref, x_ref, wg_hbm, wu_hbm, wd_hbm, o_ref,
                    wg_f, wu_f, wd_f, wg_bf, wu_bf, wd_bf, sem_w, *, layer):
    j = pl.program_id(0)
    valid = tv_ref[j] == 1
    slot = ts_ref[j]

    def weight_copies(e, s):
        return (pltpu.make_async_copy(wg_hbm.at[layer, e], wg_f.at[s], sem_w.at[s, 0]),
                pltpu.make_async_copy(wu_hbm.at[layer, e], wu_f.at[s], sem_w.at[s, 1]),
                pltpu.make_async_copy(wd_hbm.at[layer, e], wd_f.at[s], sem_w.at[s, 2]))

    @pl.when(j == 0)
    def _():
        for c in weight_copies(te_ref[0], slot):
            c.start()

    @pl.when(valid & (tf_ref[j] == 1))
    def _():
        @pl.when(tn_ref[j] >= 0)
        def _():
            for c in weight_copies(tn_ref[j], 1 - slot):
                c.start()

        for c in weight_copies(te_ref[j], slot):
            c.wait()
        _cast_rows(wg_f.at[slot], wg_bf)
        _cast_rows(wu_f.at[slot], wu_bf)
        _cast_rows(wd_f.at[slot], wd_bf)

    @pl.when(valid)
    def _():
        xb = x_ref[:, :D_MODEL]
        gate = x_ref[:, D_MODEL:D_MODEL + 1].astype(F32) + x_ref[:, D_MODEL + 1:D_MODEL + 2].astype(F32)
        hg = jnp.dot(xb, wg_bf[...], preferred_element_type=F32)
        hu = jnp.dot(xb, wu_bf[...], preferred_element_type=F32)
        act = (hg * (1.0 / (1.0 + jnp.exp(-hg))) * hu).astype(BF16)
        o_ref[...] = (jnp.dot(act, wd_bf[...], preferred_element_type=F32) * gate).astype(o_ref.dtype)


def _moe_ffn(x_sorted, w_gate, w_up, w_down, layer, tables, tm):
    n_rows, width = x_sorted.shape
    d, f = w_gate.shape[2], w_gate.shape[3]
    any_spec = pl.BlockSpec(memory_space=pl.ANY)
    grid_spec = pltpu.PrefetchScalarGridSpec(
        num_scalar_prefetch=6,
        grid=(n_rows // tm,),
        in_specs=[pl.BlockSpec((tm, width), lambda j, te, tv, tf, tn, nv, ts: (jnp.minimum(j, nv[0] - 1), 0)),
                  any_spec, any_spec, any_spec],
        out_specs=pl.BlockSpec((tm, d), lambda j, te, tv, tf, tn, nv, ts: (jnp.minimum(j, nv[0] - 1), 0)),
        scratch_shapes=[pltpu.VMEM((2, d, f), F32), pltpu.VMEM((2, d, f), F32), pltpu.VMEM((2, f, d), F32),
                        pltpu.VMEM((d, f), BF16), pltpu.VMEM((d, f), BF16), pltpu.VMEM((f, d), BF16),
                        pltpu.SemaphoreType.DMA((2, 3))],
    )
    return pl.pallas_call(
        functools.partial(_moe_ffn_kernel, layer=layer),
        name="moe_ffn",
        grid_spec=grid_spec,
        out_shape=jax.ShapeDtypeStruct((n_rows, width), BF16),
        input_output_aliases={6: 0},
        compiler_params=_params(("arbitrary",)),
    )(tables["tile_expert"], tables["tile_valid"], tables["tile_first"], tables["tile_next"], tables["n_valid"],
      tables["tile_slot"], x_sorted, w_gate, w_up, w_down)


def _moe_combine_kernel(nch_ref, crow_ref, ys_hbm, x_ref, r_ref, ptab_ref, g_ref, b_ref, o_ref, obf_ref,
                        buf, sem, *, tm):
    i = pl.program_id(0)
    n_t = pl.num_programs(0)
    slot = i % 2

    def fetch(tile, s):
        def body(c, carry):
            _chunk_copy_in(ys_hbm, crow_ref[tile * TILE_CHUNKS + c], buf, s, c, sem).start()
            return carry
        lax.fori_loop(0, nch_ref[tile], body, 0)

    @pl.when(i == 0)
    def _():
        buf[...] = jnp.zeros(buf.shape, buf.dtype)
        fetch(0, 0)

    @pl.when(i + 1 < n_t)
    def _():
        fetch(i + 1, 1 - slot)

    def arrived(c, carry):
        _chunk_copy_in(ys_hbm, 0, buf, slot, 0, sem).wait()
        return carry
    lax.fori_loop(0, nch_ref[i], arrived, 0)

    lane = lax.broadcasted_iota(I32, (tm, LANES), 1)
    shift = ptab_ref[...]
    e1 = r_ref[:, 0:1].astype(I32)
    e2 = r_ref[:, 1:2].astype(I32)
    pos1 = (r_ref[:, 4:5] + jnp.sum(jnp.where(lane == e1, shift, 0.0), axis=1, keepdims=True)).astype(I32)
    pos2 = (r_ref[:, 5:6] + jnp.sum(jnp.where(lane == e2, shift, 0.0), axis=1, keepdims=True)).astype(I32)
    col = lax.broadcasted_iota(I32, (tm, TILE_ROWS), 1)
    pick = jnp.where((col == pos1) | (col == pos2), 1.0, 0.0).astype(BF16)
    y = jnp.dot(pick, buf[slot], preferred_element_type=F32)
    out = _layernorm_rows(DEEPNORM_ALPHA * x_ref[...] + y, g_ref[...], b_ref[...])
    o_ref[...] = out
    obf_ref[...] = out.astype(BF16)


def _moe_combine(y_sorted, x, route, tables, g, b, tm, out_batch=None):
    m, d = x.shape
    out_f32 = jax.ShapeDtypeStruct((m, d), F32)
    out_spec = pl.BlockSpec((tm, d), lambda i, *_: (i, 0))
    if out_batch:
        per = m // out_batch // tm
        out_f32 = jax.ShapeDtypeStruct((out_batch, m // out_batch, d), F32)
        out_spec = pl.BlockSpec((None, tm, d), lambda i, *_: (i // per, i % per, 0))
    grid_spec = pltpu.PrefetchScalarGridSpec(
        num_scalar_prefetch=2,
        grid=(m // tm,),
        in_specs=[pl.BlockSpec(memory_space=pl.ANY),
                  pl.BlockSpec((tm, d), lambda i, *_: (i, 0)),
                  pl.BlockSpec((tm, LANES), lambda i, *_: (i, 0)),
                  pl.BlockSpec((None, 1, LANES), lambda i, *_: (i, 0, 0)),
                  pl.BlockSpec((1, d), lambda i, *_: (0, 0)),
                  pl.BlockSpec((1, d), lambda i, *_: (0, 0))],
        out_specs=[out_spec, pl.BlockSpec((tm, d), lambda i, *_: (i, 0))],
        scratch_shapes=[pltpu.VMEM((2, TILE_ROWS, d), BF16), pltpu.SemaphoreType.DMA((2,))],
    )
    pos_rows = _pad_cols(tables["pos_rows"])[:, None, :]
    return pl.pallas_call(
        functools.partial(_moe_combine_kernel, tm=tm),
        name="moe_combine",
        grid_spec=grid_spec,
        out_shape=[out_f32, jax.ShapeDtypeStruct((m, d), BF16)],
        compiler_params=_params(("arbitrary",)),
    )(tables["n_chunks"], tables["chunk_row"], y_sorted, x, route, pos_rows, g, b)


def _route_weights(w_group, b_group, w_router, b_router):
    w = _pad_cols(jnp.concatenate([w_group, w_router], axis=1))
    w_hi = w.astype(BF16)
    w_mid = (w - w_hi.astype(F32)).astype(BF16)
    return w_hi, w_mid, _pad_cols(jnp.concatenate([b_group, b_router])[None, :])


def _moe_layer(x, x_bf, route, route_t, before, counts, w_gate, w_up, w_down, layer, ln_g, ln_b, out_batch=None):
    n_tokens = x.shape[0]
    tm = min(MOE_TM, n_tokens)
    tables = _dispatch_tables(before, counts, n_tokens, tm)
    x_sorted = _moe_dispatch(x_bf, route_t, route, tables, tm)
    y_sorted = _moe_ffn(x_sorted, w_gate, w_up, w_down, layer, tables, tm)
    return _moe_combine(y_sorted, x, route, tables, ln_g[None, :], ln_b[None, :], tm, out_batch)


def kernel(x, t5_rel_bias, mla_w_in, mla_q_norm, mla_kv_norm, mla_w_q_up, mla_w_kv_up, mla_w_o, fox_w_in, fox_b_f, fox_w_o, dsa_w_in, dsa_w_o, ln_g, ln_b, moe_w_group, moe_b_group, moe_w_router, moe_b_router, moe_w_gate, moe_w_up, moe_w_down):
    batch, seq, d = x.shape
    x_bf = x
    for layer in range(DEPTH):
        kind = layer % N_MIXERS
        j = layer // N_MIXERS
        g0, b0 = ln_g[layer, 0], ln_b[layer, 0]
        route_wb = _route_weights(moe_w_group[layer], moe_b_group[layer], moe_w_router[layer], moe_b_router[layer])
        if kind == 0:
            mixed = _mla_layer(x, x_bf, mla_w_in[j], mla_q_norm[j], mla_kv_norm[j], mla_w_q_up[j],
                               mla_w_kv_up, j, mla_w_o[j], g0, b0, route_wb, batch, seq)
        elif kind == 1:
            mixed = _fox_layer(x, x_bf, fox_w_in, j, fox_b_f[j], fox_w_o[j], g0, b0, route_wb, batch, seq)
        else:
            mixed = _dsa_layer(x, x_bf, t5_rel_bias, dsa_w_in, j, dsa_w_o[j], g0, b0, route_wb, batch, seq)
        x, x_bf = _moe_layer(*mixed, moe_w_gate, moe_w_up, moe_w_down, layer, ln_g[layer, 1], ln_b[layer, 1],
                             batch if layer == DEPTH - 1 else None)
    return x
```

```python
import functools
import math

import numpy as np
import jax
import jax.numpy as jnp
from jax import lax
from jax.experimental import pallas as pl
from jax.experimental.pallas import tpu as pltpu

F32 = jnp.float32
BF16 = jnp.bfloat16
I32 = jnp.int32

D_MODEL = 2048
DEPTH = 4
CHUNK = 64
N_MIXERS = 3
DEEPNORM_ALPHA = (2.0 * DEPTH) ** 0.25

HEADS = 16
HEAD_DIM = 128
MLA_NOPE_DIM = 128
MLA_ROPE_DIM = 64
MLA_Q_RANK = 512
MLA_KV_RANK = 512
MLA_QK_PAD = 256
ROPE_THETA = 10000.0

IDX_HEADS = 16
IDX_DIM = 64
IDX_TOPK_MAX = 256

T5_BUCKETS = 32
T5_MAX_DISTANCE = 128
T5_FAR_BUCKET = T5_BUCKETS // 2 - 1
T5_BAND_BACK = 128

MOE_GROUPS = 4
MOE_EXPERTS_PER_GROUP = 8
MOE_EXPERTS = MOE_GROUPS * MOE_EXPERTS_PER_GROUP
MOE_D_FF = 512

LN_EPS = 1e-5
RMS_EPS = 1e-6
NEG_INF = -1e30

LANES = 128
VMEM_LIMIT = 56 * 1024 * 1024

ATTN_TQ = 256
MOE_TM = 256
ROW_ALIGN = 16
TILE_ROWS = -(-(2 * MOE_TM + MOE_EXPERTS * (ROW_ALIGN - 1)) // 256) * 256
TILE_CHUNKS = TILE_ROWS // ROW_ALIGN
SORTED_WIDTH = D_MODEL + LANES


def _params(semantics, vmem=VMEM_LIMIT):
    return pltpu.CompilerParams(dimension_semantics=semantics, vmem_limit_bytes=vmem)


def _cast_rows(src_ref, dst_ref):
    rows = 16
    while 2 * rows * src_ref.shape[1] <= 32 * 8 * LANES and src_ref.shape[0] % (2 * rows) == 0:
        rows *= 2

    def body(i, carry):
        at = pl.ds(pl.multiple_of(i * rows, rows), rows)
        dst_ref[at, :] = src_ref[at, :].astype(dst_ref.dtype)
        return carry
    lax.fori_loop(0, src_ref.shape[0] // rows, body, 0, unroll=2)


def _mm_kernel(a_ref, w_ref, o_ref, w_bf, *, scale, scaled_cols):
    @pl.when(pl.program_id(1) == 0)
    def _():
        _cast_rows(w_ref, w_bf)

    acc = lax.dot_general(a_ref[...], w_bf[...], (((1,), (1,)), ((), ())), preferred_element_type=F32)
    if scaled_cols:
        tn = o_ref.shape[1]
        col = pl.program_id(0) * tn + lax.broadcasted_iota(I32, (1, tn), 1)
        acc = acc * jnp.where(col < scaled_cols, scale, 1.0)
    o_ref[...] = acc.astype(o_ref.dtype)


def _matmul(a, w_t, layer, *, n_out, tn, tm=1024, scale=1.0, scaled_cols=0, out_dtype=BF16):
    m, k = a.shape
    tm = min(tm, m)
    kern = functools.partial(_mm_kernel, scale=scale, scaled_cols=scaled_cols)
    return pl.pallas_call(
        kern,
        name="proj_mm",
        grid=(n_out // tn, m // tm),
        in_specs=[pl.BlockSpec((tm, k), lambda j, i: (i, 0)),
                  pl.BlockSpec((None, tn, k), lambda j, i: (layer, j, 0))],
        out_specs=pl.BlockSpec((tm, tn), lambda j, i: (i, j)),
        out_shape=jax.ShapeDtypeStruct((m, n_out), out_dtype),
        scratch_shapes=[pltpu.VMEM((tn, k), BF16)],
        compiler_params=_params(("arbitrary", "arbitrary")),
    )(a, w_t)


def _split3(v):
    hi = v.astype(BF16)
    r1 = v - hi.astype(F32)
    mid = r1.astype(BF16)
    lo = (r1 - mid.astype(F32)).astype(BF16)
    return hi, mid, lo


def _dot_precise(x, w):
    xh = x.astype(BF16)
    xm = (x - xh.astype(F32)).astype(BF16)
    wh = w.astype(BF16)
    wm = (w - wh.astype(F32)).astype(BF16)
    d = lambda p, q: jnp.dot(p, q, preferred_element_type=F32)
    return (d(xm, wh) + d(xh, wm)) + d(xh, wh)


def _small_mm_kernel(x_ref, w_ref, b_ref, o_ref):
    o_ref[...] = _dot_precise(x_ref[...], w_ref[...]) + b_ref[...]


def _small_matmul(x, w, b, tm=512):
    m, k = x.shape
    tm = min(tm, m)
    return pl.pallas_call(
        _small_mm_kernel,
        name="small_mm",
        grid=(m // tm,),
        in_specs=[pl.BlockSpec((tm, k), lambda i: (i, 0)),
                  pl.BlockSpec((k, LANES), lambda i: (0, 0)),
                  pl.BlockSpec((1, LANES), lambda i: (0, 0))],
        out_specs=pl.BlockSpec((tm, LANES), lambda i: (i, 0)),
        out_shape=jax.ShapeDtypeStruct((m, LANES), F32),
        compiler_params=_params(("arbitrary",)),
    )(x, w, b)


def _rows_spec(arr, tm):
    d = arr.shape[-1]
    if arr.ndim == 2:
        return pl.BlockSpec((tm, d), lambda i: (i, 0))
    per = arr.shape[1] // tm
    return pl.BlockSpec((None, tm, d), lambda i: (i // per, i % per, 0))


def _pad_cols(w, width=LANES):
    return jnp.pad(w, ((0, 0), (0, width - w.shape[1])))


def _layernorm_rows(z, g, b):
    mu = jnp.mean(z, axis=-1, keepdims=True)
    zc = z - mu
    var = jnp.mean(zc * zc, axis=-1, keepdims=True)
    return zc * lax.rsqrt(var + LN_EPS) * g + b


def _route_tile(x, w_hi, w_mid, b, run_ref):
    tm = x.shape[0]
    x_hi = x.astype(BF16)
    x_mid = (x - x_hi.astype(F32)).astype(BF16)
    d = lambda p, q: jnp.dot(p, q, preferred_element_type=F32)
    logit = (d(x_mid, w_hi) + d(x_hi, w_mid)) + d(x_hi, w_hi) + b
    lane = lax.broadcasted_iota(I32, logit.shape, 1)
    big = jnp.int32(LANES)

    def first_argmax(mask):
        top = jnp.max(jnp.where(mask, logit, -jnp.inf), axis=1, keepdims=True)
        idx = jnp.min(jnp.where(mask & (logit == top), lane, big), axis=1, keepdims=True)
        return top, idx

    is_group = lane < MOE_GROUPS
    g_top, g_sel = first_argmax(is_group)
    p_g = 1.0 / jnp.sum(jnp.where(is_group, jnp.exp(logit - g_top), 0.0), axis=1, keepdims=True)
    lo = MOE_GROUPS + g_sel * MOE_EXPERTS_PER_GROUP
    in_group = (lane >= lo) & (lane < lo + MOE_EXPERTS_PER_GROUP)
    v1, i1 = first_argmax(in_group)
    v2, i2 = first_argmax(in_group & (lane != i1))
    ex = jnp.exp(v2 - v1)
    w1 = 1.0 / (1.0 + ex)
    w2 = ex / (1.0 + ex)
    e1 = i1 - MOE_GROUPS
    e2 = i2 - MOE_GROUPS
    pick1 = lane == e1
    pick2 = lane == e2
    both = jnp.where(pick1 | pick2, 1.0, 0.0)
    earlier = lax.broadcasted_iota(I32, (tm, tm), 1) < lax.broadcasted_iota(I32, (tm, tm), 0)
    start = run_ref[...]
    before = jnp.dot(jnp.where(earlier, 1.0, 0.0).astype(BF16), both.astype(BF16),
                     preferred_element_type=F32) + start
    r1 = jnp.sum(jnp.where(pick1, before, 0.0), axis=1, keepdims=True)
    r2 = jnp.sum(jnp.where(pick2, before, 0.0), axis=1, keepdims=True)
    run_ref[...] = start + jnp.sum(both, axis=0, keepdims=True)
    out = jnp.where(lane == 0, e1.astype(F32), 0.0)
    out = jnp.where(lane == 1, e2.astype(F32), out)
    out = jnp.where(lane == 2, w1 * p_g, out)
    out = jnp.where(lane == 3, w2 * p_g, out)
    out = jnp.where(lane == 4, r1, out)
    out = jnp.where(lane == 5, r2, out)
    return out, start


def _mm_res_ln_kernel(a_ref, w_ref, x_ref, g_ref, b_ref, wrh_ref, wrm_ref, br_ref,
                      o_ref, obf_ref, r_ref, rt_ref, before_ref, cnt_ref, run_ref, *, sub):
    @pl.when(pl.program_id(0) == 0)
    def _():
        run_ref[...] = jnp.zeros(run_ref.shape, F32)

    y = jnp.dot(a_ref[...], w_ref[...], preferred_element_type=F32)
    out = _layernorm_rows(DEEPNORM_ALPHA * x_ref[...] + y, g_ref[...], b_ref[...])
    o_ref[...] = out
    obf_ref[...] = out.astype(BF16)
    for k in range(out.shape[0] // sub):
        rows = slice(k * sub, (k + 1) * sub)
        slab, before = _route_tile(out[rows, :], wrh_ref[...], wrm_ref[...], br_ref[...], run_ref)
        r_ref[rows, :] = slab
        rt_ref[:, rows] = slab.T[:8, :]
        before_ref[k] = before
    cnt_ref[...] = run_ref[...]


def _mm_res_ln(a, w, x, g, b, w_route_hi, w_route_mid, b_route, tm=512):
    m, k = a.shape
    d = w.shape[1]
    tm = min(tm, m)
    sub = min(MOE_TM, tm)
    per = tm // sub
    return pl.pallas_call(
        functools.partial(_mm_res_ln_kernel, sub=sub),
        name="out_proj_res_ln",
        grid=(m // tm,),
        in_specs=[pl.BlockSpec((tm, k), lambda i: (i, 0)),
                  pl.BlockSpec((k, d), lambda i: (0, 0)),
                  _rows_spec(x, tm),
                  pl.BlockSpec((1, d), lambda i: (0, 0)),
                  pl.BlockSpec((1, d), lambda i: (0, 0)),
                  pl.BlockSpec((d, LANES), lambda i: (0, 0)),
                  pl.BlockSpec((d, LANES), lambda i: (0, 0)),
                  pl.BlockSpec((1, LANES), lambda i: (0, 0))],
        out_specs=[pl.BlockSpec((tm, d), lambda i: (i, 0)),
                   pl.BlockSpec((tm, d), lambda i: (i, 0)),
                   pl.BlockSpec((tm, LANES), lambda i: (i, 0)),
                   pl.BlockSpec((8, tm), lambda i: (0, i)),
                   pl.BlockSpec((per, 1, LANES), lambda i: (i, 0, 0)),
                   pl.BlockSpec((1, LANES), lambda i: (0, 0))],
        out_shape=[jax.ShapeDtypeStruct((m, d), F32), jax.ShapeDtypeStruct((m, d), BF16),
                   jax.ShapeDtypeStruct((m, LANES), F32), jax.ShapeDtypeStruct((8, m), F32),
                   jax.ShapeDtypeStruct((m // sub, 1, LANES), F32), jax.ShapeDtypeStruct((1, LANES), F32)],
        scratch_shapes=[pltpu.VMEM((1, LANES), F32)],
        compiler_params=_params(("arbitrary",)),
    )(a, w, x, g, b, w_route_hi, w_route_mid, b_route)


def _rope_tables(seq):
    half = MLA_ROPE_DIM // 2
    inv_freq = ROPE_THETA ** (-jnp.arange(half, dtype=F32) / half)
    ang = jnp.arange(seq, dtype=I32).astype(F32)[:, None] * inv_freq[None, :]
    cos, sin = jnp.cos(ang), jnp.sin(ang)
    z = jnp.zeros_like(cos)
    cos_t = jnp.concatenate([cos, cos, z, z], axis=1)
    sin_a = jnp.concatenate([-sin, z, z, z], axis=1)
    sin_b = jnp.concatenate([z, sin, z, z], axis=1)
    return cos_t, sin_a, sin_b


def _rope_lanes(r, cos_t, sin_a, sin_b):
    from_right = pltpu.roll(r, LANES - MLA_ROPE_DIM // 2, 1)
    from_left = pltpu.roll(r, MLA_ROPE_DIM // 2, 1)
    return r * cos_t + from_right * sin_a + from_left * sin_b


def _rms_rows(v, g):
    return v * lax.rsqrt(jnp.mean(v * v, axis=-1, keepdims=True) + RMS_EPS) * g


def _mla_in_kernel(x_ref, w_ref, qn_ref, kvn_ref, cos_ref, sina_ref, sinb_ref, cq_ref, ckv_ref, kr_ref):
    h = _nt_dot(x_ref[...].astype(BF16), w_ref[...])
    cq_ref[...] = _rms_rows(h[:, :MLA_Q_RANK], qn_ref[...]).astype(BF16)
    ckv_ref[...] = _rms_rows(h[:, MLA_Q_RANK:MLA_Q_RANK + MLA_KV_RANK], kvn_ref[...]).astype(BF16)
    r = h[:, MLA_Q_RANK + MLA_KV_RANK:]
    kr_ref[...] = _rope_lanes(r, cos_ref[...], sina_ref[...], sinb_ref[...]).astype(BF16)


def _mla_in(x_bf, w_in_pad, q_norm, kv_norm, tables, seq, tm=512):
    m, k = math.prod(x_bf.shape[:-1]), x_bf.shape[-1]
    tm = min(tm, seq)
    nw = w_in_pad.shape[0]
    per_seq = seq // tm
    tab_spec = pl.BlockSpec((tm, LANES), lambda i: (i % per_seq, 0))
    return pl.pallas_call(
        _mla_in_kernel,
        name="mla_in",
        grid=(m // tm,),
        in_specs=[_rows_spec(x_bf, tm),
                  pl.BlockSpec((nw, k), lambda i: (0, 0)),
                  pl.BlockSpec((1, MLA_Q_RANK), lambda i: (0, 0)),
                  pl.BlockSpec((1, MLA_KV_RANK), lambda i: (0, 0)),
                  tab_spec, tab_spec, tab_spec],
        out_specs=[pl.BlockSpec((tm, MLA_Q_RANK), lambda i: (i, 0)),
                   pl.BlockSpec((tm, MLA_KV_RANK), lambda i: (i, 0)),
                   pl.BlockSpec((tm, LANES), lambda i: (i, 0))],
        out_shape=[jax.ShapeDtypeStruct((m, MLA_Q_RANK), BF16),
                   jax.ShapeDtypeStruct((m, MLA_KV_RANK), BF16),
                   jax.ShapeDtypeStruct((m, LANES), BF16)],
        compiler_params=_params(("arbitrary",)),
    )(x_bf, w_in_pad, q_norm, kv_norm, *tables)


def _fill_v_aug(va_ref, v_ref):
    va_ref[:, :HEAD_DIM] = v_ref[...]
    va_ref[:, HEAD_DIM:] = jnp.ones((v_ref.shape[0], HEAD_DIM), BF16)


def _softmax_pv(pieces):
    m = None
    for s, _ in pieces:
        pm = jnp.max(s, axis=1, keepdims=True)
        m = pm if m is None else jnp.maximum(m, pm)
    acc = None
    for s, v in pieces:
        t = jnp.dot(jnp.exp((s - m).astype(BF16)), v, preferred_element_type=F32)
        acc = t if acc is None else acc + t
    return acc[:, :HEAD_DIM] / acc[:, HEAD_DIM:HEAD_DIM + 1]


def _nt_dot(q, k):
    return lax.dot_general(q, k, (((1,), (1,)), ((), ())), preferred_element_type=F32)


def _attend_tiles(n_tiles, scores, finish):
    s_next = scores(0)
    for i in range(n_tiles):
        s = s_next
        if i + 1 < n_tiles:
            s_next = scores(i + 1)
        finish(i, s)


def _mla_q_up_kernel(c_ref, w_ref, cos_ref, sina_ref, sinb_ref, o_ref, w_bf, *, scale):
    @pl.when(pl.program_id(1) == 0)
    def _():
        _cast_rows(w_ref, w_bf)

    acc = jnp.dot(c_ref[...], w_bf[...], preferred_element_type=F32) * scale
    for h in range(o_ref.shape[1] // MLA_QK_PAD):
        at = h * MLA_QK_PAD
        o_ref[:, at:at + MLA_NOPE_DIM] = acc[:, at:at + MLA_NOPE_DIM].astype(BF16)
        r = acc[:, at + MLA_NOPE_DIM:at + MLA_QK_PAD]
        o_ref[:, at + MLA_NOPE_DIM:at + MLA_QK_PAD] = _rope_lanes(
            r, cos_ref[...], sina_ref[...], sinb_ref[...]).astype(BF16)


def _mla_kv_up_kernel(c_ref, w_ref, kr_ref, k_ref, v_ref, w_bf):
    @pl.when(pl.program_id(1) == 0)
    def _():
        _cast_rows(w_ref, w_bf)

    acc = jnp.dot(c_ref[...], w_bf[...], preferred_element_type=F32)
    ones = jnp.ones((acc.shape[0], HEAD_DIM), BF16)
    for h in range(k_ref.shape[1] // MLA_QK_PAD):
        at = h * MLA_QK_PAD
        k_ref[:, at:at + MLA_NOPE_DIM] = acc[:, at:at + MLA_NOPE_DIM].astype(BF16)
        k_ref[:, at + MLA_NOPE_DIM:at + MLA_QK_PAD] = kr_ref[...]
        v_ref[:, at:at + HEAD_DIM] = acc[:, at + MLA_NOPE_DIM:at + MLA_QK_PAD].astype(BF16)
        v_ref[:, at + HEAD_DIM:at + 2 * HEAD_DIM] = ones


def _mla_up(cq, ckv, kr, w_q_up, w_kv_up, j, tables, seq, scale, tm=1024, tn=1024):
    m, rank = cq.shape
    tm = min(tm, seq)
    per_seq = seq // tm
    n = HEADS * MLA_QK_PAD
    tab_spec = pl.BlockSpec((tm, LANES), lambda j, i: (i % per_seq, 0))
    row_spec = pl.BlockSpec((tm, rank), lambda j, i: (i, 0))
    w_spec = pl.BlockSpec((rank, tn), lambda j, i: (0, j))
    out_spec = pl.BlockSpec((tm, tn), lambda j, i: (i, j))
    out = jax.ShapeDtypeStruct((m, n), BF16)
    q = pl.pallas_call(
        functools.partial(_mla_q_up_kernel, scale=scale),
        name="mla_q_up",
        grid=(n // tn, m // tm),
        in_specs=[row_spec, w_spec, tab_spec, tab_spec, tab_spec],
        out_specs=out_spec,
        out_shape=out,
        scratch_shapes=[pltpu.VMEM((rank, tn), BF16)],
        compiler_params=_params(("arbitrary", "arbitrary")),
    )(cq, w_q_up, *tables)
    k, v = pl.pallas_call(
        _mla_kv_up_kernel,
        name="mla_kv_up",
        grid=(n // tn, m // tm),
        in_specs=[row_spec, pl.BlockSpec((None, rank, tn), lambda c, i: (j, 0, c)),
                  pl.BlockSpec((tm, LANES), lambda c, i: (i, 0))],
        out_specs=[out_spec, out_spec],
        out_shape=[out, out],
        scratch_shapes=[pltpu.VMEM((rank, tn), BF16)],
        compiler_params=_params(("arbitrary", "arbitrary")),
    )(ckv, w_kv_up, kr)
    return q, k, v


def _mla_attn_kernel(q_ref, k_ref, v_ref, o_ref, *, tq):
    seq = q_ref.shape[0]
    row = lax.broadcasted_iota(I32, (tq, tq), 0)
    col = lax.broadcasted_iota(I32, (tq, tq), 1)
    visible = (col // CHUNK) <= (row // CHUNK)

    def scores(i):
        return _nt_dot(q_ref[i * tq:(i + 1) * tq, :], k_ref[:(i + 1) * tq, :])

    def finish(i, s):
        lo, kv = i * tq, (i + 1) * tq
        pieces = [(jnp.where(visible, s[:, lo:], NEG_INF), v_ref[lo:kv, :])]
        if lo:
            pieces.append((s[:, :lo], v_ref[:lo, :]))
        o_ref[lo:kv, :] = _softmax_pv(pieces).astype(o_ref.dtype)

    _attend_tiles(seq // tq, scores, finish)


def _mla_attn(q, k, v, batch, seq):
    tq = min(ATTN_TQ, seq)
    head_spec = pl.BlockSpec((seq, MLA_QK_PAD), lambda b, h: (b, h))
    return pl.pallas_call(
        functools.partial(_mla_attn_kernel, tq=tq),
        name="mla_attn",
        grid=(batch, HEADS),
        in_specs=[head_spec, head_spec, head_spec],
        out_specs=pl.BlockSpec((seq, HEAD_DIM), lambda b, h: (b, h)),
        out_shape=jax.ShapeDtypeStruct((batch * seq, HEADS * HEAD_DIM), BF16),
        compiler_params=_params(("arbitrary", "arbitrary")),
    )(q, k, v)


def _mla_q_up_layout(w_q_up):
    rank = w_q_up.shape[0]
    w = w_q_up.reshape(rank, HEADS, MLA_NOPE_DIM + MLA_ROPE_DIM)
    w = jnp.pad(w, ((0, 0), (0, 0), (0, MLA_QK_PAD - MLA_NOPE_DIM - MLA_ROPE_DIM)))
    return w.reshape(rank, HEADS * MLA_QK_PAD)


def _mla_layer(x, x_bf, w_in, q_norm, kv_norm, w_q_up, w_kv_up, j, w_o, ln_g, ln_b, route_wb, batch, seq):
    tables = _rope_tables(seq)
    w_in_pad = jnp.pad(w_in.T, ((0, MLA_Q_RANK + MLA_KV_RANK + LANES - w_in.shape[1]), (0, 0))).astype(BF16)
    cq, ckv, kr = _mla_in(x_bf, w_in_pad, q_norm[None, :], kv_norm[None, :], tables, seq)
    scale = (MLA_NOPE_DIM + MLA_ROPE_DIM) ** -0.5
    q, k, v = _mla_up(cq, ckv, kr, _mla_q_up_layout(w_q_up), w_kv_up, j, tables, seq, scale)
    o = _mla_attn(q, k, v, batch, seq)
    return _mm_res_ln(o, w_o.astype(BF16), x, ln_g[None, :], ln_b[None, :], *route_wb)


def _log_sigmoid(z):
    return jnp.minimum(z, 0.0) - jnp.log1p(jnp.exp(-jnp.abs(z)))


def _fox_gate_kernel(f_ref, c_ref, *, blk):
    seq = f_ref.shape[0]
    tri = (lax.broadcasted_iota(I32, (blk, blk), 1) <= lax.broadcasted_iota(I32, (blk, blk), 0)).astype(BF16)
    carry = jnp.zeros((1, LANES), F32)
    for j in range(seq // blk):
        lf = _log_sigmoid(f_ref[j * blk:(j + 1) * blk, :])
        hi, mid, lo = _split3(lf)
        d = lambda p: jnp.dot(tri, p, preferred_element_type=F32)
        c = (d(lo) + d(mid)) + d(hi) + carry
        c_ref[j * blk:(j + 1) * blk, :] = c
        carry = c[blk - 1:blk, :]


def _fox_gate(f_logit, batch, seq):
    blk = min(256, seq)
    return pl.pallas_call(
        functools.partial(_fox_gate_kernel, blk=blk),
        name="fox_gate",
        grid=(batch,),
        in_specs=[pl.BlockSpec((seq, LANES), lambda b: (b, 0))],
        out_specs=pl.BlockSpec((seq, LANES), lambda b: (b, 0)),
        out_shape=jax.ShapeDtypeStruct((batch * seq, LANES), F32),
        compiler_params=_params(("arbitrary",)),
    )(f_logit)


def _fox_attn_kernel(q_ref, k_ref, v_ref, c_ref, o_ref, qs_ref, ks_ref, *, tq):
    seq = q_ref.shape[0]
    h = pl.program_id(1)
    lane = lax.broadcasted_iota(I32, (seq, LANES), 1)
    c = jnp.sum(jnp.where(lane == h, c_ref[...], 0.0), axis=1, keepdims=True)
    terms = [t.astype(F32) for t in _split3(c)]
    q_side = jnp.where((lane >= 3) & (lane < 6), 1.0, 0.0)
    k_side = jnp.where(lane < 3, 1.0, 0.0)
    for j, t in enumerate(terms):
        q_side = jnp.where(lane == j, t, q_side)
        k_side = jnp.where(lane == 3 + j, -t, k_side)
    qs_ref[...] = q_side.astype(BF16)
    ks_ref[...] = k_side.astype(BF16)
    row = lax.broadcasted_iota(I32, (tq, tq), 0)
    col = lax.broadcasted_iota(I32, (tq, tq), 1)

    def wide(ref, side_ref, rows):
        return jnp.concatenate([ref[rows, :], side_ref[rows, :]], axis=1)

    def values(rows):
        v = v_ref[rows, :]
        return jnp.concatenate([v, jnp.ones(v.shape, BF16)], axis=1)

    def scores(i):
        return _nt_dot(wide(q_ref, qs_ref, slice(i * tq, (i + 1) * tq)), wide(k_ref, ks_ref, slice(0, (i + 1) * tq)))

    def finish(i, s):
        lo, kv = i * tq, (i + 1) * tq
        pieces = [(jnp.where(col <= row, s[:, lo:], NEG_INF), values(slice(lo, kv)))]
        if lo:
            pieces.append((s[:, :lo], values(slice(0, lo))))
        o_ref[lo:kv, :] = _softmax_pv(pieces).astype(o_ref.dtype)

    _attend_tiles(seq // tq, scores, finish)


def _fox_attn(qkv, c_col, batch, seq):
    tq = min(ATTN_TQ, seq)
    side = pltpu.VMEM((seq, LANES), BF16)
    return pl.pallas_call(
        functools.partial(_fox_attn_kernel, tq=tq),
        name="fox_attn",
        grid=(batch, HEADS),
        in_specs=[pl.BlockSpec((seq, HEAD_DIM), lambda b, h: (b, h)),
                  pl.BlockSpec((seq, HEAD_DIM), lambda b, h: (b, HEADS + h)),
                  pl.BlockSpec((seq, HEAD_DIM), lambda b, h: (b, 2 * HEADS + h)),
                  pl.BlockSpec((seq, LANES), lambda b, h: (b, 0))],
        out_specs=pl.BlockSpec((seq, HEAD_DIM), lambda b, h: (b, h)),
        out_shape=jax.ShapeDtypeStruct((batch * seq, HEADS * HEAD_DIM), BF16),
        scratch_shapes=[side, side],
        compiler_params=_params(("arbitrary", "arbitrary")),
    )(qkv, qkv, qkv, c_col)


def _fox_layer(x, x_bf, w_in, j, b_f, w_o, ln_g, ln_b, route_wb, batch, seq):
    hd = HEADS * HEAD_DIM
    scale = HEAD_DIM ** -0.5
    qkv = _matmul(x_bf, jnp.swapaxes(w_in, 1, 2), j, n_out=3 * hd, tn=1024, scale=scale, scaled_cols=hd)
    f_logit = _small_matmul(x, _pad_cols(w_in[j, :, 3 * hd:]), _pad_cols(b_f[None, :]))
    o = _fox_attn(qkv, _fox_gate(f_logit, batch, seq), batch, seq)
    return _mm_res_ln(o, w_o.astype(BF16), x, ln_g[None, :], ln_b[None, :], *route_wb)


def _t5_bucket_table(tq):
    a = np.arange(tq, dtype=np.int64)[:, None]
    b = np.arange(tq + T5_BAND_BACK, dtype=np.int64)[None, :]
    rel = (b - T5_BAND_BACK) - a
    nb = T5_BUCKETS // 2
    max_exact = nb // 2
    ret = np.where(rel > 0, nb, 0)
    n = np.abs(rel)
    nf = np.maximum(n, 1).astype(np.float32)
    large = max_exact + (np.log(nf / np.float32(max_exact)) / np.float32(math.log(T5_MAX_DISTANCE / max_exact))
                         * np.float32(nb - max_exact)).astype(np.int32)
    large = np.minimum(large, nb - 1)
    return (ret + np.where(n < max_exact, n, large)).astype(np.int32)


def _t5_band_kernel(t5_ref, bucket_ref, o_ref):
    bucket = bucket_ref[...]

    def per_head(h, carry):
        acc = jnp.zeros(bucket.shape, F32)
        for b in range(T5_BUCKETS):
            acc = jnp.where(bucket == b, t5_ref[b, h], acc)
        o_ref[h] = acc - t5_ref[T5_FAR_BUCKET, h]
        return carry

    lax.fori_loop(0, HEADS, per_head, 0)


def _t5_band(t5_bias, tq):
    bucket = jnp.asarray(_t5_bucket_table(tq))
    width = tq + T5_BAND_BACK
    return pl.pallas_call(
        _t5_band_kernel,
        name="t5_band",
        grid=(1,),
        in_specs=[pl.BlockSpec(memory_space=pltpu.SMEM),
                  pl.BlockSpec((tq, width), lambda i: (0, 0))],
        out_specs=pl.BlockSpec((HEADS, tq, width), lambda i: (0, 0, 0)),
        out_shape=jax.ShapeDtypeStruct((HEADS, tq, width), F32),
        compiler_params=_params(("arbitrary",)),
    )(t5_bias, bucket)


def _sortable_key(v):
    bits = pltpu.bitcast(v, I32)
    return bits ^ ((bits >> 31) & 0x7FFFFFFF)


def _kth_largest_key(key_ref, kv, k):
    rows = key_ref.shape[0]

    def count_ge(t):
        t_b = jnp.broadcast_to(t, (rows, LANES))
        acc = jnp.zeros((rows, LANES), I32)
        for c in range(kv // LANES):
            acc = acc + jnp.where(key_ref[:, c * LANES:(c + 1) * LANES] >= t_b, 1, 0)
        return jnp.sum(acc, axis=1, keepdims=True)

    t0 = jnp.where(count_ge(jnp.zeros((rows, 1), I32)) >= k, 0, jnp.iinfo(jnp.int32).min).astype(I32)

    def body(it, t):
        cand = t | (jnp.int32(1) << (30 - it))
        return jnp.where(count_ge(cand) >= k, cand, t)

    return lax.fori_loop(0, 31, body, t0)


def _dsa_index_kernel(qi_ref, tail_ref, o_ref, key_ref, *, tq, topk):
    seq = tail_ref.shape[0]
    k_idx = tail_ref[:, :IDX_DIM].astype(BF16)
    for i in range(seq // tq):
        kv = (i + 1) * tq
        rows = slice(i * tq, kv)
        w = tail_ref[rows, IDX_DIM:IDX_DIM + IDX_HEADS] * ((IDX_HEADS * IDX_DIM) ** -0.5)
        score = jnp.zeros((tq, kv), F32)
        for h in range(IDX_HEADS):
            logits = _nt_dot(qi_ref[rows, h * IDX_DIM:(h + 1) * IDX_DIM], k_idx[:kv, :])
            score = score + jnp.maximum(logits, 0.0) * w[:, h:h + 1]
        row = lax.broadcasted_iota(I32, (tq, kv), 0) + i * tq
        col = lax.broadcasted_iota(I32, (tq, kv), 1)
        admissible = (col // CHUNK) <= (row // CHUNK)
        if kv <= topk:
            keep = admissible
        else:
            key_ref[:, :kv] = _sortable_key(jnp.where(admissible, score, NEG_INF))
            thr = _kth_largest_key(key_ref, kv, topk)
            keep = admissible & (key_ref[:, :kv] >= thr)
        o_ref[rows, :kv] = jnp.where(keep, 0.0, NEG_INF).astype(o_ref.dtype)
        if kv < seq:
            o_ref[rows, kv:] = jnp.full((tq, seq - kv), NEG_INF, o_ref.dtype)


def _dsa_index(q_idx, tail, batch, seq, topk):
    tq = min(ATTN_TQ, seq)
    return pl.pallas_call(
        functools.partial(_dsa_index_kernel, tq=tq, topk=topk),
        name="dsa_index",
        grid=(batch,),
        in_specs=[pl.BlockSpec((seq, IDX_HEADS * IDX_DIM), lambda b: (b, 0)),
                  pl.BlockSpec((seq, LANES), lambda b: (b, 0))],
        out_specs=pl.BlockSpec((seq, seq), lambda b: (b, 0)),
        out_shape=jax.ShapeDtypeStruct((batch * seq, seq), BF16),
        scratch_shapes=[pltpu.VMEM((tq, seq), I32)],
        compiler_params=_params(("arbitrary",)),
    )(q_idx, tail)


def _dsa_attn_kernel(q_ref, k_ref, v_ref, sel_ref, band_ref, o_ref, va_ref, *, tq):
    seq = k_ref.shape[0]
    width = band_ref.shape[1]
    _fill_v_aug(va_ref, v_ref)
    def scores(i):
        return _nt_dot(q_ref[i * tq:(i + 1) * tq, :], k_ref[:(i + 1) * tq, :])

    def finish(i, s):
        lo, kv = i * tq, (i + 1) * tq
        near = min(width, kv)
        far = kv - near
        s = s + sel_ref[lo:kv, :kv].astype(F32)
        pieces = [(s[:, far:] + band_ref[:, width - near:], va_ref[far:kv, :])]
        if far:
            pieces.append((s[:, :far], va_ref[:far, :]))
        o_ref[lo:kv, :] = _softmax_pv(pieces).astype(o_ref.dtype)

    _attend_tiles(seq // tq, scores, finish)


def _dsa_attn(qkv, sel, band, batch, seq):
    tq = min(ATTN_TQ, seq)
    width = band.shape[2]
    return pl.pallas_call(
        functools.partial(_dsa_attn_kernel, tq=tq),
        name="dsa_attn",
        grid=(batch, HEADS),
        in_specs=[pl.BlockSpec((seq, HEAD_DIM), lambda b, h: (b, h)),
                  pl.BlockSpec((seq, HEAD_DIM), lambda b, h: (b, HEADS)),
                  pl.BlockSpec((seq, HEAD_DIM), lambda b, h: (b, HEADS + 1)),
                  pl.BlockSpec((seq, seq), lambda b, h: (b, 0)),
                  pl.BlockSpec((None, tq, width), lambda b, h: (h, 0, 0))],
        out_specs=pl.BlockSpec((seq, HEAD_DIM), lambda b, h: (b, h)),
        out_shape=jax.ShapeDtypeStruct((batch * seq, HEADS * HEAD_DIM), BF16),
        scratch_shapes=[pltpu.VMEM((seq, 2 * HEAD_DIM), BF16)],
        compiler_params=_params(("arbitrary", "arbitrary")),
    )(qkv, qkv, qkv, sel, band)


def _dsa_layer(x, x_bf, t5_bias, w_in, j, w_o, ln_g, ln_b, route_wb, batch, seq):
    hd = HEADS * HEAD_DIM
    n_qkv = hd + 2 * HEAD_DIM
    n_qi = IDX_HEADS * IDX_DIM
    scale = HEAD_DIM ** -0.5
    w_t = jnp.swapaxes(w_in, 1, 2)
    qkv = _matmul(x_bf, w_t, j, n_out=n_qkv, tn=n_qkv // 3, scale=scale, scaled_cols=hd)
    q_idx = _matmul(x_bf, w_t[j:j + 1, n_qkv:n_qkv + n_qi, :], 0, n_out=n_qi, tn=n_qi)
    tail = _small_matmul(x, _pad_cols(w_in[j, :, n_qkv + n_qi:]), jnp.zeros((1, LANES), F32))
    topk = min(IDX_TOPK_MAX, seq // 4)
    sel = _dsa_index(q_idx, tail, batch, seq, topk)
    band = _t5_band(t5_bias, min(ATTN_TQ, seq))
    o = _dsa_attn(qkv, sel, band, batch, seq)
    return _mm_res_ln(o, w_o.astype(BF16), x, ln_g[None, :], ln_b[None, :], *route_wb)


def _dispatch_tables(before, counts, n_tokens, tm):
    n_t = n_tokens // tm
    before = before[:, 0, :MOE_EXPERTS].astype(I32)
    total = counts[0, :MOE_EXPERTS].astype(I32)
    after = jnp.concatenate([before[1:], total[None, :]], axis=0)
    length = after - before
    length_al = (length + ROW_ALIGN - 1) // ROW_ALIGN * ROW_ALIGN
    t_ids = jnp.arange(n_t, dtype=I32)
    e_ids = jnp.arange(MOE_EXPERTS, dtype=I32)
    before_al = jnp.sum(jnp.where((t_ids[None, :] < t_ids[:, None])[:, :, None], length_al[None, :, :], 0), axis=1)
    total_al = jnp.sum(length_al, axis=0)
    seg = (total_al + tm - 1) // tm * tm
    ends = jnp.sum(jnp.where(e_ids[None, :] <= e_ids[:, None], seg[None, :], 0), axis=1)
    starts = ends - seg
    off = jnp.sum(jnp.where((e_ids[None, :] < e_ids[:, None])[None, :, :], length_al[:, None, :], 0), axis=2)
    used = jnp.sum(length_al, axis=1)
    n_chunks = used // ROW_ALIGN
    dstart = starts[None, :] + before_al
    c_row = jnp.arange(TILE_CHUNKS, dtype=I32) * ROW_ALIGN
    c_exp = jnp.sum(((off + length_al)[:, None, :] <= c_row[None, :, None]).astype(I32), axis=2)
    c_exp = jnp.minimum(c_exp, MOE_EXPERTS - 1)
    onehot = c_exp[:, :, None] == e_ids[None, None, :]
    chunk_row = jnp.sum(jnp.where(onehot, (dstart - off)[:, None, :], 0), axis=2) + c_row[None, :]
    pos_tab = off - before
    n_rows = _sorted_rows(n_tokens, tm)
    n_tiles = n_rows // tm
    tile_start = jnp.arange(n_tiles, dtype=I32) * tm
    tile_expert = jnp.minimum(jnp.sum((ends[None, :] <= tile_start[:, None]).astype(I32), axis=1), MOE_EXPERTS - 1)
    tile_valid = (tile_start < ends[-1]).astype(I32)
    tile_first = jnp.concatenate([jnp.ones((1,), I32), (tile_expert[1:] != tile_expert[:-1]).astype(I32)])
    j_ids = jnp.arange(n_tiles, dtype=I32)
    tile_slot = (jnp.sum(jnp.where(j_ids[None, :] <= j_ids[:, None], tile_first[None, :], 0), axis=1) - 1) % 2
    later = (e_ids[None, :] > e_ids[:, None]) & (total[None, :] > 0)
    next_used = jnp.min(jnp.where(later, e_ids[None, :], MOE_EXPERTS), axis=1)
    next_used = jnp.where(next_used == MOE_EXPERTS, -1, next_used).astype(I32)
    tile_next = jnp.sum(jnp.where(tile_expert[:, None] == e_ids[None, :], next_used[None, :], 0), axis=1)
    tail_first = starts + total_al
    tail_chunks = (seg - total_al) // ROW_ALIGN
    return dict(n_chunks=n_chunks.astype(I32), chunk_row=chunk_row.reshape(-1).astype(I32),
                pos_tab=pos_tab.reshape(-1).astype(I32), pos_rows=pos_tab.astype(F32),
                tile_expert=tile_expert.astype(I32), tile_valid=tile_valid, tile_first=tile_first,
                tile_next=tile_next.astype(I32), n_valid=jnp.sum(tile_valid)[None].astype(I32),
                tile_slot=tile_slot.astype(I32),
                tail_first=tail_first.astype(I32),
                tail_chunks=tail_chunks.astype(I32))


def _sorted_rows(n_tokens, tm):
    n_t = n_tokens // tm
    bound = 2 * n_tokens + n_t * MOE_EXPERTS * (ROW_ALIGN - 1) + MOE_EXPERTS * (tm - ROW_ALIGN) + 1
    return (bound + tm - 1) // tm * tm


def _chunk_copy_out(buf, slot, c, row, hbm, sem):
    src = buf.at[slot, pl.ds(pl.multiple_of(c * ROW_ALIGN, ROW_ALIGN), ROW_ALIGN)]
    return pltpu.make_async_copy(src, hbm.at[pl.ds(pl.multiple_of(row, ROW_ALIGN), ROW_ALIGN)], sem.at[slot])


def _chunk_copy_in(hbm, row, buf, slot, c, sem):
    dst = buf.at[slot, pl.ds(pl.multiple_of(c * ROW_ALIGN, ROW_ALIGN), ROW_ALIGN)]
    src = hbm.at[pl.ds(pl.multiple_of(row, ROW_ALIGN), ROW_ALIGN), pl.ds(0, buf.shape[2])]
    return pltpu.make_async_copy(src, dst, sem.at[slot])


def _moe_dispatch_kernel(nch_ref, crow_ref, ptab_ref, tfirst_ref, tchunks_ref, tvalid_ref,
                         x_ref, rt_ref, r_ref, xs_hbm, buf, zeros, sem, sem_z, *, tm, n_tiles):
    i = pl.program_id(0)
    n_t = pl.num_programs(0)
    slot = i % 2

    def wait_tile(tile, s):
        def body(c, carry):
            _chunk_copy_out(buf, s, 0, 0, xs_hbm, sem).wait()
            return carry
        lax.fori_loop(0, nch_ref[tile], body, 0)

    def zero_tail(e, k):
        return pltpu.make_async_copy(
            zeros.at[pl.ds(0, ROW_ALIGN)],
            xs_hbm.at[pl.ds(pl.multiple_of(tfirst_ref[e] + k * ROW_ALIGN, ROW_ALIGN), ROW_ALIGN)], sem_z.at[0])

    def zero_tile(t):
        return pltpu.make_async_copy(zeros, xs_hbm.at[pl.ds(pl.multiple_of(t * tm, tm), tm)], sem_z.at[1])

    def for_each_fill(tail_fn, tile_fn):
        def per_expert(e, carry):
            def per_chunk(k, c2):
                tail_fn(e, k)
                return c2
            lax.fori_loop(0, tchunks_ref[e], per_chunk, 0)
            return carry
        lax.fori_loop(0, MOE_EXPERTS, per_expert, 0)

        def per_tile(t, carry):
            @pl.when(tvalid_ref[t] == 0)
            def _():
                tile_fn(t)
            return carry
        lax.fori_loop(0, n_tiles, per_tile, 0)

    @pl.when(i == 0)
    def _():
        zeros[...] = jnp.zeros(zeros.shape, zeros.dtype)
        for_each_fill(lambda e, k: zero_tail(e, k).start(), lambda t: zero_tile(t).start())

    @pl.when(i >= 2)
    def _():
        wait_tile(i - 2, slot)

    pos1 = rt_ref[4:5, :].astype(I32)
    pos2 = rt_ref[5:6, :].astype(I32)
    e1 = rt_ref[0:1, :].astype(I32)
    e2 = rt_ref[1:2, :].astype(I32)
    for e in range(MOE_EXPERTS):
        shift = ptab_ref[i * MOE_EXPERTS + e]
        pos1 = pos1 + jnp.where(e1 == e, shift, 0)
        pos2 = pos2 + jnp.where(e2 == e, shift, 0)
    lane = lax.broadcasted_iota(I32, (tm, LANES), 1)

    def gate_terms(g):
        hi = g.astype(BF16).astype(F32)
        return jnp.where(lane == 0, hi, jnp.where(lane == 1, g - hi, 0.0)).astype(BF16)

    gate1 = gate_terms(r_ref[:, 2:3])
    gate2 = gate_terms(r_ref[:, 3:4])
    blk = 256
    for k in range(TILE_ROWS // blk):
        row = lax.broadcasted_iota(I32, (blk, tm), 0) + k * blk
        pick1 = jnp.where(row == pos1, 1.0, 0.0).astype(BF16)
        pick2 = jnp.where(row == pos2, 1.0, 0.0).astype(BF16)
        at = slice(k * blk, (k + 1) * blk)
        buf[slot, at, :D_MODEL] = jnp.dot(pick1 + pick2, x_ref[...], preferred_element_type=F32).astype(BF16)
        buf[slot, at, D_MODEL:] = (jnp.dot(pick1, gate1, preferred_element_type=F32)
                                   + jnp.dot(pick2, gate2, preferred_element_type=F32)).astype(BF16)

    def send(c, carry):
        _chunk_copy_out(buf, slot, c, crow_ref[i * TILE_CHUNKS + c], xs_hbm, sem).start()
        return carry
    lax.fori_loop(0, nch_ref[i], send, 0)

    @pl.when(i == n_t - 1)
    def _():
        @pl.when(i >= 1)
        def _():
            wait_tile(i - 1, 1 - slot)
        wait_tile(i, slot)
        for_each_fill(lambda e, k: zero_tail(e, k).wait(), lambda t: zero_tile(t).wait())


def _moe_dispatch(x_bf, route_t, route, tables, tm):
    m, d = x_bf.shape
    n_rows = _sorted_rows(m, tm)
    grid_spec = pltpu.PrefetchScalarGridSpec(
        num_scalar_prefetch=6,
        grid=(m // tm,),
        in_specs=[pl.BlockSpec((tm, d), lambda i, *_: (i, 0)),
                  pl.BlockSpec((8, tm), lambda i, *_: (0, i)),
                  pl.BlockSpec((tm, LANES), lambda i, *_: (i, 0))],
        out_specs=pl.BlockSpec(memory_space=pl.ANY),
        scratch_shapes=[pltpu.VMEM((2, TILE_ROWS, SORTED_WIDTH), BF16), pltpu.VMEM((tm, SORTED_WIDTH), BF16),
                        pltpu.SemaphoreType.DMA((2,)), pltpu.SemaphoreType.DMA((2,))],
    )
    return pl.pallas_call(
        functools.partial(_moe_dispatch_kernel, tm=tm, n_tiles=n_rows // tm),
        name="moe_dispatch",
        grid_spec=grid_spec,
        out_shape=jax.ShapeDtypeStruct((n_rows, SORTED_WIDTH), BF16),
        compiler_params=_params(("arbitrary",)),
    )(tables["n_chunks"], tables["chunk_row"], tables["pos_tab"], tables["tail_first"], tables["tail_chunks"],
      tables["tile_valid"], x_bf, route_t, route)


def _moe_ffn_kernel(te_ref, tv_ref, tf_ref, tn_ref, nv_ref, ts_ref, x_ref, wg_hbm, wu_hbm, wd_hbm, o_ref,
                    wg_f, wu_f, wd_f, wg_bf, wu_bf, wd_bf, sem_w, *, layer):
    j = pl.program_id(0)
    valid = tv_ref[j] == 1
    slot = ts_ref[j]

    def weight_copies(e, s):
        return (pltpu.make_async_copy(wg_hbm.at[layer, e], wg_f.at[s], sem_w.at[s, 0]),
                pltpu.make_async_copy(wu_hbm.at[layer, e], wu_f.at[s], sem_w.at[s, 1]),
                pltpu.make_async_copy(wd_hbm.at[layer, e], wd_f.at[s], sem_w.at[s, 2]))

    @pl.when(j == 0)
    def _():
        for c in weight_copies(te_ref[0], slot):
            c.start()

    @pl.when(valid & (tf_ref[j] == 1))
    def _():
        @pl.when(tn_ref[j] >= 0)
        def _():
            for c in weight_copies(tn_ref[j], 1 - slot):
                c.start()

        for c in weight_copies(te_ref[j], slot):
            c.wait()
        _cast_rows(wg_f.at[slot], wg_bf)
        _cast_rows(wu_f.at[slot], wu_bf)
        _cast_rows(wd_f.at[slot], wd_bf)

    @pl.when(valid)
    def _():
        xb = x_ref[:, :D_MODEL]
        gate = x_ref[:, D_MODEL:D_MODEL + 1].astype(F32) + x_ref[:, D_MODEL + 1:D_MODEL + 2].astype(F32)
        hg = jnp.dot(xb, wg_bf[...], preferred_element_type=F32)
        hu = jnp.dot(xb, wu_bf[...], preferred_element_type=F32)
        act = (hg * (1.0 / (1.0 + jnp.exp(-hg))) * hu).astype(BF16)
        o_ref[...] = (jnp.dot(act, wd_bf[...], preferred_element_type=F32) * gate).astype(o_ref.dtype)


def _moe_ffn(x_sorted, w_gate, w_up, w_down, layer, tables, tm):
    n_rows, width = x_sorted.shape
    d, f = w_gate.shape[2], w_gate.shape[3]
    any_spec = pl.BlockSpec(memory_space=pl.ANY)
    grid_spec = pltpu.PrefetchScalarGridSpec(
        num_scalar_prefetch=6,
        grid=(n_rows // tm,),
        in_specs=[pl.BlockSpec((tm, width), lambda j, te, tv, tf, tn, nv, ts: (jnp.minimum(j, nv[0] - 1), 0)),
                  any_spec, any_spec, any_spec],
        out_specs=pl.BlockSpec((tm, d), lambda j, te, tv, tf, tn, nv, ts: (jnp.minimum(j, nv[0] - 1), 0)),
        scratch_shapes=[pltpu.VMEM((2, d, f), F32), pltpu.VMEM((2, d, f), F32), pltpu.VMEM((2, f, d), F32),
                        pltpu.VMEM((d, f), BF16), pltpu.VMEM((d, f), BF16), pltpu.VMEM((f, d), BF16),
                        pltpu.SemaphoreType.DMA((2, 3))],
    )
    return pl.pallas_call(
        functools.partial(_moe_ffn_kernel, layer=layer),
        name="moe_ffn",
        grid_spec=grid_spec,
        out_shape=jax.ShapeDtypeStruct((n_rows, width), BF16),
        input_output_aliases={6: 0},
        compiler_params=_params(("arbitrary",)),
    )(tables["tile_expert"], tables["tile_valid"], tables["tile_first"], tables["tile_next"], tables["n_valid"],
      tables["tile_slot"], x_sorted, w_gate, w_up, w_down)


def _moe_combine_kernel(nch_ref, crow_ref, ys_hbm, x_ref, r_ref, ptab_ref, g_ref, b_ref, o_ref, obf_ref,
                        buf, sem, *, tm):
    i = pl.program_id(0)
    n_t = pl.num_programs(0)
    slot = i % 2

    def fetch(tile, s):
        def body(c, carry):
            _chunk_copy_in(ys_hbm, crow_ref[tile * TILE_CHUNKS + c], buf, s, c, sem).start()
            return carry
        lax.fori_loop(0, nch_ref[tile], body, 0)

    @pl.when(i == 0)
    def _():
        buf[...] = jnp.zeros(buf.shape, buf.dtype)
        fetch(0, 0)

    @pl.when(i + 1 < n_t)
    def _():
        fetch(i + 1, 1 - slot)

    def arrived(c, carry):
        _chunk_copy_in(ys_hbm, 0, buf, slot, 0, sem).wait()
        return carry
    lax.fori_loop(0, nch_ref[i], arrived, 0)

    lane = lax.broadcasted_iota(I32, (tm, LANES), 1)
    shift = ptab_ref[...]
    e1 = r_ref[:, 0:1].astype(I32)
    e2 = r_ref[:, 1:2].astype(I32)
    pos1 = (r_ref[:, 4:5] + jnp.sum(jnp.where(lane == e1, shift, 0.0), axis=1, keepdims=True)).astype(I32)
    pos2 = (r_ref[:, 5:6] + jnp.sum(jnp.where(lane == e2, shift, 0.0), axis=1, keepdims=True)).astype(I32)
    col = lax.broadcasted_iota(I32, (tm, TILE_ROWS), 1)
    pick = jnp.where((col == pos1) | (col == pos2), 1.0, 0.0).astype(BF16)
    y = jnp.dot(pick, buf[slot], preferred_element_type=F32)
    out = _layernorm_rows(DEEPNORM_ALPHA * x_ref[...] + y, g_ref[...], b_ref[...])
    o_ref[...] = out
    obf_ref[...] = out.astype(BF16)


def _moe_combine(y_sorted, x, route, tables, g, b, tm, out_batch=None):
    m, d = x.shape
    out_f32 = jax.ShapeDtypeStruct((m, d), F32)
    out_spec = pl.BlockSpec((tm, d), lambda i, *_: (i, 0))
    if out_batch:
        per = m // out_batch // tm
        out_f32 = jax.ShapeDtypeStruct((out_batch, m // out_batch, d), F32)
        out_spec = pl.BlockSpec((None, tm, d), lambda i, *_: (i // per, i % per, 0))
    grid_spec = pltpu.PrefetchScalarGridSpec(
        num_scalar_prefetch=2,
        grid=(m // tm,),
        in_specs=[pl.BlockSpec(memory_space=pl.ANY),
                  pl.BlockSpec((tm, d), lambda i, *_: (i, 0)),
                  pl.BlockSpec((tm, LANES), lambda i, *_: (i, 0)),
                  pl.BlockSpec((None, 1, LANES), lambda i, *_: (i, 0, 0)),
                  pl.BlockSpec((1, d), lambda i, *_: (0, 0)),
                  pl.BlockSpec((1, d), lambda i, *_: (0, 0))],
        out_specs=[out_spec, pl.BlockSpec((tm, d), lambda i, *_: (i, 0))],
        scratch_shapes=[pltpu.VMEM((2, TILE_ROWS, d), BF16), pltpu.SemaphoreType.DMA((2,))],
    )
    pos_rows = _pad_cols(tables["pos_rows"])[:, None, :]
    return pl.pallas_call(
        functools.partial(_moe_combine_kernel, tm=tm),
        name="moe_combine",
        grid_spec=grid_spec,
        out_shape=[out_f32, jax.ShapeDtypeStruct((m, d), BF16)],
        compiler_params=_params(("arbitrary",)),
    )(tables["n_chunks"], tables["chunk_row"], y_sorted, x, route, pos_rows, g, b)


def _route_weights(w_group, b_group, w_router, b_router):
    w = _pad_cols(jnp.concatenate([w_group, w_router], axis=1))
    w_hi = w.astype(BF16)
    w_mid = (w - w_hi.astype(F32)).astype(BF16)
    return w_hi, w_mid, _pad_cols(jnp.concatenate([b_group, b_router])[None, :])


def _moe_layer(x, x_bf, route, route_t, before, counts, w_gate, w_up, w_down, layer, ln_g, ln_b, out_batch=None):
    n_tokens = x.shape[0]
    tm = min(MOE_TM, n_tokens)
    tables = _dispatch_tables(before, counts, n_tokens, tm)
    x_sorted = _moe_dispatch(x_bf, route_t, route, tables, tm)
    y_sorted = _moe_ffn(x_sorted, w_gate, w_up, w_down, layer, tables, tm)
    return _moe_combine(y_sorted, x, route, tables, ln_g[None, :], ln_b[None, :], tm, out_batch)


def kernel(x, t5_rel_bias, mla_w_in, mla_q_norm, mla_kv_norm, mla_w_q_up, mla_w_kv_up, mla_w_o, fox_w_in, fox_b_f, fox_w_o, dsa_w_in, dsa_w_o, ln_g, ln_b, moe_w_group, moe_b_group, moe_w_router, moe_b_router, moe_w_gate, moe_w_up, moe_w_down):
    batch, seq, d = x.shape
    x_bf = x
    for layer in range(DEPTH):
        kind = layer % N_MIXERS
        j = layer // N_MIXERS
        g0, b0 = ln_g[layer, 0], ln_b[layer, 0]
        route_wb = _route_weights(moe_w_group[layer], moe_b_group[layer], moe_w_router[layer], moe_b_router[layer])
        if kind == 0:
            mixed = _mla_layer(x, x_bf, mla_w_in[j], mla_q_norm[j], mla_kv_norm[j], mla_w_q_up[j],
                               mla_w_kv_up, j, mla_w_o[j], g0, b0, route_wb, batch, seq)
        elif kind == 1:
            mixed = _fox_layer(x, x_bf, fox_w_in, j, fox_b_f[j], fox_w_o[j], g0, b0, route_wb, batch, seq)
        else:
            mixed = _dsa_layer(x, x_bf, t5_rel_bias, dsa_w_in, j, dsa_w_o[j], g0, b0, route_wb, batch, seq)
        x, x_bf = _moe_layer(*mixed, moe_w_gate, moe_w_up, moe_w_down, layer, ln_g[layer, 1], ln_b[layer, 1],
                             batch if layer == DEPTH - 1 else None)
    return x
```

```python
import functools
import math

import numpy as np
import jax
import jax.numpy as jnp
from jax import lax
from jax.experimental import pallas as pl
from jax.experimental.pallas import tpu as pltpu

F32 = jnp.float32
BF16 = jnp.bfloat16
I32 = jnp.int32

D_MODEL = 2048
DEPTH = 4
CHUNK = 64
N_MIXERS = 3
DEEPNORM_ALPHA = (2.0 * DEPTH) ** 0.25

HEADS = 16
HEAD_DIM = 128
MLA_NOPE_DIM = 128
MLA_ROPE_DIM = 64
MLA_Q_RANK = 512
MLA_KV_RANK = 512
MLA_QK_PAD = 256
ROPE_THETA = 10000.0

IDX_HEADS = 16
IDX_DIM = 64
IDX_TOPK_MAX = 256

T5_BUCKETS = 32
T5_MAX_DISTANCE = 128
T5_FAR_BUCKET = T5_BUCKETS // 2 - 1
T5_BAND_BACK = 128

MOE_GROUPS = 4
MOE_EXPERTS_PER_GROUP = 8
MOE_EXPERTS = MOE_GROUPS * MOE_EXPERTS_PER_GROUP
MOE_D_FF = 512

LN_EPS = 1e-5
RMS_EPS = 1e-6
NEG_INF = -1e30

LANES = 128
VMEM_LIMIT = 56 * 1024 * 1024

ATTN_TQ = 256
MOE_TM = 256
ROW_ALIGN = 16
TILE_ROWS = -(-(2 * MOE_TM + MOE_EXPERTS * (ROW_ALIGN - 1)) // 256) * 256
TILE_CHUNKS = TILE_ROWS // ROW_ALIGN
SORTED_WIDTH = D_MODEL + LANES


def _params(semantics, vmem=VMEM_LIMIT):
    return pltpu.CompilerParams(dimension_semantics=semantics, vmem_limit_bytes=vmem)


def _cast_rows(src_ref, dst_ref):
    rows = 16
    while 2 * rows * src_ref.shape[1] <= 32 * 8 * LANES and src_ref.shape[0] % (2 * rows) == 0:
        rows *= 2

    def body(i, carry):
        at = pl.ds(pl.multiple_of(i * rows, rows), rows)
        dst_ref[at, :] = src_ref[at, :].astype(dst_ref.dtype)
        return carry
    lax.fori_loop(0, src_ref.shape[0] // rows, body, 0, unroll=2)


def _mm_kernel(a_ref, w_ref, o_ref, w_bf, *, scale, scaled_cols):
    @pl.when(pl.program_id(1) == 0)
    def _():
        _cast_rows(w_ref, w_bf)

    acc = lax.dot_general(a_ref[...], w_bf[...], (((1,), (1,)), ((), ())), preferred_element_type=F32)
    if scaled_cols:
        tn = o_ref.shape[1]
        col = pl.program_id(0) * tn + lax.broadcasted_iota(I32, (1, tn), 1)
        acc = acc * jnp.where(col < scaled_cols, scale, 1.0)
    o_ref[...] = acc.astype(o_ref.dtype)


def _matmul(a, w_t, layer, *, n_out, tn, tm=1024, scale=1.0, scaled_cols=0, out_dtype=BF16):
    m, k = a.shape
    tm = min(tm, m)
    kern = functools.partial(_mm_kernel, scale=scale, scaled_cols=scaled_cols)
    return pl.pallas_call(
        kern,
        name="proj_mm",
        grid=(n_out // tn, m // tm),
        in_specs=[pl.BlockSpec((tm, k), lambda j, i: (i, 0)),
                  pl.BlockSpec((None, tn, k), lambda j, i: (layer, j, 0))],
        out_specs=pl.BlockSpec((tm, tn), lambda j, i: (i, j)),
        out_shape=jax.ShapeDtypeStruct((m, n_out), out_dtype),
        scratch_shapes=[pltpu.VMEM((tn, k), BF16)],
        compiler_params=_params(("arbitrary", "arbitrary")),
    )(a, w_t)


def _split3(v):
    hi = v.astype(BF16)
    r1 = v - hi.astype(F32)
    mid = r1.astype(BF16)
    lo = (r1 - mid.astype(F32)).astype(BF16)
    return hi, mid, lo


def _dot_precise(x, w):
    xh = x.astype(BF16)
    xm = (x - xh.astype(F32)).astype(BF16)
    wh = w.astype(BF16)
    wm = (w - wh.astype(F32)).astype(BF16)
    d = lambda p, q: jnp.dot(p, q, preferred_element_type=F32)
    return (d(xm, wh) + d(xh, wm)) + d(xh, wh)


def _small_mm_kernel(x_ref, w_ref, b_ref, o_ref):
    o_ref[...] = _dot_precise(x_ref[...], w_ref[...]) + b_ref[...]


def _small_matmul(x, w, b, tm=512):
    m, k = x.shape
    tm = min(tm, m)
    return pl.pallas_call(
        _small_mm_kernel,
        name="small_mm",
        grid=(m // tm,),
        in_specs=[pl.BlockSpec((tm, k), lambda i: (i, 0)),
                  pl.BlockSpec((k, LANES), lambda i: (0, 0)),
                  pl.BlockSpec((1, LANES), lambda i: (0, 0))],
        out_specs=pl.BlockSpec((tm, LANES), lambda i: (i, 0)),
        out_shape=jax.ShapeDtypeStruct((m, LANES), F32),
        compiler_params=_params(("arbitrary",)),
    )(x, w, b)


def _rows_spec(arr, tm):
    d = arr.shape[-1]
    if arr.ndim == 2:
        return pl.BlockSpec((tm, d), lambda i: (i, 0))
    per = arr.shape[1] // tm
    return pl.BlockSpec((None, tm, d), lambda i: (i // per, i % per, 0))


def _pad_cols(w, width=LANES):
    return jnp.pad(w, ((0, 0), (0, width - w.shape[1])))


def _layernorm_rows(z, g, b):
    mu = jnp.mean(z, axis=-1, keepdims=True)
    zc = z - mu
    var = jnp.mean(zc * zc, axis=-1, keepdims=True)
    return zc * lax.rsqrt(var + LN_EPS) * g + b


def _route_tile(x, w_hi, w_mid, b, run_ref):
    tm = x.shape[0]
    x_hi = x.astype(BF16)
    x_mid = (x - x_hi.astype(F32)).astype(BF16)
    d = lambda p, q: jnp.dot(p, q, preferred_element_type=F32)
    logit = (d(x_mid, w_hi) + d(x_hi, w_mid)) + d(x_hi, w_hi) + b
    lane = lax.broadcasted_iota(I32, logit.shape, 1)
    big = jnp.int32(LANES)

    def first_argmax(mask):
        top = jnp.max(jnp.where(mask, logit, -jnp.inf), axis=1, keepdims=True)
        idx = jnp.min(jnp.where(mask & (logit == top), lane, big), axis=1, keepdims=True)
        return top, idx

    is_group = lane < MOE_GROUPS
    g_top, g_sel = first_argmax(is_group)
    p_g = 1.0 / jnp.sum(jnp.where(is_group, jnp.exp(logit - g_top), 0.0), axis=1, keepdims=True)
    lo = MOE_GROUPS + g_sel * MOE_EXPERTS_PER_GROUP
    in_group = (lane >= lo) & (lane < lo + MOE_EXPERTS_PER_GROUP)
    v1, i1 = first_argmax(in_group)
    v2, i2 = first_argmax(in_group & (lane != i1))
    ex = jnp.exp(v2 - v1)
    w1 = 1.0 / (1.0 + ex)
    w2 = ex / (1.0 + ex)
    e1 = i1 - MOE_GROUPS
    e2 = i2 - MOE_GROUPS
    pick1 = lane == e1
    pick2 = lane == e2
    both = jnp.where(pick1 | pick2, 1.0, 0.0)
    earlier = lax.broadcasted_iota(I32, (tm, tm), 1) < lax.broadcasted_iota(I32, (tm, tm), 0)
    start = run_ref[...]
    before = jnp.dot(jnp.where(earlier, 1.0, 0.0).astype(BF16), both.astype(BF16),
                     preferred_element_type=F32) + start
    r1 = jnp.sum(jnp.where(pick1, before, 0.0), axis=1, keepdims=True)
    r2 = jnp.sum(jnp.where(pick2, before, 0.0), axis=1, keepdims=True)
    run_ref[...] = start + jnp.sum(both, axis=0, keepdims=True)
    out = jnp.where(lane == 0, e1.astype(F32), 0.0)
    out = jnp.where(lane == 1, e2.astype(F32), out)
    out = jnp.where(lane == 2, w1 * p_g, out)
    out = jnp.where(lane == 3, w2 * p_g, out)
    out = jnp.where(lane == 4, r1, out)
    out = jnp.where(lane == 5, r2, out)
    return out, start


def _mm_res_ln_kernel(a_ref, w_ref, x_ref, g_ref, b_ref, wrh_ref, wrm_ref, br_ref,
                      o_ref, obf_ref, r_ref, rt_ref, before_ref, cnt_ref, run_ref, *, sub):
    @pl.when(pl.program_id(0) == 0)
    def _():
        run_ref[...] = jnp.zeros(run_ref.shape, F32)

    y = jnp.dot(a_ref[...], w_ref[...], preferred_element_type=F32)
    out = _layernorm_rows(DEEPNORM_ALPHA * x_ref[...] + y, g_ref[...], b_ref[...])
    o_ref[...] = out
    obf_ref[...] = out.astype(BF16)
    for k in range(out.shape[0] // sub):
        rows = slice(k * sub, (k + 1) * sub)
        slab, before = _route_tile(out[rows, :], wrh_ref[...], wrm_ref[...], br_ref[...], run_ref)
        r_ref[rows, :] = slab
        rt_ref[:, rows] = slab.T[:8, :]
        before_ref[k] = before
    cnt_ref[...] = run_ref[...]


def _mm_res_ln(a, w, x, g, b, w_route_hi, w_route_mid, b_route, tm=512):
    m, k = a.shape
    d = w.shape[1]
    tm = min(tm, m)
    sub = min(MOE_TM, tm)
    per = tm // sub
    return pl.pallas_call(
        functools.partial(_mm_res_ln_kernel, sub=sub),
        name="out_proj_res_ln",
        grid=(m // tm,),
        in_specs=[pl.BlockSpec((tm, k), lambda i: (i, 0)),
                  pl.BlockSpec((k, d), lambda i: (0, 0)),
                  _rows_spec(x, tm),
                  pl.BlockSpec((1, d), lambda i: (0, 0)),
                  pl.BlockSpec((1, d), lambda i: (0, 0)),
                  pl.BlockSpec((d, LANES), lambda i: (0, 0)),
                  pl.BlockSpec((d, LANES), lambda i: (0, 0)),
                  pl.BlockSpec((1, LANES), lambda i: (0, 0))],
        out_specs=[pl.BlockSpec((tm, d), lambda i: (i, 0)),
                   pl.BlockSpec((tm, d), lambda i: (i, 0)),
                   pl.BlockSpec((tm, LANES), lambda i: (i, 0)),
                   pl.BlockSpec((8, tm), lambda i: (0, i)),
                   pl.BlockSpec((per, 1, LANES), lambda i: (i, 0, 0)),
                   pl.BlockSpec((1, LANES), lambda i: (0, 0))],
        out_shape=[jax.ShapeDtypeStruct((m, d), F32), jax.ShapeDtypeStruct((m, d), BF16),
                   jax.ShapeDtypeStruct((m, LANES), F32), jax.ShapeDtypeStruct((8, m), F32),
                   jax.ShapeDtypeStruct((m // sub, 1, LANES), F32), jax.ShapeDtypeStruct((1, LANES), F32)],
        scratch_shapes=[pltpu.VMEM((1, LANES), F32)],
        compiler_params=_params(("arbitrary",)),
    )(a, w, x, g, b, w_route_hi, w_route_mid, b_route)


def _rope_tables(seq):
    half = MLA_ROPE_DIM // 2
    inv_freq = ROPE_THETA ** (-jnp.arange(half, dtype=F32) / half)
    ang = jnp.arange(seq, dtype=I32).astype(F32)[:, None] * inv_freq[None, :]
    cos, sin = jnp.cos(ang), jnp.sin(ang)
    z = jnp.zeros_like(cos)
    cos_t = jnp.concatenate([cos, cos, z, z], axis=1)
    sin_a = jnp.concatenate([-sin, z, z, z], axis=1)
    sin_b = jnp.concatenate([z, sin, z, z], axis=1)
    return cos_t, sin_a, sin_b


def _rope_lanes(r, cos_t, sin_a, sin_b):
    from_right = pltpu.roll(r, LANES - MLA_ROPE_DIM // 2, 1)
    from_left = pltpu.roll(r, MLA_ROPE_DIM // 2, 1)
    return r * cos_t + from_right * sin_a + from_left * sin_b


def _rms_rows(v, g):
    return v * lax.rsqrt(jnp.mean(v * v, axis=-1, keepdims=True) + RMS_EPS) * g


def _mla_in_kernel(x_ref, w_ref, qn_ref, kvn_ref, cos_ref, sina_ref, sinb_ref, cq_ref, ckv_ref, kr_ref):
    h = _nt_dot(x_ref[...].astype(BF16), w_ref[...])
    cq_ref[...] = _rms_rows(h[:, :MLA_Q_RANK], qn_ref[...]).astype(BF16)
    ckv_ref[...] = _rms_rows(h[:, MLA_Q_RANK:MLA_Q_RANK + MLA_KV_RANK], kvn_ref[...]).astype(BF16)
    r = h[:, MLA_Q_RANK + MLA_KV_RANK:]
    kr_ref[...] = _rope_lanes(r, cos_ref[...], sina_ref[...], sinb_ref[...]).astype(BF16)


def _mla_in(x_bf, w_in_pad, q_norm, kv_norm, tables, seq, tm=512):
    m, k = math.prod(x_bf.shape[:-1]), x_bf.shape[-1]
    tm = min(tm, seq)
    nw = w_in_pad.shape[0]
    per_seq = seq // tm
    tab_spec = pl.BlockSpec((tm, LANES), lambda i: (i % per_seq, 0))
    return pl.pallas_call(
        _mla_in_kernel,
        name="mla_in",
        grid=(m // tm,),
        in_specs=[_rows_spec(x_bf, tm),
                  pl.BlockSpec((nw, k), lambda i: (0, 0)),
                  pl.BlockSpec((1, MLA_Q_RANK), lambda i: (0, 0)),
                  pl.BlockSpec((1, MLA_KV_RANK), lambda i: (0, 0)),
                  tab_spec, tab_spec, tab_spec],
        out_specs=[pl.BlockSpec((tm, MLA_Q_RANK), lambda i: (i, 0)),
                   pl.BlockSpec((tm, MLA_KV_RANK), lambda i: (i, 0)),
                   pl.BlockSpec((tm, LANES), lambda i: (i, 0))],
        out_shape=[jax.ShapeDtypeStruct((m, MLA_Q_RANK), BF16),
                   jax.ShapeDtypeStruct((m, MLA_KV_RANK), BF16),
                   jax.ShapeDtypeStruct((m, LANES), BF16)],
        compiler_params=_params(("arbitrary",)),
    )(x_bf, w_in_pad, q_norm, kv_norm, *tables)


def _values_and_ones(v_ref, rows):
    v = v_ref[rows, :]
    return jnp.concatenate([v, jnp.ones(v.shape, BF16)], axis=1)


def _softmax_pv(pieces):
    m = None
    for s, _ in pieces:
        pm = jnp.max(s, axis=1, keepdims=True)
        m = pm if m is None else jnp.maximum(m, pm)
    acc = None
    for s, v in pieces:
        t = jnp.dot(jnp.exp((s - m).astype(BF16)), v, preferred_element_type=F32)
        acc = t if acc is None else acc + t
    return acc[:, :HEAD_DIM] / acc[:, HEAD_DIM:HEAD_DIM + 1]


def _nt_dot(q, k):
    return lax.dot_general(q, k, (((1,), (1,)), ((), ())), preferred_element_type=F32)


def _attend_tiles(n_tiles, scores, finish):
    s_next = scores(0)
    for i in range(n_tiles):
        s = s_next
        if i + 1 < n_tiles:
            s_next = scores(i + 1)
        finish(i, s)


def _mla_q_up_kernel(c_ref, w_ref, cos_ref, sina_ref, sinb_ref, o_ref, w_bf, *, scale):
    @pl.when(pl.program_id(1) == 0)
    def _():
        _cast_rows(w_ref, w_bf)

    acc = jnp.dot(c_ref[...], w_bf[...], preferred_element_type=F32) * scale
    for h in range(o_ref.shape[1] // MLA_QK_PAD):
        at = h * MLA_QK_PAD
        o_ref[:, at:at + MLA_NOPE_DIM] = acc[:, at:at + MLA_NOPE_DIM].astype(BF16)
        r = acc[:, at + MLA_NOPE_DIM:at + MLA_QK_PAD]
        o_ref[:, at + MLA_NOPE_DIM:at + MLA_QK_PAD] = _rope_lanes(
            r, cos_ref[...], sina_ref[...], sinb_ref[...]).astype(BF16)


def _mla_kv_up_kernel(c_ref, w_ref, kr_ref, k_ref, v_ref, w_bf):
    @pl.when(pl.program_id(1) == 0)
    def _():
        _cast_rows(w_ref, w_bf)

    acc = jnp.dot(c_ref[...], w_bf[...], preferred_element_type=F32)
    for h in range(k_ref.shape[1] // MLA_QK_PAD):
        at = h * MLA_QK_PAD
        k_ref[:, at:at + MLA_NOPE_DIM] = acc[:, at:at + MLA_NOPE_DIM].astype(BF16)
        k_ref[:, at + MLA_NOPE_DIM:at + MLA_QK_PAD] = kr_ref[...]
        v_ref[:, h * HEAD_DIM:(h + 1) * HEAD_DIM] = acc[:, at + MLA_NOPE_DIM:at + MLA_QK_PAD].astype(BF16)


def _mla_up(cq, ckv, kr, w_q_up, w_kv_up, j, tables, seq, scale, tm=1024, tn=1024):
    m, rank = cq.shape
    tm = min(tm, seq)
    per_seq = seq // tm
    n = HEADS * MLA_QK_PAD
    tab_spec = pl.BlockSpec((tm, LANES), lambda j, i: (i % per_seq, 0))
    row_spec = pl.BlockSpec((tm, rank), lambda j, i: (i, 0))
    w_spec = pl.BlockSpec((rank, tn), lambda j, i: (0, j))
    out_spec = pl.BlockSpec((tm, tn), lambda j, i: (i, j))
    out = jax.ShapeDtypeStruct((m, n), BF16)
    q = pl.pallas_call(
        functools.partial(_mla_q_up_kernel, scale=scale),
        name="mla_q_up",
        grid=(n // tn, m // tm),
        in_specs=[row_spec, w_spec, tab_spec, tab_spec, tab_spec],
        out_specs=out_spec,
        out_shape=out,
        scratch_shapes=[pltpu.VMEM((rank, tn), BF16)],
        compiler_params=_params(("arbitrary", "arbitrary")),
    )(cq, w_q_up, *tables)
    k, v = pl.pallas_call(
        _mla_kv_up_kernel,
        name="mla_kv_up",
        grid=(n // tn, m // tm),
        in_specs=[row_spec, pl.BlockSpec((None, rank, tn), lambda c, i: (j, 0, c)),
                  pl.BlockSpec((tm, LANES), lambda c, i: (i, 0))],
        out_specs=[out_spec, pl.BlockSpec((tm, tn // 2), lambda c, i: (i, c))],
        out_shape=[out, jax.ShapeDtypeStruct((m, n // 2), BF16)],
        scratch_shapes=[pltpu.VMEM((rank, tn), BF16)],
        compiler_params=_params(("arbitrary", "arbitrary")),
    )(ckv, w_kv_up, kr)
    return q, k, v


def _mla_attn_kernel(q_ref, k_ref, v_ref, o_ref, *, tq):
    seq = q_ref.shape[0]
    row = lax.broadcasted_iota(I32, (tq, tq), 0)
    col = lax.broadcasted_iota(I32, (tq, tq), 1)
    visible = (col // CHUNK) <= (row // CHUNK)

    def scores(i):
        return _nt_dot(q_ref[i * tq:(i + 1) * tq, :], k_ref[:(i + 1) * tq, :])

    def finish(i, s):
        lo, kv = i * tq, (i + 1) * tq
        pieces = [(jnp.where(visible, s[:, lo:], NEG_INF), _values_and_ones(v_ref, slice(lo, kv)))]
        if lo:
            pieces.append((s[:, :lo], _values_and_ones(v_ref, slice(0, lo))))
        o_ref[lo:kv, :] = _softmax_pv(pieces).astype(o_ref.dtype)

    _attend_tiles(seq // tq, scores, finish)


def _mla_attn(q, k, v, batch, seq):
    tq = min(ATTN_TQ, seq)
    head_spec = pl.BlockSpec((seq, MLA_QK_PAD), lambda b, h: (b, h))
    return pl.pallas_call(
        functools.partial(_mla_attn_kernel, tq=tq),
        name="mla_attn",
        grid=(batch, HEADS),
        in_specs=[head_spec, head_spec, pl.BlockSpec((seq, HEAD_DIM), lambda b, h: (b, h))],
        out_specs=pl.BlockSpec((seq, HEAD_DIM), lambda b, h: (b, h)),
        out_shape=jax.ShapeDtypeStruct((batch * seq, HEADS * HEAD_DIM), BF16),
        compiler_params=_params(("arbitrary", "arbitrary")),
    )(q, k, v)


def _mla_q_up_layout(w_q_up):
    rank = w_q_up.shape[0]
    w = w_q_up.reshape(rank, HEADS, MLA_NOPE_DIM + MLA_ROPE_DIM)
    w = jnp.pad(w, ((0, 0), (0, 0), (0, MLA_QK_PAD - MLA_NOPE_DIM - MLA_ROPE_DIM)))
    return w.reshape(rank, HEADS * MLA_QK_PAD)


def _mla_layer(x, x_bf, w_in, q_norm, kv_norm, w_q_up, w_kv_up, j, w_o, ln_g, ln_b, route_wb, batch, seq):
    tables = _rope_tables(seq)
    w_in_pad = jnp.pad(w_in.T, ((0, MLA_Q_RANK + MLA_KV_RANK + LANES - w_in.shape[1]), (0, 0))).astype(BF16)
    cq, ckv, kr = _mla_in(x_bf, w_in_pad, q_norm[None, :], kv_norm[None, :], tables, seq)
    scale = (MLA_NOPE_DIM + MLA_ROPE_DIM) ** -0.5
    q, k, v = _mla_up(cq, ckv, kr, _mla_q_up_layout(w_q_up), w_kv_up, j, tables, seq, scale)
    o = _mla_attn(q, k, v, batch, seq)
    return _mm_res_ln(o, w_o.astype(BF16), x, ln_g[None, :], ln_b[None, :], *route_wb)


def _log_sigmoid(z):
    return jnp.minimum(z, 0.0) - jnp.log1p(jnp.exp(-jnp.abs(z)))


def _fox_gate_kernel(f_ref, c_ref, *, blk):
    seq = f_ref.shape[0]
    tri = (lax.broadcasted_iota(I32, (blk, blk), 1) <= lax.broadcasted_iota(I32, (blk, blk), 0)).astype(BF16)
    carry = jnp.zeros((1, LANES), F32)
    for j in range(seq // blk):
        lf = _log_sigmoid(f_ref[j * blk:(j + 1) * blk, :])
        hi, mid, lo = _split3(lf)
        d = lambda p: jnp.dot(tri, p, preferred_element_type=F32)
        c = (d(lo) + d(mid)) + d(hi) + carry
        c_ref[j * blk:(j + 1) * blk, :] = c
        carry = c[blk - 1:blk, :]


def _fox_gate(f_logit, batch, seq):
    blk = min(256, seq)
    return pl.pallas_call(
        functools.partial(_fox_gate_kernel, blk=blk),
        name="fox_gate",
        grid=(batch,),
        in_specs=[pl.BlockSpec((seq, LANES), lambda b: (b, 0))],
        out_specs=pl.BlockSpec((seq, LANES), lambda b: (b, 0)),
        out_shape=jax.ShapeDtypeStruct((batch * seq, LANES), F32),
        compiler_params=_params(("arbitrary",)),
    )(f_logit)


def _fox_attn_kernel(q_ref, k_ref, v_ref, c_ref, o_ref, qs_ref, ks_ref, *, tq):
    seq = q_ref.shape[0]
    h = pl.program_id(1)
    lane = lax.broadcasted_iota(I32, (seq, LANES), 1)
    c = jnp.sum(jnp.where(lane == h, c_ref[...], 0.0), axis=1, keepdims=True)
    terms = [t.astype(F32) for t in _split3(c)]
    q_side = jnp.where((lane >= 3) & (lane < 6), 1.0, 0.0)
    k_side = jnp.where(lane < 3, 1.0, 0.0)
    for j, t in enumerate(terms):
        q_side = jnp.where(lane == j, t, q_side)
        k_side = jnp.where(lane == 3 + j, -t, k_side)
    qs_ref[...] = q_side.astype(BF16)
    ks_ref[...] = k_side.astype(BF16)
    row = lax.broadcasted_iota(I32, (tq, tq), 0)
    col = lax.broadcasted_iota(I32, (tq, tq), 1)

    def wide(ref, side_ref, rows):
        return jnp.concatenate([ref[rows, :], side_ref[rows, :]], axis=1)

    values = functools.partial(_values_and_ones, v_ref)

    def scores(i):
        return _nt_dot(wide(q_ref, qs_ref, slice(i * tq, (i + 1) * tq)), wide(k_ref, ks_ref, slice(0, (i + 1) * tq)))

    def finish(i, s):
        lo, kv = i * tq, (i + 1) * tq
        pieces = [(jnp.where(col <= row, s[:, lo:], NEG_INF), values(slice(lo, kv)))]
        if lo:
            pieces.append((s[:, :lo], values(slice(0, lo))))
        o_ref[lo:kv, :] = _softmax_pv(pieces).astype(o_ref.dtype)

    _attend_tiles(seq // tq, scores, finish)


def _fox_attn(qkv, c_col, batch, seq):
    tq = min(ATTN_TQ, seq)
    side = pltpu.VMEM((seq, LANES), BF16)
    return pl.pallas_call(
        functools.partial(_fox_attn_kernel, tq=tq),
        name="fox_attn",
        grid=(batch, HEADS),
        in_specs=[pl.BlockSpec((seq, HEAD_DIM), lambda b, h: (b, h)),
                  pl.BlockSpec((seq, HEAD_DIM), lambda b, h: (b, HEADS + h)),
                  pl.BlockSpec((seq, HEAD_DIM), lambda b, h: (b, 2 * HEADS + h)),
                  pl.BlockSpec((seq, LANES), lambda b, h: (b, 0))],
        out_specs=pl.BlockSpec((seq, HEAD_DIM), lambda b, h: (b, h)),
        out_shape=jax.ShapeDtypeStruct((batch * seq, HEADS * HEAD_DIM), BF16),
        scratch_shapes=[side, side],
        compiler_params=_params(("arbitrary", "arbitrary")),
    )(qkv, qkv, qkv, c_col)


def _fox_layer(x, x_bf, w_in, j, b_f, w_o, ln_g, ln_b, route_wb, batch, seq):
    hd = HEADS * HEAD_DIM
    scale = HEAD_DIM ** -0.5
    qkv = _matmul(x_bf, jnp.swapaxes(w_in, 1, 2), j, n_out=3 * hd, tn=1024, scale=scale, scaled_cols=hd)
    f_logit = _small_matmul(x, _pad_cols(w_in[j, :, 3 * hd:]), _pad_cols(b_f[None, :]))
    o = _fox_attn(qkv, _fox_gate(f_logit, batch, seq), batch, seq)
    return _mm_res_ln(o, w_o.astype(BF16), x, ln_g[None, :], ln_b[None, :], *route_wb)


def _t5_bucket_table(tq):
    a = np.arange(tq, dtype=np.int64)[:, None]
    b = np.arange(tq + T5_BAND_BACK, dtype=np.int64)[None, :]
    rel = (b - T5_BAND_BACK) - a
    nb = T5_BUCKETS // 2
    max_exact = nb // 2
    ret = np.where(rel > 0, nb, 0)
    n = np.abs(rel)
    nf = np.maximum(n, 1).astype(np.float32)
    large = max_exact + (np.log(nf / np.float32(max_exact)) / np.float32(math.log(T5_MAX_DISTANCE / max_exact))
                         * np.float32(nb - max_exact)).astype(np.int32)
    large = np.minimum(large, nb - 1)
    return (ret + np.where(n < max_exact, n, large)).astype(np.int32)


def _t5_band_kernel(t5_ref, bucket_ref, o_ref):
    bucket = bucket_ref[...]

    def per_head(h, carry):
        acc = jnp.zeros(bucket.shape, F32)
        for b in range(T5_BUCKETS):
            acc = jnp.where(bucket == b, t5_ref[b, h], acc)
        o_ref[h] = acc - t5_ref[T5_FAR_BUCKET, h]
        return carry

    lax.fori_loop(0, HEADS, per_head, 0)


def _t5_band(t5_bias, tq):
    bucket = jnp.asarray(_t5_bucket_table(tq))
    width = tq + T5_BAND_BACK
    return pl.pallas_call(
        _t5_band_kernel,
        name="t5_band",
        grid=(1,),
        in_specs=[pl.BlockSpec(memory_space=pltpu.SMEM),
                  pl.BlockSpec((tq, width), lambda i: (0, 0))],
        out_specs=pl.BlockSpec((HEADS, tq, width), lambda i: (0, 0, 0)),
        out_shape=jax.ShapeDtypeStruct((HEADS, tq, width), F32),
        compiler_params=_params(("arbitrary",)),
    )(t5_bias, bucket)


def _sortable_key(v):
    bits = pltpu.bitcast(v, I32)
    return bits ^ ((bits >> 31) & 0x7FFFFFFF)


def _kth_largest_key(key_ref, kv, k):
    rows = key_ref.shape[0]

    def count_ge(t):
        t_b = jnp.broadcast_to(t, (rows, LANES))
        acc = jnp.zeros((rows, LANES), I32)
        for c in range(kv // LANES):
            acc = acc + jnp.where(key_ref[:, c * LANES:(c + 1) * LANES] >= t_b, 1, 0)
        return jnp.sum(acc, axis=1, keepdims=True)

    t0 = jnp.where(count_ge(jnp.zeros((rows, 1), I32)) >= k, 0, jnp.iinfo(jnp.int32).min).astype(I32)

    def body(it, t):
        cand = t | (jnp.int32(1) << (30 - it))
        return jnp.where(count_ge(cand) >= k, cand, t)

    return lax.fori_loop(0, 31, body, t0)


def _dsa_index_kernel(qi_ref, tail_ref, o_ref, key_ref, *, tq, topk):
    seq = tail_ref.shape[0]
    k_idx = tail_ref[:, :IDX_DIM].astype(BF16)
    for i in range(seq // tq):
        kv = (i + 1) * tq
        rows = slice(i * tq, kv)
        w = tail_ref[rows, IDX_DIM:IDX_DIM + IDX_HEADS] * ((IDX_HEADS * IDX_DIM) ** -0.5)
        score = jnp.zeros((tq, kv), F32)
        for h in range(IDX_HEADS):
            logits = _nt_dot(qi_ref[rows, h * IDX_DIM:(h + 1) * IDX_DIM], k_idx[:kv, :])
            score = score + jnp.maximum(logits, 0.0) * w[:, h:h + 1]
        row = lax.broadcasted_iota(I32, (tq, kv), 0) + i * tq
        col = lax.broadcasted_iota(I32, (tq, kv), 1)
        admissible = (col // CHUNK) <= (row // CHUNK)
        if kv <= topk:
            keep = admissible
        else:
            key_ref[:, :kv] = _sortable_key(jnp.where(admissible, score, NEG_INF))
            thr = _kth_largest_key(key_ref, kv, topk)
            keep = admissible & (key_ref[:, :kv] >= thr)
        o_ref[rows, :kv] = jnp.where(keep, 0.0, NEG_INF).astype(o_ref.dtype)
        if kv < seq:
            o_ref[rows, kv:] = jnp.full((tq, seq - kv), NEG_INF, o_ref.dtype)


def _dsa_index(q_idx, tail, batch, seq, topk):
    tq = min(ATTN_TQ, seq)
    return pl.pallas_call(
        functools.partial(_dsa_index_kernel, tq=tq, topk=topk),
        name="dsa_index",
        grid=(batch,),
        in_specs=[pl.BlockSpec((seq, IDX_HEADS * IDX_DIM), lambda b: (b, 0)),
                  pl.BlockSpec((seq, LANES), lambda b: (b, 0))],
        out_specs=pl.BlockSpec((seq, seq), lambda b: (b, 0)),
        out_shape=jax.ShapeDtypeStruct((batch * seq, seq), BF16),
        scratch_shapes=[pltpu.VMEM((tq, seq), I32)],
        compiler_params=_params(("arbitrary",)),
    )(q_idx, tail)


def _dsa_attn_kernel(q_ref, k_ref, v_ref, sel_ref, band_ref, o_ref, *, tq):
    seq = k_ref.shape[0]
    width = band_ref.shape[1]
    values = functools.partial(_values_and_ones, v_ref)
    def scores(i):
        return _nt_dot(q_ref[i * tq:(i + 1) * tq, :], k_ref[:(i + 1) * tq, :])

    def finish(i, s):
        lo, kv = i * tq, (i + 1) * tq
        near = min(width, kv)
        far = kv - near
        s = s + sel_ref[lo:kv, :kv].astype(F32)
        pieces = [(s[:, far:] + band_ref[:, width - near:], values(slice(far, kv)))]
        if far:
            pieces.append((s[:, :far], values(slice(0, far))))
        o_ref[lo:kv, :] = _softmax_pv(pieces).astype(o_ref.dtype)

    _attend_tiles(seq // tq, scores, finish)


def _dsa_attn(qkv, sel, band, batch, seq):
    tq = min(ATTN_TQ, seq)
    width = band.shape[2]
    return pl.pallas_call(
        functools.partial(_dsa_attn_kernel, tq=tq),
        name="dsa_attn",
        grid=(batch, HEADS),
        in_specs=[pl.BlockSpec((seq, HEAD_DIM), lambda b, h: (b, h)),
                  pl.BlockSpec((seq, HEAD_DIM), lambda b, h: (b, HEADS)),
                  pl.BlockSpec((seq, HEAD_DIM), lambda b, h: (b, HEADS + 1)),
                  pl.BlockSpec((seq, seq), lambda b, h: (b, 0)),
                  pl.BlockSpec((None, tq, width), lambda b, h: (h, 0, 0))],
        out_specs=pl.BlockSpec((seq, HEAD_DIM), lambda b, h: (b, h)),
        out_shape=jax.ShapeDtypeStruct((batch * seq, HEADS * HEAD_DIM), BF16),
        compiler_params=_params(("arbitrary", "arbitrary")),
    )(qkv, qkv, qkv, sel, band)


def _dsa_layer(x, x_bf, t5_bias, w_in, j, w_o, ln_g, ln_b, route_wb, batch, seq):
    hd = HEADS * HEAD_DIM
    n_qkv = hd + 2 * HEAD_DIM
    n_qi = IDX_HEADS * IDX_DIM
    scale = HEAD_DIM ** -0.5
    w_t = jnp.swapaxes(w_in, 1, 2)
    qkv = _matmul(x_bf, w_t, j, n_out=n_qkv, tn=n_qkv // 3, scale=scale, scaled_cols=hd)
    q_idx = _matmul(x_bf, w_t[j:j + 1, n_qkv:n_qkv + n_qi, :], 0, n_out=n_qi, tn=n_qi)
    tail = _small_matmul(x, _pad_cols(w_in[j, :, n_qkv + n_qi:]), jnp.zeros((1, LANES), F32))
    topk = min(IDX_TOPK_MAX, seq // 4)
    sel = _dsa_index(q_idx, tail, batch, seq, topk)
    band = _t5_band(t5_bias, min(ATTN_TQ, seq))
    o = _dsa_attn(qkv, sel, band, batch, seq)
    return _mm_res_ln(o, w_o.astype(BF16), x, ln_g[None, :], ln_b[None, :], *route_wb)


def _dispatch_tables(before, counts, n_tokens, tm):
    n_t = n_tokens // tm
    before = before[:, 0, :MOE_EXPERTS].astype(I32)
    total = counts[0, :MOE_EXPERTS].astype(I32)
    after = jnp.concatenate([before[1:], total[None, :]], axis=0)
    length = after - before
    length_al = (length + ROW_ALIGN - 1) // ROW_ALIGN * ROW_ALIGN
    t_ids = jnp.arange(n_t, dtype=I32)
    e_ids = jnp.arange(MOE_EXPERTS, dtype=I32)
    before_al = jnp.sum(jnp.where((t_ids[None, :] < t_ids[:, None])[:, :, None], length_al[None, :, :], 0), axis=1)
    total_al = jnp.sum(length_al, axis=0)
    seg = (total_al + tm - 1) // tm * tm
    ends = jnp.sum(jnp.where(e_ids[None, :] <= e_ids[:, None], seg[None, :], 0), axis=1)
    starts = ends - seg
    off = jnp.sum(jnp.where((e_ids[None, :] < e_ids[:, None])[None, :, :], length_al[:, None, :], 0), axis=2)
    used = jnp.sum(length_al, axis=1)
    n_chunks = used // ROW_ALIGN
    dstart = starts[None, :] + before_al
    c_row = jnp.arange(TILE_CHUNKS, dtype=I32) * ROW_ALIGN
    c_exp = jnp.sum(((off + length_al)[:, None, :] <= c_row[None, :, None]).astype(I32), axis=2)
    c_exp = jnp.minimum(c_exp, MOE_EXPERTS - 1)
    onehot = c_exp[:, :, None] == e_ids[None, None, :]
    chunk_row = jnp.sum(jnp.where(onehot, (dstart - off)[:, None, :], 0), axis=2) + c_row[None, :]
    pos_tab = off - before
    n_rows = _sorted_rows(n_tokens, tm)
    n_tiles = n_rows // tm
    tile_start = jnp.arange(n_tiles, dtype=I32) * tm
    tile_expert = jnp.minimum(jnp.sum((ends[None, :] <= tile_start[:, None]).astype(I32), axis=1), MOE_EXPERTS - 1)
    tile_valid = (tile_start < ends[-1]).astype(I32)
    tile_first = jnp.concatenate([jnp.ones((1,), I32), (tile_expert[1:] != tile_expert[:-1]).astype(I32)])
    j_ids = jnp.arange(n_tiles, dtype=I32)
    tile_slot = (jnp.sum(jnp.where(j_ids[None, :] <= j_ids[:, None], tile_first[None, :], 0), axis=1) - 1) % 2
    later = (e_ids[None, :] > e_ids[:, None]) & (total[None, :] > 0)
    next_used = jnp.min(jnp.where(later, e_ids[None, :], MOE_EXPERTS), axis=1)
    next_used = jnp.where(next_used == MOE_EXPERTS, -1, next_used).astype(I32)
    tile_next = jnp.sum(jnp.where(tile_expert[:, None] == e_ids[None, :], next_used[None, :], 0), axis=1)
    tail_first = starts + total_al
    tail_chunks = (seg - total_al) // ROW_ALIGN
    return dict(n_chunks=n_chunks.astype(I32), chunk_row=chunk_row.reshape(-1).astype(I32),
                pos_tab=pos_tab.reshape(-1).astype(I32), pos_rows=pos_tab.astype(F32),
                tile_expert=tile_expert.astype(I32), tile_valid=tile_valid, tile_first=tile_first,
                tile_next=tile_next.astype(I32), n_valid=jnp.sum(tile_valid)[None].astype(I32),
                tile_slot=tile_slot.astype(I32),
                tail_first=tail_first.astype(I32),
                tail_chunks=tail_chunks.astype(I32))


def _sorted_rows(n_tokens, tm):
    n_t = n_tokens // tm
    bound = 2 * n_tokens + n_t * MOE_EXPERTS * (ROW_ALIGN - 1) + MOE_EXPERTS * (tm - ROW_ALIGN) + 1
    return (bound + tm - 1) // tm * tm


def _chunk_copy_out(buf, slot, c, row, hbm, sem):
    src = buf.at[slot, pl.ds(pl.multiple_of(c * ROW_ALIGN, ROW_ALIGN), ROW_ALIGN)]
    return pltpu.make_async_copy(src, hbm.at[pl.ds(pl.multiple_of(row, ROW_ALIGN), ROW_ALIGN)], sem.at[slot])


def _chunk_copy_in(hbm, row, buf, slot, c, sem):
    dst = buf.at[slot, pl.ds(pl.multiple_of(c * ROW_ALIGN, ROW_ALIGN), ROW_ALIGN)]
    src = hbm.at[pl.ds(pl.multiple_of(row, ROW_ALIGN), ROW_ALIGN), pl.ds(0, buf.shape[2])]
    return pltpu.make_async_copy(src, dst, sem.at[slot])


def _moe_dispatch_kernel(nch_ref, crow_ref, ptab_ref, tfirst_ref, tchunks_ref, tvalid_ref,
                         x_ref, rt_ref, r_ref, xs_hbm, buf, zeros, sem, sem_z, *, tm, n_tiles):
    i = pl.program_id(0)
    n_t = pl.num_programs(0)
    slot = i % 2

    def wait_tile(tile, s):
        def body(c, carry):
            _chunk_copy_out(buf, s, 0, 0, xs_hbm, sem).wait()
            return carry
        lax.fori_loop(0, nch_ref[tile], body, 0)

    def zero_tail(e, k):
        return pltpu.make_async_copy(
            zeros.at[pl.ds(0, ROW_ALIGN)],
            xs_hbm.at[pl.ds(pl.multiple_of(tfirst_ref[e] + k * ROW_ALIGN, ROW_ALIGN), ROW_ALIGN)], sem_z.at[0])

    def zero_tile(t):
        return pltpu.make_async_copy(zeros, xs_hbm.at[pl.ds(pl.multiple_of(t * tm, tm), tm)], sem_z.at[1])

    def for_each_fill(tail_fn, tile_fn):
        def per_expert(e, carry):
            def per_chunk(k, c2):
                tail_fn(e, k)
                return c2
            lax.fori_loop(0, tchunks_ref[e], per_chunk, 0)
            return carry
        lax.fori_loop(0, MOE_EXPERTS, per_expert, 0)

        def per_tile(t, carry):
            @pl.when(tvalid_ref[t] == 0)
            def _():
                tile_fn(t)
            return carry
        lax.fori_loop(0, n_tiles, per_tile, 0)

    @pl.when(i == 0)
    def _():
        zeros[...] = jnp.zeros(zeros.shape, zeros.dtype)
        for_each_fill(lambda e, k: zero_tail(e, k).start(), lambda t: zero_tile(t).start())

    @pl.when(i >= 2)
    def _():
        wait_tile(i - 2, slot)

    pos1 = rt_ref[4:5, :].astype(I32)
    pos2 = rt_ref[5:6, :].astype(I32)
    e1 = rt_ref[0:1, :].astype(I32)
    e2 = rt_ref[1:2, :].astype(I32)
    for e in range(MOE_EXPERTS):
        shift = ptab_ref[i * MOE_EXPERTS + e]
        pos1 = pos1 + jnp.where(e1 == e, shift, 0)
        pos2 = pos2 + jnp.where(e2 == e, shift, 0)
    lane = lax.broadcasted_iota(I32, (tm, LANES), 1)

    def gate_terms(g):
        hi = g.astype(BF16).astype(F32)
        return jnp.where(lane == 0, hi, jnp.where(lane == 1, g - hi, 0.0)).astype(BF16)

    gate1 = gate_terms(r_ref[:, 2:3])
    gate2 = gate_terms(r_ref[:, 3:4])
    blk = 256
    for k in range(TILE_ROWS // blk):
        row = lax.broadcasted_iota(I32, (blk, tm), 0) + k * blk
        pick1 = jnp.where(row == pos1, 1.0, 0.0).astype(BF16)
        pick2 = jnp.where(row == pos2, 1.0, 0.0).astype(BF16)
        at = slice(k * blk, (k + 1) * blk)
        buf[slot, at, :D_MODEL] = jnp.dot(pick1 + pick2, x_ref[...], preferred_element_type=F32).astype(BF16)
        buf[slot, at, D_MODEL:] = (jnp.dot(pick1, gate1, preferred_element_type=F32)
                                   + jnp.dot(pick2, gate2, preferred_element_type=F32)).astype(BF16)

    def send(c, carry):
        _chunk_copy_out(buf, slot, c, crow_ref[i * TILE_CHUNKS + c], xs_hbm, sem).start()
        return carry
    lax.fori_loop(0, nch_ref[i], send, 0)

    @pl.when(i == n_t - 1)
    def _():
        @pl.when(i >= 1)
        def _():
            wait_tile(i - 1, 1 - slot)
        wait_tile(i, slot)
        for_each_fill(lambda e, k: zero_tail(e, k).wait(), lambda t: zero_tile(t).wait())


def _moe_dispatch(x_bf, route_t, route, tables, tm):
    m, d = x_bf.shape
    n_rows = _sorted_rows(m, tm)
    grid_spec = pltpu.PrefetchScalarGridSpec(
        num_scalar_prefetch=6,
        grid=(m // tm,),
        in_specs=[pl.BlockSpec((tm, d), lambda i, *_: (i, 0)),
                  pl.BlockSpec((8, tm), lambda i, *_: (0, i)),
                  pl.BlockSpec((tm, LANES), lambda i, *_: (i, 0))],
        out_specs=pl.BlockSpec(memory_space=pl.ANY),
        scratch_shapes=[pltpu.VMEM((2, TILE_ROWS, SORTED_WIDTH), BF16), pltpu.VMEM((tm, SORTED_WIDTH), BF16),
                        pltpu.SemaphoreType.DMA((2,)), pltpu.SemaphoreType.DMA((2,))],
    )
    return pl.pallas_call(
        functools.partial(_moe_dispatch_kernel, tm=tm, n_tiles=n_rows // tm),
        name="moe_dispatch",
        grid_spec=grid_spec,
        out_shape=jax.ShapeDtypeStruct((n_rows, SORTED_WIDTH), BF16),
        compiler_params=_params(("arbitrary",)),
    )(tables["n_chunks"], tables["chunk_row"], tables["pos_tab"], tables["tail_first"], tables["tail_chunks"],
      tables["tile_valid"], x_bf, route_t, route)


def _moe_ffn_kernel(te_ref, tv_ref, tf_ref, tn_ref, nv_ref, ts_ref, x_ref, wg_hbm, wu_hbm, wd_hbm, o_ref,
                    wg_f, wu_f, wd_f, wg_bf, wu_bf, wd_bf, sem_w, *, layer):
    j = pl.program_id(0)
    valid = tv_ref[j] == 1
    slot = ts_ref[j]

    def weight_copies(e, s):
        return (pltpu.make_async_copy(wg_hbm.at[layer, e], wg_f.at[s], sem_w.at[s, 0]),
                pltpu.make_async_copy(wu_hbm.at[layer, e], wu_f.at[s], sem_w.at[s, 1]),
                pltpu.make_async_copy(wd_hbm.at[layer, e], wd_f.at[s], sem_w.at[s, 2]))

    @pl.when(j == 0)
    def _():
        for c in weight_copies(te_ref[0], slot):
            c.start()

    @pl.when(valid & (tf_ref[j] == 1))
    def _():
        @pl.when(tn_ref[j] >= 0)
        def _():
            for c in weight_copies(tn_ref[j], 1 - slot):
                c.start()

        for c in weight_copies(te_ref[j], slot):
            c.wait()
        _cast_rows(wg_f.at[slot], wg_bf)
        _cast_rows(wu_f.at[slot], wu_bf)
        _cast_rows(wd_f.at[slot], wd_bf)

    @pl.when(valid)
    def _():
        xb = x_ref[:, :D_MODEL]
        gate = x_ref[:, D_MODEL:D_MODEL + 1].astype(F32) + x_ref[:, D_MODEL + 1:D_MODEL + 2].astype(F32)
        hg = jnp.dot(xb, wg_bf[...], preferred_element_type=F32)
        hu = jnp.dot(xb, wu_bf[...], preferred_element_type=F32)
        act = (hg * (1.0 / (1.0 + jnp.exp(-hg))) * hu).astype(BF16)
        o_ref[...] = (jnp.dot(act, wd_bf[...], preferred_element_type=F32) * gate).astype(o_ref.dtype)


def _moe_ffn(x_sorted, w_gate, w_up, w_down, layer, tables, tm):
    n_rows, width = x_sorted.shape
    d, f = w_gate.shape[2], w_gate.shape[3]
    any_spec = pl.BlockSpec(memory_space=pl.ANY)
    grid_spec = pltpu.PrefetchScalarGridSpec(
        num_scalar_prefetch=6,
        grid=(n_rows // tm,),
        in_specs=[pl.BlockSpec((tm, width), lambda j, te, tv, tf, tn, nv, ts: (jnp.minimum(j, nv[0] - 1), 0)),
                  any_spec, any_spec, any_spec],
        out_specs=pl.BlockSpec((tm, d), lambda j, te, tv, tf, tn, nv, ts: (jnp.minimum(j, nv[0] - 1), 0)),
        scratch_shapes=[pltpu.VMEM((2, d, f), F32), pltpu.VMEM((2, d, f), F32), pltpu.VMEM((2, f, d), F32),
                        pltpu.VMEM((d, f), BF16), pltpu.VMEM((d, f), BF16), pltpu.VMEM((f, d), BF16),
                        pltpu.SemaphoreType.DMA((2, 3))],
    )
    return pl.pallas_call(
        functools.partial(_moe_ffn_kernel, layer=layer),
        name="moe_ffn",
        grid_spec=grid_spec,
        out_shape=jax.ShapeDtypeStruct((n_rows, width), BF16),
        input_output_aliases={6: 0},
        compiler_params=_params(("arbitrary",)),
    )(tables["tile_expert"], tables["tile_valid"], tables["tile_first"], tables["tile_next"], tables["n_valid"],
      tables["tile_slot"], x_sorted, w_gate, w_up, w_down)


def _moe_combine_kernel(nch_ref, crow_ref, ys_hbm, x_ref, r_ref, ptab_ref, g_ref, b_ref, o_ref, obf_ref,
                        buf, sem, *, tm):
    i = pl.program_id(0)
    n_t = pl.num_programs(0)
    slot = i % 2

    def fetch(tile, s):
        def body(c, carry):
            _chunk_copy_in(ys_hbm, crow_ref[tile * TILE_CHUNKS + c], buf, s, c, sem).start()
            return carry
        lax.fori_loop(0, nch_ref[tile], body, 0)

    @pl.when(i == 0)
    def _():
        buf[...] = jnp.zeros(buf.shape, buf.dtype)
        fetch(0, 0)

    @pl.when(i + 1 < n_t)
    def _():
        fetch(i + 1, 1 - slot)

    def arrived(c, carry):
        _chunk_copy_in(ys_hbm, 0, buf, slot, 0, sem).wait()
        return carry
    lax.fori_loop(0, nch_ref[i], arrived, 0)

    lane = lax.broadcasted_iota(I32, (tm, LANES), 1)
    shift = ptab_ref[...]
    e1 = r_ref[:, 0:1].astype(I32)
    e2 = r_ref[:, 1:2].astype(I32)
    pos1 = (r_ref[:, 4:5] + jnp.sum(jnp.where(lane == e1, shift, 0.0), axis=1, keepdims=True)).astype(I32)
    pos2 = (r_ref[:, 5:6] + jnp.sum(jnp.where(lane == e2, shift, 0.0), axis=1, keepdims=True)).astype(I32)
    col = lax.broadcasted_iota(I32, (tm, TILE_ROWS), 1)
    pick = jnp.where((col == pos1) | (col == pos2), 1.0, 0.0).astype(BF16)
    y = jnp.dot(pick, buf[slot], preferred_element_type=F32)
    out = _layernorm_rows(DEEPNORM_ALPHA * x_ref[...] + y, g_ref[...], b_ref[...])
    o_ref[...] = out
    obf_ref[...] = out.astype(BF16)


def _moe_combine(y_sorted, x, route, tables, g, b, tm, out_batch=None):
    m, d = x.shape
    out_f32 = jax.ShapeDtypeStruct((m, d), F32)
    out_spec = pl.BlockSpec((tm, d), lambda i, *_: (i, 0))
    if out_batch:
        per = m // out_batch // tm
        out_f32 = jax.ShapeDtypeStruct((out_batch, m // out_batch, d), F32)
        out_spec = pl.BlockSpec((None, tm, d), lambda i, *_: (i // per, i % per, 0))
    grid_spec = pltpu.PrefetchScalarGridSpec(
        num_scalar_prefetch=2,
        grid=(m // tm,),
        in_specs=[pl.BlockSpec(memory_space=pl.ANY),
                  pl.BlockSpec((tm, d), lambda i, *_: (i, 0)),
                  pl.BlockSpec((tm, LANES), lambda i, *_: (i, 0)),
                  pl.BlockSpec((None, 1, LANES), lambda i, *_: (i, 0, 0)),
                  pl.BlockSpec((1, d), lambda i, *_: (0, 0)),
                  pl.BlockSpec((1, d), lambda i, *_: (0, 0))],
        out_specs=[out_spec, pl.BlockSpec((tm, d), lambda i, *_: (i, 0))],
        scratch_shapes=[pltpu.VMEM((2, TILE_ROWS, d), BF16), pltpu.SemaphoreType.DMA((2,))],
    )
    pos_rows = _pad_cols(tables["pos_rows"])[:, None, :]
    return pl.pallas_call(
        functools.partial(_moe_combine_kernel, tm=tm),
        name="moe_combine",
        grid_spec=grid_spec,
        out_shape=[out_f32, jax.ShapeDtypeStruct((m, d), BF16)],
        compiler_params=_params(("arbitrary",)),
    )(tables["n_chunks"], tables["chunk_row"], y_sorted, x, route, pos_rows, g, b)


def _route_weights(w_group, b_group, w_router, b_router):
    w = _pad_cols(jnp.concatenate([w_group, w_router], axis=1))
    w_hi = w.astype(BF16)
    w_mid = (w - w_hi.astype(F32)).astype(BF16)
    return w_hi, w_mid, _pad_cols(jnp.concatenate([b_group, b_router])[None, :])


def _moe_layer(x, x_bf, route, route_t, before, counts, w_gate, w_up, w_down, layer, ln_g, ln_b, out_batch=None):
    n_tokens = x.shape[0]
    tm = min(MOE_TM, n_tokens)
    tables = _dispatch_tables(before, counts, n_tokens, tm)
    x_sorted = _moe_dispatch(x_bf, route_t, route, tables, tm)
    y_sorted = _moe_ffn(x_sorted, w_gate, w_up, w_down, layer, tables, tm)
    return _moe_combine(y_sorted, x, route, tables, ln_g[None, :], ln_b[None, :], tm, out_batch)


def kernel(x, t5_rel_bias, mla_w_in, mla_q_norm, mla_kv_norm, mla_w_q_up, mla_w_kv_up, mla_w_o, fox_w_in, fox_b_f, fox_w_o, dsa_w_in, dsa_w_o, ln_g, ln_b, moe_w_group, moe_b_group, moe_w_router, moe_b_router, moe_w_gate, moe_w_up, moe_w_down):
    batch, seq, d = x.shape
    x_bf = x
    for layer in range(DEPTH):
        kind = layer % N_MIXERS
        j = layer // N_MIXERS
        g0, b0 = ln_g[layer, 0], ln_b[layer, 0]
        route_wb = _route_weights(moe_w_group[layer], moe_b_group[layer], moe_w_router[layer], moe_b_router[layer])
        if kind == 0:
            mixed = _mla_layer(x, x_bf, mla_w_in[j], mla_q_norm[j], mla_kv_norm[j], mla_w_q_up[j],
                               mla_w_kv_up, j, mla_w_o[j], g0, b0, route_wb, batch, seq)
        elif kind == 1:
            mixed = _fox_layer(x, x_bf, fox_w_in, j, fox_b_f[j], fox_w_o[j], g0, b0, route_wb, batch, seq)
        else:
            mixed = _dsa_layer(x, x_bf, t5_rel_bias, dsa_w_in, j, dsa_w_o[j], g0, b0, route_wb, batch, seq)
        x, x_bf = _moe_layer(*mixed, moe_w_gate, moe_w_up, moe_w_down, layer, ln_g[layer, 1], ln_b[layer, 1],
                             batch if layer == DEPTH - 1 else None)
    return x
```

```python
import functools
import math

import numpy as np
import jax
import jax.numpy as jnp
from jax import lax
from jax.experimental import pallas as pl
from jax.experimental.pallas import tpu as pltpu

F32 = jnp.float32
BF16 = jnp.bfloat16
I32 = jnp.int32

D_MODEL = 2048
DEPTH = 4
CHUNK = 64
N_MIXERS = 3
DEEPNORM_ALPHA = (2.0 * DEPTH) ** 0.25

HEADS = 16
HEAD_DIM = 128
MLA_NOPE_DIM = 128
MLA_ROPE_DIM = 64
MLA_Q_RANK = 512
MLA_KV_RANK = 512
MLA_QK_PAD = 256
ROPE_THETA = 10000.0

IDX_HEADS = 16
IDX_DIM = 64
IDX_TOPK_MAX = 256

T5_BUCKETS = 32
T5_MAX_DISTANCE = 128
T5_FAR_BUCKET = T5_BUCKETS // 2 - 1
T5_BAND_BACK = 128

MOE_GROUPS = 4
MOE_EXPERTS_PER_GROUP = 8
MOE_EXPERTS = MOE_GROUPS * MOE_EXPERTS_PER_GROUP
MOE_D_FF = 512

LN_EPS = 1e-5
RMS_EPS = 1e-6
NEG_INF = -1e30

LANES = 128
VMEM_LIMIT = 56 * 1024 * 1024

ATTN_TQ = 256
MOE_TM = 256
ROW_ALIGN = 16
TILE_ROWS = -(-(2 * MOE_TM + MOE_EXPERTS * (ROW_ALIGN - 1)) // 256) * 256
TILE_CHUNKS = TILE_ROWS // ROW_ALIGN
SORTED_WIDTH = D_MODEL + LANES


def _params(semantics, vmem=VMEM_LIMIT):
    return pltpu.CompilerParams(dimension_semantics=semantics, vmem_limit_bytes=vmem)


def _cast_rows(src_ref, dst_ref):
    rows = 16
    while 2 * rows * src_ref.shape[1] <= 32 * 8 * LANES and src_ref.shape[0] % (2 * rows) == 0:
        rows *= 2

    def body(i, carry):
        at = pl.ds(pl.multiple_of(i * rows, rows), rows)
        dst_ref[at, :] = src_ref[at, :].astype(dst_ref.dtype)
        return carry
    lax.fori_loop(0, src_ref.shape[0] // rows, body, 0, unroll=2)


def _mm_kernel(a_ref, w_ref, o_ref, w_bf, *, scale, scaled_cols):
    @pl.when(pl.program_id(1) == 0)
    def _():
        _cast_rows(w_ref, w_bf)

    acc = lax.dot_general(a_ref[...], w_bf[...], (((1,), (1,)), ((), ())), preferred_element_type=F32)
    if scaled_cols:
        tn = o_ref.shape[1]
        col = pl.program_id(0) * tn + lax.broadcasted_iota(I32, (1, tn), 1)
        acc = acc * jnp.where(col < scaled_cols, scale, 1.0)
    o_ref[...] = acc.astype(o_ref.dtype)


def _matmul(a, w_t, layer, *, n_out, tn, tm=1024, scale=1.0, scaled_cols=0, out_dtype=BF16):
    m, k = a.shape
    tm = min(tm, m)
    kern = functools.partial(_mm_kernel, scale=scale, scaled_cols=scaled_cols)
    return pl.pallas_call(
        kern,
        name="proj_mm",
        grid=(n_out // tn, m // tm),
        in_specs=[pl.BlockSpec((tm, k), lambda j, i: (i, 0)),
                  pl.BlockSpec((None, tn, k), lambda j, i: (layer, j, 0))],
        out_specs=pl.BlockSpec((tm, tn), lambda j, i: (i, j)),
        out_shape=jax.ShapeDtypeStruct((m, n_out), out_dtype),
        scratch_shapes=[pltpu.VMEM((tn, k), BF16)],
        compiler_params=_params(("arbitrary", "arbitrary")),
    )(a, w_t)


def _split3(v):
    hi = v.astype(BF16)
    r1 = v - hi.astype(F32)
    mid = r1.astype(BF16)
    lo = (r1 - mid.astype(F32)).astype(BF16)
    return hi, mid, lo


def _dot_precise(x, w):
    xh = x.astype(BF16)
    xm = (x - xh.astype(F32)).astype(BF16)
    wh = w.astype(BF16)
    wm = (w - wh.astype(F32)).astype(BF16)
    d = lambda p, q: jnp.dot(p, q, preferred_element_type=F32)
    return (d(xm, wh) + d(xh, wm)) + d(xh, wh)


def _small_mm_kernel(x_ref, w_ref, b_ref, o_ref):
    o_ref[...] = _dot_precise(x_ref[...], w_ref[...]) + b_ref[...]


def _small_matmul(x, w, b, tm=512):
    m, k = x.shape
    tm = min(tm, m)
    return pl.pallas_call(
        _small_mm_kernel,
        name="small_mm",
        grid=(m // tm,),
        in_specs=[pl.BlockSpec((tm, k), lambda i: (i, 0)),
                  pl.BlockSpec((k, LANES), lambda i: (0, 0)),
                  pl.BlockSpec((1, LANES), lambda i: (0, 0))],
        out_specs=pl.BlockSpec((tm, LANES), lambda i: (i, 0)),
        out_shape=jax.ShapeDtypeStruct((m, LANES), F32),
        compiler_params=_params(("arbitrary",)),
    )(x, w, b)


def _rows_spec(arr, tm):
    d = arr.shape[-1]
    if arr.ndim == 2:
        return pl.BlockSpec((tm, d), lambda i: (i, 0))
    per = arr.shape[1] // tm
    return pl.BlockSpec((None, tm, d), lambda i: (i // per, i % per, 0))


def _pad_cols(w, width=LANES):
    return jnp.pad(w, ((0, 0), (0, width - w.shape[1])))


def _layernorm_rows(z, g, b):
    mu = jnp.mean(z, axis=-1, keepdims=True)
    zc = z - mu
    var = jnp.mean(zc * zc, axis=-1, keepdims=True)
    return zc * lax.rsqrt(var + LN_EPS) * g + b


def _route_tile(x, w_hi, w_mid, b, run_ref):
    tm = x.shape[0]
    x_hi = x.astype(BF16)
    x_mid = (x - x_hi.astype(F32)).astype(BF16)
    d = lambda p, q: jnp.dot(p, q, preferred_element_type=F32)
    logit = (d(x_mid, w_hi) + d(x_hi, w_mid)) + d(x_hi, w_hi) + b
    lane = lax.broadcasted_iota(I32, logit.shape, 1)
    big = jnp.int32(LANES)

    def first_argmax(mask):
        top = jnp.max(jnp.where(mask, logit, -jnp.inf), axis=1, keepdims=True)
        idx = jnp.min(jnp.where(mask & (logit == top), lane, big), axis=1, keepdims=True)
        return top, idx

    is_group = lane < MOE_GROUPS
    g_top, g_sel = first_argmax(is_group)
    p_g = 1.0 / jnp.sum(jnp.where(is_group, jnp.exp(logit - g_top), 0.0), axis=1, keepdims=True)
    lo = MOE_GROUPS + g_sel * MOE_EXPERTS_PER_GROUP
    in_group = (lane >= lo) & (lane < lo + MOE_EXPERTS_PER_GROUP)
    v1, i1 = first_argmax(in_group)
    v2, i2 = first_argmax(in_group & (lane != i1))
    ex = jnp.exp(v2 - v1)
    w1 = 1.0 / (1.0 + ex)
    w2 = ex / (1.0 + ex)
    e1 = i1 - MOE_GROUPS
    e2 = i2 - MOE_GROUPS
    pick1 = lane == e1
    pick2 = lane == e2
    both = jnp.where(pick1 | pick2, 1.0, 0.0)
    earlier = lax.broadcasted_iota(I32, (tm, tm), 1) < lax.broadcasted_iota(I32, (tm, tm), 0)
    start = run_ref[...]
    before = jnp.dot(jnp.where(earlier, 1.0, 0.0).astype(BF16), both.astype(BF16),
                     preferred_element_type=F32) + start
    r1 = jnp.sum(jnp.where(pick1, before, 0.0), axis=1, keepdims=True)
    r2 = jnp.sum(jnp.where(pick2, before, 0.0), axis=1, keepdims=True)
    run_ref[...] = start + jnp.sum(both, axis=0, keepdims=True)
    out = jnp.where(lane == 0, e1.astype(F32), 0.0)
    out = jnp.where(lane == 1, e2.astype(F32), out)
    out = jnp.where(lane == 2, w1 * p_g, out)
    out = jnp.where(lane == 3, w2 * p_g, out)
    out = jnp.where(lane == 4, r1, out)
    out = jnp.where(lane == 5, r2, out)
    return out, start, both


def _mm_res_ln_kernel(a_ref, w_ref, x_ref, g_ref, b_ref, wrh_ref, wrm_ref, br_ref,
                      o_ref, obf_ref, r_ref, rt_ref, before_ref, cnt_ref, run_ref, *, sub):
    @pl.when(pl.program_id(0) == 0)
    def _():
        run_ref[...] = jnp.zeros(run_ref.shape, F32)

    y = jnp.dot(a_ref[...], w_ref[...], preferred_element_type=F32)
    out = _layernorm_rows(DEEPNORM_ALPHA * x_ref[...] + y, g_ref[...], b_ref[...])
    o_ref[...] = out
    obf_ref[...] = out.astype(BF16)
    slab, before, both = _route_tile(out, wrh_ref[...], wrm_ref[...], br_ref[...], run_ref)
    r_ref[...] = slab
    rt_ref[...] = slab.T[:8, :]
    for k in range(out.shape[0] // sub):
        before_ref[k] = before
        before = before + jnp.sum(both[k * sub:(k + 1) * sub, :], axis=0, keepdims=True)
    cnt_ref[...] = run_ref[...]


def _mm_res_ln(a, w, x, g, b, w_route_hi, w_route_mid, b_route, tm=512):
    m, k = a.shape
    d = w.shape[1]
    tm = min(tm, m)
    sub = min(MOE_TM, tm)
    per = tm // sub
    return pl.pallas_call(
        functools.partial(_mm_res_ln_kernel, sub=sub),
        name="out_proj_res_ln",
        grid=(m // tm,),
        in_specs=[pl.BlockSpec((tm, k), lambda i: (i, 0)),
                  pl.BlockSpec((k, d), lambda i: (0, 0)),
                  _rows_spec(x, tm),
                  pl.BlockSpec((1, d), lambda i: (0, 0)),
                  pl.BlockSpec((1, d), lambda i: (0, 0)),
                  pl.BlockSpec((d, LANES), lambda i: (0, 0)),
                  pl.BlockSpec((d, LANES), lambda i: (0, 0)),
                  pl.BlockSpec((1, LANES), lambda i: (0, 0))],
        out_specs=[pl.BlockSpec((tm, d), lambda i: (i, 0)),
                   pl.BlockSpec((tm, d), lambda i: (i, 0)),
                   pl.BlockSpec((tm, LANES), lambda i: (i, 0)),
                   pl.BlockSpec((8, tm), lambda i: (0, i)),
                   pl.BlockSpec((per, 1, LANES), lambda i: (i, 0, 0)),
                   pl.BlockSpec((1, LANES), lambda i: (0, 0))],
        out_shape=[jax.ShapeDtypeStruct((m, d), F32), jax.ShapeDtypeStruct((m, d), BF16),
                   jax.ShapeDtypeStruct((m, LANES), F32), jax.ShapeDtypeStruct((8, m), F32),
                   jax.ShapeDtypeStruct((m // sub, 1, LANES), F32), jax.ShapeDtypeStruct((1, LANES), F32)],
        scratch_shapes=[pltpu.VMEM((1, LANES), F32)],
        compiler_params=_params(("arbitrary",)),
    )(a, w, x, g, b, w_route_hi, w_route_mid, b_route)


def _rope_tables(seq):
    half = MLA_ROPE_DIM // 2
    inv_freq = ROPE_THETA ** (-jnp.arange(half, dtype=F32) / half)
    ang = jnp.arange(seq, dtype=I32).astype(F32)[:, None] * inv_freq[None, :]
    cos, sin = jnp.cos(ang), jnp.sin(ang)
    z = jnp.zeros_like(cos)
    cos_t = jnp.concatenate([cos, cos, z, z], axis=1)
    sin_a = jnp.concatenate([-sin, z, z, z], axis=1)
    sin_b = jnp.concatenate([z, sin, z, z], axis=1)
    return cos_t, sin_a, sin_b


def _rope_lanes(r, cos_t, sin_a, sin_b):
    from_right = pltpu.roll(r, LANES - MLA_ROPE_DIM // 2, 1)
    from_left = pltpu.roll(r, MLA_ROPE_DIM // 2, 1)
    return r * cos_t + from_right * sin_a + from_left * sin_b


def _rms_rows(v, g):
    return v * lax.rsqrt(jnp.mean(v * v, axis=-1, keepdims=True) + RMS_EPS) * g


def _mla_in_kernel(x_ref, w_ref, qn_ref, kvn_ref, cos_ref, sina_ref, sinb_ref, cq_ref, ckv_ref, kr_ref):
    h = _nt_dot(x_ref[...].astype(BF16), w_ref[...])
    cq_ref[...] = _rms_rows(h[:, :MLA_Q_RANK], qn_ref[...]).astype(BF16)
    ckv_ref[...] = _rms_rows(h[:, MLA_Q_RANK:MLA_Q_RANK + MLA_KV_RANK], kvn_ref[...]).astype(BF16)
    r = h[:, MLA_Q_RANK + MLA_KV_RANK:]
    kr_ref[...] = _rope_lanes(r, cos_ref[...], sina_ref[...], sinb_ref[...]).astype(BF16)


def _mla_in(x_bf, w_in_pad, q_norm, kv_norm, tables, seq, tm=512):
    m, k = math.prod(x_bf.shape[:-1]), x_bf.shape[-1]
    tm = min(tm, seq)
    nw = w_in_pad.shape[0]
    per_seq = seq // tm
    tab_spec = pl.BlockSpec((tm, LANES), lambda i: (i % per_seq, 0))
    return pl.pallas_call(
        _mla_in_kernel,
        name="mla_in",
        grid=(m // tm,),
        in_specs=[_rows_spec(x_bf, tm),
                  pl.BlockSpec((nw, k), lambda i: (0, 0)),
                  pl.BlockSpec((1, MLA_Q_RANK), lambda i: (0, 0)),
                  pl.BlockSpec((1, MLA_KV_RANK), lambda i: (0, 0)),
                  tab_spec, tab_spec, tab_spec],
        out_specs=[pl.BlockSpec((tm, MLA_Q_RANK), lambda i: (i, 0)),
                   pl.BlockSpec((tm, MLA_KV_RANK), lambda i: (i, 0)),
                   pl.BlockSpec((tm, LANES), lambda i: (i, 0))],
        out_shape=[jax.ShapeDtypeStruct((m, MLA_Q_RANK), BF16),
                   jax.ShapeDtypeStruct((m, MLA_KV_RANK), BF16),
                   jax.ShapeDtypeStruct((m, LANES), BF16)],
        compiler_params=_params(("arbitrary",)),
    )(x_bf, w_in_pad, q_norm, kv_norm, *tables)


def _values_and_ones(v_ref, rows):
    v = v_ref[rows, :]
    return jnp.concatenate([v, jnp.ones(v.shape, BF16)], axis=1)


def _softmax_pv(pieces):
    m = None
    for s, _ in pieces:
        pm = jnp.max(s, axis=1, keepdims=True)
        m = pm if m is None else jnp.maximum(m, pm)
    acc = None
    for s, v in pieces:
        t = jnp.dot(jnp.exp((s - m).astype(BF16)), v, preferred_element_type=F32)
        acc = t if acc is None else acc + t
    return acc[:, :HEAD_DIM] / acc[:, HEAD_DIM:HEAD_DIM + 1]


def _nt_dot(q, k):
    return lax.dot_general(q, k, (((1,), (1,)), ((), ())), preferred_element_type=F32)


def _attend_tiles(n_tiles, scores, finish):
    s_next = scores(0)
    for i in range(n_tiles):
        s = s_next
        if i + 1 < n_tiles:
            s_next = scores(i + 1)
        finish(i, s)


def _mla_q_up_kernel(c_ref, w_ref, cos_ref, sina_ref, sinb_ref, o_ref, w_bf, *, scale):
    @pl.when(pl.program_id(1) == 0)
    def _():
        _cast_rows(w_ref, w_bf)

    acc = jnp.dot(c_ref[...], w_bf[...], preferred_element_type=F32) * scale
    for h in range(o_ref.shape[1] // MLA_QK_PAD):
        at = h * MLA_QK_PAD
        o_ref[:, at:at + MLA_NOPE_DIM] = acc[:, at:at + MLA_NOPE_DIM].astype(BF16)
        r = acc[:, at + MLA_NOPE_DIM:at + MLA_QK_PAD]
        o_ref[:, at + MLA_NOPE_DIM:at + MLA_QK_PAD] = _rope_lanes(
            r, cos_ref[...], sina_ref[...], sinb_ref[...]).astype(BF16)


def _mla_kv_up_kernel(c_ref, w_ref, kr_ref, k_ref, v_ref, w_bf):
    @pl.when(pl.program_id(1) == 0)
    def _():
        _cast_rows(w_ref, w_bf)

    acc = jnp.dot(c_ref[...], w_bf[...], preferred_element_type=F32)
    for h in range(k_ref.shape[1] // MLA_QK_PAD):
        at = h * MLA_QK_PAD
        k_ref[:, at:at + MLA_NOPE_DIM] = acc[:, at:at + MLA_NOPE_DIM].astype(BF16)
        k_ref[:, at + MLA_NOPE_DIM:at + MLA_QK_PAD] = kr_ref[...]
        v_ref[:, h * HEAD_DIM:(h + 1) * HEAD_DIM] = acc[:, at + MLA_NOPE_DIM:at + MLA_QK_PAD].astype(BF16)


def _mla_up(cq, ckv, kr, w_q_up, w_kv_up, j, tables, seq, scale, tm=1024, tn=1024):
    m, rank = cq.shape
    tm = min(tm, seq)
    per_seq = seq // tm
    n = HEADS * MLA_QK_PAD
    tab_spec = pl.BlockSpec((tm, LANES), lambda j, i: (i % per_seq, 0))
    row_spec = pl.BlockSpec((tm, rank), lambda j, i: (i, 0))
    w_spec = pl.BlockSpec((rank, tn), lambda j, i: (0, j))
    out_spec = pl.BlockSpec((tm, tn), lambda j, i: (i, j))
    out = jax.ShapeDtypeStruct((m, n), BF16)
    q = pl.pallas_call(
        functools.partial(_mla_q_up_kernel, scale=scale),
        name="mla_q_up",
        grid=(n // tn, m // tm),
        in_specs=[row_spec, w_spec, tab_spec, tab_spec, tab_spec],
        out_specs=out_spec,
        out_shape=out,
        scratch_shapes=[pltpu.VMEM((rank, tn), BF16)],
        compiler_params=_params(("arbitrary", "arbitrary")),
    )(cq, w_q_up, *tables)
    k, v = pl.pallas_call(
        _mla_kv_up_kernel,
        name="mla_kv_up",
        grid=(n // tn, m // tm),
        in_specs=[row_spec, pl.BlockSpec((None, rank, tn), lambda c, i: (j, 0, c)),
                  pl.BlockSpec((tm, LANES), lambda c, i: (i, 0))],
        out_specs=[out_spec, pl.BlockSpec((tm, tn // 2), lambda c, i: (i, c))],
        out_shape=[out, jax.ShapeDtypeStruct((m, n // 2), BF16)],
        scratch_shapes=[pltpu.VMEM((rank, tn), BF16)],
        compiler_params=_params(("arbitrary", "arbitrary")),
    )(ckv, w_kv_up, kr)
    return q, k, v


def _mla_attn_kernel(q_ref, k_ref, v_ref, o_ref, *, tq):
    seq = q_ref.shape[0]
    row = lax.broadcasted_iota(I32, (tq, tq), 0)
    col = lax.broadcasted_iota(I32, (tq, tq), 1)
    visible = (col // CHUNK) <= (row // CHUNK)

    def scores(i):
        return _nt_dot(q_ref[i * tq:(i + 1) * tq, :], k_ref[:(i + 1) * tq, :])

    def finish(i, s):
        lo, kv = i * tq, (i + 1) * tq
        pieces = [(jnp.where(visible, s[:, lo:], NEG_INF), _values_and_ones(v_ref, slice(lo, kv)))]
        if lo:
            pieces.append((s[:, :lo], _values_and_ones(v_ref, slice(0, lo))))
        o_ref[lo:kv, :] = _softmax_pv(pieces).astype(o_ref.dtype)

    _attend_tiles(seq // tq, scores, finish)


def _mla_attn(q, k, v, batch, seq):
    tq = min(ATTN_TQ, seq)
    head_spec = pl.BlockSpec((seq, MLA_QK_PAD), lambda b, h: (b, h))
    return pl.pallas_call(
        functools.partial(_mla_attn_kernel, tq=tq),
        name="mla_attn",
        grid=(batch, HEADS),
        in_specs=[head_spec, head_spec, pl.BlockSpec((seq, HEAD_DIM), lambda b, h: (b, h))],
        out_specs=pl.BlockSpec((seq, HEAD_DIM), lambda b, h: (b, h)),
        out_shape=jax.ShapeDtypeStruct((batch * seq, HEADS * HEAD_DIM), BF16),
        compiler_params=_params(("arbitrary", "arbitrary")),
    )(q, k, v)


def _mla_q_up_layout(w_q_up):
    rank = w_q_up.shape[0]
    w = w_q_up.reshape(rank, HEADS, MLA_NOPE_DIM + MLA_ROPE_DIM)
    w = jnp.pad(w, ((0, 0), (0, 0), (0, MLA_QK_PAD - MLA_NOPE_DIM - MLA_ROPE_DIM)))
    return w.reshape(rank, HEADS * MLA_QK_PAD)


def _mla_layer(x, x_bf, w_in, q_norm, kv_norm, w_q_up, w_kv_up, j, w_o, ln_g, ln_b, route_wb, batch, seq):
    tables = _rope_tables(seq)
    w_in_pad = jnp.pad(w_in.T, ((0, MLA_Q_RANK + MLA_KV_RANK + LANES - w_in.shape[1]), (0, 0))).astype(BF16)
    cq, ckv, kr = _mla_in(x_bf, w_in_pad, q_norm[None, :], kv_norm[None, :], tables, seq)
    scale = (MLA_NOPE_DIM + MLA_ROPE_DIM) ** -0.5
    q, k, v = _mla_up(cq, ckv, kr, _mla_q_up_layout(w_q_up), w_kv_up, j, tables, seq, scale)
    o = _mla_attn(q, k, v, batch, seq)
    return _mm_res_ln(o, w_o.astype(BF16), x, ln_g[None, :], ln_b[None, :], *route_wb)


def _log_sigmoid(z):
    return jnp.minimum(z, 0.0) - jnp.log1p(jnp.exp(-jnp.abs(z)))


def _fox_gate_kernel(f_ref, c_ref, *, blk):
    seq = f_ref.shape[0]
    tri = (lax.broadcasted_iota(I32, (blk, blk), 1) <= lax.broadcasted_iota(I32, (blk, blk), 0)).astype(BF16)
    carry = jnp.zeros((1, LANES), F32)
    for j in range(seq // blk):
        lf = _log_sigmoid(f_ref[j * blk:(j + 1) * blk, :])
        hi, mid, lo = _split3(lf)
        d = lambda p: jnp.dot(tri, p, preferred_element_type=F32)
        c = (d(lo) + d(mid)) + d(hi) + carry
        c_ref[j * blk:(j + 1) * blk, :] = c
        carry = c[blk - 1:blk, :]


def _fox_gate(f_logit, batch, seq):
    blk = min(256, seq)
    return pl.pallas_call(
        functools.partial(_fox_gate_kernel, blk=blk),
        name="fox_gate",
        grid=(batch,),
        in_specs=[pl.BlockSpec((seq, LANES), lambda b: (b, 0))],
        out_specs=pl.BlockSpec((seq, LANES), lambda b: (b, 0)),
        out_shape=jax.ShapeDtypeStruct((batch * seq, LANES), F32),
        compiler_params=_params(("arbitrary",)),
    )(f_logit)


def _fox_attn_kernel(q_ref, k_ref, v_ref, c_ref, o_ref, qs_ref, ks_ref, *, tq):
    seq = q_ref.shape[0]
    h = pl.program_id(1)
    lane = lax.broadcasted_iota(I32, (seq, LANES), 1)
    c = jnp.sum(jnp.where(lane == h, c_ref[...], 0.0), axis=1, keepdims=True)
    terms = [t.astype(F32) for t in _split3(c)]
    q_side = jnp.where((lane >= 3) & (lane < 6), 1.0, 0.0)
    k_side = jnp.where(lane < 3, 1.0, 0.0)
    for j, t in enumerate(terms):
        q_side = jnp.where(lane == j, t, q_side)
        k_side = jnp.where(lane == 3 + j, -t, k_side)
    qs_ref[...] = q_side.astype(BF16)
    ks_ref[...] = k_side.astype(BF16)
    row = lax.broadcasted_iota(I32, (tq, tq), 0)
    col = lax.broadcasted_iota(I32, (tq, tq), 1)

    def wide(ref, side_ref, rows):
        return jnp.concatenate([ref[rows, :], side_ref[rows, :]], axis=1)

    values = functools.partial(_values_and_ones, v_ref)

    def scores(i):
        return _nt_dot(wide(q_ref, qs_ref, slice(i * tq, (i + 1) * tq)), wide(k_ref, ks_ref, slice(0, (i + 1) * tq)))

    def finish(i, s):
        lo, kv = i * tq, (i + 1) * tq
        pieces = [(jnp.where(col <= row, s[:, lo:], NEG_INF), values(slice(lo, kv)))]
        if lo:
            pieces.append((s[:, :lo], values(slice(0, lo))))
        o_ref[lo:kv, :] = _softmax_pv(pieces).astype(o_ref.dtype)

    _attend_tiles(seq // tq, scores, finish)


def _fox_attn(qkv, c_col, batch, seq):
    tq = min(ATTN_TQ, seq)
    side = pltpu.VMEM((seq, LANES), BF16)
    return pl.pallas_call(
        functools.partial(_fox_attn_kernel, tq=tq),
        name="fox_attn",
        grid=(batch, HEADS),
        in_specs=[pl.BlockSpec((seq, HEAD_DIM), lambda b, h: (b, h)),
                  pl.BlockSpec((seq, HEAD_DIM), lambda b, h: (b, HEADS + h)),
                  pl.BlockSpec((seq, HEAD_DIM), lambda b, h: (b, 2 * HEADS + h)),
                  pl.BlockSpec((seq, LANES), lambda b, h: (b, 0))],
        out_specs=pl.BlockSpec((seq, HEAD_DIM), lambda b, h: (b, h)),
        out_shape=jax.ShapeDtypeStruct((batch * seq, HEADS * HEAD_DIM), BF16),
        scratch_shapes=[side, side],
        compiler_params=_params(("arbitrary", "arbitrary")),
    )(qkv, qkv, qkv, c_col)


def _fox_layer(x, x_bf, w_in, j, b_f, w_o, ln_g, ln_b, route_wb, batch, seq):
    hd = HEADS * HEAD_DIM
    scale = HEAD_DIM ** -0.5
    qkv = _matmul(x_bf, jnp.swapaxes(w_in, 1, 2), j, n_out=3 * hd, tn=1024, scale=scale, scaled_cols=hd)
    f_logit = _small_matmul(x, _pad_cols(w_in[j, :, 3 * hd:]), _pad_cols(b_f[None, :]))
    o = _fox_attn(qkv, _fox_gate(f_logit, batch, seq), batch, seq)
    return _mm_res_ln(o, w_o.astype(BF16), x, ln_g[None, :], ln_b[None, :], *route_wb)


def _t5_bucket_table(tq):
    a = np.arange(tq, dtype=np.int64)[:, None]
    b = np.arange(tq + T5_BAND_BACK, dtype=np.int64)[None, :]
    rel = (b - T5_BAND_BACK) - a
    nb = T5_BUCKETS // 2
    max_exact = nb // 2
    ret = np.where(rel > 0, nb, 0)
    n = np.abs(rel)
    nf = np.maximum(n, 1).astype(np.float32)
    large = max_exact + (np.log(nf / np.float32(max_exact)) / np.float32(math.log(T5_MAX_DISTANCE / max_exact))
                         * np.float32(nb - max_exact)).astype(np.int32)
    large = np.minimum(large, nb - 1)
    return (ret + np.where(n < max_exact, n, large)).astype(np.int32)


def _t5_band_kernel(t5_ref, bucket_ref, o_ref):
    bucket = bucket_ref[...]

    def per_head(h, carry):
        acc = jnp.zeros(bucket.shape, F32)
        for b in range(T5_BUCKETS):
            acc = jnp.where(bucket == b, t5_ref[b, h], acc)
        o_ref[h] = acc - t5_ref[T5_FAR_BUCKET, h]
        return carry

    lax.fori_loop(0, HEADS, per_head, 0)


def _t5_band(t5_bias, tq):
    bucket = jnp.asarray(_t5_bucket_table(tq))
    width = tq + T5_BAND_BACK
    return pl.pallas_call(
        _t5_band_kernel,
        name="t5_band",
        grid=(1,),
        in_specs=[pl.BlockSpec(memory_space=pltpu.SMEM),
                  pl.BlockSpec((tq, width), lambda i: (0, 0))],
        out_specs=pl.BlockSpec((HEADS, tq, width), lambda i: (0, 0, 0)),
        out_shape=jax.ShapeDtypeStruct((HEADS, tq, width), F32),
        compiler_params=_params(("arbitrary",)),
    )(t5_bias, bucket)


def _sortable_key(v):
    bits = pltpu.bitcast(v, I32)
    return bits ^ ((bits >> 31) & 0x7FFFFFFF)


def _kth_largest_key(key_ref, kv, k):
    rows = key_ref.shape[0]

    def count_ge(t):
        t_b = jnp.broadcast_to(t, (rows, LANES))
        acc = jnp.zeros((rows, LANES), I32)
        for c in range(kv // LANES):
            acc = acc + jnp.where(key_ref[:, c * LANES:(c + 1) * LANES] >= t_b, 1, 0)
        return jnp.sum(acc, axis=1, keepdims=True)

    t0 = jnp.where(count_ge(jnp.zeros((rows, 1), I32)) >= k, 0, jnp.iinfo(jnp.int32).min).astype(I32)

    def body(it, t):
        cand = t | (jnp.int32(1) << (30 - it))
        return jnp.where(count_ge(cand) >= k, cand, t)

    return lax.fori_loop(0, 31, body, t0)


def _dsa_index_kernel(qi_ref, tail_ref, o_ref, key_ref, *, tq, topk):
    seq = tail_ref.shape[0]
    k_idx = tail_ref[:, :IDX_DIM].astype(BF16)
    for i in range(seq // tq):
        kv = (i + 1) * tq
        rows = slice(i * tq, kv)
        w = tail_ref[rows, IDX_DIM:IDX_DIM + IDX_HEADS] * ((IDX_HEADS * IDX_DIM) ** -0.5)
        score = jnp.zeros((tq, kv), F32)
        for h in range(IDX_HEADS):
            logits = _nt_dot(qi_ref[rows, h * IDX_DIM:(h + 1) * IDX_DIM], k_idx[:kv, :])
            score = score + jnp.maximum(logits, 0.0) * w[:, h:h + 1]
        row = lax.broadcasted_iota(I32, (tq, kv), 0) + i * tq
        col = lax.broadcasted_iota(I32, (tq, kv), 1)
        admissible = (col // CHUNK) <= (row // CHUNK)
        if kv <= topk:
            keep = admissible
        else:
            key_ref[:, :kv] = _sortable_key(jnp.where(admissible, score, NEG_INF))
            thr = _kth_largest_key(key_ref, kv, topk)
            keep = admissible & (key_ref[:, :kv] >= thr)
        o_ref[rows, :kv] = jnp.where(keep, 0.0, NEG_INF).astype(o_ref.dtype)
        if kv < seq:
            o_ref[rows, kv:] = jnp.full((tq, seq - kv), NEG_INF, o_ref.dtype)


def _dsa_index(q_idx, tail, batch, seq, topk):
    tq = min(ATTN_TQ, seq)
    return pl.pallas_call(
        functools.partial(_dsa_index_kernel, tq=tq, topk=topk),
        name="dsa_index",
        grid=(batch,),
        in_specs=[pl.BlockSpec((seq, IDX_HEADS * IDX_DIM), lambda b: (b, 0)),
                  pl.BlockSpec((seq, LANES), lambda b: (b, 0))],
        out_specs=pl.BlockSpec((seq, seq), lambda b: (b, 0)),
        out_shape=jax.ShapeDtypeStruct((batch * seq, seq), BF16),
        scratch_shapes=[pltpu.VMEM((tq, seq), I32)],
        compiler_params=_params(("arbitrary",)),
    )(q_idx, tail)


def _dsa_attn_kernel(q_ref, k_ref, v_ref, sel_ref, band_ref, o_ref, *, tq):
    seq = k_ref.shape[0]
    width = band_ref.shape[1]
    values = functools.partial(_values_and_ones, v_ref)
    def scores(i):
        return _nt_dot(q_ref[i * tq:(i + 1) * tq, :], k_ref[:(i + 1) * tq, :])

    def finish(i, s):
        lo, kv = i * tq, (i + 1) * tq
        near = min(width, kv)
        far = kv - near
        s = s + sel_ref[lo:kv, :kv].astype(F32)
        pieces = [(s[:, far:] + band_ref[:, width - near:], values(slice(far, kv)))]
        if far:
            pieces.append((s[:, :far], values(slice(0, far))))
        o_ref[lo:kv, :] = _softmax_pv(pieces).astype(o_ref.dtype)

    _attend_tiles(seq // tq, scores, finish)


def _dsa_attn(qkv, sel, band, batch, seq):
    tq = min(ATTN_TQ, seq)
    width = band.shape[2]
    return pl.pallas_call(
        functools.partial(_dsa_attn_kernel, tq=tq),
        name="dsa_attn",
        grid=(batch, HEADS),
        in_specs=[pl.BlockSpec((seq, HEAD_DIM), lambda b, h: (b, h)),
                  pl.BlockSpec((seq, HEAD_DIM), lambda b, h: (b, HEADS)),
                  pl.BlockSpec((seq, HEAD_DIM), lambda b, h: (b, HEADS + 1)),
                  pl.BlockSpec((seq, seq), lambda b, h: (b, 0)),
                  pl.BlockSpec((None, tq, width), lambda b, h: (h, 0, 0))],
        out_specs=pl.BlockSpec((seq, HEAD_DIM), lambda b, h: (b, h)),
        out_shape=jax.ShapeDtypeStruct((batch * seq, HEADS * HEAD_DIM), BF16),
        compiler_params=_params(("arbitrary", "arbitrary")),
    )(qkv, qkv, qkv, sel, band)


def _dsa_layer(x, x_bf, t5_bias, w_in, j, w_o, ln_g, ln_b, route_wb, batch, seq):
    hd = HEADS * HEAD_DIM
    n_qkv = hd + 2 * HEAD_DIM
    n_qi = IDX_HEADS * IDX_DIM
    scale = HEAD_DIM ** -0.5
    w_t = jnp.swapaxes(w_in, 1, 2)
    qkv = _matmul(x_bf, w_t, j, n_out=n_qkv, tn=n_qkv // 3, scale=scale, scaled_cols=hd)
    q_idx = _matmul(x_bf, w_t[j:j + 1, n_qkv:n_qkv + n_qi, :], 0, n_out=n_qi, tn=n_qi)
    tail = _small_matmul(x, _pad_cols(w_in[j, :, n_qkv + n_qi:]), jnp.zeros((1, LANES), F32))
    topk = min(IDX_TOPK_MAX, seq // 4)
    sel = _dsa_index(q_idx, tail, batch, seq, topk)
    band = _t5_band(t5_bias, min(ATTN_TQ, seq))
    o = _dsa_attn(qkv, sel, band, batch, seq)
    return _mm_res_ln(o, w_o.astype(BF16), x, ln_g[None, :], ln_b[None, :], *route_wb)


def _dispatch_tables(before, counts, n_tokens, tm):
    n_t = n_tokens // tm
    before = before[:, 0, :MOE_EXPERTS].astype(I32)
    total = counts[0, :MOE_EXPERTS].astype(I32)
    after = jnp.concatenate([before[1:], total[None, :]], axis=0)
    length = after - before
    length_al = (length + ROW_ALIGN - 1) // ROW_ALIGN * ROW_ALIGN
    t_ids = jnp.arange(n_t, dtype=I32)
    e_ids = jnp.arange(MOE_EXPERTS, dtype=I32)
    before_al = jnp.sum(jnp.where((t_ids[None, :] < t_ids[:, None])[:, :, None], length_al[None, :, :], 0), axis=1)
    total_al = jnp.sum(length_al, axis=0)
    seg = (total_al + tm - 1) // tm * tm
    ends = jnp.sum(jnp.where(e_ids[None, :] <= e_ids[:, None], seg[None, :], 0), axis=1)
    starts = ends - seg
    off = jnp.sum(jnp.where((e_ids[None, :] < e_ids[:, None])[None, :, :], length_al[:, None, :], 0), axis=2)
    used = jnp.sum(length_al, axis=1)
    n_chunks = used // ROW_ALIGN
    dstart = starts[None, :] + before_al
    c_row = jnp.arange(TILE_CHUNKS, dtype=I32) * ROW_ALIGN
    c_exp = jnp.sum(((off + length_al)[:, None, :] <= c_row[None, :, None]).astype(I32), axis=2)
    c_exp = jnp.minimum(c_exp, MOE_EXPERTS - 1)
    onehot = c_exp[:, :, None] == e_ids[None, None, :]
    chunk_row = jnp.sum(jnp.where(onehot, (dstart - off)[:, None, :], 0), axis=2) + c_row[None, :]
    pos_tab = off - before
    n_rows = _sorted_rows(n_tokens, tm)
    n_tiles = n_rows // tm
    tile_start = jnp.arange(n_tiles, dtype=I32) * tm
    tile_expert = jnp.minimum(jnp.sum((ends[None, :] <= tile_start[:, None]).astype(I32), axis=1), MOE_EXPERTS - 1)
    tile_valid = (tile_start < ends[-1]).astype(I32)
    tile_first = jnp.concatenate([jnp.ones((1,), I32), (tile_expert[1:] != tile_expert[:-1]).astype(I32)])
    j_ids = jnp.arange(n_tiles, dtype=I32)
    tile_slot = (jnp.sum(jnp.where(j_ids[None, :] <= j_ids[:, None], tile_first[None, :], 0), axis=1) - 1) % 2
    later = (e_ids[None, :] > e_ids[:, None]) & (total[None, :] > 0)
    next_used = jnp.min(jnp.where(later, e_ids[None, :], MOE_EXPERTS), axis=1)
    next_used = jnp.where(next_used == MOE_EXPERTS, -1, next_used).astype(I32)
    tile_next = jnp.sum(jnp.where(tile_expert[:, None] == e_ids[None, :], next_used[None, :], 0), axis=1)
    tail_first = starts + total_al
    tail_chunks = (seg - total_al) // ROW_ALIGN
    return dict(n_chunks=n_chunks.astype(I32), chunk_row=chunk_row.reshape(-1).astype(I32),
                pos_tab=pos_tab.reshape(-1).astype(I32), pos_rows=pos_tab.astype(F32),
                tile_expert=tile_expert.astype(I32), tile_valid=tile_valid, tile_first=tile_first,
                tile_next=tile_next.astype(I32), n_valid=jnp.sum(tile_valid)[None].astype(I32),
                tile_slot=tile_slot.astype(I32),
                tail_first=tail_first.astype(I32),
                tail_chunks=tail_chunks.astype(I32))


def _sorted_rows(n_tokens, tm):
    n_t = n_tokens // tm
    bound = 2 * n_tokens + n_t * MOE_EXPERTS * (ROW_ALIGN - 1) + MOE_EXPERTS * (tm - ROW_ALIGN) + 1
    return (bound + tm - 1) // tm * tm


def _chunk_copy_out(buf, slot, c, row, hbm, sem):
    src = buf.at[slot, pl.ds(pl.multiple_of(c * ROW_ALIGN, ROW_ALIGN), ROW_ALIGN)]
    return pltpu.make_async_copy(src, hbm.at[pl.ds(pl.multiple_of(row, ROW_ALIGN), ROW_ALIGN)], sem.at[slot])


def _chunk_copy_in(hbm, row, buf, slot, c, sem):
    dst = buf.at[slot, pl.ds(pl.multiple_of(c * ROW_ALIGN, ROW_ALIGN), ROW_ALIGN)]
    src = hbm.at[pl.ds(pl.multiple_of(row, ROW_ALIGN), ROW_ALIGN), pl.ds(0, buf.shape[2])]
    return pltpu.make_async_copy(src, dst, sem.at[slot])


def _moe_dispatch_kernel(nch_ref, crow_ref, ptab_ref, tfirst_ref, tchunks_ref, tvalid_ref,
                         x_ref, rt_ref, r_ref, xs_hbm, buf, zeros, sem, sem_z, *, tm, n_tiles):
    i = pl.program_id(0)
    n_t = pl.num_programs(0)
    slot = i % 2

    def wait_tile(tile, s):
        def body(c, carry):
            _chunk_copy_out(buf, s, 0, 0, xs_hbm, sem).wait()
            return carry
        lax.fori_loop(0, nch_ref[tile], body, 0)

    def zero_tail(e, k):
        return pltpu.make_async_copy(
            zeros.at[pl.ds(0, ROW_ALIGN)],
            xs_hbm.at[pl.ds(pl.multiple_of(tfirst_ref[e] + k * ROW_ALIGN, ROW_ALIGN), ROW_ALIGN)], sem_z.at[0])

    def zero_tile(t):
        return pltpu.make_async_copy(zeros, xs_hbm.at[pl.ds(pl.multiple_of(t * tm, tm), tm)], sem_z.at[1])

    def for_each_fill(tail_fn, tile_fn):
        def per_expert(e, carry):
            def per_chunk(k, c2):
                tail_fn(e, k)
                return c2
            lax.fori_loop(0, tchunks_ref[e], per_chunk, 0)
            return carry
        lax.fori_loop(0, MOE_EXPERTS, per_expert, 0)

        def per_tile(t, carry):
            @pl.when(tvalid_ref[t] == 0)
            def _():
                tile_fn(t)
            return carry
        lax.fori_loop(0, n_tiles, per_tile, 0)

    @pl.when(i == 0)
    def _():
        zeros[...] = jnp.zeros(zeros.shape, zeros.dtype)
        for_each_fill(lambda e, k: zero_tail(e, k).start(), lambda t: zero_tile(t).start())

    @pl.when(i >= 2)
    def _():
        wait_tile(i - 2, slot)

    pos1 = rt_ref[4:5, :].astype(I32)
    pos2 = rt_ref[5:6, :].astype(I32)
    e1 = rt_ref[0:1, :].astype(I32)
    e2 = rt_ref[1:2, :].astype(I32)
    for e in range(MOE_EXPERTS):
        shift = ptab_ref[i * MOE_EXPERTS + e]
        pos1 = pos1 + jnp.where(e1 == e, shift, 0)
        pos2 = pos2 + jnp.where(e2 == e, shift, 0)
    lane = lax.broadcasted_iota(I32, (tm, LANES), 1)

    def gate_terms(g):
        hi = g.astype(BF16).astype(F32)
        return jnp.where(lane == 0, hi, jnp.where(lane == 1, g - hi, 0.0)).astype(BF16)

    gate1 = gate_terms(r_ref[:, 2:3])
    gate2 = gate_terms(r_ref[:, 3:4])
    blk = 256
    for k in range(TILE_ROWS // blk):
        row = lax.broadcasted_iota(I32, (blk, tm), 0) + k * blk
        pick1 = jnp.where(row == pos1, 1.0, 0.0).astype(BF16)
        pick2 = jnp.where(row == pos2, 1.0, 0.0).astype(BF16)
        at = slice(k * blk, (k + 1) * blk)
        buf[slot, at, :D_MODEL] = jnp.dot(pick1 + pick2, x_ref[...], preferred_element_type=F32).astype(BF16)
        buf[slot, at, D_MODEL:] = (jnp.dot(pick1, gate1, preferred_element_type=F32)
                                   + jnp.dot(pick2, gate2, preferred_element_type=F32)).astype(BF16)

    def send(c, carry):
        _chunk_copy_out(buf, slot, c, crow_ref[i * TILE_CHUNKS + c], xs_hbm, sem).start()
        return carry
    lax.fori_loop(0, nch_ref[i], send, 0)

    @pl.when(i == n_t - 1)
    def _():
        @pl.when(i >= 1)
        def _():
            wait_tile(i - 1, 1 - slot)
        wait_tile(i, slot)
        for_each_fill(lambda e, k: zero_tail(e, k).wait(), lambda t: zero_tile(t).wait())


def _moe_dispatch(x_bf, route_t, route, tables, tm):
    m, d = x_bf.shape
    n_rows = _sorted_rows(m, tm)
    grid_spec = pltpu.PrefetchScalarGridSpec(
        num_scalar_prefetch=6,
        grid=(m // tm,),
        in_specs=[pl.BlockSpec((tm, d), lambda i, *_: (i, 0)),
                  pl.BlockSpec((8, tm), lambda i, *_: (0, i)),
                  pl.BlockSpec((tm, LANES), lambda i, *_: (i, 0))],
        out_specs=pl.BlockSpec(memory_space=pl.ANY),
        scratch_shapes=[pltpu.VMEM((2, TILE_ROWS, SORTED_WIDTH), BF16), pltpu.VMEM((tm, SORTED_WIDTH), BF16),
                        pltpu.SemaphoreType.DMA((2,)), pltpu.SemaphoreType.DMA((2,))],
    )
    return pl.pallas_call(
        functools.partial(_moe_dispatch_kernel, tm=tm, n_tiles=n_rows // tm),
        name="moe_dispatch",
        grid_spec=grid_spec,
        out_shape=jax.ShapeDtypeStruct((n_rows, SORTED_WIDTH), BF16),
        compiler_params=_params(("arbitrary",)),
    )(tables["n_chunks"], tables["chunk_row"], tables["pos_tab"], tables["tail_first"], tables["tail_chunks"],
      tables["tile_valid"], x_bf, route_t, route)


def _moe_ffn_kernel(te_ref, tv_ref, tf_ref, tn_ref, nv_ref, ts_ref, x_ref, wg_hbm, wu_hbm, wd_hbm, o_ref,
                    wg_f, wu_f, wd_f, wg_bf, wu_bf, wd_bf, sem_w, *, layer):
    j = pl.program_id(0)
    valid = tv_ref[j] == 1
    slot = ts_ref[j]

    def weight_copies(e, s):
        return (pltpu.make_async_copy(wg_hbm.at[layer, e], wg_f.at[s], sem_w.at[s, 0]),
                pltpu.make_async_copy(wu_hbm.at[layer, e], wu_f.at[s], sem_w.at[s, 1]),
                pltpu.make_async_copy(wd_hbm.at[layer, e], wd_f.at[s], sem_w.at[s, 2]))

    @pl.when(j == 0)
    def _():
        for c in weight_copies(te_ref[0], slot):
            c.start()

    @pl.when(valid & (tf_ref[j] == 1))
    def _():
        @pl.when(tn_ref[j] >= 0)
        def _():
            for c in weight_copies(tn_ref[j], 1 - slot):
                c.start()

        for c in weight_copies(te_ref[j], slot):
            c.wait()
        _cast_rows(wg_f.at[slot], wg_bf)
        _cast_rows(wu_f.at[slot], wu_bf)
        _cast_rows(wd_f.at[slot], wd_bf)

    @pl.when(valid)
    def _():
        xb = x_ref[:, :D_MODEL]
        gate = x_ref[:, D_MODEL:D_MODEL + 1].astype(F32) + x_ref[:, D_MODEL + 1:D_MODEL + 2].astype(F32)
        hg = jnp.dot(xb, wg_bf[...], preferred_element_type=F32)
        hu = jnp.dot(xb, wu_bf[...], preferred_element_type=F32)
        act = (hg * (1.0 / (1.0 + jnp.exp(-hg))) * hu).astype(BF16)
        o_ref[...] = (jnp.dot(act, wd_bf[...], preferred_element_type=F32) * gate).astype(o_ref.dtype)


def _moe_ffn(x_sorted, w_gate, w_up, w_down, layer, tables, tm):
    n_rows, width = x_sorted.shape
    d, f = w_gate.shape[2], w_gate.shape[3]
    any_spec = pl.BlockSpec(memory_space=pl.ANY)
    grid_spec = pltpu.PrefetchScalarGridSpec(
        num_scalar_prefetch=6,
        grid=(n_rows // tm,),
        in_specs=[pl.BlockSpec((tm, width), lambda j, te, tv, tf, tn, nv, ts: (jnp.minimum(j, nv[0] - 1), 0)),
                  any_spec, any_spec, any_spec],
        out_specs=pl.BlockSpec((tm, d), lambda j, te, tv, tf, tn, nv, ts: (jnp.minimum(j, nv[0] - 1), 0)),
        scratch_shapes=[pltpu.VMEM((2, d, f), F32), pltpu.VMEM((2, d, f), F32), pltpu.VMEM((2, f, d), F32),
                        pltpu.VMEM((d, f), BF16), pltpu.VMEM((d, f), BF16), pltpu.VMEM((f, d), BF16),
                        pltpu.SemaphoreType.DMA((2, 3))],
    )
    return pl.pallas_call(
        functools.partial(_moe_ffn_kernel, layer=layer),
        name="moe_ffn",
        grid_spec=grid_spec,
        out_shape=jax.ShapeDtypeStruct((n_rows, width), BF16),
        input_output_aliases={6: 0},
        compiler_params=_params(("arbitrary",)),
    )(tables["tile_expert"], tables["tile_valid"], tables["tile_first"], tables["tile_next"], tables["n_valid"],
      tables["tile_slot"], x_sorted, w_gate, w_up, w_down)


def _moe_combine_kernel(nch_ref, crow_ref, ys_hbm, x_ref, r_ref, ptab_ref, g_ref, b_ref, o_ref, obf_ref,
                        buf, sem, *, tm):
    i = pl.program_id(0)
    n_t = pl.num_programs(0)
    slot = i % 2

    def fetch(tile, s):
        def body(c, carry):
            _chunk_copy_in(ys_hbm, crow_ref[tile * TILE_CHUNKS + c], buf, s, c, sem).start()
            return carry
        lax.fori_loop(0, nch_ref[tile], body, 0)

    @pl.when(i == 0)
    def _():
        buf[...] = jnp.zeros(buf.shape, buf.dtype)
        fetch(0, 0)

    @pl.when(i + 1 < n_t)
    def _():
        fetch(i + 1, 1 - slot)

    def arrived(c, carry):
        _chunk_copy_in(ys_hbm, 0, buf, slot, 0, sem).wait()
        return carry
    lax.fori_loop(0, nch_ref[i], arrived, 0)

    lane = lax.broadcasted_iota(I32, (tm, LANES), 1)
    shift = ptab_ref[...]
    e1 = r_ref[:, 0:1].astype(I32)
    e2 = r_ref[:, 1:2].astype(I32)
    pos1 = (r_ref[:, 4:5] + jnp.sum(jnp.where(lane == e1, shift, 0.0), axis=1, keepdims=True)).astype(I32)
    pos2 = (r_ref[:, 5:6] + jnp.sum(jnp.where(lane == e2, shift, 0.0), axis=1, keepdims=True)).astype(I32)
    col = lax.broadcasted_iota(I32, (tm, TILE_ROWS), 1)
    pick = jnp.where((col == pos1) | (col == pos2), 1.0, 0.0).astype(BF16)
    y = jnp.dot(pick, buf[slot], preferred_element_type=F32)
    out = _layernorm_rows(DEEPNORM_ALPHA * x_ref[...] + y, g_ref[...], b_ref[...])
    o_ref[...] = out
    obf_ref[...] = out.astype(BF16)


def _moe_combine(y_sorted, x, route, tables, g, b, tm, out_batch=None):
    m, d = x.shape
    out_f32 = jax.ShapeDtypeStruct((m, d), F32)
    out_spec = pl.BlockSpec((tm, d), lambda i, *_: (i, 0))
    if out_batch:
        per = m // out_batch // tm
        out_f32 = jax.ShapeDtypeStruct((out_batch, m // out_batch, d), F32)
        out_spec = pl.BlockSpec((None, tm, d), lambda i, *_: (i // per, i % per, 0))
    grid_spec = pltpu.PrefetchScalarGridSpec(
        num_scalar_prefetch=2,
        grid=(m // tm,),
        in_specs=[pl.BlockSpec(memory_space=pl.ANY),
                  pl.BlockSpec((tm, d), lambda i, *_: (i, 0)),
                  pl.BlockSpec((tm, LANES), lambda i, *_: (i, 0)),
                  pl.BlockSpec((None, 1, LANES), lambda i, *_: (i, 0, 0)),
                  pl.BlockSpec((1, d), lambda i, *_: (0, 0)),
                  pl.BlockSpec((1, d), lambda i, *_: (0, 0))],
        out_specs=[out_spec, pl.BlockSpec((tm, d), lambda i, *_: (i, 0))],
        scratch_shapes=[pltpu.VMEM((2, TILE_ROWS, d), BF16), pltpu.SemaphoreType.DMA((2,))],
    )
    pos_rows = _pad_cols(tables["pos_rows"])[:, None, :]
    return pl.pallas_call(
        functools.partial(_moe_combine_kernel, tm=tm),
        name="moe_combine",
        grid_spec=grid_spec,
        out_shape=[out_f32, jax.ShapeDtypeStruct((m, d), BF16)],
        compiler_params=_params(("arbitrary",)),
    )(tables["n_chunks"], tables["chunk_row"], y_sorted, x, route, pos_rows, g, b)


def _route_weights(w_group, b_group, w_router, b_router):
    w = _pad_cols(jnp.concatenate([w_group, w_router], axis=1))
    w_hi = w.astype(BF16)
    w_mid = (w - w_hi.astype(F32)).astype(BF16)
    return w_hi, w_mid, _pad_cols(jnp.concatenate([b_group, b_router])[None, :])


def _moe_layer(x, x_bf, route, route_t, before, counts, w_gate, w_up, w_down, layer, ln_g, ln_b, out_batch=None):
    n_tokens = x.shape[0]
    tm = min(MOE_TM, n_tokens)
    tables = _dispatch_tables(before, counts, n_tokens, tm)
    x_sorted = _moe_dispatch(x_bf, route_t, route, tables, tm)
    y_sorted = _moe_ffn(x_sorted, w_gate, w_up, w_down, layer, tables, tm)
    return _moe_combine(y_sorted, x, route, tables, ln_g[None, :], ln_b[None, :], tm, out_batch)


def kernel(x, t5_rel_bias, mla_w_in, mla_q_norm, mla_kv_norm, mla_w_q_up, mla_w_kv_up, mla_w_o, fox_w_in, fox_b_f, fox_w_o, dsa_w_in, dsa_w_o, ln_g, ln_b, moe_w_group, moe_b_group, moe_w_router, moe_b_router, moe_w_gate, moe_w_up, moe_w_down):
    batch, seq, d = x.shape
    x_bf = x
    for layer in range(DEPTH):
        kind = layer % N_MIXERS
        j = layer // N_MIXERS
        g0, b0 = ln_g[layer, 0], ln_b[layer, 0]
        route_wb = _route_weights(moe_w_group[layer], moe_b_group[layer], moe_w_router[layer], moe_b_router[layer])
        if kind == 0:
            mixed = _mla_layer(x, x_bf, mla_w_in[j], mla_q_norm[j], mla_kv_norm[j], mla_w_q_up[j],
                               mla_w_kv_up, j, mla_w_o[j], g0, b0, route_wb, batch, seq)
        elif kind == 1:
            mixed = _fox_layer(x, x_bf, fox_w_in, j, fox_b_f[j], fox_w_o[j], g0, b0, route_wb, batch, seq)
        else:
            mixed = _dsa_layer(x, x_bf, t5_rel_bias, dsa_w_in, j, dsa_w_o[j], g0, b0, route_wb, batch, seq)
        x, x_bf = _moe_layer(*mixed, moe_w_gate, moe_w_up, moe_w_down, layer, ln_g[layer, 1], ln_b[layer, 1],
                             batch if layer == DEPTH - 1 else None)
    return x
```
